```python
import jax, jax.numpy as jnp
from jax import lax
import numpy as np

D_MODEL = 2048
BATCH = 1
SEQ = 8192
DEPTH = 2
DEC_BATCH = 128
DEC_SEQ = 4
PAST_LEN = 2048
PAGE_SIZE = 128

N_META = 16
N_EVEN = (DEPTH + 1) // 2
N_ODD = DEPTH // 2
GDN_HEADS = 8
GDN_DK = 128
GDN_DV = 128
GDN_QK = GDN_HEADS * GDN_DK
GDN_V = GDN_HEADS * GDN_DV
GDN_CONV = 4
GDN_CONV_DIM = 2 * GDN_QK + GDN_V
GDN_CHUNK = 64
FOX_HEADS = 8
FOX_DH = 128
FOX_W = FOX_HEADS * FOX_DH
FOX_BLOCK = 128
IN_SIZES = (GDN_CONV_DIM, GDN_V, GDN_HEADS, GDN_HEADS, FOX_W, FOX_W, FOX_W, FOX_HEADS, FOX_W)
IN_COLS = GDN_CONV_DIM + GDN_V + 2 * GDN_HEADS + 4 * FOX_W + FOX_HEADS
MIX_OUT = GDN_V + FOX_W
RWKV_HEAD = 64
RWKV_HEADS = D_MODEL // RWKV_HEAD
LORA_W = 96
LORA_A = 96
LORA_G = 256
D_FF = 4 * D_MODEL
NORM_EPS = 1e-6
L2_EPS = 1e-6
GN_EPS = 64e-5
NEG_INF = -1e30

kernel_name = "hybrid_gdn_fox_rwkv7_step"


def rmsnorm(x, w):
    xf = x.astype(jnp.float32)
    xf = xf * lax.rsqrt(jnp.mean(xf * xf, axis=-1, keepdims=True) + NORM_EPS)
    return (xf * w.astype(jnp.float32)).astype(x.dtype)


def l2norm(x):
    xf = x.astype(jnp.float32)
    return xf * lax.rsqrt(jnp.sum(xf * xf, axis=-1, keepdims=True) + L2_EPS)


def split_cols(p, sizes):
    return jnp.split(p, np.cumsum(sizes)[:-1], axis=-1)


def pad_front(t, pad):
    return jnp.pad(t, ((0, 0), (pad, 0)) + ((0, 0),) * (t.ndim - 2))


def causal_conv(x, buf, w):
    T = x.shape[1]
    xp = jnp.concatenate([buf.astype(x.dtype), x], axis=1)
    y = sum(xp[:, i:i + T] * w[i] for i in range(GDN_CONV))
    return y, xp[:, T:]


def gdn_chunked(q, k, v, g, beta, S0, chunk):
    B, T, H, DK = q.shape
    DV = v.shape[-1]
    NC = T // chunk

    def to_chunks(t):
        t = t.reshape((B, NC, chunk, H) + t.shape[3:])
        return jnp.moveaxis(t, (1, 3), (0, 2))

    qc, kc, vc, gc, bc = map(to_chunks, (q, k, v, g, beta))
    gcum = jnp.cumsum(gc, axis=-1)
    causal = jnp.tril(jnp.ones((chunk, chunk), bool))
    strict = jnp.tril(jnp.ones((chunk, chunk), bool), -1)
    diff = gcum[..., :, None] - gcum[..., None, :]
    decay = jnp.where(causal, jnp.exp(jnp.where(causal, diff, 0.0)), 0.0)
    kb = kc * bc[..., None]
    A = jnp.where(strict, jnp.einsum('nbhcd,nbhsd->nbhcs', kb, kc) * decay, 0.0)
    IA = A + jnp.eye(chunk, dtype=A.dtype)
    rhs = jnp.concatenate([vc * bc[..., None], kb * jnp.exp(gcum)[..., None]], axis=-1)
    sol = lax.linalg.triangular_solve(IA, rhs, left_side=True, lower=True, unit_diagonal=True)
    u, w = sol[..., :DV], sol[..., DV:]
    qk = jnp.einsum('nbhcd,nbhsd->nbhcs', qc, kc) * decay
    q_dec = qc * jnp.exp(gcum)[..., None]
    k_dec = kc * jnp.exp(gcum[..., -1:] - gcum)[..., None]
    g_last = jnp.exp(gcum[..., -1])

    def step(S, xs):
        u_i, w_i, qk_i, qd_i, kd_i, gl_i = xs
        v_new = u_i - jnp.einsum('bhcd,bhde->bhce', w_i, S)
        o = jnp.einsum('bhcd,bhde->bhce', qd_i, S) + jnp.einsum('bhcs,bhse->bhce', qk_i, v_new)
        S = S * gl_i[..., None, None] + jnp.einsum('bhcd,bhce->bhde', kd_i, v_new)
        return S, o

    S_fin, o = lax.scan(step, S0, (u, w, qk, q_dec, k_dec, g_last))
    o = jnp.moveaxis(o, (0, 2), (1, 3)).reshape(B, T, H, DV)
    return o, S_fin


def gdn_mix(qkv, z, a, b, conv_buf, S0, conv_w, A_log, dt_bias, norm_w, chunk, pad):
    B, T, _ = qkv.shape
    y, new_buf = causal_conv(qkv, conv_buf, conv_w)
    y = jax.nn.silu(y)
    q, k, v = jnp.split(y, [GDN_QK, 2 * GDN_QK], axis=-1)
    q = l2norm(q.reshape(B, T, GDN_HEADS, GDN_DK)) * GDN_DK ** -0.5
    k = l2norm(k.reshape(B, T, GDN_HEADS, GDN_DK))
    v = v.reshape(B, T, GDN_HEADS, GDN_DV).astype(jnp.float32)
    g = -jnp.exp(A_log.astype(jnp.float32)) * jax.nn.softplus(a.astype(jnp.float32) + dt_bias.astype(jnp.float32))
    beta = jax.nn.sigmoid(b.astype(jnp.float32))
    q, k, v, g, beta = (pad_front(t, pad) for t in (q, k, v, g, beta))
    o, S_new = gdn_chunked(q, k, v, g, beta, S0.astype(jnp.float32), chunk)
    o = o[:, pad:]
    o = rmsnorm(o, norm_w) * jax.nn.silu(z.reshape(B, T, GDN_HEADS, GDN_DV).astype(jnp.float32))
    return o.reshape(B, T, GDN_V).astype(qkv.dtype), new_buf, S_new


def fox_attend(q, k, v, cq, ck, qpos, kpos, kmin):
    s = jnp.einsum('bqhd,bkhd->bhqk', q, k, preferred_element_type=jnp.float32) * FOX_DH ** -0.5
    s = s + jnp.moveaxis(cq, 2, 1)[..., :, None] - jnp.moveaxis(ck, 2, 1)[..., None, :]
    mask = (kpos[None, :] <= qpos[:, None]) & (kpos[None, :] >= kmin)
    p = jax.nn.softmax(jnp.where(mask, s, NEG_INF), axis=-1)
    return jnp.einsum('bhqk,bkhd->bqhd', p.astype(v.dtype), v)


def fox_prompt(q, k, v, logf):
    B, T = q.shape[:2]
    pad = (-T) % FOX_BLOCK
    qp, kp, vp, lp = (pad_front(t, pad) for t in (q, k, v, logf))
    Tp = T + pad
    NB = Tp // FOX_BLOCK
    c = jnp.cumsum(lp, axis=1)
    pos = jnp.arange(Tp)
    qb = jnp.moveaxis(qp.reshape(B, NB, FOX_BLOCK, FOX_HEADS, FOX_DH), 1, 0)
    cb = jnp.moveaxis(c.reshape(B, NB, FOX_BLOCK, FOX_HEADS), 1, 0)
    pb = pos.reshape(NB, FOX_BLOCK)
    o = lax.map(lambda xs: fox_attend(xs[0], kp, vp, xs[1], c, xs[2], pos, pad), (qb, cb, pb))
    o = jnp.moveaxis(o, 0, 1).reshape(B, Tp, FOX_HEADS, FOX_DH)
    return o[:, pad:]


def fox_sample(q, k, v, logf, cache_k, cache_v, cache_logf, page_table):
    Bd, Tn = q.shape[:2]
    pk = cache_k[page_table].reshape(Bd, -1, FOX_HEADS, FOX_DH).astype(k.dtype)
    pv = cache_v[page_table].reshape(Bd, -1, FOX_HEADS, FOX_DH).astype(v.dtype)
    pl = cache_logf[page_table].reshape(Bd, -1, FOX_HEADS).astype(jnp.float32)
    L = pk.shape[1]
    kf = jnp.concatenate([pk, k], axis=1)
    vf = jnp.concatenate([pv, v], axis=1)
    c = jnp.cumsum(jnp.concatenate([pl, logf], axis=1), axis=1)
    kpos = jnp.arange(L + Tn)
    qpos = L + jnp.arange(Tn)
    return fox_attend(q, kf, vf, c[:, L:], c, qpos, kpos, 0)


def hybrid_mixer(h, conv_buf, S0, past, w_in, conv_w, A_log, dt_bias, norm_w, b_f, w_out):
    B, T, _ = h.shape
    p = h @ w_in
    qkv, z, a, b, fq, fk, fv, ff, fz = split_cols(p, IN_SIZES)
    if past is None:
        chunk, pad = GDN_CHUNK, (-T) % GDN_CHUNK
    else:
        chunk, pad = T, 0
    o_gdn, new_buf, S_new = gdn_mix(qkv, z, a, b, conv_buf, S0, conv_w, A_log, dt_bias, norm_w, chunk, pad)
    fq = fq.reshape(B, T, FOX_HEADS, FOX_DH)
    fk = fk.reshape(B, T, FOX_HEADS, FOX_DH)
    fv = fv.reshape(B, T, FOX_HEADS, FOX_DH)
    logf = jax.nn.log_sigmoid(ff.astype(jnp.float32) + b_f.astype(jnp.float32))
    if past is None:
        o_fox = fox_prompt(fq, fk, fv, logf)
    else:
        o_fox = fox_sample(fq, fk, fv, logf, *past)
    o_fox = (o_fox.reshape(B, T, FOX_W) * jax.nn.sigmoid(fz)).astype(h.dtype)
    out = jnp.concatenate([o_gdn, o_fox], axis=-1) @ w_out
    return out, (fk, fv, logf, new_buf, S_new)


def rwkv7_scan(r, w, k, v, a, b, S0):
    def step(S, xs):
        r_t, w_t, k_t, v_t, a_t, b_t = xs
        sa = jnp.einsum('bhij,bhj->bhi', S, a_t)
        S = S * w_t[:, :, None, :] + sa[..., None] * b_t[:, :, None, :] + v_t[..., None] * k_t[:, :, None, :]
        return S, jnp.einsum('bhij,bhj->bhi', S, r_t)

    xs = tuple(jnp.moveaxis(t, 1, 0) for t in (r, w, k, v, a, b))
    S, y = lax.scan(step, S0, xs)
    return jnp.moveaxis(y, 0, 1), S


def rwkv7_mixer(h, shift, S0, mu, w0, w1, w2, a0, a1, a2, g1, g2, k_k, k_a, r_k, w_r, w_k, w_v, w_o, ln_w, ln_b):
    B, T, D = h.shape
    f32 = jnp.float32
    prev = jnp.concatenate([shift[:, None].astype(h.dtype), h[:, :-1]], axis=1)
    xx = prev - h
    xr, xw, xk, xv, xa, xg = [h + xx * mu[i] for i in range(6)]
    r = (xr @ w_r).astype(f32)
    k = (xk @ w_k).astype(f32)
    v = (xv @ w_v).astype(f32)
    wlog = -jax.nn.softplus(-(w0 + jnp.tanh(xw @ w1) @ w2).astype(f32)) - 0.5
    decay = jnp.exp(-jnp.exp(wlog))
    a = jax.nn.sigmoid((a0 + (xa @ a1) @ a2).astype(f32))
    g = jax.nn.sigmoid(xg @ g1) @ g2
    shp = (B, T, RWKV_HEADS, RWKV_HEAD)
    kk = l2norm((k * k_k).reshape(shp))
    k = k * (1.0 + (a - 1.0) * k_a)
    r, k, v, a, decay = (t.reshape(shp) for t in (r, k, v, a, decay))
    y, S_new = rwkv7_scan(r, decay, k, v, -kk, kk * a, S0.astype(f32))
    mean = jnp.mean(y, axis=-1, keepdims=True)
    var = jnp.mean((y - mean) ** 2, axis=-1, keepdims=True)
    yn = ((y - mean) * lax.rsqrt(var + GN_EPS)).reshape(B, T, D) * ln_w + ln_b
    bonus = (jnp.sum(r * k * r_k, axis=-1, keepdims=True) * v).reshape(B, T, D)
    out = ((yn + bonus) * g).astype(h.dtype) @ w_o
    return out, (h[:, -1], S_new)


def sqrelu_mlp(h, w_up, w_down):
    u = jax.nn.relu(h @ w_up)
    return (u * u) @ w_down


def setup_inputs(seed: int = 0) -> dict:
    key = jax.random.key(seed)
    ks = iter(jax.random.split(key, 64))
    f32 = jnp.float32

    def nrm(shape, scale):
        return scale * jax.random.normal(next(ks), shape, f32)

    def unif(shape, lo, hi):
        return jax.random.uniform(next(ks), shape, f32, minval=lo, maxval=hi)

    n_pages = PAST_LEN // PAGE_SIZE
    n_pool = (DEC_BATCH * n_pages * 5) // 4
    page_table = jax.random.permutation(next(ks), n_pool)[:DEC_BATCH * n_pages].reshape(DEC_BATCH, n_pages).astype(jnp.int32)
    dt = jnp.exp(unif((N_EVEN, GDN_HEADS), float(np.log(1e-3)), float(np.log(1e-1))))
    return {
        "x_prompt": nrm((BATCH, SEQ, D_MODEL), 1.0),
        "x_sample": nrm((DEC_BATCH, DEC_SEQ, D_MODEL), 1.0),
        "cache_fox_k": nrm((N_EVEN, n_pool, PAGE_SIZE, FOX_HEADS, FOX_DH), 1.0),
        "cache_fox_v": nrm((N_EVEN, n_pool, PAGE_SIZE, FOX_HEADS, FOX_DH), 1.0),
        "cache_fox_logf": -unif((N_EVEN, n_pool, PAGE_SIZE, FOX_HEADS), 0.02, 0.3),
        "state_gdn_conv": nrm((N_EVEN, DEC_BATCH, GDN_CONV - 1, GDN_CONV_DIM), 1.0),
        "state_gdn_S": nrm((N_EVEN, DEC_BATCH, GDN_HEADS, GDN_DK, GDN_DV), 0.1),
        "state_rwkv_shift": nrm((N_ODD, DEC_BATCH, D_MODEL), 1.0),
        "state_rwkv_S": nrm((N_ODD, DEC_BATCH, RWKV_HEADS, RWKV_HEAD, RWKV_HEAD), 0.1),
        "page_table": page_table,
        "meta_tokens": nrm((N_META, D_MODEL), 1.0),
        "ln_mix": 1.0 + nrm((DEPTH, D_MODEL), 0.02),
        "ln_mlp": 1.0 + nrm((DEPTH, D_MODEL), 0.02),
        "ln_final": 1.0 + nrm((D_MODEL,), 0.02),
        "w_in0": nrm((N_EVEN, D_MODEL, IN_COLS), D_MODEL ** -0.5),
        "gdn_conv_w": nrm((N_EVEN, GDN_CONV, GDN_CONV_DIM), GDN_CONV ** -0.5),
        "gdn_A_log": jnp.log(unif((N_EVEN, GDN_HEADS), 1.0, 16.0)),
        "gdn_dt_bias": dt + jnp.log(-jnp.expm1(-dt)),
        "gdn_norm_w": 1.0 + nrm((N_EVEN, GDN_DV), 0.02),
        "fox_b_f": unif((N_EVEN, FOX_HEADS), 1.0, 4.0),
        "w_out0": nrm((N_EVEN, MIX_OUT, D_MODEL), MIX_OUT ** -0.5),
        "rwkv_mu": unif((N_ODD, 6, D_MODEL), 0.0, 1.0),
        "rwkv_w0": unif((N_ODD, D_MODEL), -6.0, -1.0),
        "rwkv_w1": nrm((N_ODD, D_MODEL, LORA_W), D_MODEL ** -0.5),
        "rwkv_w2": nrm((N_ODD, LORA_W, D_MODEL), 0.1 * LORA_W ** -0.5),
        "rwkv_a0": nrm((N_ODD, D_MODEL), 0.1),
        "rwkv_a1": nrm((N_ODD, D_MODEL, LORA_A), D_MODEL ** -0.5),
        "rwkv_a2": nrm((N_ODD, LORA_A, D_MODEL), 0.1 * LORA_A ** -0.5),
        "rwkv_g1": nrm((N_ODD, D_MODEL, LORA_G), D_MODEL ** -0.5),
        "rwkv_g2": nrm((N_ODD, LORA_G, D_MODEL), LORA_G ** -0.5),
        "rwkv_k_k": 0.85 + nrm((N_ODD, D_MODEL), 0.02),
        "rwkv_k_a": 1.0 + nrm((N_ODD, D_MODEL), 0.02),
        "rwkv_r_k": nrm((N_ODD, RWKV_HEADS, RWKV_HEAD), 0.1),
        "rwkv_w_r": nrm((N_ODD, D_MODEL, D_MODEL), D_MODEL ** -0.5),
        "rwkv_w_k": nrm((N_ODD, D_MODEL, D_MODEL), D_MODEL ** -0.5),
        "rwkv_w_v": nrm((N_ODD, D_MODEL, D_MODEL), D_MODEL ** -0.5),
        "rwkv_w_o": nrm((N_ODD, D_MODEL, D_MODEL), D_MODEL ** -0.5),
        "rwkv_ln_w": 1.0 + nrm((N_ODD, D_MODEL), 0.02),
        "rwkv_ln_b": nrm((N_ODD, D_MODEL), 0.02),
        "w_up": nrm((DEPTH, D_MODEL, D_FF), D_MODEL ** -0.5),
        "w_down": nrm((DEPTH, D_FF, D_MODEL), D_FF ** -0.5),
    }


def reference(x_prompt, x_sample, cache_fox_k, cache_fox_v, cache_fox_logf, state_gdn_conv, state_gdn_S,
              state_rwkv_shift, state_rwkv_S, page_table, meta_tokens, ln_mix, ln_mlp, ln_final,
              w_in0, gdn_conv_w, gdn_A_log, gdn_dt_bias, gdn_norm_w, fox_b_f, w_out0,
              rwkv_mu, rwkv_w0, rwkv_w1, rwkv_w2, rwkv_a0, rwkv_a1, rwkv_a2, rwkv_g1, rwkv_g2,
              rwkv_k_k, rwkv_k_a, rwkv_r_k, rwkv_w_r, rwkv_w_k, rwkv_w_v, rwkv_w_o, rwkv_ln_w, rwkv_ln_b,
              w_up, w_down):
    B = x_prompt.shape[0]
    meta = jnp.broadcast_to(meta_tokens[None].astype(x_prompt.dtype), (B, N_META, D_MODEL))
    xp = jnp.concatenate([meta, x_prompt], axis=1)
    xs = x_sample
    fk_p, fk_s, fv_p, fv_s, lf_p, lf_s = [], [], [], [], [], []
    cb_p, cb_s, gs_p, gs_s = [], [], [], []
    sh_p, sh_s, rs_p, rs_s = [], [], [], []
    for l in range(DEPTH):
        if l % 2 == 0:
            e = l // 2
            wts = (w_in0[e], gdn_conv_w[e], gdn_A_log[e], gdn_dt_bias[e], gdn_norm_w[e], fox_b_f[e], w_out0[e])
            conv0 = jnp.zeros((B, GDN_CONV - 1, GDN_CONV_DIM), xp.dtype)
            S0 = jnp.zeros((B, GDN_HEADS, GDN_DK, GDN_DV), jnp.float32)
            o_p, st_p = hybrid_mixer(rmsnorm(xp, ln_mix[l]), conv0, S0, None, *wts)
            past = (cache_fox_k[e], cache_fox_v[e], cache_fox_logf[e], page_table)
            o_s, st_s = hybrid_mixer(rmsnorm(xs, ln_mix[l]), state_gdn_conv[e], state_gdn_S[e], past, *wts)
            for lst, val in zip((fk_p, fv_p, lf_p, cb_p, gs_p), st_p):
                lst.append(val)
            for lst, val in zip((fk_s, fv_s, lf_s, cb_s, gs_s), st_s):
                lst.append(val)
        else:
            o = l // 2
            wts = (rwkv_mu[o], rwkv_w0[o], rwkv_w1[o], rwkv_w2[o], rwkv_a0[o], rwkv_a1[o], rwkv_a2[o],
                   rwkv_g1[o], rwkv_g2[o], rwkv_k_k[o], rwkv_k_a[o], rwkv_r_k[o], rwkv_w_r[o], rwkv_w_k[o],
                   rwkv_w_v[o], rwkv_w_o[o], rwkv_ln_w[o], rwkv_ln_b[o])
            sh0 = jnp.zeros((B, D_MODEL), xp.dtype)
            R0 = jnp.zeros((B, RWKV_HEADS, RWKV_HEAD, RWKV_HEAD), jnp.float32)
            o_p, st_p = rwkv7_mixer(rmsnorm(xp, ln_mix[l]), sh0, R0, *wts)
            o_s, st_s = rwkv7_mixer(rmsnorm(xs, ln_mix[l]), state_rwkv_shift[o], state_rwkv_S[o], *wts)
            sh_p.append(st_p[0])
            rs_p.append(st_p[1])
            sh_s.append(st_s[0])
            rs_s.append(st_s[1])
        xp = xp + o_p
        xs = xs + o_s
        xp = xp + sqrelu_mlp(rmsnorm(xp, ln_mlp[l]), w_up[l], w_down[l])
        xs = xs + sqrelu_mlp(rmsnorm(xs, ln_mlp[l]), w_up[l], w_down[l])
    y_prompt = rmsnorm(xp, ln_final)[:, N_META:]
    y_sample = rmsnorm(xs, ln_final)
    return (y_prompt, y_sample,
            jnp.stack(fk_p), jnp.stack(fk_s), jnp.stack(fv_p), jnp.stack(fv_s),
            jnp.stack(lf_p), jnp.stack(lf_s), jnp.stack(cb_p), jnp.stack(cb_s),
            jnp.stack(gs_p), jnp.stack(gs_s), jnp.stack(sh_p), jnp.stack(sh_s),
            jnp.stack(rs_p), jnp.stack(rs_s))
```

```python
import functools

import jax
import jax.numpy as jnp
from jax import lax
from jax.experimental import pallas as pl
from jax.experimental.pallas import tpu as pltpu

F32 = jnp.float32
BF16 = jnp.bfloat16
HI = lax.Precision.HIGHEST

D_MODEL = 2048
N_META = 16
GDN_HEADS = 8
GDN_D = 128
GDN_QK = GDN_HEADS * GDN_D
GDN_QKV = 3 * GDN_QK
FOX_HEADS = 8
FOX_DH = 128
FOX_W = FOX_HEADS * FOX_DH
RWKV_HEAD = 64
RWKV_PAIRS = D_MODEL // (2 * RWKV_HEAD)
NORM_EPS = 1e-6
L2_EPS = 1e-6
GN_EPS = 64e-5
NEG_INF = -1e30

LANES = 128
SUBLANES = 8
ROW_TILE = 640
VMEM_LIMIT = 48 * 1024 * 1024

COL_Z = GDN_QKV
COL_FQ = COL_Z + GDN_QK
COL_FK = COL_FQ + FOX_W
COL_FV = COL_FK + FOX_W
COL_FZ = COL_FV + FOX_W
LANE_A = 0
LANE_B = 8
LANE_F = 16


def _pick(n, cands):
    for c in cands:
        if n % c == 0:
            return c
    raise ValueError(f"no tile for {n}")


def _cparams(sem):
    return pltpu.CompilerParams(dimension_semantics=sem, vmem_limit_bytes=VMEM_LIMIT)


def _dot(a, b, prec=None):
    return jnp.dot(a, b, preferred_element_type=F32, precision=prec)


def _dot_nt(a, b, prec=None):
    return lax.dot_general(a, b, (((1,), (1,)), ((), ())), preferred_element_type=F32, precision=prec)


def _dot_tn(a, b, prec=None):
    return lax.dot_general(a, b, (((0,), (0,)), ((), ())), preferred_element_type=F32, precision=prec)


def _bf(x):
    return x.astype(BF16)


def _softplus(x):
    return jnp.maximum(x, 0.0) + jnp.log(1.0 + jnp.exp(-jnp.abs(x)))


def _sigmoid(x):
    return 1.0 / (1.0 + jnp.exp(-x))


def _iota2(shape, dim):
    return lax.broadcasted_iota(jnp.int32, shape, dim)


def _div(x, n):
    assert n & (n - 1) == 0
    return x >> (n.bit_length() - 1)


def _mod(x, n):
    assert n & (n - 1) == 0
    return x & (n - 1)


def _tri_inv(n_mat, n):
    size = n_mat.shape[0]
    eye = (_iota2((size, size), 0) == _iota2((size, size), 1)).astype(F32)
    t = eye + n_mat
    p = n_mat
    m = 2
    while m < n:
        p = _dot(p, p, HI)
        t = t + _dot(t, p, HI)
        m *= 2
    return t


def _rmsnorm_kernel(x_ref, w_ref, o_ref):
    x = x_ref[...]
    ms = jnp.mean(x * x, axis=-1, keepdims=True)
    o_ref[...] = (x * lax.rsqrt(ms + NORM_EPS) * w_ref[...]).astype(o_ref.dtype)


def rmsnorm(x, w, out_dtype):
    m, d = x.shape
    tm = _pick(m, (320, 256, 128))
    return pl.pallas_call(
        _rmsnorm_kernel,
        grid=(m // tm,),
        in_specs=[pl.BlockSpec((tm, d), lambda i: (i, 0)), pl.BlockSpec((1, d), lambda i: (0, 0))],
        out_specs=pl.BlockSpec((tm, d), lambda i: (i, 0)),
        out_shape=jax.ShapeDtypeStruct((m, d), out_dtype),
        compiler_params=_cparams(("parallel",)),
        name="rmsnorm",
    )(x, w.reshape(1, d))


def _mm_kernel(nk, act, has_res, *refs):
    if has_res:
        x_ref, w_ref, r_ref, o_ref = refs[:4]
        scr = refs[4:]
    else:
        x_ref, w_ref, o_ref = refs[:3]
        r_ref = None
        scr = refs[3:]

    def finish(acc):
        if act == "relu2":
            acc = jnp.square(jnp.maximum(acc, 0.0))
        elif act == "tanh":
            acc = jnp.tanh(acc)
        elif act == "sigmoid":
            acc = _sigmoid(acc)
        if has_res:
            acc = r_ref[...] + acc
        o_ref[...] = acc.astype(o_ref.dtype)

    if nk == 1:
        finish(_dot(x_ref[...], w_ref[...]))
    else:
        acc_ref = scr[0]
        k = pl.program_id(2)

        @pl.when(k == 0)
        def _():
            acc_ref[...] = jnp.zeros(acc_ref.shape, F32)

        acc_ref[...] += _dot(x_ref[...], w_ref[...])

        @pl.when(k == nk - 1)
        def _():
            finish(acc_ref[...])


def matmul(x, w, *, act=None, res=None, out_dtype=F32, name="matmul"):
    m, k = x.shape
    n = w.shape[1]
    tm = _pick(m, (ROW_TILE, 512, 256, 128))
    tn = _pick(n, (512, 256, 128))
    tk = k if k <= 2048 else 2048
    nk = k // tk
    in_specs = [pl.BlockSpec((tm, tk), lambda i, j, kk: (i, kk)), pl.BlockSpec((tk, tn), lambda i, j, kk: (kk, j))]
    args = [x, w]
    if res is not None:
        in_specs.append(pl.BlockSpec((tm, tn), lambda i, j, kk: (i, j)))
        args.append(res)
    return pl.pallas_call(
        functools.partial(_mm_kernel, nk, act, res is not None),
        grid=(m // tm, n // tn, nk),
        in_specs=in_specs,
        out_specs=pl.BlockSpec((tm, tn), lambda i, j, kk: (i, j)),
        out_shape=jax.ShapeDtypeStruct((m, n), out_dtype),
        scratch_shapes=[pltpu.VMEM((tm, tn), F32)] if nk > 1 else [],
        compiler_params=_cparams(("parallel", "parallel", "arbitrary")),
        name=name,
    )(*args)


def _gdn_kernel(C, NC, first_valid, qkv_ref, z_ref, ps_ref, cw_ref, alog_ref, dt_ref, nw_ref, s0_ref,
                o_ref, sout_ref, S, ext):
    c = pl.program_id(1)

    @pl.when(c == 0)
    def _():
        S[...] = s0_ref[0]
        ext[0:SUBLANES, :] = jnp.zeros((SUBLANES, GDN_QKV), F32)

    x = qkv_ref[...]
    ext[SUBLANES:SUBLANES + C, :] = x
    cw = cw_ref[...]
    y = ((ext[5:5 + C, :] * cw[0:1] + ext[6:6 + C, :] * cw[1:2]) + ext[7:7 + C, :] * cw[2:3]) + ext[8:8 + C, :] * cw[3:4]
    ext[0:SUBLANES, :] = x[C - SUBLANES:C]
    y = y * _sigmoid(y)

    rows = c * C + _iota2((C, 1), 0)
    valid = rows >= first_valid
    ps = ps_ref[...]
    g_all = jnp.where(valid, -jnp.exp(alog_ref[...]) * _softplus(ps + dt_ref[...]), 0.0)
    beta_all = jnp.where(valid, _sigmoid(ps), 0.0)
    ri = _iota2((C, C), 0)
    ci = _iota2((C, C), 1)
    causal = ri >= ci
    strict = ri > ci
    tri = causal.astype(F32)
    gc = _dot(tri, g_all, HI)
    gct = _dot(g_all.T, (ri <= ci).astype(F32), HI)
    nw = nw_ref[...]

    for h in range(GDN_HEADS):
        sl = slice(h * GDN_D, (h + 1) * GDN_D)
        qh = y[:, sl]
        kh = y[:, GDN_QK + h * GDN_D:GDN_QK + (h + 1) * GDN_D]
        vh = jnp.where(valid, y[:, 2 * GDN_QK + h * GDN_D:2 * GDN_QK + (h + 1) * GDN_D], 0.0)
        qh = jnp.where(valid, qh * lax.rsqrt(jnp.sum(qh * qh, axis=-1, keepdims=True) + L2_EPS) * GDN_D ** -0.5, 0.0)
        kh = jnp.where(valid, kh * lax.rsqrt(jnp.sum(kh * kh, axis=-1, keepdims=True) + L2_EPS), 0.0)
        bcol = beta_all[:, LANE_B + h:LANE_B + h + 1]
        gcol = gc[:, LANE_A + h:LANE_A + h + 1]
        grow = gct[LANE_A + h:LANE_A + h + 1, :]
        glast = gc[C - 1:C, LANE_A + h:LANE_A + h + 1]
        diff = gcol - grow
        decay = jnp.where(causal, jnp.exp(jnp.where(causal, diff, 0.0)), 0.0)
        kb = kh * bcol
        a_mat = jnp.where(strict, _dot_nt(_bf(kb), _bf(kh)) * decay, 0.0)
        t_mat = _tri_inv(-a_mat, C)
        eg = jnp.exp(gcol)
        rhs = jnp.concatenate([vh * bcol, kb * eg], axis=1)
        sol = _dot(t_mat, rhs, HI)
        u = sol[:, :GDN_D]
        w = sol[:, GDN_D:]
        qk = _dot_nt(_bf(qh), _bf(kh)) * decay
        q_dec = qh * eg
        k_dec = kh * jnp.exp(glast - gcol)
        s_h = S[h]
        s_bf = _bf(s_h)
        v_new = u - _dot(_bf(w), s_bf)
        o = _dot(_bf(q_dec), s_bf) + _dot(_bf(qk), _bf(v_new))
        S[h] = s_h * jnp.exp(glast) + _dot_tn(_bf(k_dec), _bf(v_new))
        on = o * lax.rsqrt(jnp.mean(o * o, axis=-1, keepdims=True) + NORM_EPS) * nw
        zh = z_ref[:, sl]
        o_ref[:, sl] = (on * (zh * _sigmoid(zh))).astype(o_ref.dtype)

    @pl.when(c == NC - 1)
    def _():
        sout_ref[0] = S[...]


def gdn(qkv_arr, z_arr, ps_arr, z_col, conv_w, alog_pad, dt_pad, norm_w, s0, *, B, NC, C, first_valid, name):
    rows = B * NC * C
    return pl.pallas_call(
        functools.partial(_gdn_kernel, C, NC, first_valid),
        grid=(B, NC),
        in_specs=[
            pl.BlockSpec((C, GDN_QKV), lambda b, c: (b * NC + c, 0)),
            pl.BlockSpec((C, GDN_QK), lambda b, c: (b * NC + c, z_col)),
            pl.BlockSpec((C, LANES), lambda b, c: (b * NC + c, 0)),
            pl.BlockSpec((4, GDN_QKV), lambda b, c: (0, 0)),
            pl.BlockSpec((1, LANES), lambda b, c: (0, 0)),
            pl.BlockSpec((1, LANES), lambda b, c: (0, 0)),
            pl.BlockSpec((1, GDN_D), lambda b, c: (0, 0)),
            pl.BlockSpec((1, GDN_HEADS, GDN_D, GDN_D), lambda b, c: (b, 0, 0, 0)),
        ],
        out_specs=[
            pl.BlockSpec((C, GDN_QK), lambda b, c: (b * NC + c, 0)),
            pl.BlockSpec((1, GDN_HEADS, GDN_D, GDN_D), lambda b, c: (b, 0, 0, 0)),
        ],
        out_shape=[
            jax.ShapeDtypeStruct((rows, GDN_QK), BF16),
            jax.ShapeDtypeStruct((B, GDN_HEADS, GDN_D, GDN_D), F32),
        ],
        scratch_shapes=[pltpu.VMEM((GDN_HEADS, GDN_D, GDN_D), F32), pltpu.VMEM((C + SUBLANES, GDN_QKV), F32)],
        compiler_params=_cparams(("parallel", "arbitrary")),
        name=name,
    )(qkv_arr, z_arr, ps_arr, conv_w, alog_pad, dt_pad, norm_w, s0)


def _fox_prep_kernel(tb, first_valid, ps_ref, bf_ref, lf_ref, cq_ref, ck_ref, carry):
    i = pl.program_id(0)

    @pl.when(i == 0)
    def _():
        carry[...] = jnp.zeros(carry.shape, F32)

    x = ps_ref[...] + bf_ref[...]
    rows = i * tb + _iota2((tb, 1), 0)
    lf = jnp.where(rows >= first_valid, jnp.minimum(x, 0.0) - jnp.log(1.0 + jnp.exp(-jnp.abs(x))), 0.0)
    tri = (_iota2((tb, tb), 0) >= _iota2((tb, tb), 1)).astype(F32)
    c = _dot(tri, lf, HI) + carry[0:1, :]
    carry[...] = jnp.broadcast_to(c[tb - 1:tb, :], carry.shape)
    lf_ref[...] = lf
    for h in range(FOX_HEADS):
        cq_ref[h] = jnp.broadcast_to(c[:, LANE_F + h:LANE_F + h + 1], (tb, LANES))
    ck_ref[...] = c.T[LANE_F:LANE_F + FOX_HEADS, :]


def fox_prep(ps, bf_pad, first_valid):
    r = ps.shape[0]
    tb = _pick(r, (ROW_TILE, 512, 256, 128))
    return pl.pallas_call(
        functools.partial(_fox_prep_kernel, tb, first_valid),
        grid=(r // tb,),
        in_specs=[pl.BlockSpec((tb, LANES), lambda i: (i, 0)), pl.BlockSpec((1, LANES), lambda i: (0, 0))],
        out_specs=[
            pl.BlockSpec((tb, LANES), lambda i: (i, 0)),
            pl.BlockSpec((FOX_HEADS, tb, LANES), lambda i: (0, i, 0)),
            pl.BlockSpec((FOX_HEADS, tb), lambda i: (0, i)),
        ],
        out_shape=[
            jax.ShapeDtypeStruct((r, LANES), F32),
            jax.ShapeDtypeStruct((FOX_HEADS, r, LANES), F32),
            jax.ShapeDtypeStruct((FOX_HEADS, r), F32),
        ],
        scratch_shapes=[pltpu.VMEM((SUBLANES, LANES), F32)],
        compiler_params=_cparams(("arbitrary",)),
        name="fox_prep",
    )(ps, bf_pad)


def _fox_flash_kernel(tq, nk, first_valid, q_ref, k_ref, v_ref, cq_ref, ck_ref, fz_ref, o_ref, m_s, l_s, acc_s):
    qi = pl.program_id(1)
    ki = pl.program_id(2)

    @pl.when(ki == 0)
    def _():
        m_s[...] = jnp.full(m_s.shape, NEG_INF, F32)
        l_s[...] = jnp.zeros(l_s.shape, F32)
        acc_s[...] = jnp.zeros(acc_s.shape, F32)

    @pl.when(ki <= qi)
    def _():
        s = _dot_nt(_bf(q_ref[...]), _bf(k_ref[...])) * FOX_DH ** -0.5
        t = s - ck_ref[0]
        kpos = ki * tq + _iota2((tq, tq), 1)
        qpos = qi * tq + _iota2((tq, tq), 0)
        t = jnp.where((kpos <= qpos) & (kpos >= first_valid), t, NEG_INF)
        cq = cq_ref[0][:, 0:1]
        m_prev = m_s[:, 0:1]
        m_new = jnp.maximum(m_prev, jnp.max(t, axis=-1, keepdims=True) + cq)
        p = jnp.exp(t + (cq - m_new))
        alpha = jnp.exp(m_prev - m_new)
        l_s[...] = alpha * l_s[...] + jnp.sum(p, axis=-1, keepdims=True)
        acc_s[...] = alpha * acc_s[...] + _dot(_bf(p), _bf(v_ref[...]))
        m_s[...] = jnp.broadcast_to(m_new, m_s.shape)

    @pl.when(ki == nk - 1)
    def _():
        rows = qi * tq + _iota2((tq, 1), 0)
        o = acc_s[...] / l_s[...] * _sigmoid(fz_ref[...])
        o_ref[...] = jnp.where(rows >= first_valid, o, 0.0).astype(o_ref.dtype)


def fox_prompt(p, cq, ck, tp, first_valid):
    tq = _pick(tp, (ROW_TILE, 512, 384, 256, 128))
    nq = tp // tq
    cb = lambda col, h: col // FOX_DH + h
    return pl.pallas_call(
        functools.partial(_fox_flash_kernel, tq, nq, first_valid),
        grid=(FOX_HEADS, nq, nq),
        in_specs=[
            pl.BlockSpec((tq, FOX_DH), lambda h, qi, ki: (qi, cb(COL_FQ, h))),
            pl.BlockSpec((tq, FOX_DH), lambda h, qi, ki: (jnp.minimum(ki, qi), cb(COL_FK, h))),
            pl.BlockSpec((tq, FOX_DH), lambda h, qi, ki: (jnp.minimum(ki, qi), cb(COL_FV, h))),
            pl.BlockSpec((1, tq, LANES), lambda h, qi, ki: (h, qi, 0)),
            pl.BlockSpec((1, 1, tq), lambda h, qi, ki: (h, 0, jnp.minimum(ki, qi))),
            pl.BlockSpec((tq, FOX_DH), lambda h, qi, ki: (qi, cb(COL_FZ, h))),
        ],
        out_specs=pl.BlockSpec((tq, FOX_DH), lambda h, qi, ki: (qi, h)),
        out_shape=jax.ShapeDtypeStruct((tp, FOX_W), BF16),
        scratch_shapes=[pltpu.VMEM((tq, LANES), F32), pltpu.VMEM((tq, LANES), F32), pltpu.VMEM((tq, FOX_DH), F32)],
        compiler_params=_cparams(("parallel", "parallel", "arbitrary")),
        name="fox_prompt",
    )(p, p, p, cq, ck.reshape(FOX_HEADS, 1, -1), p)


FOX_ROWS = 32


def _fox_past_c_kernel(npages, pt_ref, lf_ref, ct_ref, tot_ref, carry):
    j = pl.program_id(1)

    @pl.when(j == 0)
    def _():
        carry[...] = jnp.zeros(carry.shape, F32)

    lf = lf_ref[0]
    page = lf.shape[0]
    rep = FOX_ROWS // FOX_HEADS
    expand = (_iota2((FOX_HEADS, FOX_ROWS), 0) == _div(_iota2((FOX_HEADS, FOX_ROWS), 1), rep)).astype(F32)
    lf32 = _dot(lf, expand, HI)
    upper = (_iota2((page, page), 0) <= _iota2((page, page), 1)).astype(F32)
    ct = _dot_tn(lf32, upper, HI) + carry[:, 0:1]
    ct_ref[0] = ct
    carry[...] = jnp.broadcast_to(ct[:, page - 1:page], carry.shape)

    @pl.when(j == npages - 1)
    def _():
        tot_ref[0] = carry[...]


def fox_past_c(page_table_flat, cache_logf, nb, npages):
    page = cache_logf.shape[1]
    return pl.pallas_call(
        functools.partial(_fox_past_c_kernel, npages),
        grid_spec=pltpu.PrefetchScalarGridSpec(
            num_scalar_prefetch=1,
            grid=(nb, npages),
            in_specs=[pl.BlockSpec((1, page, FOX_HEADS), lambda b, j, pt: (pt[b * npages + j], 0, 0))],
            out_specs=[
                pl.BlockSpec((1, FOX_ROWS, page), lambda b, j, pt: (b, 0, j)),
                pl.BlockSpec((1, FOX_ROWS, LANES), lambda b, j, pt: (b, 0, 0)),
            ],
            scratch_shapes=[pltpu.VMEM((FOX_ROWS, LANES), F32)],
        ),
        out_shape=[
            jax.ShapeDtypeStruct((nb, FOX_ROWS, npages * page), F32),
            jax.ShapeDtypeStruct((nb, FOX_ROWS, LANES), F32),
        ],
        compiler_params=_cparams(("parallel", "arbitrary")),
        name="fox_past_c",
    )(page_table_flat, cache_logf)


def _fox_sample_kernel(npages, nq, pt_ref, qt_ref, k_ref, v_ref, ct_ref, tot_ref, lfn_ref, kn_ref, vn_ref, fz_ref,
                       o_ref, qblk, cq_s, cn_s, m_s, l_s, acc_s):
    j = pl.program_id(1)
    rows = FOX_ROWS
    rep = rows // FOX_HEADS

    @pl.when(j == 0)
    def _():
        head_of_row = _div(_iota2((rows, FOX_W), 0), rep)
        head_of_lane = _div(_iota2((rows, FOX_W), 1), FOX_DH)
        qblk[...] = jnp.where(head_of_row == head_of_lane, qt_ref[0], 0.0).astype(BF16)
        nn = lfn_ref.shape[2]
        upper = (_iota2((nn, nn), 0) <= _iota2((nn, nn), 1)).astype(F32)
        cn = tot_ref[0][:, 0:1] + _dot(lfn_ref[0], upper, HI)
        cn_s[...] = cn
        sel = _iota2((rows, nn), 1) == _mod(_iota2((rows, nn), 0), rep)
        cq = jnp.sum(jnp.where(sel, cn, 0.0), axis=-1, keepdims=True)
        cq_s[...] = jnp.broadcast_to(cq, cq_s.shape)
        m_s[...] = jnp.full(m_s.shape, NEG_INF, F32)
        l_s[...] = jnp.zeros(l_s.shape, F32)
        acc_s[...] = jnp.zeros(acc_s.shape, F32)

    def update(t, vals):
        cq = cq_s[:, 0:1]
        m_prev = m_s[:, 0:1]
        m_new = jnp.maximum(m_prev, jnp.max(t, axis=-1, keepdims=True) + cq)
        p = jnp.exp(t + (cq - m_new))
        alpha = jnp.exp(m_prev - m_new)
        l_s[...] = alpha * l_s[...] + jnp.sum(p, axis=-1, keepdims=True)
        acc_s[...] = alpha * acc_s[...] + _dot(_bf(p), _bf(vals))
        m_s[...] = jnp.broadcast_to(m_new, m_s.shape)

    s = _dot_nt(qblk[...], _bf(k_ref[0])) * FOX_DH ** -0.5
    update(s - ct_ref[0], v_ref[0])

    @pl.when(j == npages - 1)
    def _():
        nn = kn_ref.shape[1]
        sn = _dot_nt(qblk[...], _bf(kn_ref[0])) * FOX_DH ** -0.5
        key = _iota2((rows, nn), 1)
        ok = (key <= _mod(_iota2((rows, nn), 0), rep)) & (key < nq)
        update(jnp.where(ok, sn - cn_s[...], NEG_INF), vn_ref[0])
        o_ref[0] = acc_s[...] / l_s[:, 0:1] * _sigmoid(fz_ref[0])


def fox_sample(page_table_flat, q_t, cache_k, cache_v, ct, tot, lfn, k_new, v_new, fz_t, nb, npages, nq):
    page = cache_k.shape[1]
    nn = k_new.shape[1]
    return pl.pallas_call(
        functools.partial(_fox_sample_kernel, npages, nq),
        grid_spec=pltpu.PrefetchScalarGridSpec(
            num_scalar_prefetch=1,
            grid=(nb, npages),
            in_specs=[
                pl.BlockSpec((1, FOX_ROWS, FOX_W), lambda b, j, pt: (b, 0, 0)),
                pl.BlockSpec((1, page, FOX_W), lambda b, j, pt: (pt[b * npages + j], 0, 0)),
                pl.BlockSpec((1, page, FOX_W), lambda b, j, pt: (pt[b * npages + j], 0, 0)),
                pl.BlockSpec((1, FOX_ROWS, page), lambda b, j, pt: (b, 0, j)),
                pl.BlockSpec((1, FOX_ROWS, LANES), lambda b, j, pt: (b, 0, 0)),
                pl.BlockSpec((1, FOX_ROWS, nn), lambda b, j, pt: (b, 0, 0)),
                pl.BlockSpec((1, nn, FOX_W), lambda b, j, pt: (b, 0, 0)),
                pl.BlockSpec((1, nn, FOX_W), lambda b, j, pt: (b, 0, 0)),
                pl.BlockSpec((1, FOX_ROWS, FOX_W), lambda b, j, pt: (b, 0, 0)),
            ],
            out_specs=pl.BlockSpec((1, FOX_ROWS, FOX_W), lambda b, j, pt: (b, 0, 0)),
            scratch_shapes=[
                pltpu.VMEM((FOX_ROWS, FOX_W), BF16),
                pltpu.VMEM((FOX_ROWS, LANES), F32),
                pltpu.VMEM((FOX_ROWS, nn), F32),
                pltpu.VMEM((FOX_ROWS, LANES), F32),
                pltpu.VMEM((FOX_ROWS, LANES), F32),
                pltpu.VMEM((FOX_ROWS, FOX_W), F32),
            ],
        ),
        out_shape=jax.ShapeDtypeStruct((nb, FOX_ROWS, FOX_W), F32),
        compiler_params=_cparams(("parallel", "arbitrary")),
        name="fox_sample",
    )(page_table_flat, q_t, cache_k, cache_v, ct, tot, lfn, k_new, v_new, fz_t)


def _rwkv_mix_kernel(h_ref, p_ref, mu_ref, *o_refs):
    h = h_ref[...]
    xx = p_ref[...] - h
    mu = mu_ref[...]
    for i, o_ref in enumerate(o_refs):
        o_ref[...] = (h + xx * mu[i:i + 1]).astype(o_ref.dtype)


def rwkv_mix(h, prev, mu):
    m, d = h.shape
    tm = _pick(m, (320, 256, 128))
    spec = pl.BlockSpec((tm, d), lambda i: (i, 0))
    return pl.pallas_call(
        _rwkv_mix_kernel,
        grid=(m // tm,),
        in_specs=[spec, spec, pl.BlockSpec((6, d), lambda i: (0, 0))],
        out_specs=[spec] * 6,
        out_shape=[jax.ShapeDtypeStruct((m, d), BF16)] * 6,
        compiler_params=_cparams(("parallel",)),
        name="rwkv_mix",
    )(h, prev, mu)


def _rwkv_kernel(C, NC, first_valid, r_ref, k_ref, v_ref, wl_ref, al_ref, g_ref, prm_ref, s0_ref,
                 o_ref, sout_ref, H):
    c = pl.program_id(2)
    HD = RWKV_HEAD
    lane = _iota2((1, LANES), 1)
    m0 = lane < HD
    bones = (_div(_iota2((LANES, LANES), 0), HD) == _div(_iota2((LANES, LANES), 1), HD)).astype(F32)

    @pl.when(c == 0)
    def _():
        zero = jnp.zeros((HD, HD), F32)
        top = jnp.concatenate([s0_ref[0, 0], zero], axis=1)
        bot = jnp.concatenate([zero, s0_ref[0, 1]], axis=1)
        H[...] = jnp.concatenate([top, bot], axis=0).T

    prm = prm_ref[0]
    w0, a0, k_k, k_a, r_k, ln_w, ln_b = (prm[i:i + 1] for i in range(7))
    rows = c * C + _iota2((C, 1), 0)
    valid = rows >= first_valid
    r = r_ref[...]
    k = k_ref[...]
    v = v_ref[...]
    wlog = -_softplus(-(w0 + wl_ref[...])) - 0.5
    lw = jnp.where(valid, -jnp.exp(wlog), 0.0)
    a = _sigmoid(a0 + al_ref[...])
    kkr = k * k_k
    kk = kkr * lax.rsqrt(_dot(kkr * kkr, bones, HI) + L2_EPS)
    k2 = k * (1.0 + (a - 1.0) * k_a)
    bonus = _dot(r * k2 * r_k, bones, HI) * v
    rm = jnp.where(valid, r, 0.0)
    k2 = jnp.where(valid, k2, 0.0)
    vm = jnp.where(valid, v, 0.0)
    av = jnp.where(valid, -kk, 0.0)
    bv = jnp.where(valid, kk * a, 0.0)

    ri = _iota2((C, C), 0)
    ci = _iota2((C, C), 1)
    cum = _dot((ri >= ci).astype(F32), lw, HI)
    cum_last = cum[C - 1:C, :]
    gam = jnp.exp(cum)
    inv = jnp.exp(-cum)
    rt = rm * gam
    at = av * jnp.exp(cum - lw)
    bt = bv * inv
    kt = k2 * inv

    def split(x):
        return jnp.concatenate([jnp.where(m0, x, 0.0), jnp.where(m0, 0.0, x)], axis=0)

    gram = _dot_nt(jnp.concatenate([split(at), split(rt)], axis=0), jnp.concatenate([bt, kt], axis=0), HI)
    r2 = _iota2((2 * C, 2 * C), 0)
    c2 = _iota2((2 * C, 2 * C), 1)
    same = _div(r2, C) == _div(c2, C)
    bd_strict = same & (r2 > c2)
    bd_incl = same & (r2 >= c2)

    def bd(block, mask):
        return jnp.where(mask, jnp.concatenate([block, block], axis=1), 0.0)

    a_ab = bd(gram[0:2 * C, 0:C], bd_strict)
    a_ak = bd(gram[0:2 * C, C:2 * C], bd_strict)
    r_b = bd(gram[2 * C:4 * C, 0:C], bd_incl)
    r_k2 = bd(gram[2 * C:4 * C, C:2 * C], bd_incl)
    t_bd = _tri_inv(a_ab, C)

    def halves(x):
        return x[0:C] + x[C:2 * C]

    h_bd = H[...]
    ah_rh = _dot(jnp.concatenate([at, rt], axis=0), h_bd, HI)
    vs = split(vm)
    x_mat = ah_rh[0:C] + halves(_dot(a_ak, vs, HI))
    u = halves(_dot(t_bd, split(x_mat), HI))
    y = ah_rh[C:2 * C] + halves(_dot(jnp.concatenate([r_b, r_k2], axis=1), jnp.concatenate([split(u), vs], axis=0), HI))
    to_end = jnp.exp(cum_last - cum)
    upd = _dot_tn(jnp.concatenate([bv * to_end, k2 * to_end], axis=0), jnp.concatenate([u, vm], axis=0), HI)
    gl_col = jnp.broadcast_to(jnp.exp(cum_last), (SUBLANES, LANES)).T[:, 0:1]
    H[...] = gl_col * h_bd + bones * upd

    mean = _dot(y, bones, HI) * (1.0 / HD)
    dlt = y - mean
    var = _dot(dlt * dlt, bones, HI) * (1.0 / HD)
    yn = dlt * lax.rsqrt(var + GN_EPS) * ln_w + ln_b
    o_ref[...] = ((yn + bonus) * g_ref[...]).astype(o_ref.dtype)

    @pl.when(c == NC - 1)
    def _():
        ht = H[...].T
        sout_ref[0, 0] = ht[0:HD, 0:HD]
        sout_ref[0, 1] = ht[HD:2 * HD, HD:2 * HD]


def rwkv(r, k, v, wl, al, g, prm, s0, *, B, NC, C, first_valid, name):
    rows = B * NC * C
    spec = pl.BlockSpec((C, LANES), lambda b, hp, c: (b * NC + c, hp))
    st_spec = pl.BlockSpec((1, 2, RWKV_HEAD, RWKV_HEAD), lambda b, hp, c: (b, hp, 0, 0))
    return pl.pallas_call(
        functools.partial(_rwkv_kernel, C, NC, first_valid),
        grid=(B, RWKV_PAIRS, NC),
        in_specs=[spec] * 6 + [pl.BlockSpec((1, SUBLANES, LANES), lambda b, hp, c: (hp, 0, 0)), st_spec],
        out_specs=[spec, st_spec],
        out_shape=[
            jax.ShapeDtypeStruct((rows, D_MODEL), BF16),
            jax.ShapeDtypeStruct((B, 2 * RWKV_PAIRS, RWKV_HEAD, RWKV_HEAD), F32),
        ],
        scratch_shapes=[pltpu.VMEM((LANES, LANES), F32)],
        compiler_params=_cparams(("parallel", "parallel", "arbitrary")),
        name=name,
    )(r, k, v, wl, al, g, prm, s0)


def _pad_lanes(vec, offset):
    out = jnp.zeros((1, LANES), F32)
    return lax.dynamic_update_slice(out, vec.reshape(1, -1).astype(F32), (0, offset))


def _sample_rows(arr, row0, nb, nq, front):
    cols = arr.shape[1]
    s = arr[row0:row0 + nb * nq].reshape(nb, nq, cols)
    s = jnp.pad(s, ((0, 0), (front, 0), (0, 0)))
    return s.reshape(nb * (front + nq), cols)


def kernel(x_prompt, x_sample, cache_fox_k, cache_fox_v, cache_fox_logf, state_gdn_conv, state_gdn_S,
           state_rwkv_shift, state_rwkv_S, page_table, meta_tokens, ln_mix, ln_mlp, ln_final,
           w_in0, gdn_conv_w, gdn_A_log, gdn_dt_bias, gdn_norm_w, fox_b_f, w_out0,
           rwkv_mu, rwkv_w0, rwkv_w1, rwkv_w2, rwkv_a0, rwkv_a1, rwkv_a2, rwkv_g1, rwkv_g2,
           rwkv_k_k, rwkv_k_a, rwkv_r_k, rwkv_w_r, rwkv_w_k, rwkv_w_v, rwkv_w_o, rwkv_ln_w, rwkv_ln_b,
           w_up, w_down):
    D = D_MODEL
    assert x_prompt.shape[0] == 1 and x_prompt.shape[2] == D
    seq = x_prompt.shape[1]
    nb, nq = x_sample.shape[0], x_sample.shape[1]
    assert nq * FOX_HEADS == FOX_ROWS
    npages = page_table.shape[1]
    page = cache_fox_k.shape[2]
    tprompt = N_META + seq
    pad = (-tprompt) % LANES
    tp = tprompt + pad
    ns = nb * nq
    R = -(-(tp + ns) // ROW_TILE) * ROW_TILE
    front = SUBLANES - nq
    CS = SUBLANES

    x0 = jnp.concatenate([jnp.zeros((pad, D), F32), meta_tokens.astype(F32), x_prompt[0],
                          x_sample.reshape(ns, D), jnp.zeros((R - tp - ns, D), F32)], axis=0)

    w_in = w_in0[0]
    o_z = GDN_QKV
    o_a = o_z + GDN_QK
    o_b = o_a + GDN_HEADS
    o_fq = o_b + GDN_HEADS
    o_fk = o_fq + FOX_W
    o_fv = o_fk + FOX_W
    o_ff = o_fv + FOX_W
    o_fz = o_ff + FOX_HEADS
    w_big = jnp.concatenate([w_in[:, :o_a], w_in[:, o_fq:o_ff], w_in[:, o_fz:]], axis=1).astype(BF16)
    w_small = jnp.concatenate([w_in[:, o_a:o_fq], w_in[:, o_ff:o_fz],
                               jnp.zeros((D, LANES - 3 * GDN_HEADS), F32)], axis=1).astype(BF16)

    h0 = rmsnorm(x0, ln_mix[0], BF16)
    p = matmul(h0, w_big, name="in_proj")
    ps = matmul(h0, w_small, name="in_proj_small")

    alog_pad = _pad_lanes(gdn_A_log[0], LANE_A)
    dt_pad = _pad_lanes(gdn_dt_bias[0], LANE_A)
    bf_pad = _pad_lanes(fox_b_f[0], LANE_F)
    conv_w = gdn_conv_w[0]
    norm_w = gdn_norm_w[0].reshape(1, GDN_D)

    GC = 64
    o_gdn_p, s_gdn_p = gdn(p, p, ps, COL_Z // GDN_QK, conv_w, alog_pad, dt_pad, norm_w,
                           jnp.zeros((1, GDN_HEADS, GDN_D, GDN_D), F32),
                           B=1, NC=tp // GC, C=GC, first_valid=pad, name="gdn_prompt")
    qkv_s = p[tp:tp + ns, :GDN_QKV].reshape(nb, nq, GDN_QKV)
    qkv_ext = jnp.concatenate([jnp.zeros((nb, front - 3, GDN_QKV), F32), state_gdn_conv[0], qkv_s], axis=1)
    z_ext = _sample_rows(p[:, COL_Z:COL_Z + GDN_QK], tp, nb, nq, front)
    ps_ext = _sample_rows(ps, tp, nb, nq, front)
    o_gdn_s, s_gdn_s = gdn(qkv_ext.reshape(nb * CS, GDN_QKV), z_ext, ps_ext, 0, conv_w, alog_pad, dt_pad, norm_w,
                           state_gdn_S[0], B=nb, NC=1, C=CS, first_valid=front, name="gdn_sample")
    o_gdn_s = o_gdn_s.reshape(nb, CS, GDN_QK)[:, front:].reshape(ns, GDN_QK)

    lf, cq, ck = fox_prep(ps, bf_pad, pad)
    o_fox_p = fox_prompt(p, cq, ck, tp, pad)
    pt_flat = page_table.reshape(-1).astype(jnp.int32)
    lf_s = lf[tp:tp + ns, LANE_F:LANE_F + FOX_HEADS].reshape(nb, nq, FOX_HEADS)
    lfn = jnp.repeat(jnp.swapaxes(lf_s, 1, 2), nq, axis=1)
    lfn = jnp.pad(lfn, ((0, 0), (0, 0), (0, SUBLANES - nq)))
    ct, tot = fox_past_c(pt_flat, cache_fox_logf[0], nb, npages)
    q_t = jnp.tile(p[tp:tp + ns, COL_FQ:COL_FQ + FOX_W].reshape(nb, nq, FOX_W), (1, FOX_HEADS, 1))
    fz_t = jnp.tile(p[tp:tp + ns, COL_FZ:COL_FZ + FOX_W].reshape(nb, nq, FOX_W), (1, FOX_HEADS, 1))
    k_new = jnp.pad(p[tp:tp + ns, COL_FK:COL_FK + FOX_W].reshape(nb, nq, FOX_W), ((0, 0), (0, SUBLANES - nq), (0, 0)))
    v_new = jnp.pad(p[tp:tp + ns, COL_FV:COL_FV + FOX_W].reshape(nb, nq, FOX_W), ((0, 0), (0, SUBLANES - nq), (0, 0)))
    o_fox_full = fox_sample(pt_flat, q_t, cache_fox_k[0].reshape(-1, page, FOX_W), cache_fox_v[0].reshape(-1, page, FOX_W),
                            ct, tot, lfn, k_new, v_new, fz_t, nb, npages, nq)
    hh = jnp.arange(FOX_HEADS)
    o_fox_s = o_fox_full.reshape(nb, FOX_HEADS, nq, FOX_HEADS, FOX_DH)[:, hh, :, hh, :]
    o_fox_s = jnp.transpose(o_fox_s, (1, 2, 0, 3)).reshape(ns, FOX_W).astype(BF16)

    mix = jnp.concatenate([
        jnp.concatenate([o_gdn_p, o_fox_p], axis=1),
        jnp.concatenate([o_gdn_s, o_fox_s], axis=1),
        jnp.zeros((R - tp - ns, 2 * GDN_QK), BF16)], axis=0)
    x1 = matmul(mix, w_out0[0].astype(BF16), res=x0, name="out_proj")
    u0 = matmul(rmsnorm(x1, ln_mlp[0], BF16), w_up[0].astype(BF16), act="relu2", out_dtype=BF16, name="mlp_up0")
    x2 = matmul(u0, w_down[0].astype(BF16), res=x1, name="mlp_down0")

    h1 = rmsnorm(x2, ln_mix[1], F32)
    h1_s = h1[tp:tp + ns].reshape(nb, nq, D)
    prev_s = jnp.concatenate([state_rwkv_shift[0][:, None, :].astype(F32), h1_s[:, :-1]], axis=1).reshape(ns, D)
    prev = jnp.concatenate([jnp.zeros((1, D), F32), h1[:tp - 1], prev_s, jnp.zeros((R - tp - ns, D), F32)], axis=0)
    xr, xw, xk, xv, xa, xg = rwkv_mix(h1, prev, rwkv_mu[0])

    def pad_cols(w):
        return jnp.pad(w, ((0, 0), (0, LANES - w.shape[1]))).astype(BF16)

    def pad_rows(w):
        return jnp.pad(w, ((0, LANES - w.shape[0]), (0, 0))).astype(BF16)

    r_ = matmul(xr, rwkv_w_r[0].astype(BF16), name="rwkv_r")
    k_ = matmul(xk, rwkv_w_k[0].astype(BF16), name="rwkv_k")
    v_ = matmul(xv, rwkv_w_v[0].astype(BF16), name="rwkv_v")
    wl = matmul(matmul(xw, pad_cols(rwkv_w1[0]), act="tanh", out_dtype=BF16, name="rwkv_w1"), pad_rows(rwkv_w2[0]), name="rwkv_w2")
    al = matmul(matmul(xa, pad_cols(rwkv_a1[0]), out_dtype=BF16, name="rwkv_a1"), pad_rows(rwkv_a2[0]), name="rwkv_a2")
    g_ = matmul(matmul(xg, rwkv_g1[0].astype(BF16), act="sigmoid", out_dtype=BF16, name="rwkv_g1"), rwkv_g2[0].astype(BF16), name="rwkv_g2")

    prm = jnp.stack([rwkv_w0[0], rwkv_a0[0], rwkv_k_k[0], rwkv_k_a[0], rwkv_r_k[0].reshape(D), rwkv_ln_w[0], rwkv_ln_b[0],
                     jnp.zeros((D,), F32)], axis=0).astype(F32)
    prm = jnp.transpose(prm.reshape(SUBLANES, RWKV_PAIRS, LANES), (1, 0, 2))
    RC = 64
    o_rw_p, s_rw_p = rwkv(r_, k_, v_, wl, al, g_, prm, jnp.zeros((1, 2 * RWKV_PAIRS, RWKV_HEAD, RWKV_HEAD), F32),
                          B=1, NC=tp // RC, C=RC, first_valid=pad, name="rwkv_prompt")
    sx = [_sample_rows(t, tp, nb, nq, front) for t in (r_, k_, v_, wl, al, g_)]
    o_rw_s, s_rw_s = rwkv(*sx, prm, state_rwkv_S[0], B=nb, NC=1, C=CS, first_valid=front, name="rwkv_sample")
    o_rw_s = o_rw_s.reshape(nb, CS, D)[:, front:].reshape(ns, D)
    o_rw = jnp.concatenate([o_rw_p, o_rw_s, jnp.zeros((R - tp - ns, D), BF16)], axis=0)
    x3 = matmul(o_rw, rwkv_w_o[0].astype(BF16), res=x2, name="rwkv_o")
    u1 = matmul(rmsnorm(x3, ln_mlp[1], BF16), w_up[1].astype(BF16), act="relu2", out_dtype=BF16, name="mlp_up1")
    x4 = matmul(u1, w_down[1].astype(BF16), res=x3, name="mlp_down1")
    y = rmsnorm(x4, ln_final, F32)

    r0 = pad
    y_prompt = y[r0 + N_META:tp].reshape(1, seq, D)
    y_sample = y[tp:tp + ns].reshape(nb, nq, D)

    def kv_rows(col):
        blk = p[:, col:col + FOX_W]
        return (blk[r0:tp].reshape(1, 1, tprompt, FOX_HEADS, FOX_DH),
                blk[tp:tp + ns].reshape(1, nb, nq, FOX_HEADS, FOX_DH))

    fk_p, fk_s = kv_rows(COL_FK)
    fv_p, fv_s = kv_rows(COL_FV)
    lf8 = lf[:, LANE_F:LANE_F + FOX_HEADS]
    lf_p = lf8[r0:tp].reshape(1, 1, tprompt, FOX_HEADS)
    lf_sm = lf8[tp:tp + ns].reshape(1, nb, nq, FOX_HEADS)
    cb_p = p[tp - 3:tp, :GDN_QKV].reshape(1, 1, 3, GDN_QKV)
    cb_s = qkv_ext[:, CS - 3:][None]
    gs_p = s_gdn_p[None]
    gs_s = s_gdn_s[None]
    sh_p = h1[tp - 1].reshape(1, 1, D)
    sh_s = h1_s[:, nq - 1][None]
    rs_p = s_rw_p[None]
    rs_s = s_rw_s[None]
    return (y_prompt, y_sample, fk_p, fk_s, fv_p, fv_s, lf_p, lf_sm, cb_p, cb_s, gs_p, gs_s, sh_p, sh_s, rs_p, rs_s)
```

```python
import functools

import jax
import jax.numpy as jnp
from jax import lax
from jax.experimental import pallas as pl
from jax.experimental.pallas import tpu as pltpu

F32 = jnp.float32
BF16 = jnp.bfloat16

D_MODEL = 2048
N_META = 16
GDN_HEADS = 8
GDN_D = 128
GDN_QK = GDN_HEADS * GDN_D
GDN_QKV = 3 * GDN_QK
FOX_HEADS = 8
FOX_DH = 128
FOX_W = FOX_HEADS * FOX_DH
RWKV_HEAD = 64
RWKV_PAIRS = D_MODEL // (2 * RWKV_HEAD)
NORM_EPS = 1e-6
L2_EPS = 1e-6
GN_EPS = 64e-5
NEG_INF = -1e30

LANES = 128
SUBLANES = 8
ROW_TILE = 640
VMEM_LIMIT = 48 * 1024 * 1024
FOX_GROUP = 8
RWKV_PREP_PAIRS = 8

COL_Z = GDN_QKV
COL_FQ = COL_Z + GDN_QK
COL_FK = COL_FQ + FOX_W
COL_FV = COL_FK + FOX_W
COL_FZ = COL_FV + FOX_W
LANE_A = 0
LANE_B = 8
LANE_F = 16
PRM_W0, PRM_A0, PRM_KK, PRM_KA, PRM_RK, PRM_LNW, PRM_LNB = range(7)


def _pick(n, cands):
    for c in cands:
        if n % c == 0:
            return c
    raise ValueError(f"no tile for {n}")


def _cparams(sem):
    return pltpu.CompilerParams(dimension_semantics=sem, vmem_limit_bytes=VMEM_LIMIT)


def _dot(a, b):
    return jnp.dot(a, b, preferred_element_type=F32)


def _dot_nt(a, b):
    return lax.dot_general(a, b, (((1,), (1,)), ((), ())), preferred_element_type=F32)


def _dot_tn(a, b):
    return lax.dot_general(a, b, (((0,), (0,)), ((), ())), preferred_element_type=F32)


def _bf(x):
    return x.astype(BF16)


def _softplus(x):
    return jnp.maximum(x, 0.0) + jnp.log(1.0 + jnp.exp(-jnp.abs(x)))


def _sigmoid(x):
    return 1.0 / (1.0 + jnp.exp(-x))


def _iota2(shape, dim):
    return lax.broadcasted_iota(jnp.int32, shape, dim)


def _div(x, n):
    assert n & (n - 1) == 0
    return x >> (n.bit_length() - 1)


def _split2(x):
    hi = x.astype(BF16)
    return hi, (x - hi.astype(F32)).astype(BF16)


def _split3(x):
    p1 = x.astype(BF16)
    rem = x - p1.astype(F32)
    p2 = rem.astype(BF16)
    return p1, p2, (rem - p2.astype(F32)).astype(BF16)


def _dot3(a, b):
    m = a.shape[0]
    ah, al = _split2(a)
    bh, bl = _split2(b)
    top = _dot(jnp.concatenate([ah, al], axis=0), bh)
    return (top[:m] + top[m:]) + _dot(ah, bl)


def _dot_lhs01(a01, b):
    a = a01.astype(BF16)
    b1, b2, b3 = _split3(b)
    return _dot(a, b1) + (_dot(a, b2) + _dot(a, b3))


def _dot_rhs01(a, b01):
    b = b01.astype(BF16)
    a1, a2, a3 = _split3(a)
    m = a.shape[0]
    out = _dot(jnp.concatenate([a1, a2, a3], axis=0), b)
    return out[:m] + (out[m:2 * m] + out[2 * m:])


def _tri_inv_steps(n_mat, n):
    size = n_mat.shape[0]
    eye = (_iota2((size, size), 0) == _iota2((size, size), 1)).astype(F32)
    t = eye + n_mat
    if n <= 2:
        return t
    p = _dot(_bf(n_mat), _bf(n_mat))
    yield
    m = 2
    while m < n:
        p_bf = _bf(p)
        if 2 * m < n:
            both = _dot(_bf(jnp.concatenate([p, t], axis=0)), p_bf)
            p = both[:size]
            t = t + both[size:]
        else:
            t = t + _dot(_bf(t), p_bf)
        m *= 2
        yield
    return t


def _run_interleaved(gens):
    results = [None] * len(gens)
    live = list(enumerate(gens))
    while live:
        nxt = []
        for idx, gen in live:
            try:
                next(gen)
                nxt.append((idx, gen))
            except StopIteration as stop:
                results[idx] = stop.value
        live = nxt
    return results


def _rmsnorm_kernel(x_ref, w_ref, o_ref):
    x = x_ref[...]
    ms = jnp.mean(x * x, axis=-1, keepdims=True)
    o_ref[...] = (x * lax.rsqrt(ms + NORM_EPS) * w_ref[...]).astype(o_ref.dtype)


def rmsnorm(x, w, out_dtype):
    m, d = x.shape
    tm = _pick(m, (320, 256, 128))
    return pl.pallas_call(
        _rmsnorm_kernel,
        grid=(m // tm,),
        in_specs=[pl.BlockSpec((tm, d), lambda i: (i, 0)), pl.BlockSpec((1, d), lambda i: (0, 0))],
        out_specs=pl.BlockSpec((tm, d), lambda i: (i, 0)),
        out_shape=jax.ShapeDtypeStruct((m, d), out_dtype),
        compiler_params=_cparams(("parallel",)),
        name="rmsnorm",
    )(x, w.reshape(1, d))


def _mm_kernel(nk, act, has_res, *refs):
    if has_res:
        x_ref, w_ref, r_ref, o_ref = refs[:4]
        scr = refs[4:]
    else:
        x_ref, w_ref, o_ref = refs[:3]
        r_ref = None
        scr = refs[3:]

    def finish(acc):
        if act == "relu2":
            acc = jnp.square(jnp.maximum(acc, 0.0))
        elif act == "tanh":
            acc = jnp.tanh(acc)
        elif act == "sigmoid":
            acc = _sigmoid(acc)
        if has_res:
            acc = r_ref[...] + acc
        o_ref[...] = acc.astype(o_ref.dtype)

    if nk == 1:
        finish(_dot(x_ref[...], w_ref[...]))
    else:
        acc_ref = scr[0]
        k = pl.program_id(2)

        @pl.when(k == 0)
        def _():
            acc_ref[...] = jnp.zeros(acc_ref.shape, F32)

        acc_ref[...] += _dot(x_ref[...], w_ref[...])

        @pl.when(k == nk - 1)
        def _():
            finish(acc_ref[...])


def matmul(x, w, *, act=None, res=None, out_dtype=F32, name="matmul"):
    m, k = x.shape
    n = w.shape[1]
    tm = _pick(m, (ROW_TILE, 512, 256, 128))
    tn = _pick(n, (512, 256, 128))
    tk = k if k <= 2048 else 2048
    nk = k // tk
    in_specs = [pl.BlockSpec((tm, tk), lambda i, j, kk: (i, kk)), pl.BlockSpec((tk, tn), lambda i, j, kk: (kk, j))]
    args = [x, w]
    if res is not None:
        in_specs.append(pl.BlockSpec((tm, tn), lambda i, j, kk: (i, j)))
        args.append(res)
    return pl.pallas_call(
        functools.partial(_mm_kernel, nk, act, res is not None),
        grid=(m // tm, n // tn, nk),
        in_specs=in_specs,
        out_specs=pl.BlockSpec((tm, tn), lambda i, j, kk: (i, j)),
        out_shape=jax.ShapeDtypeStruct((m, n), out_dtype),
        scratch_shapes=[pltpu.VMEM((tm, tn), F32)] if nk > 1 else [],
        compiler_params=_cparams(("parallel", "parallel", "arbitrary")),
        name=name,
    )(*args)


def _gdn_kernel(C, NC, first_valid, qkv_ref, z_ref, ps_ref, cw_ref, alog_ref, dt_ref, nw_ref, s0_ref,
                o_ref, sout_ref, S, ext):
    c = pl.program_id(1)

    @pl.when(c == 0)
    def _():
        S[...] = s0_ref[0]
        ext[0:SUBLANES, :] = jnp.zeros((SUBLANES, GDN_QKV), F32)

    x = qkv_ref[...]
    ext[SUBLANES:SUBLANES + C, :] = x
    cw = cw_ref[...]
    y = ((ext[5:5 + C, :] * cw[0:1] + ext[6:6 + C, :] * cw[1:2]) + ext[7:7 + C, :] * cw[2:3]) + ext[8:8 + C, :] * cw[3:4]
    ext[0:SUBLANES, :] = x[C - SUBLANES:C]
    y = y * _sigmoid(y)

    rows = c * C + _iota2((C, 1), 0)
    valid = rows >= first_valid
    ps = ps_ref[...]
    g_all = jnp.where(valid, -jnp.exp(alog_ref[...]) * _softplus(ps + dt_ref[...]), 0.0)
    beta_all = jnp.where(valid, _sigmoid(ps), 0.0)
    ri = _iota2((C, C), 0)
    ci = _iota2((C, C), 1)
    causal = ri >= ci
    strict = ri > ci
    gc = _dot_lhs01(causal.astype(F32), g_all)
    gct = _dot_rhs01(g_all.T, (ri <= ci).astype(F32))
    nw = nw_ref[...]

    def head_steps(h):
        sl = slice(h * GDN_D, (h + 1) * GDN_D)
        qh = y[:, sl]
        kh = y[:, GDN_QK + h * GDN_D:GDN_QK + (h + 1) * GDN_D]
        vh = jnp.where(valid, y[:, 2 * GDN_QK + h * GDN_D:2 * GDN_QK + (h + 1) * GDN_D], 0.0)
        qh = jnp.where(valid, qh * lax.rsqrt(jnp.sum(qh * qh, axis=-1, keepdims=True) + L2_EPS) * GDN_D ** -0.5, 0.0)
        kh = jnp.where(valid, kh * lax.rsqrt(jnp.sum(kh * kh, axis=-1, keepdims=True) + L2_EPS), 0.0)
        bcol = beta_all[:, LANE_B + h:LANE_B + h + 1]
        gcol = gc[:, LANE_A + h:LANE_A + h + 1]
        grow = gct[LANE_A + h:LANE_A + h + 1, :]
        glast = gc[C - 1:C, LANE_A + h:LANE_A + h + 1]
        diff = gcol - grow
        decay = jnp.where(causal, jnp.exp(jnp.where(causal, diff, 0.0)), 0.0)
        kb = kh * bcol
        a_mat = jnp.where(strict, _dot_nt(_bf(kb), _bf(kh)) * decay, 0.0)
        qk = _dot_nt(_bf(qh), _bf(kh)) * decay
        eg = jnp.exp(gcol)
        rhs = jnp.concatenate([vh * bcol, kb * eg], axis=1)
        q_dec = qh * eg
        k_dec = kh * jnp.exp(glast - gcol)
        yield
        t_mat = yield from _tri_inv_steps(-a_mat, C)
        sol = _dot3(t_mat, rhs)
        u = sol[:, :GDN_D]
        w = sol[:, GDN_D:]
        yield
        s_h = S[h]
        s_bf = _bf(s_h)
        v_new = u - _dot(_bf(w), s_bf)
        o_state = _dot(_bf(q_dec), s_bf)
        yield
        o = o_state + _dot(_bf(qk), _bf(v_new))
        S[h] = s_h * jnp.exp(glast) + _dot_tn(_bf(k_dec), _bf(v_new))
        yield
        on = o * lax.rsqrt(jnp.mean(o * o, axis=-1, keepdims=True) + NORM_EPS) * nw
        zh = z_ref[:, sl]
        o_ref[:, sl] = (on * (zh * _sigmoid(zh))).astype(o_ref.dtype)

    _run_interleaved([head_steps(h) for h in range(GDN_HEADS)])

    @pl.when(c == NC - 1)
    def _():
        sout_ref[0] = S[...]


def gdn(qkv_arr, z_arr, ps_arr, z_col, conv_w, alog_pad, dt_pad, norm_w, s0, *, B, NC, C, first_valid, name):
    rows = B * NC * C
    return pl.pallas_call(
        functools.partial(_gdn_kernel, C, NC, first_valid),
        grid=(B, NC),
        in_specs=[
            pl.BlockSpec((C, GDN_QKV), lambda b, c: (b * NC + c, 0)),
            pl.BlockSpec((C, GDN_QK), lambda b, c: (b * NC + c, z_col)),
            pl.BlockSpec((C, LANES), lambda b, c: (b * NC + c, 0)),
            pl.BlockSpec((4, GDN_QKV), lambda b, c: (0, 0)),
            pl.BlockSpec((1, LANES), lambda b, c: (0, 0)),
            pl.BlockSpec((1, LANES), lambda b, c: (0, 0)),
            pl.BlockSpec((1, GDN_D), lambda b, c: (0, 0)),
            pl.BlockSpec((1, GDN_HEADS, GDN_D, GDN_D), lambda b, c: (b, 0, 0, 0)),
        ],
        out_specs=[
            pl.BlockSpec((C, GDN_QK), lambda b, c: (b * NC + c, 0)),
            pl.BlockSpec((1, GDN_HEADS, GDN_D, GDN_D), lambda b, c: (b, 0, 0, 0)),
        ],
        out_shape=[
            jax.ShapeDtypeStruct((rows, GDN_QK), BF16),
            jax.ShapeDtypeStruct((B, GDN_HEADS, GDN_D, GDN_D), F32),
        ],
        scratch_shapes=[pltpu.VMEM((GDN_HEADS, GDN_D, GDN_D), F32), pltpu.VMEM((C + SUBLANES, GDN_QKV), F32)],
        compiler_params=_cparams(("parallel", "arbitrary")),
        name=name,
    )(qkv_arr, z_arr, ps_arr, conv_w, alog_pad, dt_pad, norm_w, s0)


def _fox_prep_kernel(tb, first_valid, ps_ref, bf_ref, lf_ref, cq_ref, ck_ref, carry):
    i = pl.program_id(0)

    @pl.when(i == 0)
    def _():
        carry[...] = jnp.zeros(carry.shape, F32)

    x = ps_ref[...] + bf_ref[...]
    rows = i * tb + _iota2((tb, 1), 0)
    lf = jnp.where(rows >= first_valid, jnp.minimum(x, 0.0) - jnp.log(1.0 + jnp.exp(-jnp.abs(x))), 0.0)
    tri = (_iota2((tb, tb), 0) >= _iota2((tb, tb), 1)).astype(F32)
    c = _dot_lhs01(tri, lf) + carry[0:1, :]
    carry[...] = jnp.broadcast_to(c[tb - 1:tb, :], carry.shape)
    lf_ref[...] = lf
    for h in range(FOX_HEADS):
        cq_ref[h] = jnp.broadcast_to(c[:, LANE_F + h:LANE_F + h + 1], (tb, LANES))
    ck_ref[...] = c.T[LANE_F:LANE_F + FOX_HEADS, :]


def fox_prep(ps, bf_pad, first_valid):
    r = ps.shape[0]
    tb = _pick(r, (ROW_TILE, 512, 256, 128))
    return pl.pallas_call(
        functools.partial(_fox_prep_kernel, tb, first_valid),
        grid=(r // tb,),
        in_specs=[pl.BlockSpec((tb, LANES), lambda i: (i, 0)), pl.BlockSpec((1, LANES), lambda i: (0, 0))],
        out_specs=[
            pl.BlockSpec((tb, LANES), lambda i: (i, 0)),
            pl.BlockSpec((FOX_HEADS, tb, LANES), lambda i: (0, i, 0)),
            pl.BlockSpec((FOX_HEADS, tb), lambda i: (0, i)),
        ],
        out_shape=[
            jax.ShapeDtypeStruct((r, LANES), F32),
            jax.ShapeDtypeStruct((FOX_HEADS, r, LANES), F32),
            jax.ShapeDtypeStruct((FOX_HEADS, r), F32),
        ],
        scratch_shapes=[pltpu.VMEM((SUBLANES, LANES), F32)],
        compiler_params=_cparams(("arbitrary",)),
        name="fox_prep",
    )(ps, bf_pad)


def _fox_flash_kernel(tq, first_valid, qi_ref, ki_ref, q_ref, k_ref, v_ref, cq_ref, ck_ref, fz_ref, o_ref,
                      m_s, l_s, acc_s):
    step = pl.program_id(1)
    qi = qi_ref[step]
    ki = ki_ref[step]

    @pl.when(ki == 0)
    def _():
        m_s[...] = jnp.full(m_s.shape, NEG_INF, F32)
        l_s[...] = jnp.zeros(l_s.shape, F32)
        acc_s[...] = jnp.zeros(acc_s.shape, F32)

    def accumulate(masked):
        s = _dot_nt(_bf(q_ref[...]), _bf(k_ref[...])) * FOX_DH ** -0.5
        t = s - ck_ref[0]
        if masked:
            kpos = ki * tq + _iota2((tq, tq), 1)
            qpos = qi * tq + _iota2((tq, tq), 0)
            t = jnp.where((kpos <= qpos) & (kpos >= first_valid), t, NEG_INF)
        cq = cq_ref[0][:, 0:1]
        m_prev = m_s[:, 0:1]
        m_new = jnp.maximum(m_prev, jnp.max(t, axis=-1, keepdims=True) + cq)
        p = jnp.exp(t + (cq - m_new))
        alpha = jnp.exp(m_prev - m_new)
        l_s[...] = alpha * l_s[...] + jnp.sum(p, axis=-1, keepdims=True)
        acc_s[...] = alpha * acc_s[...] + _dot(_bf(p), _bf(v_ref[...]))
        m_s[...] = jnp.broadcast_to(m_new, m_s.shape)

    edge = (ki == qi) | (ki == 0)
    pl.when(edge)(lambda: accumulate(True))
    pl.when(jnp.logical_not(edge))(lambda: accumulate(False))

    @pl.when(ki == qi)
    def _():
        rows = qi * tq + _iota2((tq, 1), 0)
        o = acc_s[...] / l_s[...] * _sigmoid(fz_ref[...])
        o_ref[...] = jnp.where(rows >= first_valid, o, 0.0).astype(o_ref.dtype)


def fox_prompt(p, cq, ck, tp, first_valid):
    tq = _pick(tp, (ROW_TILE, 512, 384, 256, 128))
    nq = tp // tq
    assert first_valid < tq
    pairs = [(qi, ki) for qi in range(nq) for ki in range(qi + 1)]
    qi_arr = jnp.array([a for a, _ in pairs], jnp.int32)
    ki_arr = jnp.array([b for _, b in pairs], jnp.int32)
    cb = lambda col, h: col // FOX_DH + h
    return pl.pallas_call(
        functools.partial(_fox_flash_kernel, tq, first_valid),
        grid_spec=pltpu.PrefetchScalarGridSpec(
            num_scalar_prefetch=2,
            grid=(FOX_HEADS, len(pairs)),
            in_specs=[
                pl.BlockSpec((tq, FOX_DH), lambda h, t, qa, ka: (qa[t], cb(COL_FQ, h))),
                pl.BlockSpec((tq, FOX_DH), lambda h, t, qa, ka: (ka[t], cb(COL_FK, h))),
                pl.BlockSpec((tq, FOX_DH), lambda h, t, qa, ka: (ka[t], cb(COL_FV, h))),
                pl.BlockSpec((1, tq, LANES), lambda h, t, qa, ka: (h, qa[t], 0)),
                pl.BlockSpec((1, 1, tq), lambda h, t, qa, ka: (h, 0, ka[t])),
                pl.BlockSpec((tq, FOX_DH), lambda h, t, qa, ka: (qa[t], cb(COL_FZ, h))),
            ],
            out_specs=pl.BlockSpec((tq, FOX_DH), lambda h, t, qa, ka: (qa[t], h)),
            scratch_shapes=[pltpu.VMEM((tq, LANES), F32), pltpu.VMEM((tq, LANES), F32),
                            pltpu.VMEM((tq, FOX_DH), F32)],
        ),
        out_shape=jax.ShapeDtypeStruct((tp, FOX_W), BF16),
        compiler_params=_cparams(("parallel", "arbitrary")),
        name="fox_prompt",
    )(qi_arr, ki_arr, p, p, p, cq, ck.reshape(FOX_HEADS, 1, -1), p)


def _fox_sample_kernel(G, ngroups, pt_ref, q_ref, *refs):
    k_refs, v_refs, lft_refs = refs[0:G], refs[G:2 * G], refs[2 * G:3 * G]
    lfn_ref, kn_ref, vn_ref, fz_ref, o_ref, cq_s, cn_s, carry_s, m_s, l_s, acc_s = refs[3 * G:]
    jg = pl.program_id(1)
    rows = q_ref.shape[1]
    rep = rows // FOX_HEADS
    page = lft_refs[0].shape[2]
    flat = page * FOX_HEADS
    nn = lfn_ref.shape[2]
    scale = FOX_DH ** -0.5

    @pl.when(jg == 0)
    def _():
        upper = (_iota2((nn, nn), 0) <= _iota2((nn, nn), 1)).astype(F32)
        cn = _dot_rhs01(lfn_ref[0], upper)
        cn_s[...] = cn
        own_q = _iota2((rows, nn), 1) == (_iota2((rows, nn), 0) & (rep - 1))
        cq_s[...] = jnp.broadcast_to(jnp.sum(jnp.where(own_q, cn, 0.0), axis=-1, keepdims=True), cq_s.shape)
        carry_s[...] = jnp.zeros(carry_s.shape, F32)
        m_s[...] = jnp.full(m_s.shape, NEG_INF, F32)
        l_s[...] = jnp.zeros(l_s.shape, F32)
        acc_s[...] = jnp.zeros(acc_s.shape, F32)

    def update(ts, vals_bf):
        cq = cq_s[:, 0:1]
        m_prev = m_s[:, 0:1]
        t_max = functools.reduce(jnp.maximum, ts)
        m_new = jnp.maximum(m_prev, jnp.max(t_max, axis=-1, keepdims=True) + cq)
        shift = cq - m_new
        ps = [jnp.exp(t + shift) for t in ts]
        alpha = jnp.exp(m_prev - m_new)
        l_s[...] = alpha * l_s[...] + jnp.sum(functools.reduce(jnp.add, ps), axis=-1, keepdims=True)
        pv = functools.reduce(jnp.add, [_dot(_bf(p), v) for p, v in zip(ps, vals_bf)])
        acc_s[...] = alpha * acc_s[...] + pv
        m_s[...] = jnp.broadcast_to(m_new, m_s.shape)

    own = _div(_iota2((rows, flat), 0), rep) == (_iota2((rows, flat), 1) & (FOX_HEADS - 1))
    later = (_iota2((page, flat), 0) > _div(_iota2((page, flat), 1), FOX_HEADS)).astype(BF16)
    head_col = _iota2((FOX_HEADS, flat), 0) == (_iota2((FOX_HEADS, flat), 1) & (FOX_HEADS - 1))
    carry = carry_s[:, 0:1]
    q_bf = _bf(q_ref[0])
    ts = []
    for g in range(G):
        lft = lft_refs[g][0]
        suffix = _dot_rhs01(lft, later)
        d = jnp.sum(jnp.where(head_col, suffix + carry, 0.0), axis=0, keepdims=True)
        carry = carry + jnp.sum(lft, axis=-1, keepdims=True)
        s = _dot_nt(q_bf, _bf(k_refs[g][0])) * scale
        ts.append(jnp.where(own, s + d, NEG_INF))
    update(ts, [_bf(v_refs[g][0]) for g in range(G)])
    carry_s[...] = jnp.broadcast_to(carry, carry_s.shape)

    @pl.when(jg == ngroups - 1)
    def _():
        nflat = kn_ref.shape[1]
        spread = (_iota2((nn, nflat), 0) == _div(_iota2((nn, nflat), 1), FOX_HEADS)).astype(F32)
        cn_cols = _dot_rhs01(cn_s[...], spread)
        ri = _iota2((rows, nflat), 0)
        ci = _iota2((rows, nflat), 1)
        ok = (_div(ri, rep) == (ci & (FOX_HEADS - 1))) & (_div(ci, FOX_HEADS) <= (ri & (rep - 1)))
        sn = _dot_nt(_bf(q_ref[0]), _bf(kn_ref[0])) * scale
        update([jnp.where(ok, sn - cn_cols, NEG_INF)], [_bf(vn_ref[0])])
        o_ref[0] = acc_s[...] / l_s[...] * _sigmoid(fz_ref[0])


def fox_sample(page_table_flat, q_rows, cache_k, cache_v, cache_lft, lfn, kn_flat, vn_flat, fz_rows, nb, npages):
    flat = cache_k.shape[1]
    page = cache_lft.shape[2]
    rows = q_rows.shape[1]
    nn = lfn.shape[2]
    G = _pick(npages, (FOX_GROUP, 2, 1))
    ngroups = npages // G

    def page_map(g):
        return lambda b, jg, pt: (pt[b * npages + (npages - 1 - (jg * G + g))], 0, 0)

    seq = lambda b, jg, pt: (b, 0, 0)
    row_blk = pl.BlockSpec((1, rows, FOX_DH), seq)
    new_blk = pl.BlockSpec((1, kn_flat.shape[1], FOX_DH), seq)
    in_specs = [row_blk]
    in_specs += [pl.BlockSpec((1, flat, FOX_DH), page_map(g)) for g in range(G)]
    in_specs += [pl.BlockSpec((1, flat, FOX_DH), page_map(g)) for g in range(G)]
    in_specs += [pl.BlockSpec((1, FOX_HEADS, page), page_map(g)) for g in range(G)]
    in_specs += [pl.BlockSpec((1, rows, nn), seq), new_blk, new_blk, row_blk]
    return pl.pallas_call(
        functools.partial(_fox_sample_kernel, G, ngroups),
        grid_spec=pltpu.PrefetchScalarGridSpec(
            num_scalar_prefetch=1,
            grid=(nb, ngroups),
            in_specs=in_specs,
            out_specs=row_blk,
            scratch_shapes=[
                pltpu.VMEM((rows, LANES), F32),
                pltpu.VMEM((rows, nn), F32),
                pltpu.VMEM((FOX_HEADS, LANES), F32),
                pltpu.VMEM((rows, LANES), F32),
                pltpu.VMEM((rows, LANES), F32),
                pltpu.VMEM((rows, FOX_DH), F32),
            ],
        ),
        out_shape=jax.ShapeDtypeStruct((nb, rows, FOX_DH), F32),
        compiler_params=_cparams(("parallel", "arbitrary")),
        name="fox_sample",
    )(page_table_flat, q_rows, *([cache_k] * G), *([cache_v] * G), *([cache_lft] * G), lfn, kn_flat, vn_flat, fz_rows)


def _rwkv_mix_kernel(h_ref, p_ref, mu_ref, *o_refs):
    h = h_ref[...]
    xx = p_ref[...] - h
    mu = mu_ref[...]
    for i, o_ref in enumerate(o_refs):
        o_ref[...] = (h + xx * mu[i:i + 1]).astype(o_ref.dtype)


def rwkv_mix(h, prev, mu):
    m, d = h.shape
    tm = _pick(m, (320, 256, 128))
    spec = pl.BlockSpec((tm, d), lambda i: (i, 0))
    return pl.pallas_call(
        _rwkv_mix_kernel,
        grid=(m // tm,),
        in_specs=[spec, spec, pl.BlockSpec((6, d), lambda i: (0, 0))],
        out_specs=[spec] * 6,
        out_shape=[jax.ShapeDtypeStruct((m, d), BF16)] * 6,
        compiler_params=_cparams(("parallel",)),
        name="rwkv_mix",
    )(h, prev, mu)


def _head_ones():
    return (_div(_iota2((LANES, LANES), 0), RWKV_HEAD) == _div(_iota2((LANES, LANES), 1), RWKV_HEAD)).astype(F32)


def _rwkv_chunk_terms(C, valid, r, k, v, wl, al, prm):
    HD = RWKV_HEAD
    m0 = _iota2((1, LANES), 1) < HD
    bones = _head_ones()
    w0, a0, k_k, k_a, r_k = (prm[i:i + 1] for i in (PRM_W0, PRM_A0, PRM_KK, PRM_KA, PRM_RK))
    wlog = -_softplus(-(w0 + wl)) - 0.5
    lw = jnp.where(valid, -jnp.exp(wlog), 0.0)
    a = _sigmoid(a0 + al)
    kkr = k * k_k
    kk = kkr * lax.rsqrt(_dot_rhs01(kkr * kkr, bones) + L2_EPS)
    k2 = k * (1.0 + (a - 1.0) * k_a)
    bonus = _dot_rhs01(r * k2 * r_k, bones) * v
    rm = jnp.where(valid, r, 0.0)
    k2 = jnp.where(valid, k2, 0.0)
    vm = jnp.where(valid, v, 0.0)
    av = jnp.where(valid, -kk, 0.0)
    bv = jnp.where(valid, kk * a, 0.0)

    ri = _iota2((C, C), 0)
    ci = _iota2((C, C), 1)
    yield
    cum = _dot_lhs01((ri >= ci).astype(F32), lw)
    cum_last = cum[C - 1:C, :]
    inv = jnp.exp(-cum)
    rt = rm * jnp.exp(cum)
    at = av * jnp.exp(cum - lw)
    bt = bv * inv
    kt = k2 * inv
    to_end = jnp.exp(cum_last - cum)
    b_end = bv * to_end
    k_end = k2 * to_end

    def split(x):
        return jnp.concatenate([jnp.where(m0, x, 0.0), jnp.where(m0, 0.0, x)], axis=0)

    def halves(x):
        return x[0:C] + x[C:2 * C]

    yield
    at_s = split(at)
    gram = _dot_nt(_bf(jnp.concatenate([at_s, split(rt)], axis=0)), _bf(jnp.concatenate([bt, kt], axis=0)))
    r2 = _iota2((2 * C, 2 * C), 0)
    c2 = _iota2((2 * C, 2 * C), 1)
    same = _div(r2, C) == _div(c2, C)
    bd_strict = same & (r2 > c2)
    bd_incl = same & (r2 >= c2)

    def bd(block, mask):
        return jnp.where(mask, jnp.concatenate([block, block], axis=1), 0.0)

    a_ab = bd(gram[0:2 * C, 0:C], bd_strict)
    a_ak = bd(gram[0:2 * C, C:2 * C], bd_strict)
    r_b = bd(gram[2 * C:4 * C, 0:C], bd_incl)
    r_k2 = bd(gram[2 * C:4 * C, C:2 * C], bd_incl)
    vs = split(vm)
    akv = halves(_dot(_bf(a_ak), _bf(vs)))
    yield
    t_bd = yield from _tri_inv_steps(a_ab, C)
    tw = _dot(_bf(t_bd), _bf(jnp.concatenate([at_s, split(akv)], axis=1)))
    wt = halves(tw[:, :LANES])
    ut = halves(tw[:, LANES:])
    yield
    rp = rt + halves(_dot(_bf(r_b), _bf(split(wt))))
    y0 = halves(_dot(_bf(jnp.concatenate([r_b, r_k2], axis=1)), _bf(jnp.concatenate([split(ut), vs], axis=0))))
    eye = (_iota2((LANES, LANES), 0) == _iota2((LANES, LANES), 1)).astype(F32)
    m_mat = eye * jnp.exp(cum_last) + bones * _dot_tn(_bf(b_end), _bf(wt))
    n_mat = bones * _dot_tn(_bf(jnp.concatenate([b_end, k_end], axis=0)), _bf(jnp.concatenate([ut, vm], axis=0)))
    return rp, y0, bonus, m_mat, n_mat


def _rwkv_prep_kernel(C, chunk_of_step, first_valid, PG, r_ref, k_ref, v_ref, wl_ref, al_ref, prm_ref,
                      rp_ref, y0_ref, bo_ref, m_ref, n_ref):
    c = pl.program_id(0) if chunk_of_step else 0
    valid = (c * C + _iota2((C, 1), 0)) >= first_valid
    sls = [slice(i * LANES, (i + 1) * LANES) for i in range(PG)]
    terms = _run_interleaved([
        _rwkv_chunk_terms(C, valid, r_ref[:, sl], k_ref[:, sl], v_ref[:, sl], wl_ref[:, sl], al_ref[:, sl], prm_ref[:, sl])
        for sl in sls])
    for i, sl in enumerate(sls):
        rp, y0, bonus, m_mat, n_mat = terms[i]
        rp_ref[:, sl] = rp
        y0_ref[:, sl] = y0
        bo_ref[:, sl] = bonus
        m_ref[0, i] = m_mat
        n_ref[0, i] = n_mat


def _rwkv_scan_kernel(NC, rp_ref, y0_ref, bo_ref, g_ref, m_ref, n_ref, prm_ref, s0_ref, o_ref, sout_ref, H):
    c = pl.program_id(1)
    HD = RWKV_HEAD
    bones = _head_ones()

    @pl.when(c == 0)
    def _():
        zero = jnp.zeros((HD, HD), F32)
        for i in range(RWKV_PAIRS):
            top = jnp.concatenate([s0_ref[0, 2 * i], zero], axis=1)
            bot = jnp.concatenate([zero, s0_ref[0, 2 * i + 1]], axis=1)
            H[i] = jnp.concatenate([top, bot], axis=0).T

    def pair_steps(i):
        sl = slice(i * LANES, (i + 1) * LANES)
        h_bd = H[i]
        y = _dot(_bf(rp_ref[:, sl]), _bf(h_bd)) + y0_ref[:, sl]
        H[i] = _dot3(m_ref[0, i], h_bd) + n_ref[0, i]
        yield
        mean = _dot_rhs01(y, bones) * (1.0 / HD)
        dlt = y - mean
        yield
        var = _dot_rhs01(dlt * dlt, bones) * (1.0 / HD)
        yn = dlt * lax.rsqrt(var + GN_EPS) * prm_ref[PRM_LNW:PRM_LNW + 1, sl] + prm_ref[PRM_LNB:PRM_LNB + 1, sl]
        o_ref[:, sl] = ((yn + bo_ref[:, sl]) * g_ref[:, sl]).astype(o_ref.dtype)

    _run_interleaved([pair_steps(i) for i in range(RWKV_PAIRS)])

    @pl.when(c == NC - 1)
    def _():
        for i in range(RWKV_PAIRS):
            ht = H[i].T
            sout_ref[0, 2 * i] = ht[0:HD, 0:HD]
            sout_ref[0, 2 * i + 1] = ht[HD:2 * HD, HD:2 * HD]


def rwkv(r, k, v, wl, al, g, prm, s0, *, B, NC, C, first_valid, name):
    assert B == 1 or NC == 1
    nblk = B * NC
    rows = nblk * C
    PG = RWKV_PREP_PAIRS
    wide = pl.BlockSpec((C, PG * LANES), lambda blk, pg: (blk, pg))
    mat = pl.BlockSpec((1, PG, LANES, LANES), lambda blk, pg: (blk, pg, 0, 0))
    mat_shape = jax.ShapeDtypeStruct((nblk, RWKV_PAIRS, LANES, LANES), F32)
    row_shape = jax.ShapeDtypeStruct((rows, D_MODEL), F32)
    rp, y0, bonus, m_all, n_all = pl.pallas_call(
        functools.partial(_rwkv_prep_kernel, C, B == 1, first_valid, PG),
        grid=(nblk, RWKV_PAIRS // PG),
        in_specs=[wide] * 5 + [pl.BlockSpec((SUBLANES, PG * LANES), lambda blk, pg: (0, pg))],
        out_specs=[wide, wide, wide, mat, mat],
        out_shape=[row_shape, row_shape, row_shape, mat_shape, mat_shape],
        compiler_params=_cparams(("parallel", "parallel")),
        name=name + "_terms",
    )(r, k, v, wl, al, prm)

    full = pl.BlockSpec((C, D_MODEL), lambda b, c: (b * NC + c, 0))
    mats = pl.BlockSpec((1, RWKV_PAIRS, LANES, LANES), lambda b, c: (b * NC + c, 0, 0, 0))
    st_spec = pl.BlockSpec((1, 2 * RWKV_PAIRS, RWKV_HEAD, RWKV_HEAD), lambda b, c: (b, 0, 0, 0))
    return pl.pallas_call(
        functools.partial(_rwkv_scan_kernel, NC),
        grid=(B, NC),
        in_specs=[full] * 4 + [mats, mats, pl.BlockSpec((SUBLANES, D_MODEL), lambda b, c: (0, 0)), st_spec],
        out_specs=[full, st_spec],
        out_shape=[
            jax.ShapeDtypeStruct((rows, D_MODEL), BF16),
            jax.ShapeDtypeStruct((B, 2 * RWKV_PAIRS, RWKV_HEAD, RWKV_HEAD), F32),
        ],
        scratch_shapes=[pltpu.VMEM((RWKV_PAIRS, LANES, LANES), F32)],
        compiler_params=_cparams(("parallel", "arbitrary")),
        name=name + "_scan",
    )(rp, y0, bonus, g, m_all, n_all, prm, s0)


def _pad_lanes(vec, offset):
    out = jnp.zeros((1, LANES), F32)
    return lax.dynamic_update_slice(out, vec.reshape(1, -1).astype(F32), (0, offset))


def _sample_rows(arr, row0, nb, nq, front):
    cols = arr.shape[1]
    s = arr[row0:row0 + nb * nq].reshape(nb, nq, cols)
    s = jnp.pad(s, ((0, 0), (front, 0), (0, 0)))
    return s.reshape(nb * (front + nq), cols)


def _heads_first(arr, row0, nb, nq, col):
    s = arr[row0:row0 + nb * nq, col:col + FOX_W].reshape(nb, nq, FOX_HEADS, FOX_DH)
    return jnp.transpose(s, (0, 2, 1, 3)).reshape(nb, FOX_HEADS * nq, FOX_DH)


def _only(x):
    assert x.shape[0] == 1
    return x.reshape(x.shape[1:])


def kernel(x_prompt, x_sample, cache_fox_k, cache_fox_v, cache_fox_logf, state_gdn_conv, state_gdn_S,
           state_rwkv_shift, state_rwkv_S, page_table, meta_tokens, ln_mix, ln_mlp, ln_final,
           w_in0, gdn_conv_w, gdn_A_log, gdn_dt_bias, gdn_norm_w, fox_b_f, w_out0,
           rwkv_mu, rwkv_w0, rwkv_w1, rwkv_w2, rwkv_a0, rwkv_a1, rwkv_a2, rwkv_g1, rwkv_g2,
           rwkv_k_k, rwkv_k_a, rwkv_r_k, rwkv_w_r, rwkv_w_k, rwkv_w_v, rwkv_w_o, rwkv_ln_w, rwkv_ln_b,
           w_up, w_down):
    D = D_MODEL
    assert x_prompt.shape[0] == 1 and x_prompt.shape[2] == D
    seq = x_prompt.shape[1]
    nb, nq = x_sample.shape[0], x_sample.shape[1]
    npages = page_table.shape[1]
    tprompt = N_META + seq
    pad = (-tprompt) % LANES
    tp = tprompt + pad
    ns = nb * nq
    R = -(-(tp + ns) // ROW_TILE) * ROW_TILE
    CS = SUBLANES
    front = CS - nq
    assert 3 <= front

    x0 = jnp.concatenate([jnp.zeros((pad, D), F32), meta_tokens.astype(F32), x_prompt[0],
                          x_sample.reshape(ns, D), jnp.zeros((R - tp - ns, D), F32)], axis=0)

    w_in = w_in0[0]
    o_z = GDN_QKV
    o_a = o_z + GDN_QK
    o_b = o_a + GDN_HEADS
    o_fq = o_b + GDN_HEADS
    o_fk = o_fq + FOX_W
    o_fv = o_fk + FOX_W
    o_ff = o_fv + FOX_W
    o_fz = o_ff + FOX_HEADS
    w_big = jnp.concatenate([w_in[:, :o_a], w_in[:, o_fq:o_ff], w_in[:, o_fz:]], axis=1).astype(BF16)
    w_small = jnp.concatenate([w_in[:, o_a:o_fq], w_in[:, o_ff:o_fz],
                               jnp.zeros((D, LANES - 3 * GDN_HEADS), F32)], axis=1).astype(BF16)

    h0 = rmsnorm(x0, ln_mix[0], BF16)
    p = matmul(h0, w_big, name="in_proj")
    ps = matmul(h0, w_small, name="in_proj_small")

    alog_pad = _pad_lanes(gdn_A_log[0], LANE_A)
    dt_pad = _pad_lanes(gdn_dt_bias[0], LANE_A)
    bf_pad = _pad_lanes(fox_b_f[0], LANE_F)
    conv_w = gdn_conv_w[0]
    norm_w = gdn_norm_w[0].reshape(1, GDN_D)

    GC = 64
    o_gdn_p, s_gdn_p = gdn(p, p, ps, COL_Z // GDN_QK, conv_w, alog_pad, dt_pad, norm_w,
                           jnp.zeros((1, GDN_HEADS, GDN_D, GDN_D), F32),
                           B=1, NC=tp // GC, C=GC, first_valid=pad, name="gdn_prompt")
    qkv_s = p[tp:tp + ns, :GDN_QKV].reshape(nb, nq, GDN_QKV)
    qkv_ext = jnp.concatenate([jnp.zeros((nb, front - 3, GDN_QKV), F32), _only(state_gdn_conv), qkv_s], axis=1)
    z_ext = _sample_rows(p[:, COL_Z:COL_Z + GDN_QK], tp, nb, nq, front)
    ps_ext = _sample_rows(ps, tp, nb, nq, front)
    o_gdn_s, s_gdn_s = gdn(qkv_ext.reshape(nb * CS, GDN_QKV), z_ext, ps_ext, 0, conv_w, alog_pad, dt_pad, norm_w,
                           _only(state_gdn_S), B=nb, NC=1, C=CS, first_valid=front, name="gdn_sample")
    o_gdn_s = o_gdn_s.reshape(nb, CS, GDN_QK)[:, front:].reshape(ns, GDN_QK)

    lf, cq, ck = fox_prep(ps, bf_pad, pad)
    o_fox_p = fox_prompt(p, cq, ck, tp, pad)
    pt_flat = page_table.reshape(-1).astype(jnp.int32)
    lf_s = lf[tp:tp + ns, LANE_F:LANE_F + FOX_HEADS].reshape(nb, nq, FOX_HEADS)
    lfn = jnp.repeat(jnp.swapaxes(lf_s, 1, 2), nq, axis=1)
    lfn = jnp.pad(lfn, ((0, 0), (0, 0), (0, SUBLANES - nq)))
    pool = cache_fox_k.shape[1]
    page = cache_fox_k.shape[2]
    cache_k = _only(cache_fox_k).reshape(pool, page * FOX_HEADS, FOX_DH)
    cache_v = _only(cache_fox_v).reshape(pool, page * FOX_HEADS, FOX_DH)
    cache_lft = jnp.swapaxes(_only(cache_fox_logf), 1, 2)
    o_fox_s = fox_sample(pt_flat, _heads_first(p, tp, nb, nq, COL_FQ), cache_k, cache_v, cache_lft, lfn,
                         p[tp:tp + ns, COL_FK:COL_FK + FOX_W].reshape(nb, nq * FOX_HEADS, FOX_DH),
                         p[tp:tp + ns, COL_FV:COL_FV + FOX_W].reshape(nb, nq * FOX_HEADS, FOX_DH),
                         _heads_first(p, tp, nb, nq, COL_FZ), nb, npages)
    o_fox_s = jnp.transpose(o_fox_s.reshape(nb, FOX_HEADS, nq, FOX_DH), (0, 2, 1, 3)).reshape(ns, FOX_W).astype(BF16)

    mix = jnp.concatenate([
        jnp.concatenate([o_gdn_p, o_fox_p], axis=1),
        jnp.concatenate([o_gdn_s, o_fox_s], axis=1),
        jnp.zeros((R - tp - ns, 2 * GDN_QK), BF16)], axis=0)
    x1 = matmul(mix, w_out0[0].astype(BF16), res=x0, name="out_proj")
    u0 = matmul(rmsnorm(x1, ln_mlp[0], BF16), w_up[0].astype(BF16), act="relu2", out_dtype=BF16, name="mlp_up0")
    x2 = matmul(u0, w_down[0].astype(BF16), res=x1, name="mlp_down0")

    h1 = rmsnorm(x2, ln_mix[1], F32)
    h1_s = h1[tp:tp + ns].reshape(nb, nq, D)
    prev_s = jnp.concatenate([state_rwkv_shift[0][:, None, :].astype(F32), h1_s[:, :-1]], axis=1).reshape(ns, D)
    prev = jnp.concatenate([jnp.zeros((1, D), F32), h1[:tp - 1], prev_s, jnp.zeros((R - tp - ns, D), F32)], axis=0)
    xr, xw, xk, xv, xa, xg = rwkv_mix(h1, prev, rwkv_mu[0])

    def pad_cols(w):
        return jnp.pad(w, ((0, 0), (0, LANES - w.shape[1]))).astype(BF16)

    def pad_rows(w):
        return jnp.pad(w, ((0, LANES - w.shape[0]), (0, 0))).astype(BF16)

    r_ = matmul(xr, rwkv_w_r[0].astype(BF16), name="rwkv_r")
    k_ = matmul(xk, rwkv_w_k[0].astype(BF16), name="rwkv_k")
    v_ = matmul(xv, rwkv_w_v[0].astype(BF16), name="rwkv_v")
    wl = matmul(matmul(xw, pad_cols(rwkv_w1[0]), act="tanh", out_dtype=BF16, name="rwkv_w1"), pad_rows(rwkv_w2[0]), name="rwkv_w2")
    al = matmul(matmul(xa, pad_cols(rwkv_a1[0]), out_dtype=BF16, name="rwkv_a1"), pad_rows(rwkv_a2[0]), name="rwkv_a2")
    g_ = matmul(matmul(xg, rwkv_g1[0].astype(BF16), act="sigmoid", out_dtype=BF16, name="rwkv_g1"), rwkv_g2[0].astype(BF16), name="rwkv_g2")

    prm = jnp.stack([rwkv_w0[0], rwkv_a0[0], rwkv_k_k[0], rwkv_k_a[0], rwkv_r_k[0].reshape(D), rwkv_ln_w[0], rwkv_ln_b[0],
                     jnp.zeros((D,), F32)], axis=0).astype(F32)
    RC = 64
    o_rw_p, s_rw_p = rwkv(r_, k_, v_, wl, al, g_, prm, jnp.zeros((1, 2 * RWKV_PAIRS, RWKV_HEAD, RWKV_HEAD), F32),
                          B=1, NC=tp // RC, C=RC, first_valid=pad, name="rwkv_prompt")
    sx = [_sample_rows(t, tp, nb, nq, front) for t in (r_, k_, v_, wl, al, g_)]
    o_rw_s, s_rw_s = rwkv(*sx, prm, _only(state_rwkv_S), B=nb, NC=1, C=CS, first_valid=front, name="rwkv_sample")
    o_rw_s = o_rw_s.reshape(nb, CS, D)[:, front:].reshape(ns, D)
    o_rw = jnp.concatenate([o_rw_p, o_rw_s, jnp.zeros((R - tp - ns, D), BF16)], axis=0)
    x3 = matmul(o_rw, rwkv_w_o[0].astype(BF16), res=x2, name="rwkv_o")
    u1 = matmul(rmsnorm(x3, ln_mlp[1], BF16), w_up[1].astype(BF16), act="relu2", out_dtype=BF16, name="mlp_up1")
    x4 = matmul(u1, w_down[1].astype(BF16), res=x3, name="mlp_down1")
    y = rmsnorm(x4, ln_final, F32)

    r0 = pad
    y_prompt = y[r0 + N_META:tp].reshape(1, seq, D)
    y_sample = y[tp:tp + ns].reshape(nb, nq, D)

    def kv_rows(col):
        blk = p[:, col:col + FOX_W]
        return (blk[r0:tp].reshape(1, 1, tprompt, FOX_HEADS, FOX_DH),
                blk[tp:tp + ns].reshape(1, nb, nq, FOX_HEADS, FOX_DH))

    fk_p, fk_s = kv_rows(COL_FK)
    fv_p, fv_s = kv_rows(COL_FV)
    lf8 = lf[:, LANE_F:LANE_F + FOX_HEADS]
    lf_p = lf8[r0:tp].reshape(1, 1, tprompt, FOX_HEADS)
    lf_sm = lf8[tp:tp + ns].reshape(1, nb, nq, FOX_HEADS)
    cb_p = p[tp - 3:tp, :GDN_QKV].reshape(1, 1, 3, GDN_QKV)
    cb_s = qkv_ext[:, CS - 3:][None]
    gs_p = s_gdn_p[None]
    gs_s = s_gdn_s[None]
    sh_p = h1[tp - 1].reshape(1, 1, D)
    sh_s = h1_s[:, nq - 1][None]
    rs_p = s_rw_p[None]
    rs_s = s_rw_s[None]
    return (y_prompt, y_sample, fk_p, fk_s, fv_p, fv_s, lf_p, lf_sm, cb_p, cb_s, gs_p, gs_s, sh_p, sh_s, rs_p, rs_s)
```

```python
import functools
import math

import jax
import jax.numpy as jnp
from jax import lax
from jax.experimental import pallas as pl
from jax.experimental.pallas import tpu as pltpu

F32 = jnp.float32
BF16 = jnp.bfloat16

D_MODEL = 2048
N_META = 16
GDN_HEADS = 8
GDN_D = 128
GDN_QK = GDN_HEADS * GDN_D
GDN_QKV = 3 * GDN_QK
FOX_HEADS = 8
FOX_DH = 128
FOX_W = FOX_HEADS * FOX_DH
RWKV_HEAD = 64
RWKV_PAIRS = D_MODEL // (2 * RWKV_HEAD)
NORM_EPS = 1e-6
L2_EPS = 1e-6
GN_EPS = 64e-5
NEG_INF = -1e30

LANES = 128
SUBLANES = 8
ROW_TILE = 640
MM_ROW_TILE = 1792
LOG2E = 1.4426950408889634
VMEM_LIMIT = 48 * 1024 * 1024
FOX_GROUP = 8
RWKV_PREP_PAIRS = 8

COL_Z = GDN_QKV
COL_FQ = COL_Z + GDN_QK
COL_FK = COL_FQ + FOX_W
COL_FV = COL_FK + FOX_W
COL_FZ = COL_FV + FOX_W
LANE_A = 0
LANE_B = 8
LANE_F = 16
PRM_W0, PRM_A0, PRM_KK, PRM_KA, PRM_RK, PRM_LNW, PRM_LNB = range(7)


def _pick(n, cands):
    for c in cands:
        if n % c == 0:
            return c
    raise ValueError(f"no tile for {n}")


def _cparams(sem):
    return pltpu.CompilerParams(dimension_semantics=sem, vmem_limit_bytes=VMEM_LIMIT)


def _dot(a, b):
    return jnp.dot(a, b, preferred_element_type=F32)


def _dot_nt(a, b):
    return lax.dot_general(a, b, (((1,), (1,)), ((), ())), preferred_element_type=F32)


def _dot_tn(a, b):
    return lax.dot_general(a, b, (((0,), (0,)), ((), ())), preferred_element_type=F32)


def _bf(x):
    return x.astype(BF16)


def _softplus(x):
    return jnp.maximum(x, 0.0) + jnp.log(1.0 + jnp.exp(-jnp.abs(x)))


def _sigmoid(x):
    return 1.0 / (1.0 + jnp.exp(-x))


def _iota2(shape, dim):
    return lax.broadcasted_iota(jnp.int32, shape, dim)


def _div(x, n):
    assert n & (n - 1) == 0
    return x >> (n.bit_length() - 1)


def _split2(x):
    hi = x.astype(BF16)
    return hi, (x - hi.astype(F32)).astype(BF16)


def _split3(x):
    p1 = x.astype(BF16)
    rem = x - p1.astype(F32)
    p2 = rem.astype(BF16)
    return p1, p2, (rem - p2.astype(F32)).astype(BF16)


def _dot3(a, b):
    m = a.shape[0]
    ah, al = _split2(a)
    bh, bl = _split2(b)
    top = _dot(jnp.concatenate([ah, al], axis=0), bh)
    return (top[:m] + top[m:]) + _dot(ah, bl)


def _dot_lhs01(a01, b):
    a = a01.astype(BF16)
    b1, b2, b3 = _split3(b)
    return _dot(a, b1) + (_dot(a, b2) + _dot(a, b3))


def _dot_rhs01(a, b01):
    b = b01.astype(BF16)
    a1, a2, a3 = _split3(a)
    m = a.shape[0]
    out = _dot(jnp.concatenate([a1, a2, a3], axis=0), b)
    return out[:m] + (out[m:2 * m] + out[2 * m:])


def _tri_inv_steps(n_mat, n):
    size = n_mat.shape[0]
    eye = (_iota2((size, size), 0) == _iota2((size, size), 1)).astype(F32)
    t = eye + n_mat
    if n <= 2:
        return t
    p = _dot(_bf(n_mat), _bf(n_mat))
    yield
    m = 2
    while m < n:
        p_bf = _bf(p)
        if 2 * m < n:
            both = _dot(_bf(jnp.concatenate([p, t], axis=0)), p_bf)
            p = both[:size]
            t = t + both[size:]
        else:
            t = t + _dot(_bf(t), p_bf)
        m *= 2
        yield
    return t


def _run_interleaved(gens):
    results = [None] * len(gens)
    live = list(enumerate(gens))
    while live:
        nxt = []
        for idx, gen in live:
            try:
                next(gen)
                nxt.append((idx, gen))
            except StopIteration as stop:
                results[idx] = stop.value
        live = nxt
    return results


def _rmsnorm_kernel(x_ref, w_ref, o_ref):
    x = x_ref[...]
    ms = jnp.mean(x * x, axis=-1, keepdims=True)
    o_ref[...] = (x * lax.rsqrt(ms + NORM_EPS) * w_ref[...]).astype(o_ref.dtype)


def rmsnorm(x, w, out_dtype):
    m, d = x.shape
    tm = _pick(m, (320, 256, 128))
    return pl.pallas_call(
        _rmsnorm_kernel,
        grid=(m // tm,),
        in_specs=[pl.BlockSpec((tm, d), lambda i: (i, 0)), pl.BlockSpec((1, d), lambda i: (0, 0))],
        out_specs=pl.BlockSpec((tm, d), lambda i: (i, 0)),
        out_shape=jax.ShapeDtypeStruct((m, d), out_dtype),
        compiler_params=_cparams(("parallel",)),
        name="rmsnorm",
    )(x, w.reshape(1, d))


def _mm_kernel(nk, act, has_res, *refs):
    if has_res:
        x_ref, w_ref, r_ref, o_ref = refs[:4]
        scr = refs[4:]
    else:
        x_ref, w_ref, o_ref = refs[:3]
        r_ref = None
        scr = refs[3:]

    def finish(acc):
        if act == "relu2":
            acc = jnp.square(jnp.maximum(acc, 0.0))
        elif act == "tanh":
            acc = jnp.tanh(acc)
        elif act == "sigmoid":
            acc = _sigmoid(acc)
        if has_res:
            acc = r_ref[...] + acc
        o_ref[...] = acc.astype(o_ref.dtype)

    if nk == 1:
        finish(_dot(x_ref[...], w_ref[...]))
    else:
        acc_ref = scr[0]
        k = pl.program_id(2)

        @pl.when(k == 0)
        def _():
            acc_ref[...] = jnp.zeros(acc_ref.shape, F32)

        acc_ref[...] += _dot(x_ref[...], w_ref[...])

        @pl.when(k == nk - 1)
        def _():
            finish(acc_ref[...])


def matmul(x, w, *, act=None, res=None, out_dtype=F32, name="matmul"):
    m, k = x.shape
    n = w.shape[1]
    tm = _pick(m, (MM_ROW_TILE, ROW_TILE, 512, 256, 128))
    tn = _pick(n, (512, 256, 128))
    tk = k if k <= 2048 else 2048
    nk = k // tk
    in_specs = [pl.BlockSpec((tm, tk), lambda i, j, kk: (i, kk)), pl.BlockSpec((tk, tn), lambda i, j, kk: (kk, j))]
    args = [x, w]
    if res is not None:
        in_specs.append(pl.BlockSpec((tm, tn), lambda i, j, kk: (i, j)))
        args.append(res)
    return pl.pallas_call(
        functools.partial(_mm_kernel, nk, act, res is not None),
        grid=(m // tm, n // tn, nk),
        in_specs=in_specs,
        out_specs=pl.BlockSpec((tm, tn), lambda i, j, kk: (i, j)),
        out_shape=jax.ShapeDtypeStruct((m, n), out_dtype),
        scratch_shapes=[pltpu.VMEM((tm, tn), F32)] if nk > 1 else [],
        compiler_params=_cparams(("parallel", "parallel", "arbitrary")),
        name=name,
    )(*args)


def _gdn_kernel(C, NC, first_valid, qkv_ref, z_ref, ps_ref, cw_ref, alog_ref, dt_ref, nw_ref, s0_ref,
                o_ref, sout_ref, S, ext):
    c = pl.program_id(1)

    @pl.when(c == 0)
    def _():
        S[...] = s0_ref[0]
        ext[0:SUBLANES, :] = jnp.zeros((SUBLANES, GDN_QKV), F32)

    x = qkv_ref[...]
    ext[SUBLANES:SUBLANES + C, :] = x
    cw = cw_ref[...]
    y = ((ext[5:5 + C, :] * cw[0:1] + ext[6:6 + C, :] * cw[1:2]) + ext[7:7 + C, :] * cw[2:3]) + ext[8:8 + C, :] * cw[3:4]
    ext[0:SUBLANES, :] = x[C - SUBLANES:C]
    y = y * _sigmoid(y)

    rows = c * C + _iota2((C, 1), 0)
    valid = rows >= first_valid
    ps = ps_ref[...]
    g_all = jnp.where(valid, -jnp.exp(alog_ref[...]) * _softplus(ps + dt_ref[...]), 0.0)
    beta_all = jnp.where(valid, _sigmoid(ps), 0.0)
    ri = _iota2((C, C), 0)
    ci = _iota2((C, C), 1)
    causal = ri >= ci
    strict = ri > ci
    gc = _dot_lhs01(causal.astype(F32), g_all)
    gct = _dot_rhs01(g_all.T, (ri <= ci).astype(F32))
    nw = nw_ref[...]

    def head_steps(h):
        sl = slice(h * GDN_D, (h + 1) * GDN_D)
        qh = y[:, sl]
        kh = y[:, GDN_QK + h * GDN_D:GDN_QK + (h + 1) * GDN_D]
        vh = jnp.where(valid, y[:, 2 * GDN_QK + h * GDN_D:2 * GDN_QK + (h + 1) * GDN_D], 0.0)
        qh = jnp.where(valid, qh * lax.rsqrt(jnp.sum(qh * qh, axis=-1, keepdims=True) + L2_EPS) * GDN_D ** -0.5, 0.0)
        kh = jnp.where(valid, kh * lax.rsqrt(jnp.sum(kh * kh, axis=-1, keepdims=True) + L2_EPS), 0.0)
        bcol = beta_all[:, LANE_B + h:LANE_B + h + 1]
        gcol = gc[:, LANE_A + h:LANE_A + h + 1]
        grow = gct[LANE_A + h:LANE_A + h + 1, :]
        glast = gc[C - 1:C, LANE_A + h:LANE_A + h + 1]
        diff = gcol - grow
        decay = jnp.where(causal, jnp.exp(jnp.where(causal, diff, 0.0)), 0.0)
        kb = kh * bcol
        a_mat = jnp.where(strict, _dot_nt(_bf(kb), _bf(kh)) * decay, 0.0)
        qk = _dot_nt(_bf(qh), _bf(kh)) * decay
        eg = jnp.exp(gcol)
        rhs = jnp.concatenate([vh * bcol, kb * eg], axis=1)
        q_dec = qh * eg
        k_dec = kh * jnp.exp(glast - gcol)
        yield
        t_mat = yield from _tri_inv_steps(-a_mat, C)
        sol = _dot3(t_mat, rhs)
        u = sol[:, :GDN_D]
        w = sol[:, GDN_D:]
        yield
        s_h = S[h]
        s_bf = _bf(s_h)
        v_new = u - _dot(_bf(w), s_bf)
        o_state = _dot(_bf(q_dec), s_bf)
        yield
        o = o_state + _dot(_bf(qk), _bf(v_new))
        S[h] = s_h * jnp.exp(glast) + _dot_tn(_bf(k_dec), _bf(v_new))
        yield
        on = o * lax.rsqrt(jnp.mean(o * o, axis=-1, keepdims=True) + NORM_EPS) * nw
        zh = z_ref[:, sl]
        o_ref[:, sl] = (on * (zh * _sigmoid(zh))).astype(o_ref.dtype)

    _run_interleaved([head_steps(h) for h in range(GDN_HEADS)])

    @pl.when(c == NC - 1)
    def _():
        sout_ref[0] = S[...]


def gdn(qkv_arr, z_arr, ps_arr, z_col, conv_w, alog_pad, dt_pad, norm_w, s0, *, B, NC, C, first_valid, name):
    rows = B * NC * C
    return pl.pallas_call(
        functools.partial(_gdn_kernel, C, NC, first_valid),
        grid=(B, NC),
        in_specs=[
            pl.BlockSpec((C, GDN_QKV), lambda b, c: (b * NC + c, 0)),
            pl.BlockSpec((C, GDN_QK), lambda b, c: (b * NC + c, z_col)),
            pl.BlockSpec((C, LANES), lambda b, c: (b * NC + c, 0)),
            pl.BlockSpec((4, GDN_QKV), lambda b, c: (0, 0)),
            pl.BlockSpec((1, LANES), lambda b, c: (0, 0)),
            pl.BlockSpec((1, LANES), lambda b, c: (0, 0)),
            pl.BlockSpec((1, GDN_D), lambda b, c: (0, 0)),
            pl.BlockSpec((1, GDN_HEADS, GDN_D, GDN_D), lambda b, c: (b, 0, 0, 0)),
        ],
        out_specs=[
            pl.BlockSpec((C, GDN_QK), lambda b, c: (b * NC + c, 0)),
            pl.BlockSpec((1, GDN_HEADS, GDN_D, GDN_D), lambda b, c: (b, 0, 0, 0)),
        ],
        out_shape=[
            jax.ShapeDtypeStruct((rows, GDN_QK), BF16),
            jax.ShapeDtypeStruct((B, GDN_HEADS, GDN_D, GDN_D), F32),
        ],
        scratch_shapes=[pltpu.VMEM((GDN_HEADS, GDN_D, GDN_D), F32), pltpu.VMEM((C + SUBLANES, GDN_QKV), F32)],
        compiler_params=_cparams(("parallel", "arbitrary")),
        name=name,
    )(qkv_arr, z_arr, ps_arr, conv_w, alog_pad, dt_pad, norm_w, s0)


def _fox_prep_kernel(tb, first_valid, ps_ref, bf_ref, lf_ref, cq_ref, ck_ref, carry):
    i = pl.program_id(0)

    @pl.when(i == 0)
    def _():
        carry[...] = jnp.zeros(carry.shape, F32)

    x = ps_ref[...] + bf_ref[...]
    rows = i * tb + _iota2((tb, 1), 0)
    lf = jnp.where(rows >= first_valid, jnp.minimum(x, 0.0) - jnp.log(1.0 + jnp.exp(-jnp.abs(x))), 0.0)
    tri = (_iota2((tb, tb), 0) >= _iota2((tb, tb), 1)).astype(F32)
    c = _dot_lhs01(tri, lf) + carry[0:1, :]
    carry[...] = jnp.broadcast_to(c[tb - 1:tb, :], carry.shape)
    lf_ref[...] = lf
    c2 = c * LOG2E
    for h in range(FOX_HEADS):
        cq_ref[h] = jnp.broadcast_to(c2[:, LANE_F + h:LANE_F + h + 1], (tb, LANES))
    ck_ref[...] = c2.T[LANE_F:LANE_F + FOX_HEADS, :]


def fox_prep(ps, bf_pad, first_valid):
    r = ps.shape[0]
    tb = _pick(r, (ROW_TILE, 512, 256, 128))
    return pl.pallas_call(
        functools.partial(_fox_prep_kernel, tb, first_valid),
        grid=(r // tb,),
        in_specs=[pl.BlockSpec((tb, LANES), lambda i: (i, 0)), pl.BlockSpec((1, LANES), lambda i: (0, 0))],
        out_specs=[
            pl.BlockSpec((tb, LANES), lambda i: (i, 0)),
            pl.BlockSpec((FOX_HEADS, tb, LANES), lambda i: (0, i, 0)),
            pl.BlockSpec((FOX_HEADS, tb), lambda i: (0, i)),
        ],
        out_shape=[
            jax.ShapeDtypeStruct((r, LANES), F32),
            jax.ShapeDtypeStruct((FOX_HEADS, r, LANES), F32),
            jax.ShapeDtypeStruct((FOX_HEADS, r), F32),
        ],
        scratch_shapes=[pltpu.VMEM((SUBLANES, LANES), F32)],
        compiler_params=_cparams(("arbitrary",)),
        name="fox_prep",
    )(ps, bf_pad)


def _fox_flash_kernel(tq, first_valid, qi_ref, ki_ref, q_ref, k_ref, v_ref, cq_ref, ck_ref, fz_ref, o_ref,
                      m_s, l_s, acc_s):
    step = pl.program_id(1)
    qi = qi_ref[step]
    ki = ki_ref[step]

    @pl.when(ki == 0)
    def _():
        m_s[...] = jnp.full(m_s.shape, NEG_INF, F32)
        l_s[...] = jnp.zeros(l_s.shape, F32)
        acc_s[...] = jnp.zeros(acc_s.shape, F32)

    def accumulate(masked):
        t = _dot_nt(_bf(q_ref[...] * (FOX_DH ** -0.5 * LOG2E)), _bf(k_ref[...])) - ck_ref[0]
        if masked:
            kpos = ki * tq + _iota2((tq, tq), 1)
            qpos = qi * tq + _iota2((tq, tq), 0)
            t = jnp.where((kpos <= qpos) & (kpos >= first_valid), t, NEG_INF)
        cq = cq_ref[0][:, 0:1]
        m_prev = m_s[:, 0:1]
        m_new = jnp.maximum(m_prev, jnp.max(t, axis=-1, keepdims=True) + cq)
        p = jnp.exp2(t + (cq - m_new))
        alpha = jnp.exp2(m_prev - m_new)
        l_s[...] = alpha * l_s[...] + jnp.sum(p, axis=-1, keepdims=True)
        acc_s[...] = alpha * acc_s[...] + _dot(_bf(p), _bf(v_ref[...]))
        m_s[...] = jnp.broadcast_to(m_new, m_s.shape)

    edge = (ki == qi) | (ki == 0)
    pl.when(edge)(lambda: accumulate(True))
    pl.when(jnp.logical_not(edge))(lambda: accumulate(False))

    @pl.when(ki == qi)
    def _():
        rows = qi * tq + _iota2((tq, 1), 0)
        o = acc_s[...] / l_s[...] * _sigmoid(fz_ref[...])
        o_ref[...] = jnp.where(rows >= first_valid, o, 0.0).astype(o_ref.dtype)


def fox_prompt(p, cq, ck, tp, first_valid):
    tq = _pick(tp, (ROW_TILE, 512, 384, 256, 128))
    nq = tp // tq
    assert first_valid < tq
    pairs = [(qi, ki) for qi in range(nq) for ki in range(qi + 1)]
    qi_arr = jnp.array([a for a, _ in pairs], jnp.int32)
    ki_arr = jnp.array([b for _, b in pairs], jnp.int32)
    cb = lambda col, h: col // FOX_DH + h
    return pl.pallas_call(
        functools.partial(_fox_flash_kernel, tq, first_valid),
        grid_spec=pltpu.PrefetchScalarGridSpec(
            num_scalar_prefetch=2,
            grid=(FOX_HEADS, len(pairs)),
            in_specs=[
                pl.BlockSpec((tq, FOX_DH), lambda h, t, qa, ka: (qa[t], cb(COL_FQ, h))),
                pl.BlockSpec((tq, FOX_DH), lambda h, t, qa, ka: (ka[t], cb(COL_FK, h))),
                pl.BlockSpec((tq, FOX_DH), lambda h, t, qa, ka: (ka[t], cb(COL_FV, h))),
                pl.BlockSpec((1, tq, LANES), lambda h, t, qa, ka: (h, qa[t], 0)),
                pl.BlockSpec((1, 1, tq), lambda h, t, qa, ka: (h, 0, ka[t])),
                pl.BlockSpec((tq, FOX_DH), lambda h, t, qa, ka: (qa[t], cb(COL_FZ, h))),
            ],
            out_specs=pl.BlockSpec((tq, FOX_DH), lambda h, t, qa, ka: (qa[t], h)),
            scratch_shapes=[pltpu.VMEM((tq, LANES), F32), pltpu.VMEM((tq, LANES), F32),
                            pltpu.VMEM((tq, FOX_DH), F32)],
        ),
        out_shape=jax.ShapeDtypeStruct((tp, FOX_W), BF16),
        compiler_params=_cparams(("parallel", "arbitrary")),
        name="fox_prompt",
    )(qi_arr, ki_arr, p, p, p, cq, ck.reshape(FOX_HEADS, 1, -1), p)


def _fox_sample_kernel(G, ngroups, pt_ref, q_ref, *refs):
    k_refs, v_refs, lft_refs = refs[0:G], refs[G:2 * G], refs[2 * G:3 * G]
    lfn_ref, kn_ref, vn_ref, fz_ref, o_ref, cq_s, cn_s, carry_s, m_s, l_s, acc_s = refs[3 * G:]
    jg = pl.program_id(1)
    rows = q_ref.shape[1]
    rep = rows // FOX_HEADS
    page = lft_refs[0].shape[2]
    flat = page * FOX_HEADS
    nn = lfn_ref.shape[2]
    scale = FOX_DH ** -0.5

    @pl.when(jg == 0)
    def _():
        upper = (_iota2((nn, nn), 0) <= _iota2((nn, nn), 1)).astype(F32)
        cn = _dot_rhs01(lfn_ref[0], upper)
        cn_s[...] = cn
        own_q = _iota2((rows, nn), 1) == (_iota2((rows, nn), 0) & (rep - 1))
        cq_s[...] = jnp.broadcast_to(jnp.sum(jnp.where(own_q, cn, 0.0), axis=-1, keepdims=True), cq_s.shape)
        carry_s[...] = jnp.zeros(carry_s.shape, F32)
        m_s[...] = jnp.full(m_s.shape, NEG_INF, F32)
        l_s[...] = jnp.zeros(l_s.shape, F32)
        acc_s[...] = jnp.zeros(acc_s.shape, F32)

    def update(ts, vals_bf):
        cq = cq_s[:, 0:1]
        m_prev = m_s[:, 0:1]
        t_max = functools.reduce(jnp.maximum, ts)
        m_new = jnp.maximum(m_prev, jnp.max(t_max, axis=-1, keepdims=True) + cq)
        shift = cq - m_new
        ps = [jnp.exp(t + shift) for t in ts]
        alpha = jnp.exp(m_prev - m_new)
        l_s[...] = alpha * l_s[...] + jnp.sum(functools.reduce(jnp.add, ps), axis=-1, keepdims=True)
        pv = functools.reduce(jnp.add, [_dot(_bf(p), v) for p, v in zip(ps, vals_bf)])
        acc_s[...] = alpha * acc_s[...] + pv
        m_s[...] = jnp.broadcast_to(m_new, m_s.shape)

    own = _div(_iota2((rows, flat), 0), rep) == (_iota2((rows, flat), 1) & (FOX_HEADS - 1))
    later = (_iota2((page, flat), 0) > _div(_iota2((page, flat), 1), FOX_HEADS)).astype(BF16)
    head_col = _iota2((FOX_HEADS, flat), 0) == (_iota2((FOX_HEADS, flat), 1) & (FOX_HEADS - 1))
    carry = carry_s[:, 0:1]
    q_bf = _bf(q_ref[0])
    ts = []
    for g in range(G):
        lft = lft_refs[g][0]
        suffix = _dot_rhs01(lft, later)
        d = jnp.sum(jnp.where(head_col, suffix + carry, 0.0), axis=0, keepdims=True)
        carry = carry + jnp.sum(lft, axis=-1, keepdims=True)
        s = _dot_nt(q_bf, _bf(k_refs[g][0])) * scale
        ts.append(jnp.where(own, s + d, NEG_INF))
    update(ts, [_bf(v_refs[g][0]) for g in range(G)])
    carry_s[...] = jnp.broadcast_to(carry, carry_s.shape)

    @pl.when(jg == ngroups - 1)
    def _():
        nflat = kn_ref.shape[1]
        spread = (_iota2((nn, nflat), 0) == _div(_iota2((nn, nflat), 1), FOX_HEADS)).astype(F32)
        cn_cols = _dot_rhs01(cn_s[...], spread)
        ri = _iota2((rows, nflat), 0)
        ci = _iota2((rows, nflat), 1)
        ok = (_div(ri, rep) == (ci & (FOX_HEADS - 1))) & (_div(ci, FOX_HEADS) <= (ri & (rep - 1)))
        sn = _dot_nt(_bf(q_ref[0]), _bf(kn_ref[0])) * scale
        update([jnp.where(ok, sn - cn_cols, NEG_INF)], [_bf(vn_ref[0])])
        o_ref[0] = acc_s[...] / l_s[...] * _sigmoid(fz_ref[0])


def fox_sample(page_table_flat, q_rows, cache_k, cache_v, cache_lft, lfn, kn_flat, vn_flat, fz_rows, nb, npages):
    flat = cache_k.shape[1]
    page = cache_lft.shape[2]
    rows = q_rows.shape[1]
    nn = lfn.shape[2]
    G = _pick(npages, (FOX_GROUP, 2, 1))
    ngroups = npages // G

    def page_map(g):
        return lambda b, jg, pt: (pt[b * npages + (npages - 1 - (jg * G + g))], 0, 0)

    seq = lambda b, jg, pt: (b, 0, 0)
    row_blk = pl.BlockSpec((1, rows, FOX_DH), seq)
    new_blk = pl.BlockSpec((1, kn_flat.shape[1], FOX_DH), seq)
    in_specs = [row_blk]
    in_specs += [pl.BlockSpec((1, flat, FOX_DH), page_map(g)) for g in range(G)]
    in_specs += [pl.BlockSpec((1, flat, FOX_DH), page_map(g)) for g in range(G)]
    in_specs += [pl.BlockSpec((1, FOX_HEADS, page), page_map(g)) for g in range(G)]
    in_specs += [pl.BlockSpec((1, rows, nn), seq), new_blk, new_blk, row_blk]
    return pl.pallas_call(
        functools.partial(_fox_sample_kernel, G, ngroups),
        grid_spec=pltpu.PrefetchScalarGridSpec(
            num_scalar_prefetch=1,
            grid=(nb, ngroups),
            in_specs=in_specs,
            out_specs=row_blk,
            scratch_shapes=[
                pltpu.VMEM((rows, LANES), F32),
                pltpu.VMEM((rows, nn), F32),
                pltpu.VMEM((FOX_HEADS, LANES), F32),
                pltpu.VMEM((rows, LANES), F32),
                pltpu.VMEM((rows, LANES), F32),
                pltpu.VMEM((rows, FOX_DH), F32),
            ],
        ),
        out_shape=jax.ShapeDtypeStruct((nb, rows, FOX_DH), F32),
        compiler_params=_cparams(("parallel", "arbitrary")),
        name="fox_sample",
    )(page_table_flat, q_rows, *([cache_k] * G), *([cache_v] * G), *([cache_lft] * G), lfn, kn_flat, vn_flat, fz_rows)


def _rwkv_mix_kernel(tm, tp, ns, nq, h_ref, hb_ref, st_ref, mu_ref, *o_refs):
    i = pl.program_id(0)
    h = h_ref[...]
    above = jnp.concatenate([hb_ref[SUBLANES - 1:SUBLANES, :], h[:tm - 1]], axis=0)
    row = i * tm + _iota2((tm, 1), 0)
    seq_start = (row >= tp) & (row < tp + ns) & (((row - tp) & (nq - 1)) == 0)
    prev = jnp.where(seq_start, st_ref[...], jnp.where(row == 0, 0.0, above))
    xx = prev - h
    mu = mu_ref[...]
    for j, o_ref in enumerate(o_refs):
        o_ref[...] = (h + xx * mu[j:j + 1]).astype(o_ref.dtype)


def rwkv_mix(h, shift_rows, mu, tp, ns, nq):
    m, d = h.shape
    assert nq & (nq - 1) == 0
    tm = _pick(math.gcd(m, tp), (320, 256, 128))
    first = tp // tm
    spec = pl.BlockSpec((tm, d), lambda i: (i, 0))
    return pl.pallas_call(
        functools.partial(_rwkv_mix_kernel, tm, tp, ns, nq),
        grid=(m // tm,),
        in_specs=[spec,
                  pl.BlockSpec((SUBLANES, d), lambda i: (jnp.maximum(i * (tm // SUBLANES) - 1, 0), 0)),
                  pl.BlockSpec((tm, d), lambda i: (jnp.maximum(i - first, 0), 0)),
                  pl.BlockSpec((6, d), lambda i: (0, 0))],
        out_specs=[spec] * 6,
        out_shape=[jax.ShapeDtypeStruct((m, d), BF16)] * 6,
        compiler_params=_cparams(("parallel",)),
        name="rwkv_mix",
    )(h, h, shift_rows, mu)


def _head_ones():
    return (_div(_iota2((LANES, LANES), 0), RWKV_HEAD) == _div(_iota2((LANES, LANES), 1), RWKV_HEAD)).astype(F32)


def _rwkv_chunk_terms(C, valid, r, k, v, wl, al, prm):
    HD = RWKV_HEAD
    m0 = _iota2((1, LANES), 1) < HD
    bones = _head_ones()
    w0, a0, k_k, k_a, r_k = (prm[i:i + 1] for i in (PRM_W0, PRM_A0, PRM_KK, PRM_KA, PRM_RK))
    wlog = -_softplus(-(w0 + wl)) - 0.5
    lw = jnp.where(valid, -jnp.exp(wlog), 0.0)
    a = _sigmoid(a0 + al)
    kkr = k * k_k
    kk = kkr * lax.rsqrt(_dot_rhs01(kkr * kkr, bones) + L2_EPS)
    k2 = k * (1.0 + (a - 1.0) * k_a)
    bonus = _dot_rhs01(r * k2 * r_k, bones) * v
    rm = jnp.where(valid, r, 0.0)
    k2 = jnp.where(valid, k2, 0.0)
    vm = jnp.where(valid, v, 0.0)
    av = jnp.where(valid, -kk, 0.0)
    bv = jnp.where(valid, kk * a, 0.0)

    ri = _iota2((C, C), 0)
    ci = _iota2((C, C), 1)
    yield
    cum = _dot_lhs01((ri >= ci).astype(F32), lw)
    cum_last = cum[C - 1:C, :]
    inv = jnp.exp(-cum)
    rt = rm * jnp.exp(cum)
    at = av * jnp.exp(cum - lw)
    bt = bv * inv
    kt = k2 * inv
    to_end = jnp.exp(cum_last - cum)
    b_end = bv * to_end
    k_end = k2 * to_end

    def split(x):
        return jnp.concatenate([jnp.where(m0, x, 0.0), jnp.where(m0, 0.0, x)], axis=0)

    def halves(x):
        return x[0:C] + x[C:2 * C]

    yield
    at_s = split(at)
    gram = _dot_nt(_bf(jnp.concatenate([at_s, split(rt)], axis=0)), _bf(jnp.concatenate([bt, kt], axis=0)))
    r2 = _iota2((2 * C, 2 * C), 0)
    c2 = _iota2((2 * C, 2 * C), 1)
    same = _div(r2, C) == _div(c2, C)
    bd_strict = same & (r2 > c2)
    bd_incl = same & (r2 >= c2)

    def bd(block, mask):
        return jnp.where(mask, jnp.concatenate([block, block], axis=1), 0.0)

    a_ab = bd(gram[0:2 * C, 0:C], bd_strict)
    a_ak = bd(gram[0:2 * C, C:2 * C], bd_strict)
    r_b = bd(gram[2 * C:4 * C, 0:C], bd_incl)
    r_k2 = bd(gram[2 * C:4 * C, C:2 * C], bd_incl)
    vs = split(vm)
    akv = halves(_dot(_bf(a_ak), _bf(vs)))
    yield
    t_bd = yield from _tri_inv_steps(a_ab, C)
    tw = _dot(_bf(t_bd), _bf(jnp.concatenate([at_s, split(akv)], axis=1)))
    wt = halves(tw[:, :LANES])
    ut = halves(tw[:, LANES:])
    yield
    rp = rt + halves(_dot(_bf(r_b), _bf(split(wt))))
    y0 = halves(_dot(_bf(jnp.concatenate([r_b, r_k2], axis=1)), _bf(jnp.concatenate([split(ut), vs], axis=0))))
    eye = (_iota2((LANES, LANES), 0) == _iota2((LANES, LANES), 1)).astype(F32)
    m_mat = eye * jnp.exp(cum_last) + bones * _dot_tn(_bf(b_end), _bf(wt))
    n_mat = bones * _dot_tn(_bf(jnp.concatenate([b_end, k_end], axis=0)), _bf(jnp.concatenate([ut, vm], axis=0)))
    return rp, y0, bonus, m_mat, n_mat


def _rwkv_prep_kernel(C, chunk_of_step, first_valid, PG, r_ref, k_ref, v_ref, wl_ref, al_ref, prm_ref,
                      rp_ref, y0_ref, bo_ref, m_ref, n_ref):
    c = pl.program_id(0) if chunk_of_step else 0
    valid = (c * C + _iota2((C, 1), 0)) >= first_valid
    sls = [slice(i * LANES, (i + 1) * LANES) for i in range(PG)]
    terms = _run_interleaved([
        _rwkv_chunk_terms(C, valid, r_ref[:, sl], k_ref[:, sl], v_ref[:, sl], wl_ref[:, sl], al_ref[:, sl], prm_ref[:, sl])
        for sl in sls])
    for i, sl in enumerate(sls):
        rp, y0, bonus, m_mat, n_mat = terms[i]
        rp_ref[:, sl] = rp
        y0_ref[:, sl] = y0
        bo_ref[:, sl] = bonus
        m_ref[0, i] = m_mat
        n_ref[0, i] = n_mat


def _rwkv_scan_kernel(NC, rp_ref, y0_ref, bo_ref, g_ref, m_ref, n_ref, prm_ref, s0_ref, o_ref, sout_ref, H):
    c = pl.program_id(1)
    HD = RWKV_HEAD
    bones = _head_ones()

    @pl.when(c == 0)
    def _():
        zero = jnp.zeros((HD, HD), F32)
        for i in range(RWKV_PAIRS):
            top = jnp.concatenate([s0_ref[0, 2 * i], zero], axis=1)
            bot = jnp.concatenate([zero, s0_ref[0, 2 * i + 1]], axis=1)
            H[i] = jnp.concatenate([top, bot], axis=0).T

    def pair_steps(i):
        sl = slice(i * LANES, (i + 1) * LANES)
        h_bd = H[i]
        y = _dot(_bf(rp_ref[:, sl]), _bf(h_bd)) + y0_ref[:, sl]
        H[i] = _dot3(m_ref[0, i], h_bd) + n_ref[0, i]
        yield
        mean = _dot_rhs01(y, bones) * (1.0 / HD)
        dlt = y - mean
        yield
        var = _dot_rhs01(dlt * dlt, bones) * (1.0 / HD)
        yn = dlt * lax.rsqrt(var + GN_EPS) * prm_ref[PRM_LNW:PRM_LNW + 1, sl] + prm_ref[PRM_LNB:PRM_LNB + 1, sl]
        o_ref[:, sl] = ((yn + bo_ref[:, sl]) * g_ref[:, sl]).astype(o_ref.dtype)

    _run_interleaved([pair_steps(i) for i in range(RWKV_PAIRS)])

    @pl.when(c == NC - 1)
    def _():
        for i in range(RWKV_PAIRS):
            ht = H[i].T
            sout_ref[0, 2 * i] = ht[0:HD, 0:HD]
            sout_ref[0, 2 * i + 1] = ht[HD:2 * HD, HD:2 * HD]


def rwkv(r, k, v, wl, al, g, prm, s0, *, B, NC, C, first_valid, name):
    assert B == 1 or NC == 1
    nblk = B * NC
    rows = nblk * C
    PG = RWKV_PREP_PAIRS
    wide = pl.BlockSpec((C, PG * LANES), lambda blk, pg: (blk, pg))
    mat = pl.BlockSpec((1, PG, LANES, LANES), lambda blk, pg: (blk, pg, 0, 0))
    mat_shape = jax.ShapeDtypeStruct((nblk, RWKV_PAIRS, LANES, LANES), F32)
    row_shape = jax.ShapeDtypeStruct((rows, D_MODEL), F32)
    rp, y0, bonus, m_all, n_all = pl.pallas_call(
        functools.partial(_rwkv_prep_kernel, C, B == 1, first_valid, PG),
        grid=(nblk, RWKV_PAIRS // PG),
        in_specs=[wide] * 5 + [pl.BlockSpec((SUBLANES, PG * LANES), lambda blk, pg: (0, pg))],
        out_specs=[wide, wide, wide, mat, mat],
        out_shape=[row_shape, row_shape, row_shape, mat_shape, mat_shape],
        compiler_params=_cparams(("parallel", "parallel")),
        name=name + "_terms",
    )(r, k, v, wl, al, prm)

    full = pl.BlockSpec((C, D_MODEL), lambda b, c: (b * NC + c, 0))
    mats = pl.BlockSpec((1, RWKV_PAIRS, LANES, LANES), lambda b, c: (b * NC + c, 0, 0, 0))
    st_spec = pl.BlockSpec((1, 2 * RWKV_PAIRS, RWKV_HEAD, RWKV_HEAD), lambda b, c: (b, 0, 0, 0))
    return pl.pallas_call(
        functools.partial(_rwkv_scan_kernel, NC),
        grid=(B, NC),
        in_specs=[full] * 4 + [mats, mats, pl.BlockSpec((SUBLANES, D_MODEL), lambda b, c: (0, 0)), st_spec],
        out_specs=[full, st_spec],
        out_shape=[
            jax.ShapeDtypeStruct((rows, D_MODEL), BF16),
            jax.ShapeDtypeStruct((B, 2 * RWKV_PAIRS, RWKV_HEAD, RWKV_HEAD), F32),
        ],
        scratch_shapes=[pltpu.VMEM((RWKV_PAIRS, LANES, LANES), F32)],
        compiler_params=_cparams(("parallel", "arbitrary")),
        name=name + "_scan",
    )(rp, y0, bonus, g, m_all, n_all, prm, s0)


def _pad_lanes(vec, offset):
    out = jnp.zeros((1, LANES), F32)
    return lax.dynamic_update_slice(out, vec.reshape(1, -1).astype(F32), (0, offset))


def _sample_rows(arr, row0, nb, nq, front):
    cols = arr.shape[1]
    s = arr[row0:row0 + nb * nq].reshape(nb, nq, cols)
    s = jnp.pad(s, ((0, 0), (front, 0), (0, 0)))
    return s.reshape(nb * (front + nq), cols)


def _heads_first(arr, row0, nb, nq, col):
    s = arr[row0:row0 + nb * nq, col:col + FOX_W].reshape(nb, nq, FOX_HEADS, FOX_DH)
    return jnp.transpose(s, (0, 2, 1, 3)).reshape(nb, FOX_HEADS * nq, FOX_DH)


def _only(x):
    assert x.shape[0] == 1
    return x.reshape(x.shape[1:])


def kernel(x_prompt, x_sample, cache_fox_k, cache_fox_v, cache_fox_logf, state_gdn_conv, state_gdn_S,
           state_rwkv_shift, state_rwkv_S, page_table, meta_tokens, ln_mix, ln_mlp, ln_final,
           w_in0, gdn_conv_w, gdn_A_log, gdn_dt_bias, gdn_norm_w, fox_b_f, w_out0,
           rwkv_mu, rwkv_w0, rwkv_w1, rwkv_w2, rwkv_a0, rwkv_a1, rwkv_a2, rwkv_g1, rwkv_g2,
           rwkv_k_k, rwkv_k_a, rwkv_r_k, rwkv_w_r, rwkv_w_k, rwkv_w_v, rwkv_w_o, rwkv_ln_w, rwkv_ln_b,
           w_up, w_down):
    D = D_MODEL
    assert x_prompt.shape[0] == 1 and x_prompt.shape[2] == D
    seq = x_prompt.shape[1]
    nb, nq = x_sample.shape[0], x_sample.shape[1]
    npages = page_table.shape[1]
    tprompt = N_META + seq
    pad = (-tprompt) % LANES
    tp = tprompt + pad
    ns = nb * nq
    R = -(-(tp + ns) // ROW_TILE) * ROW_TILE
    CS = SUBLANES
    front = CS - nq
    assert 3 <= front

    x0 = jnp.concatenate([jnp.zeros((pad, D), F32), meta_tokens.astype(F32), x_prompt[0],
                          x_sample.reshape(ns, D), jnp.zeros((R - tp - ns, D), F32)], axis=0)

    w_in = w_in0[0]
    o_z = GDN_QKV
    o_a = o_z + GDN_QK
    o_b = o_a + GDN_HEADS
    o_fq = o_b + GDN_HEADS
    o_fk = o_fq + FOX_W
    o_fv = o_fk + FOX_W
    o_ff = o_fv + FOX_W
    o_fz = o_ff + FOX_HEADS
    w_big = jnp.concatenate([w_in[:, :o_a], w_in[:, o_fq:o_ff], w_in[:, o_fz:]], axis=1).astype(BF16)
    w_small = jnp.concatenate([w_in[:, o_a:o_fq], w_in[:, o_ff:o_fz],
                               jnp.zeros((D, LANES - 3 * GDN_HEADS), F32)], axis=1).astype(BF16)

    h0 = rmsnorm(x0, ln_mix[0], BF16)
    p = matmul(h0, w_big, name="in_proj")
    ps = matmul(h0, w_small, name="in_proj_small")

    alog_pad = _pad_lanes(gdn_A_log[0], LANE_A)
    dt_pad = _pad_lanes(gdn_dt_bias[0], LANE_A)
    bf_pad = _pad_lanes(fox_b_f[0], LANE_F)
    conv_w = gdn_conv_w[0]
    norm_w = gdn_norm_w[0].reshape(1, GDN_D)

    GC = 64
    o_gdn_p, s_gdn_p = gdn(p, p, ps, COL_Z // GDN_QK, conv_w, alog_pad, dt_pad, norm_w,
                           jnp.zeros((1, GDN_HEADS, GDN_D, GDN_D), F32),
                           B=1, NC=tp // GC, C=GC, first_valid=pad, name="gdn_prompt")
    qkv_s = p[tp:tp + ns, :GDN_QKV].reshape(nb, nq, GDN_QKV)
    qkv_ext = jnp.concatenate([jnp.zeros((nb, front - 3, GDN_QKV), F32), _only(state_gdn_conv), qkv_s], axis=1)
    z_ext = _sample_rows(p[:, COL_Z:COL_Z + GDN_QK], tp, nb, nq, front)
    ps_ext = _sample_rows(ps, tp, nb, nq, front)
    o_gdn_s, s_gdn_s = gdn(qkv_ext.reshape(nb * CS, GDN_QKV), z_ext, ps_ext, 0, conv_w, alog_pad, dt_pad, norm_w,
                           _only(state_gdn_S), B=nb, NC=1, C=CS, first_valid=front, name="gdn_sample")
    o_gdn_s = o_gdn_s.reshape(nb, CS, GDN_QK)[:, front:].reshape(ns, GDN_QK)

    lf, cq, ck = fox_prep(ps, bf_pad, pad)
    o_fox_p = fox_prompt(p, cq, ck, tp, pad)
    pt_flat = page_table.reshape(-1).astype(jnp.int32)
    lf_s = lf[tp:tp + ns, LANE_F:LANE_F + FOX_HEADS].reshape(nb, nq, FOX_HEADS)
    lfn = jnp.repeat(jnp.swapaxes(lf_s, 1, 2), nq, axis=1)
    lfn = jnp.pad(lfn, ((0, 0), (0, 0), (0, SUBLANES - nq)))
    pool = cache_fox_k.shape[1]
    page = cache_fox_k.shape[2]
    cache_k = _only(cache_fox_k).reshape(pool, page * FOX_HEADS, FOX_DH)
    cache_v = _only(cache_fox_v).reshape(pool, page * FOX_HEADS, FOX_DH)
    cache_lft = jnp.swapaxes(_only(cache_fox_logf), 1, 2)
    o_fox_s = fox_sample(pt_flat, _heads_first(p, tp, nb, nq, COL_FQ), cache_k, cache_v, cache_lft, lfn,
                         p[tp:tp + ns, COL_FK:COL_FK + FOX_W].reshape(nb, nq * FOX_HEADS, FOX_DH),
                         p[tp:tp + ns, COL_FV:COL_FV + FOX_W].reshape(nb, nq * FOX_HEADS, FOX_DH),
                         _heads_first(p, tp, nb, nq, COL_FZ), nb, npages)
    o_fox_s = jnp.transpose(o_fox_s.reshape(nb, FOX_HEADS, nq, FOX_DH), (0, 2, 1, 3)).reshape(ns, FOX_W).astype(BF16)

    mix = jnp.concatenate([
        jnp.concatenate([o_gdn_p, o_fox_p], axis=1),
        jnp.concatenate([o_gdn_s, o_fox_s], axis=1),
        jnp.zeros((R - tp - ns, 2 * GDN_QK), BF16)], axis=0)
    x1 = matmul(mix, w_out0[0].astype(BF16), res=x0, name="out_proj")
    u0 = matmul(rmsnorm(x1, ln_mlp[0], BF16), w_up[0].astype(BF16), act="relu2", out_dtype=BF16, name="mlp_up0")
    x2 = matmul(u0, w_down[0].astype(BF16), res=x1, name="mlp_down0")

    h1 = rmsnorm(x2, ln_mix[1], F32)
    h1_s = h1[tp:tp + ns].reshape(nb, nq, D)
    shift_rows = jnp.concatenate([jnp.repeat(state_rwkv_shift[0].astype(F32), nq, axis=0),
                                  jnp.zeros((R - tp - ns, D), F32)], axis=0)
    xr, xw, xk, xv, xa, xg = rwkv_mix(h1, shift_rows, rwkv_mu[0], tp, ns, nq)

    def pad_cols(w):
        return jnp.pad(w, ((0, 0), (0, LANES - w.shape[1]))).astype(BF16)

    def pad_rows(w):
        return jnp.pad(w, ((0, LANES - w.shape[0]), (0, 0))).astype(BF16)

    r_ = matmul(xr, rwkv_w_r[0].astype(BF16), name="rwkv_r")
    k_ = matmul(xk, rwkv_w_k[0].astype(BF16), name="rwkv_k")
    v_ = matmul(xv, rwkv_w_v[0].astype(BF16), name="rwkv_v")
    wl = matmul(matmul(xw, pad_cols(rwkv_w1[0]), act="tanh", out_dtype=BF16, name="rwkv_w1"), pad_rows(rwkv_w2[0]), name="rwkv_w2")
    al = matmul(matmul(xa, pad_cols(rwkv_a1[0]), out_dtype=BF16, name="rwkv_a1"), pad_rows(rwkv_a2[0]), name="rwkv_a2")
    g_ = matmul(matmul(xg, rwkv_g1[0].astype(BF16), act="sigmoid", out_dtype=BF16, name="rwkv_g1"), rwkv_g2[0].astype(BF16), name="rwkv_g2")

    prm = jnp.stack([rwkv_w0[0], rwkv_a0[0], rwkv_k_k[0], rwkv_k_a[0], rwkv_r_k[0].reshape(D), rwkv_ln_w[0], rwkv_ln_b[0],
                     jnp.zeros((D,), F32)], axis=0).astype(F32)
    RC = 64
    o_rw_p, s_rw_p = rwkv(r_, k_, v_, wl, al, g_, prm, jnp.zeros((1, 2 * RWKV_PAIRS, RWKV_HEAD, RWKV_HEAD), F32),
                          B=1, NC=tp // RC, C=RC, first_valid=pad, name="rwkv_prompt")
    sx = [_sample_rows(t, tp, nb, nq, front) for t in (r_, k_, v_, wl, al, g_)]
    o_rw_s, s_rw_s = rwkv(*sx, prm, _only(state_rwkv_S), B=nb, NC=1, C=CS, first_valid=front, name="rwkv_sample")
    o_rw_s = o_rw_s.reshape(nb, CS, D)[:, front:].reshape(ns, D)
    o_rw = jnp.concatenate([o_rw_p, o_rw_s, jnp.zeros((R - tp - ns, D), BF16)], axis=0)
    x3 = matmul(o_rw, rwkv_w_o[0].astype(BF16), res=x2, name="rwkv_o")
    u1 = matmul(rmsnorm(x3, ln_mlp[1], BF16), w_up[1].astype(BF16), act="relu2", out_dtype=BF16, name="mlp_up1")
    x4 = matmul(u1, w_down[1].astype(BF16), res=x3, name="mlp_down1")
    y = rmsnorm(x4, ln_final, F32)

    r0 = pad
    y_prompt = y[r0 + N_META:tp].reshape(1, seq, D)
    y_sample = y[tp:tp + ns].reshape(nb, nq, D)

    def kv_rows(col):
        blk = p[:, col:col + FOX_W]
        return (blk[r0:tp].reshape(1, 1, tprompt, FOX_HEADS, FOX_DH),
                blk[tp:tp + ns].reshape(1, nb, nq, FOX_HEADS, FOX_DH))

    fk_p, fk_s = kv_rows(COL_FK)
    fv_p, fv_s = kv_rows(COL_FV)
    lf8 = lf[:, LANE_F:LANE_F + FOX_HEADS]
    lf_p = lf8[r0:tp].reshape(1, 1, tprompt, FOX_HEADS)
    lf_sm = lf8[tp:tp + ns].reshape(1, nb, nq, FOX_HEADS)
    cb_p = p[tp - 3:tp, :GDN_QKV].reshape(1, 1, 3, GDN_QKV)
    cb_s = qkv_ext[:, CS - 3:][None]
    gs_p = s_gdn_p[None]
    gs_s = s_gdn_s[None]
    sh_p = h1[tp - 1].reshape(1, 1, D)
    sh_s = h1_s[:, nq - 1][None]
    rs_p = s_rw_p[None]
    rs_s = s_rw_s[None]
    return (y_prompt, y_sample, fk_p, fk_s, fv_p, fv_s, lf_p, lf_sm, cb_p, cb_s, gs_p, gs_s, sh_p, sh_s, rs_p, rs_s)
```

```python
import functools
import math

import jax
import jax.numpy as jnp
from jax import lax
from jax.experimental import pallas as pl
from jax.experimental.pallas import tpu as pltpu

F32 = jnp.float32
BF16 = jnp.bfloat16

D_MODEL = 2048
N_META = 16
GDN_HEADS = 8
GDN_D = 128
GDN_QK = GDN_HEADS * GDN_D
GDN_QKV = 3 * GDN_QK
FOX_HEADS = 8
FOX_DH = 128
FOX_W = FOX_HEADS * FOX_DH
RWKV_HEAD = 64
RWKV_PAIRS = D_MODEL // (2 * RWKV_HEAD)
NORM_EPS = 1e-6
L2_EPS = 1e-6
GN_EPS = 64e-5
NEG_INF = -1e30

LANES = 128
SUBLANES = 8
ROW_TILE = 640
MM_ROW_TILE = 1792
LOG2E = 1.4426950408889634
VMEM_LIMIT = 48 * 1024 * 1024
FOX_GROUP = 8
RWKV_PREP_PAIRS = 8

COL_Z = GDN_QKV
COL_FQ = COL_Z + GDN_QK
COL_FK = COL_FQ + FOX_W
COL_FV = COL_FK + FOX_W
COL_FZ = COL_FV + FOX_W
LANE_A = 0
LANE_B = 8
LANE_F = 16
PRM_W0, PRM_A0, PRM_KK, PRM_KA, PRM_RK, PRM_LNW, PRM_LNB = range(7)


def _pick(n, cands):
    for c in cands:
        if n % c == 0:
            return c
    raise ValueError(f"no tile for {n}")


def _cparams(sem):
    return pltpu.CompilerParams(dimension_semantics=sem, vmem_limit_bytes=VMEM_LIMIT)


def _dot(a, b):
    return jnp.dot(a, b, preferred_element_type=F32)


def _dot_nt(a, b):
    return lax.dot_general(a, b, (((1,), (1,)), ((), ())), preferred_element_type=F32)


def _dot_tn(a, b):
    return lax.dot_general(a, b, (((0,), (0,)), ((), ())), preferred_element_type=F32)


def _bf(x):
    return x.astype(BF16)


def _softplus(x):
    return jnp.maximum(x, 0.0) + jnp.log(1.0 + jnp.exp(-jnp.abs(x)))


def _sigmoid(x):
    return 1.0 / (1.0 + jnp.exp(-x))


def _iota2(shape, dim):
    return lax.broadcasted_iota(jnp.int32, shape, dim)


def _div(x, n):
    assert n & (n - 1) == 0
    return x >> (n.bit_length() - 1)


def _split2(x):
    hi = x.astype(BF16)
    return hi, (x - hi.astype(F32)).astype(BF16)


def _split3(x):
    p1 = x.astype(BF16)
    rem = x - p1.astype(F32)
    p2 = rem.astype(BF16)
    return p1, p2, (rem - p2.astype(F32)).astype(BF16)


def _dot3(a, b):
    m = a.shape[0]
    ah, al = _split2(a)
    bh, bl = _split2(b)
    top = _dot(jnp.concatenate([ah, al], axis=0), bh)
    return (top[:m] + top[m:]) + _dot(ah, bl)


def _dot_lhs01(a01, b):
    a = a01.astype(BF16)
    b1, b2, b3 = _split3(b)
    return _dot(a, b1) + (_dot(a, b2) + _dot(a, b3))


def _dot_rhs01(a, b01):
    b = b01.astype(BF16)
    a1, a2, a3 = _split3(a)
    m = a.shape[0]
    out = _dot(jnp.concatenate([a1, a2, a3], axis=0), b)
    return out[:m] + (out[m:2 * m] + out[2 * m:])


def _tri_inv_steps(n_mat, n):
    size = n_mat.shape[0]
    eye = (_iota2((size, size), 0) == _iota2((size, size), 1)).astype(F32)
    t = eye + n_mat
    if n <= 2:
        return t
    p = _dot(_bf(n_mat), _bf(n_mat))
    yield
    m = 2
    while m < n:
        p_bf = _bf(p)
        if 2 * m < n:
            both = _dot(_bf(jnp.concatenate([p, t], axis=0)), p_bf)
            p = both[:size]
            t = t + both[size:]
        else:
            t = t + _dot(_bf(t), p_bf)
        m *= 2
        yield
    return t


def _run_interleaved(gens):
    results = [None] * len(gens)
    live = list(enumerate(gens))
    while live:
        nxt = []
        for idx, gen in live:
            try:
                next(gen)
                nxt.append((idx, gen))
            except StopIteration as stop:
                results[idx] = stop.value
        live = nxt
    return results


def _rmsnorm_kernel(x_ref, w_ref, o_ref):
    x = x_ref[...]
    ms = jnp.mean(x * x, axis=-1, keepdims=True)
    o_ref[...] = (x * lax.rsqrt(ms + NORM_EPS) * w_ref[...]).astype(o_ref.dtype)


def rmsnorm(x, w, out_dtype):
    m, d = x.shape
    tm = _pick(m, (320, 256, 128))
    return pl.pallas_call(
        _rmsnorm_kernel,
        grid=(m // tm,),
        in_specs=[pl.BlockSpec((tm, d), lambda i: (i, 0)), pl.BlockSpec((1, d), lambda i: (0, 0))],
        out_specs=pl.BlockSpec((tm, d), lambda i: (i, 0)),
        out_shape=jax.ShapeDtypeStruct((m, d), out_dtype),
        compiler_params=_cparams(("parallel",)),
        name="rmsnorm",
    )(x, w.reshape(1, d))


def _mm_kernel(nk, act, has_res, *refs):
    if has_res:
        x_ref, w_ref, r_ref, o_ref = refs[:4]
        scr = refs[4:]
    else:
        x_ref, w_ref, o_ref = refs[:3]
        r_ref = None
        scr = refs[3:]

    def finish(acc):
        if act == "relu2":
            acc = jnp.square(jnp.maximum(acc, 0.0))
        elif act == "tanh":
            acc = jnp.tanh(acc)
        elif act == "sigmoid":
            acc = _sigmoid(acc)
        if has_res:
            acc = r_ref[...] + acc
        o_ref[...] = acc.astype(o_ref.dtype)

    if nk == 1:
        finish(_dot(x_ref[...], w_ref[...]))
    else:
        acc_ref = scr[0]
        k = pl.program_id(2)

        @pl.when(k == 0)
        def _():
            acc_ref[...] = jnp.zeros(acc_ref.shape, F32)

        acc_ref[...] += _dot(x_ref[...], w_ref[...])

        @pl.when(k == nk - 1)
        def _():
            finish(acc_ref[...])


def matmul(x, w, *, act=None, res=None, out_dtype=F32, name="matmul"):
    m, k = x.shape
    n = w.shape[1]
    tm = _pick(m, (MM_ROW_TILE, ROW_TILE, 512, 256, 128))
    tn = _pick(n, (512, 256, 128))
    tk = k if k <= 2048 else 2048
    nk = k // tk
    in_specs = [pl.BlockSpec((tm, tk), lambda i, j, kk: (i, kk)), pl.BlockSpec((tk, tn), lambda i, j, kk: (kk, j))]
    args = [x, w]
    if res is not None:
        in_specs.append(pl.BlockSpec((tm, tn), lambda i, j, kk: (i, j)))
        args.append(res)
    return pl.pallas_call(
        functools.partial(_mm_kernel, nk, act, res is not None),
        grid=(m // tm, n // tn, nk),
        in_specs=in_specs,
        out_specs=pl.BlockSpec((tm, tn), lambda i, j, kk: (i, j)),
        out_shape=jax.ShapeDtypeStruct((m, n), out_dtype),
        scratch_shapes=[pltpu.VMEM((tm, tn), F32)] if nk > 1 else [],
        compiler_params=_cparams(("parallel", "parallel", "arbitrary")),
        name=name,
    )(*args)


def _gdn_kernel(C, NC, first_valid, qkv_ref, z_ref, ps_ref, cw_ref, alog_ref, dt_ref, nw_ref, s0_ref,
                o_ref, sout_ref, S, ext):
    c = pl.program_id(1)

    @pl.when(c == 0)
    def _():
        S[...] = s0_ref[0]
        ext[0:SUBLANES, :] = jnp.zeros((SUBLANES, GDN_QKV), F32)

    x = qkv_ref[...]
    ext[SUBLANES:SUBLANES + C, :] = x
    cw = cw_ref[...]
    y = ((ext[5:5 + C, :] * cw[0:1] + ext[6:6 + C, :] * cw[1:2]) + ext[7:7 + C, :] * cw[2:3]) + ext[8:8 + C, :] * cw[3:4]
    ext[0:SUBLANES, :] = x[C - SUBLANES:C]
    y = y * _sigmoid(y)

    rows = c * C + _iota2((C, 1), 0)
    valid = rows >= first_valid
    ps = ps_ref[...]
    g_all = jnp.where(valid, -jnp.exp(alog_ref[...]) * _softplus(ps + dt_ref[...]), 0.0)
    beta_all = jnp.where(valid, _sigmoid(ps), 0.0)
    ri = _iota2((C, C), 0)
    ci = _iota2((C, C), 1)
    causal = ri >= ci
    strict = ri > ci
    gc = _dot_lhs01(causal.astype(F32), g_all)
    gct = _dot_rhs01(g_all.T, (ri <= ci).astype(F32))
    nw = nw_ref[...]

    def head_steps(h):
        sl = slice(h * GDN_D, (h + 1) * GDN_D)
        qh = y[:, sl]
        kh = y[:, GDN_QK + h * GDN_D:GDN_QK + (h + 1) * GDN_D]
        vh = jnp.where(valid, y[:, 2 * GDN_QK + h * GDN_D:2 * GDN_QK + (h + 1) * GDN_D], 0.0)
        qh = jnp.where(valid, qh * lax.rsqrt(jnp.sum(qh * qh, axis=-1, keepdims=True) + L2_EPS) * GDN_D ** -0.5, 0.0)
        kh = jnp.where(valid, kh * lax.rsqrt(jnp.sum(kh * kh, axis=-1, keepdims=True) + L2_EPS), 0.0)
        bcol = beta_all[:, LANE_B + h:LANE_B + h + 1]
        gcol = gc[:, LANE_A + h:LANE_A + h + 1]
        grow = gct[LANE_A + h:LANE_A + h + 1, :]
        glast = gc[C - 1:C, LANE_A + h:LANE_A + h + 1]
        diff = gcol - grow
        decay = jnp.where(causal, jnp.exp(jnp.where(causal, diff, 0.0)), 0.0)
        kb = kh * bcol
        a_mat = jnp.where(strict, _dot_nt(_bf(kb), _bf(kh)) * decay, 0.0)
        qk = _dot_nt(_bf(qh), _bf(kh)) * decay
        eg = jnp.exp(gcol)
        rhs = jnp.concatenate([vh * bcol, kb * eg], axis=1)
        q_dec = qh * eg
        k_dec = kh * jnp.exp(glast - gcol)
        yield
        t_mat = yield from _tri_inv_steps(-a_mat, C)
        sol = _dot3(t_mat, rhs)
        u = sol[:, :GDN_D]
        w = sol[:, GDN_D:]
        yield
        s_h = S[h]
        s_bf = _bf(s_h)
        v_new = u - _dot(_bf(w), s_bf)
        o_state = _dot(_bf(q_dec), s_bf)
        yield
        o = o_state + _dot(_bf(qk), _bf(v_new))
        S[h] = s_h * jnp.exp(glast) + _dot_tn(_bf(k_dec), _bf(v_new))
        yield
        on = o * lax.rsqrt(jnp.mean(o * o, axis=-1, keepdims=True) + NORM_EPS) * nw
        zh = z_ref[:, sl]
        o_ref[:, sl] = (on * (zh * _sigmoid(zh))).astype(o_ref.dtype)

    _run_interleaved([head_steps(h) for h in range(GDN_HEADS)])

    @pl.when(c == NC - 1)
    def _():
        sout_ref[0] = S[...]


def gdn(qkv_arr, z_arr, ps_arr, z_col, conv_w, alog_pad, dt_pad, norm_w, s0, *, B, NC, C, first_valid, name):
    rows = B * NC * C
    return pl.pallas_call(
        functools.partial(_gdn_kernel, C, NC, first_valid),
        grid=(B, NC),
        in_specs=[
            pl.BlockSpec((C, GDN_QKV), lambda b, c: (b * NC + c, 0)),
            pl.BlockSpec((C, GDN_QK), lambda b, c: (b * NC + c, z_col)),
            pl.BlockSpec((C, LANES), lambda b, c: (b * NC + c, 0)),
            pl.BlockSpec((4, GDN_QKV), lambda b, c: (0, 0)),
            pl.BlockSpec((1, LANES), lambda b, c: (0, 0)),
            pl.BlockSpec((1, LANES), lambda b, c: (0, 0)),
            pl.BlockSpec((1, GDN_D), lambda b, c: (0, 0)),
            pl.BlockSpec((1, GDN_HEADS, GDN_D, GDN_D), lambda b, c: (b, 0, 0, 0)),
        ],
        out_specs=[
            pl.BlockSpec((C, GDN_QK), lambda b, c: (b * NC + c, 0)),
            pl.BlockSpec((1, GDN_HEADS, GDN_D, GDN_D), lambda b, c: (b, 0, 0, 0)),
        ],
        out_shape=[
            jax.ShapeDtypeStruct((rows, GDN_QK), BF16),
            jax.ShapeDtypeStruct((B, GDN_HEADS, GDN_D, GDN_D), F32),
        ],
        scratch_shapes=[pltpu.VMEM((GDN_HEADS, GDN_D, GDN_D), F32), pltpu.VMEM((C + SUBLANES, GDN_QKV), F32)],
        compiler_params=_cparams(("parallel", "arbitrary")),
        name=name,
    )(qkv_arr, z_arr, ps_arr, conv_w, alog_pad, dt_pad, norm_w, s0)


def _fox_prep_kernel(tb, first_valid, ps_ref, bf_ref, lf_ref, cq_ref, ck_ref, carry):
    i = pl.program_id(0)

    @pl.when(i == 0)
    def _():
        carry[...] = jnp.zeros(carry.shape, F32)

    x = ps_ref[...] + bf_ref[...]
    rows = i * tb + _iota2((tb, 1), 0)
    lf = jnp.where(rows >= first_valid, jnp.minimum(x, 0.0) - jnp.log(1.0 + jnp.exp(-jnp.abs(x))), 0.0)
    tri = (_iota2((tb, tb), 0) >= _iota2((tb, tb), 1)).astype(F32)
    c = _dot_lhs01(tri, lf) + carry[0:1, :]
    carry[...] = jnp.broadcast_to(c[tb - 1:tb, :], carry.shape)
    lf_ref[...] = lf
    c2 = c * LOG2E
    for h in range(FOX_HEADS):
        cq_ref[h] = jnp.broadcast_to(c2[:, LANE_F + h:LANE_F + h + 1], (tb, LANES))
    ck_ref[...] = c2.T[LANE_F:LANE_F + FOX_HEADS, :]


def fox_prep(ps, bf_pad, first_valid):
    r = ps.shape[0]
    tb = _pick(r, (ROW_TILE, 512, 256, 128))
    return pl.pallas_call(
        functools.partial(_fox_prep_kernel, tb, first_valid),
        grid=(r // tb,),
        in_specs=[pl.BlockSpec((tb, LANES), lambda i: (i, 0)), pl.BlockSpec((1, LANES), lambda i: (0, 0))],
        out_specs=[
            pl.BlockSpec((tb, LANES), lambda i: (i, 0)),
            pl.BlockSpec((FOX_HEADS, tb, LANES), lambda i: (0, i, 0)),
            pl.BlockSpec((FOX_HEADS, tb), lambda i: (0, i)),
        ],
        out_shape=[
            jax.ShapeDtypeStruct((r, LANES), F32),
            jax.ShapeDtypeStruct((FOX_HEADS, r, LANES), F32),
            jax.ShapeDtypeStruct((FOX_HEADS, r), F32),
        ],
        scratch_shapes=[pltpu.VMEM((SUBLANES, LANES), F32)],
        compiler_params=_cparams(("arbitrary",)),
        name="fox_prep",
    )(ps, bf_pad)


def _fox_flash_kernel(tq, first_valid, qi_ref, ki_ref, qt_ref, k_ref, vt_ref, ck_ref, cq_ref, fz_ref, o_ref,
                      m_s, l_s, acc_s, t_s):
    step = pl.program_id(1)
    qi = qi_ref[step]
    ki = ki_ref[step]
    reps = tq // LANES

    @pl.when(ki == 0)
    def _():
        m_s[...] = jnp.full(m_s.shape, NEG_INF, F32)
        l_s[...] = jnp.zeros(l_s.shape, F32)
        acc_s[...] = jnp.zeros(acc_s.shape, F32)

    def accumulate(masked):
        qt = _bf(qt_ref[0] * (FOX_DH ** -0.5 * LOG2E))
        cq = cq_ref[0]
        m_prev = m_s[0:1, :]
        kb = LANES
        groups = kb // SUBLANES
        mx = None
        for b in range(tq // kb):
            rs = slice(b * kb, (b + 1) * kb)
            t = _dot(_bf(k_ref[rs, :]), qt) - jnp.concatenate([ck_ref[0, rs, :]] * reps, axis=1)
            if masked:
                kpos = ki * tq + b * kb + _iota2((kb, tq), 0)
                qpos = qi * tq + _iota2((kb, tq), 1)
                t = jnp.where((kpos <= qpos) & (kpos >= first_valid), t, NEG_INF)
            t_s[rs, :] = t
            part = jnp.max(t.reshape(groups, SUBLANES, tq), axis=0)
            mx = part if mx is None else jnp.maximum(mx, part)
        m_new = jnp.maximum(m_prev, jnp.max(mx, axis=0, keepdims=True) + cq)
        shift = cq - m_new
        alpha = jnp.exp2(m_prev - m_new)
        lsum = None
        pv = None
        for b in range(tq // kb):
            rs = slice(b * kb, (b + 1) * kb)
            p = jnp.exp2(t_s[rs, :] + shift)
            part = jnp.sum(p.reshape(groups, SUBLANES, tq), axis=0)
            lsum = part if lsum is None else lsum + part
            d = _dot(_bf(vt_ref[0, :, rs]), _bf(p))
            pv = d if pv is None else pv + d
        l_new = alpha * l_s[0:1, :] + jnp.sum(lsum, axis=0, keepdims=True)
        l_s[...] = jnp.broadcast_to(l_new, l_s.shape)
        acc_s[...] = alpha * acc_s[...] + pv
        m_s[...] = jnp.broadcast_to(m_new, m_s.shape)

    edge = (ki == qi) | (ki == 0)
    pl.when(edge)(lambda: accumulate(True))
    pl.when(jnp.logical_not(edge))(lambda: accumulate(False))

    @pl.when(ki == qi)
    def _():
        rows = qi * tq + _iota2((tq, 1), 0)
        o = (acc_s[...] / l_s[0:1, :]).T * _sigmoid(fz_ref[...])
        o_ref[...] = jnp.where(rows >= first_valid, o, 0.0).astype(o_ref.dtype)


def fox_prompt(p, c_rep, c_row, tp, first_valid):
    tq = _pick(tp, (ROW_TILE, 512, 384, 256, 128))
    nq = tp // tq
    assert first_valid < tq
    pairs = [(qi, ki) for qi in range(nq) for ki in range(qi + 1)]
    qi_arr = jnp.array([a for a, _ in pairs], jnp.int32)
    ki_arr = jnp.array([b for _, b in pairs], jnp.int32)
    cb = lambda col, h: col // FOX_DH + h

    def heads_t(col):
        return jnp.transpose(p[:tp, col:col + FOX_W].reshape(tp, FOX_HEADS, FOX_DH), (1, 2, 0))

    return pl.pallas_call(
        functools.partial(_fox_flash_kernel, tq, first_valid),
        grid_spec=pltpu.PrefetchScalarGridSpec(
            num_scalar_prefetch=2,
            grid=(FOX_HEADS, len(pairs)),
            in_specs=[
                pl.BlockSpec((1, FOX_DH, tq), lambda h, t, qa, ka: (h, 0, qa[t])),
                pl.BlockSpec((tq, FOX_DH), lambda h, t, qa, ka: (ka[t], cb(COL_FK, h))),
                pl.BlockSpec((1, FOX_DH, tq), lambda h, t, qa, ka: (h, 0, ka[t])),
                pl.BlockSpec((1, tq, LANES), lambda h, t, qa, ka: (h, ka[t], 0)),
                pl.BlockSpec((1, 1, tq), lambda h, t, qa, ka: (h, 0, qa[t])),
                pl.BlockSpec((tq, FOX_DH), lambda h, t, qa, ka: (qa[t], cb(COL_FZ, h))),
            ],
            out_specs=pl.BlockSpec((tq, FOX_DH), lambda h, t, qa, ka: (qa[t], h)),
            scratch_shapes=[pltpu.VMEM((SUBLANES, tq), F32), pltpu.VMEM((SUBLANES, tq), F32),
                            pltpu.VMEM((FOX_DH, tq), F32), pltpu.VMEM((tq, tq), F32)],
        ),
        out_shape=jax.ShapeDtypeStruct((tp, FOX_W), BF16),
        compiler_params=_cparams(("parallel", "arbitrary")),
        name="fox_prompt",
    )(qi_arr, ki_arr, heads_t(COL_FQ), p, heads_t(COL_FV), c_rep, c_row.reshape(FOX_HEADS, 1, -1), p)


def _fox_sample_kernel(G, ngroups, pt_ref, q_ref, *refs):
    k_refs, v_refs, lft_refs = refs[0:G], refs[G:2 * G], refs[2 * G:3 * G]
    lfn_ref, kn_ref, vn_ref, fz_ref, o_ref, cq_s, cn_s, carry_s, m_s, l_s, acc_s = refs[3 * G:]
    jg = pl.program_id(1)
    rows = q_ref.shape[1]
    page = lft_refs[0].shape[2]
    flat = page * FOX_HEADS
    nn = lfn_ref.shape[2]
    scale = FOX_DH ** -0.5

    @pl.when(jg == 0)
    def _():
        upper = (_iota2((nn, nn), 0) <= _iota2((nn, nn), 1)).astype(F32)
        cn = _dot_rhs01(lfn_ref[0], upper)
        cn_s[...] = cn
        own_q = _iota2((rows, nn), 1) == _div(_iota2((rows, nn), 0), FOX_HEADS)
        cq_s[...] = jnp.broadcast_to(jnp.sum(jnp.where(own_q, cn, 0.0), axis=-1, keepdims=True), cq_s.shape)
        carry_s[...] = jnp.zeros(carry_s.shape, F32)
        m_s[...] = jnp.full(m_s.shape, NEG_INF, F32)
        l_s[...] = jnp.zeros(l_s.shape, F32)
        acc_s[...] = jnp.zeros(acc_s.shape, F32)

    def update(ts, vals_bf):
        cq = cq_s[:, 0:1]
        m_prev = m_s[:, 0:1]
        t_max = functools.reduce(jnp.maximum, ts)
        m_new = jnp.maximum(m_prev, jnp.max(t_max, axis=-1, keepdims=True) + cq)
        shift = cq - m_new
        ps = [jnp.exp(t + shift) for t in ts]
        alpha = jnp.exp(m_prev - m_new)
        l_s[...] = alpha * l_s[...] + jnp.sum(functools.reduce(jnp.add, ps), axis=-1, keepdims=True)
        pv = functools.reduce(jnp.add, [_dot(_bf(p), v) for p, v in zip(ps, vals_bf)])
        acc_s[...] = alpha * acc_s[...] + pv
        m_s[...] = jnp.broadcast_to(m_new, m_s.shape)

    own = (_iota2((rows, flat), 0) & (FOX_HEADS - 1)) == (_iota2((rows, flat), 1) & (FOX_HEADS - 1))
    later = (_iota2((page, flat), 0) > _div(_iota2((page, flat), 1), FOX_HEADS)).astype(BF16)
    head_col = _iota2((FOX_HEADS, flat), 0) == (_iota2((FOX_HEADS, flat), 1) & (FOX_HEADS - 1))
    carry = carry_s[:, 0:1]
    q_bf = _bf(q_ref[0])
    lfts = [lft_refs[g][0] for g in range(G)]
    suffixes = _dot_rhs01(jnp.concatenate(lfts, axis=0), later)
    ts = []
    for g in range(G):
        suffix = suffixes[g * FOX_HEADS:(g + 1) * FOX_HEADS]
        d = jnp.sum(jnp.where(head_col, suffix + carry, 0.0), axis=0, keepdims=True)
        carry = carry + jnp.sum(lfts[g], axis=-1, keepdims=True)
        s = _dot_nt(q_bf, _bf(k_refs[g][0])) * scale
        ts.append(jnp.where(own, s + d, NEG_INF))
    update(ts, [_bf(v_refs[g][0]) for g in range(G)])
    carry_s[...] = jnp.broadcast_to(carry, carry_s.shape)

    @pl.when(jg == ngroups - 1)
    def _():
        nflat = kn_ref.shape[1]
        spread = (_iota2((nn, nflat), 0) == _div(_iota2((nn, nflat), 1), FOX_HEADS)).astype(F32)
        cn_cols = _dot_rhs01(cn_s[...], spread)
        ri = _iota2((rows, nflat), 0)
        ci = _iota2((rows, nflat), 1)
        ok = ((ri & (FOX_HEADS - 1)) == (ci & (FOX_HEADS - 1))) & (_div(ci, FOX_HEADS) <= _div(ri, FOX_HEADS))
        sn = _dot_nt(_bf(q_ref[0]), _bf(kn_ref[0])) * scale
        update([jnp.where(ok, sn - cn_cols, NEG_INF)], [_bf(vn_ref[0])])
        o_ref[0] = acc_s[...] / l_s[...] * _sigmoid(fz_ref[0])


def fox_sample(page_table_flat, q_rows, cache_k, cache_v, cache_lft, lfn, kn_flat, vn_flat, fz_rows, nb, npages):
    flat = cache_k.shape[1]
    page = cache_lft.shape[2]
    rows = q_rows.shape[1]
    nn = lfn.shape[2]
    G = _pick(npages, (FOX_GROUP, 2, 1))
    ngroups = npages // G

    def page_map(g):
        return lambda b, jg, pt: (pt[b * npages + (npages - 1 - (jg * G + g))], 0, 0)

    seq = lambda b, jg, pt: (b, 0, 0)
    row_blk = pl.BlockSpec((1, rows, FOX_DH), seq)
    new_blk = pl.BlockSpec((1, kn_flat.shape[1], FOX_DH), seq)
    in_specs = [row_blk]
    in_specs += [pl.BlockSpec((1, flat, FOX_DH), page_map(g)) for g in range(G)]
    in_specs += [pl.BlockSpec((1, flat, FOX_DH), page_map(g)) for g in range(G)]
    in_specs += [pl.BlockSpec((1, FOX_HEADS, page), page_map(g)) for g in range(G)]
    in_specs += [pl.BlockSpec((1, rows, nn), seq), new_blk, new_blk, row_blk]
    return pl.pallas_call(
        functools.partial(_fox_sample_kernel, G, ngroups),
        grid_spec=pltpu.PrefetchScalarGridSpec(
            num_scalar_prefetch=1,
            grid=(nb, ngroups),
            in_specs=in_specs,
            out_specs=row_blk,
            scratch_shapes=[
                pltpu.VMEM((rows, LANES), F32),
                pltpu.VMEM((rows, nn), F32),
                pltpu.VMEM((FOX_HEADS, LANES), F32),
                pltpu.VMEM((rows, LANES), F32),
                pltpu.VMEM((rows, LANES), F32),
                pltpu.VMEM((rows, FOX_DH), F32),
            ],
        ),
        out_shape=jax.ShapeDtypeStruct((nb, rows, FOX_DH), F32),
        compiler_params=_cparams(("parallel", "arbitrary")),
        name="fox_sample",
    )(page_table_flat, q_rows, *([cache_k] * G), *([cache_v] * G), *([cache_lft] * G), lfn, kn_flat, vn_flat, fz_rows)


def _rwkv_mix_kernel(tm, tp, ns, nq, h_ref, hb_ref, st_ref, mu_ref, *o_refs):
    i = pl.program_id(0)
    h = h_ref[...]
    above = jnp.concatenate([hb_ref[SUBLANES - 1:SUBLANES, :], h[:tm - 1]], axis=0)
    row = i * tm + _iota2((tm, 1), 0)
    seq_start = (row >= tp) & (row < tp + ns) & (((row - tp) & (nq - 1)) == 0)
    prev = jnp.where(seq_start, st_ref[...], jnp.where(row == 0, 0.0, above))
    xx = prev - h
    mu = mu_ref[...]
    for j, o_ref in enumerate(o_refs):
        o_ref[...] = (h + xx * mu[j:j + 1]).astype(o_ref.dtype)


def rwkv_mix(h, shift_rows, mu, tp, ns, nq):
    m, d = h.shape
    assert nq & (nq - 1) == 0
    tm = _pick(math.gcd(m, tp), (320, 256, 128))
    first = tp // tm
    spec = pl.BlockSpec((tm, d), lambda i: (i, 0))
    return pl.pallas_call(
        functools.partial(_rwkv_mix_kernel, tm, tp, ns, nq),
        grid=(m // tm,),
        in_specs=[spec,
                  pl.BlockSpec((SUBLANES, d), lambda i: (jnp.maximum(i * (tm // SUBLANES) - 1, 0), 0)),
                  pl.BlockSpec((tm, d), lambda i: (jnp.maximum(i - first, 0), 0)),
                  pl.BlockSpec((6, d), lambda i: (0, 0))],
        out_specs=[spec] * 6,
        out_shape=[jax.ShapeDtypeStruct((m, d), BF16)] * 6,
        compiler_params=_cparams(("parallel",)),
        name="rwkv_mix",
    )(h, h, shift_rows, mu)


def _head_ones():
    return (_div(_iota2((LANES, LANES), 0), RWKV_HEAD) == _div(_iota2((LANES, LANES), 1), RWKV_HEAD)).astype(F32)


def _rwkv_chunk_terms(C, valid, r, k, v, wl, al, prm):
    HD = RWKV_HEAD
    m0 = _iota2((1, LANES), 1) < HD
    bones = _head_ones()
    w0, a0, k_k, k_a, r_k = (prm[i:i + 1] for i in (PRM_W0, PRM_A0, PRM_KK, PRM_KA, PRM_RK))
    wlog = -_softplus(-(w0 + wl)) - 0.5
    lw = jnp.where(valid, -jnp.exp(wlog), 0.0)
    a = _sigmoid(a0 + al)
    kkr = k * k_k
    kk = kkr * lax.rsqrt(_dot_rhs01(kkr * kkr, bones) + L2_EPS)
    k2 = k * (1.0 + (a - 1.0) * k_a)
    bonus = _dot_rhs01(r * k2 * r_k, bones) * v
    rm = jnp.where(valid, r, 0.0)
    k2 = jnp.where(valid, k2, 0.0)
    vm = jnp.where(valid, v, 0.0)
    av = jnp.where(valid, -kk, 0.0)
    bv = jnp.where(valid, kk * a, 0.0)

    ri = _iota2((C, C), 0)
    ci = _iota2((C, C), 1)
    yield
    cum = _dot_lhs01((ri >= ci).astype(F32), lw)
    cum_last = cum[C - 1:C, :]
    inv = jnp.exp(-cum)
    rt = rm * jnp.exp(cum)
    at = av * jnp.exp(cum - lw)
    bt = bv * inv
    kt = k2 * inv
    to_end = jnp.exp(cum_last - cum)
    b_end = bv * to_end
    k_end = k2 * to_end

    def split(x):
        return jnp.concatenate([jnp.where(m0, x, 0.0), jnp.where(m0, 0.0, x)], axis=0)

    def halves(x):
        return x[0:C] + x[C:2 * C]

    yield
    at_s = split(at)
    gram = _dot_nt(_bf(jnp.concatenate([at_s, split(rt)], axis=0)), _bf(jnp.concatenate([bt, kt], axis=0)))
    r2 = _iota2((2 * C, 2 * C), 0)
    c2 = _iota2((2 * C, 2 * C), 1)
    same = _div(r2, C) == _div(c2, C)
    bd_strict = same & (r2 > c2)
    bd_incl = same & (r2 >= c2)

    def bd(block, mask):
        return jnp.where(mask, jnp.concatenate([block, block], axis=1), 0.0)

    a_ab = bd(gram[0:2 * C, 0:C], bd_strict)
    a_ak = bd(gram[0:2 * C, C:2 * C], bd_strict)
    r_b = bd(gram[2 * C:4 * C, 0:C], bd_incl)
    r_k2 = bd(gram[2 * C:4 * C, C:2 * C], bd_incl)
    vs = split(vm)
    akv = halves(_dot(_bf(a_ak), _bf(vs)))
    yield
    t_bd = yield from _tri_inv_steps(a_ab, C)
    tw = _dot(_bf(t_bd), _bf(jnp.concatenate([at_s, split(akv)], axis=1)))
    wt = halves(tw[:, :LANES])
    ut = halves(tw[:, LANES:])
    yield
    rp = rt + halves(_dot(_bf(r_b), _bf(split(wt))))
    y0 = halves(_dot(_bf(jnp.concatenate([r_b, r_k2], axis=1)), _bf(jnp.concatenate([split(ut), vs], axis=0))))
    eye = (_iota2((LANES, LANES), 0) == _iota2((LANES, LANES), 1)).astype(F32)
    m_mat = eye * jnp.exp(cum_last) + bones * _dot_tn(_bf(b_end), _bf(wt))
    n_mat = bones * _dot_tn(_bf(jnp.concatenate([b_end, k_end], axis=0)), _bf(jnp.concatenate([ut, vm], axis=0)))
    return rp, y0, bonus, m_mat, n_mat


def _rwkv_prep_kernel(C, chunk_of_step, first_valid, PG, r_ref, k_ref, v_ref, wl_ref, al_ref, prm_ref,
                      rp_ref, y0_ref, bo_ref, m_ref, n_ref):
    c = pl.program_id(0) if chunk_of_step else 0
    valid = (c * C + _iota2((C, 1), 0)) >= first_valid
    sls = [slice(i * LANES, (i + 1) * LANES) for i in range(PG)]
    terms = _run_interleaved([
        _rwkv_chunk_terms(C, valid, r_ref[:, sl], k_ref[:, sl], v_ref[:, sl], wl_ref[:, sl], al_ref[:, sl], prm_ref[:, sl])
        for sl in sls])
    for i, sl in enumerate(sls):
        rp, y0, bonus, m_mat, n_mat = terms[i]
        rp_ref[:, sl] = rp
        y0_ref[:, sl] = y0
        bo_ref[:, sl] = bonus
        m_ref[0, i] = m_mat
        n_ref[0, i] = n_mat


def _rwkv_scan_kernel(NC, rp_ref, y0_ref, bo_ref, g_ref, m_ref, n_ref, prm_ref, s0_ref, o_ref, sout_ref, H):
    c = pl.program_id(1)
    HD = RWKV_HEAD
    bones = _head_ones()

    @pl.when(c == 0)
    def _():
        zero = jnp.zeros((HD, HD), F32)
        for i in range(RWKV_PAIRS):
            top = jnp.concatenate([s0_ref[0, 2 * i], zero], axis=1)
            bot = jnp.concatenate([zero, s0_ref[0, 2 * i + 1]], axis=1)
            H[i] = jnp.concatenate([top, bot], axis=0).T

    def pair_steps(i):
        sl = slice(i * LANES, (i + 1) * LANES)
        h_bd = H[i]
        y = _dot(_bf(rp_ref[:, sl]), _bf(h_bd)) + y0_ref[:, sl]
        H[i] = _dot3(m_ref[0, i], h_bd) + n_ref[0, i]
        yield
        mean = _dot_rhs01(y, bones) * (1.0 / HD)
        dlt = y - mean
        yield
        var = _dot_rhs01(dlt * dlt, bones) * (1.0 / HD)
        yn = dlt * lax.rsqrt(var + GN_EPS) * prm_ref[PRM_LNW:PRM_LNW + 1, sl] + prm_ref[PRM_LNB:PRM_LNB + 1, sl]
        o_ref[:, sl] = ((yn + bo_ref[:, sl]) * g_ref[:, sl]).astype(o_ref.dtype)

    _run_interleaved([pair_steps(i) for i in range(RWKV_PAIRS)])

    @pl.when(c == NC - 1)
    def _():
        for i in range(RWKV_PAIRS):
            ht = H[i].T
            sout_ref[0, 2 * i] = ht[0:HD, 0:HD]
            sout_ref[0, 2 * i + 1] = ht[HD:2 * HD, HD:2 * HD]


def rwkv(r, k, v, wl, al, g, prm, s0, *, B, NC, C, first_valid, name):
    assert B == 1 or NC == 1
    nblk = B * NC
    rows = nblk * C
    PG = RWKV_PREP_PAIRS
    wide = pl.BlockSpec((C, PG * LANES), lambda blk, pg: (blk, pg))
    mat = pl.BlockSpec((1, PG, LANES, LANES), lambda blk, pg: (blk, pg, 0, 0))
    mat_shape = jax.ShapeDtypeStruct((nblk, RWKV_PAIRS, LANES, LANES), F32)
    row_shape = jax.ShapeDtypeStruct((rows, D_MODEL), F32)
    rp, y0, bonus, m_all, n_all = pl.pallas_call(
        functools.partial(_rwkv_prep_kernel, C, B == 1, first_valid, PG),
        grid=(nblk, RWKV_PAIRS // PG),
        in_specs=[wide] * 5 + [pl.BlockSpec((SUBLANES, PG * LANES), lambda blk, pg: (0, pg))],
        out_specs=[wide, wide, wide, mat, mat],
        out_shape=[row_shape, row_shape, row_shape, mat_shape, mat_shape],
        compiler_params=_cparams(("parallel", "parallel")),
        name=name + "_terms",
    )(r, k, v, wl, al, prm)

    full = pl.BlockSpec((C, D_MODEL), lambda b, c: (b * NC + c, 0))
    mats = pl.BlockSpec((1, RWKV_PAIRS, LANES, LANES), lambda b, c: (b * NC + c, 0, 0, 0))
    st_spec = pl.BlockSpec((1, 2 * RWKV_PAIRS, RWKV_HEAD, RWKV_HEAD), lambda b, c: (b, 0, 0, 0))
    return pl.pallas_call(
        functools.partial(_rwkv_scan_kernel, NC),
        grid=(B, NC),
        in_specs=[full] * 4 + [mats, mats, pl.BlockSpec((SUBLANES, D_MODEL), lambda b, c: (0, 0)), st_spec],
        out_specs=[full, st_spec],
        out_shape=[
            jax.ShapeDtypeStruct((rows, D_MODEL), BF16),
            jax.ShapeDtypeStruct((B, 2 * RWKV_PAIRS, RWKV_HEAD, RWKV_HEAD), F32),
        ],
        scratch_shapes=[pltpu.VMEM((RWKV_PAIRS, LANES, LANES), F32)],
        compiler_params=_cparams(("parallel", "arbitrary")),
        name=name + "_scan",
    )(rp, y0, bonus, g, m_all, n_all, prm, s0)


def _pad_lanes(vec, offset):
    out = jnp.zeros((1, LANES), F32)
    return lax.dynamic_update_slice(out, vec.reshape(1, -1).astype(F32), (0, offset))


def _sample_rows(arr, row0, nb, nq, front):
    cols = arr.shape[1]
    s = arr[row0:row0 + nb * nq].reshape(nb, nq, cols)
    s = jnp.pad(s, ((0, 0), (front, 0), (0, 0)))
    return s.reshape(nb * (front + nq), cols)


def _only(x):
    assert x.shape[0] == 1
    return x.reshape(x.shape[1:])


def kernel(x_prompt, x_sample, cache_fox_k, cache_fox_v, cache_fox_logf, state_gdn_conv, state_gdn_S,
           state_rwkv_shift, state_rwkv_S, page_table, meta_tokens, ln_mix, ln_mlp, ln_final,
           w_in0, gdn_conv_w, gdn_A_log, gdn_dt_bias, gdn_norm_w, fox_b_f, w_out0,
           rwkv_mu, rwkv_w0, rwkv_w1, rwkv_w2, rwkv_a0, rwkv_a1, rwkv_a2, rwkv_g1, rwkv_g2,
           rwkv_k_k, rwkv_k_a, rwkv_r_k, rwkv_w_r, rwkv_w_k, rwkv_w_v, rwkv_w_o, rwkv_ln_w, rwkv_ln_b,
           w_up, w_down):
    D = D_MODEL
    assert x_prompt.shape[0] == 1 and x_prompt.shape[2] == D
    seq = x_prompt.shape[1]
    nb, nq = x_sample.shape[0], x_sample.shape[1]
    npages = page_table.shape[1]
    tprompt = N_META + seq
    pad = (-tprompt) % LANES
    tp = tprompt + pad
    ns = nb * nq
    R = -(-(tp + ns) // ROW_TILE) * ROW_TILE
    CS = SUBLANES
    front = CS - nq
    assert 3 <= front

    x0 = jnp.concatenate([jnp.zeros((pad, D), F32), meta_tokens.astype(F32), x_prompt[0],
                          x_sample.reshape(ns, D), jnp.zeros((R - tp - ns, D), F32)], axis=0)

    w_in = w_in0[0]
    o_z = GDN_QKV
    o_a = o_z + GDN_QK
    o_b = o_a + GDN_HEADS
    o_fq = o_b + GDN_HEADS
    o_fk = o_fq + FOX_W
    o_fv = o_fk + FOX_W
    o_ff = o_fv + FOX_W
    o_fz = o_ff + FOX_HEADS
    w_big = jnp.concatenate([w_in[:, :o_a], w_in[:, o_fq:o_ff], w_in[:, o_fz:]], axis=1).astype(BF16)
    w_small = jnp.concatenate([w_in[:, o_a:o_fq], w_in[:, o_ff:o_fz],
                               jnp.zeros((D, LANES - 3 * GDN_HEADS), F32)], axis=1).astype(BF16)

    h0 = rmsnorm(x0, ln_mix[0], BF16)
    p = matmul(h0, w_big, name="in_proj")
    ps = matmul(h0, w_small, name="in_proj_small")

    alog_pad = _pad_lanes(gdn_A_log[0], LANE_A)
    dt_pad = _pad_lanes(gdn_dt_bias[0], LANE_A)
    bf_pad = _pad_lanes(fox_b_f[0], LANE_F)
    conv_w = gdn_conv_w[0]
    norm_w = gdn_norm_w[0].reshape(1, GDN_D)

    GC = 64
    o_gdn_p, s_gdn_p = gdn(p, p, ps, COL_Z // GDN_QK, conv_w, alog_pad, dt_pad, norm_w,
                           jnp.zeros((1, GDN_HEADS, GDN_D, GDN_D), F32),
                           B=1, NC=tp // GC, C=GC, first_valid=pad, name="gdn_prompt")
    qkv_s = p[tp:tp + ns, :GDN_QKV].reshape(nb, nq, GDN_QKV)
    qkv_ext = jnp.concatenate([jnp.zeros((nb, front - 3, GDN_QKV), F32), _only(state_gdn_conv), qkv_s], axis=1)
    z_ext = _sample_rows(p[:, COL_Z:COL_Z + GDN_QK], tp, nb, nq, front)
    ps_ext = _sample_rows(ps, tp, nb, nq, front)
    o_gdn_s, s_gdn_s = gdn(qkv_ext.reshape(nb * CS, GDN_QKV), z_ext, ps_ext, 0, conv_w, alog_pad, dt_pad, norm_w,
                           _only(state_gdn_S), B=nb, NC=1, C=CS, first_valid=front, name="gdn_sample")
    o_gdn_s = o_gdn_s.reshape(nb, CS, GDN_QK)[:, front:].reshape(ns, GDN_QK)

    lf, cq, ck = fox_prep(ps, bf_pad, pad)
    o_fox_p = fox_prompt(p, cq, ck, tp, pad)
    pt_flat = page_table.reshape(-1).astype(jnp.int32)
    lf_s = lf[tp:tp + ns, LANE_F:LANE_F + FOX_HEADS].reshape(nb, nq, FOX_HEADS)
    lfn = jnp.tile(jnp.swapaxes(lf_s, 1, 2), (1, nq, 1))
    lfn = jnp.pad(lfn, ((0, 0), (0, 0), (0, SUBLANES - nq)))
    pool = cache_fox_k.shape[1]
    page = cache_fox_k.shape[2]
    cache_k = _only(cache_fox_k).reshape(pool, page * FOX_HEADS, FOX_DH)
    cache_v = _only(cache_fox_v).reshape(pool, page * FOX_HEADS, FOX_DH)
    cache_lft = jnp.swapaxes(_only(cache_fox_logf), 1, 2)
    def sample_heads(col):
        return p[tp:tp + ns, col:col + FOX_W].reshape(nb, nq * FOX_HEADS, FOX_DH)

    o_fox_s = fox_sample(pt_flat, sample_heads(COL_FQ), cache_k, cache_v, cache_lft, lfn,
                         sample_heads(COL_FK), sample_heads(COL_FV), sample_heads(COL_FZ), nb, npages)
    o_fox_s = o_fox_s.reshape(ns, FOX_W).astype(BF16)

    mix = jnp.concatenate([
        jnp.concatenate([o_gdn_p, o_fox_p], axis=1),
        jnp.concatenate([o_gdn_s, o_fox_s], axis=1),
        jnp.zeros((R - tp - ns, 2 * GDN_QK), BF16)], axis=0)
    x1 = matmul(mix, w_out0[0].astype(BF16), res=x0, name="out_proj")
    u0 = matmul(rmsnorm(x1, ln_mlp[0], BF16), w_up[0].astype(BF16), act="relu2", out_dtype=BF16, name="mlp_up0")
    x2 = matmul(u0, w_down[0].astype(BF16), res=x1, name="mlp_down0")

    h1 = rmsnorm(x2, ln_mix[1], F32)
    h1_s = h1[tp:tp + ns].reshape(nb, nq, D)
    shift_rows = jnp.concatenate([jnp.repeat(state_rwkv_shift[0].astype(F32), nq, axis=0),
                                  jnp.zeros((R - tp - ns, D), F32)], axis=0)
    xr, xw, xk, xv, xa, xg = rwkv_mix(h1, shift_rows, rwkv_mu[0], tp, ns, nq)

    def pad_cols(w):
        return jnp.pad(w, ((0, 0), (0, LANES - w.shape[1]))).astype(BF16)

    def pad_rows(w):
        return jnp.pad(w, ((0, LANES - w.shape[0]), (0, 0))).astype(BF16)

    r_ = matmul(xr, rwkv_w_r[0].astype(BF16), name="rwkv_r")
    k_ = matmul(xk, rwkv_w_k[0].astype(BF16), name="rwkv_k")
    v_ = matmul(xv, rwkv_w_v[0].astype(BF16), name="rwkv_v")
    wl = matmul(matmul(xw, pad_cols(rwkv_w1[0]), act="tanh", out_dtype=BF16, name="rwkv_w1"), pad_rows(rwkv_w2[0]), name="rwkv_w2")
    al = matmul(matmul(xa, pad_cols(rwkv_a1[0]), out_dtype=BF16, name="rwkv_a1"), pad_rows(rwkv_a2[0]), name="rwkv_a2")
    g_ = matmul(matmul(xg, rwkv_g1[0].astype(BF16), act="sigmoid", out_dtype=BF16, name="rwkv_g1"), rwkv_g2[0].astype(BF16), name="rwkv_g2")

    prm = jnp.stack([rwkv_w0[0], rwkv_a0[0], rwkv_k_k[0], rwkv_k_a[0], rwkv_r_k[0].reshape(D), rwkv_ln_w[0], rwkv_ln_b[0],
                     jnp.zeros((D,), F32)], axis=0).astype(F32)
    RC = 64
    o_rw_p, s_rw_p = rwkv(r_, k_, v_, wl, al, g_, prm, jnp.zeros((1, 2 * RWKV_PAIRS, RWKV_HEAD, RWKV_HEAD), F32),
                          B=1, NC=tp // RC, C=RC, first_valid=pad, name="rwkv_prompt")
    sx = [_sample_rows(t, tp, nb, nq, front) for t in (r_, k_, v_, wl, al, g_)]
    o_rw_s, s_rw_s = rwkv(*sx, prm, _only(state_rwkv_S), B=nb, NC=1, C=CS, first_valid=front, name="rwkv_sample")
    o_rw_s = o_rw_s.reshape(nb, CS, D)[:, front:].reshape(ns, D)
    o_rw = jnp.concatenate([o_rw_p, o_rw_s, jnp.zeros((R - tp - ns, D), BF16)], axis=0)
    x3 = matmul(o_rw, rwkv_w_o[0].astype(BF16), res=x2, name="rwkv_o")
    u1 = matmul(rmsnorm(x3, ln_mlp[1], BF16), w_up[1].astype(BF16), act="relu2", out_dtype=BF16, name="mlp_up1")
    x4 = matmul(u1, w_down[1].astype(BF16), res=x3, name="mlp_down1")
    y = rmsnorm(x4, ln_final, F32)

    r0 = pad
    y_prompt = y[r0 + N_META:tp].reshape(1, seq, D)
    y_sample = y[tp:tp + ns].reshape(nb, nq, D)

    def kv_rows(col):
        blk = p[:, col:col + FOX_W]
        return (blk[r0:tp].reshape(1, 1, tprompt, FOX_HEADS, FOX_DH),
                blk[tp:tp + ns].reshape(1, nb, nq, FOX_HEADS, FOX_DH))

    fk_p, fk_s = kv_rows(COL_FK)
    fv_p, fv_s = kv_rows(COL_FV)
    lf8 = lf[:, LANE_F:LANE_F + FOX_HEADS]
    lf_p = lf8[r0:tp].reshape(1, 1, tprompt, FOX_HEADS)
    lf_sm = lf8[tp:tp + ns].reshape(1, nb, nq, FOX_HEADS)
    cb_p = p[tp - 3:tp, :GDN_QKV].reshape(1, 1, 3, GDN_QKV)
    cb_s = qkv_ext[:, CS - 3:][None]
    gs_p = s_gdn_p[None]
    gs_s = s_gdn_s[None]
    sh_p = h1[tp - 1].reshape(1, 1, D)
    sh_s = h1_s[:, nq - 1][None]
    rs_p = s_rw_p[None]
    rs_s = s_rw_s[None]
    return (y_prompt, y_sample, fk_p, fk_s, fv_p, fv_s, lf_p, lf_sm, cb_p, cb_s, gs_p, gs_s, sh_p, sh_s, rs_p, rs_s)
```

```python
import functools
import math

import jax
import jax.numpy as jnp
from jax import lax
from jax.experimental import pallas as pl
from jax.experimental.pallas import tpu as pltpu

F32 = jnp.float32
BF16 = jnp.bfloat16

D_MODEL = 2048
N_META = 16
GDN_HEADS = 8
GDN_D = 128
GDN_QK = GDN_HEADS * GDN_D
GDN_QKV = 3 * GDN_QK
FOX_HEADS = 8
FOX_DH = 128
FOX_W = FOX_HEADS * FOX_DH
RWKV_HEAD = 64
RWKV_PAIRS = D_MODEL // (2 * RWKV_HEAD)
NORM_EPS = 1e-6
L2_EPS = 1e-6
GN_EPS = 64e-5
NEG_INF = -1e30

LANES = 128
SUBLANES = 8
ROW_TILE = 640
MM_ROW_TILE = 1792
LOG2E = 1.4426950408889634
VMEM_LIMIT = 48 * 1024 * 1024
FOX_HEADS_PER_STEP = 4
FOX_GROUP = 8
RWKV_PREP_PAIRS = 8

COL_Z = GDN_QKV
COL_FQ = COL_Z + GDN_QK
COL_FK = COL_FQ + FOX_W
COL_FV = COL_FK + FOX_W
COL_FZ = COL_FV + FOX_W
LANE_A = 0
LANE_B = 8
LANE_F = 16
PRM_W0, PRM_A0, PRM_KK, PRM_KA, PRM_RK, PRM_LNW, PRM_LNB = range(7)


def _pick(n, cands):
    for c in cands:
        if n % c == 0:
            return c
    raise ValueError(f"no tile for {n}")


def _cparams(sem):
    return pltpu.CompilerParams(dimension_semantics=sem, vmem_limit_bytes=VMEM_LIMIT)


def _dot(a, b):
    return jnp.dot(a, b, preferred_element_type=F32)


def _dot_nt(a, b):
    return lax.dot_general(a, b, (((1,), (1,)), ((), ())), preferred_element_type=F32)


def _dot_tn(a, b):
    return lax.dot_general(a, b, (((0,), (0,)), ((), ())), preferred_element_type=F32)


def _bf(x):
    return x.astype(BF16)


def _softplus(x):
    return jnp.maximum(x, 0.0) + jnp.log(1.0 + jnp.exp(-jnp.abs(x)))


def _sigmoid(x):
    return 1.0 / (1.0 + jnp.exp(-x))


def _iota2(shape, dim):
    return lax.broadcasted_iota(jnp.int32, shape, dim)


def _div(x, n):
    assert n & (n - 1) == 0
    return x >> (n.bit_length() - 1)


def _split2(x):
    hi = x.astype(BF16)
    return hi, (x - hi.astype(F32)).astype(BF16)


def _split3(x):
    p1 = x.astype(BF16)
    rem = x - p1.astype(F32)
    p2 = rem.astype(BF16)
    return p1, p2, (rem - p2.astype(F32)).astype(BF16)


def _dot3(a, b):
    m = a.shape[0]
    ah, al = _split2(a)
    bh, bl = _split2(b)
    top = _dot(jnp.concatenate([ah, al], axis=0), bh)
    return (top[:m] + top[m:]) + _dot(ah, bl)


def _dot_lhs01(a01, b):
    a = a01.astype(BF16)
    b1, b2, b3 = _split3(b)
    return _dot(a, b1) + (_dot(a, b2) + _dot(a, b3))


def _dot_rhs01(a, b01):
    b = b01.astype(BF16)
    a1, a2, a3 = _split3(a)
    m = a.shape[0]
    out = _dot(jnp.concatenate([a1, a2, a3], axis=0), b)
    return out[:m] + (out[m:2 * m] + out[2 * m:])


def _tri_inv_steps(n_mat, n):
    size = n_mat.shape[0]
    eye = (_iota2((size, size), 0) == _iota2((size, size), 1)).astype(F32)
    t = eye + n_mat
    if n <= 2:
        return t
    p = _dot(_bf(n_mat), _bf(n_mat))
    yield
    m = 2
    while m < n:
        p_bf = _bf(p)
        if 2 * m < n:
            both = _dot(_bf(jnp.concatenate([p, t], axis=0)), p_bf)
            p = both[:size]
            t = t + both[size:]
        else:
            t = t + _dot(_bf(t), p_bf)
        m *= 2
        yield
    return t


def _run_interleaved(gens):
    results = [None] * len(gens)
    live = list(enumerate(gens))
    while live:
        nxt = []
        for idx, gen in live:
            try:
                next(gen)
                nxt.append((idx, gen))
            except StopIteration as stop:
                results[idx] = stop.value
        live = nxt
    return results


def _rmsnorm_kernel(x_ref, w_ref, o_ref):
    x = x_ref[...]
    ms = jnp.mean(x * x, axis=-1, keepdims=True)
    o_ref[...] = (x * lax.rsqrt(ms + NORM_EPS) * w_ref[...]).astype(o_ref.dtype)


def rmsnorm(x, w, out_dtype):
    m, d = x.shape
    tm = _pick(m, (320, 256, 128))
    return pl.pallas_call(
        _rmsnorm_kernel,
        grid=(m // tm,),
        in_specs=[pl.BlockSpec((tm, d), lambda i: (i, 0)), pl.BlockSpec((1, d), lambda i: (0, 0))],
        out_specs=pl.BlockSpec((tm, d), lambda i: (i, 0)),
        out_shape=jax.ShapeDtypeStruct((m, d), out_dtype),
        compiler_params=_cparams(("parallel",)),
        name="rmsnorm",
    )(x, w.reshape(1, d))


def _mm_kernel(nk, act, has_res, *refs):
    if has_res:
        x_ref, w_ref, r_ref, o_ref = refs[:4]
        scr = refs[4:]
    else:
        x_ref, w_ref, o_ref = refs[:3]
        r_ref = None
        scr = refs[3:]

    def finish(acc):
        if act == "relu2":
            acc = jnp.square(jnp.maximum(acc, 0.0))
        elif act == "tanh":
            acc = jnp.tanh(acc)
        elif act == "sigmoid":
            acc = _sigmoid(acc)
        if has_res:
            acc = r_ref[...] + acc
        o_ref[...] = acc.astype(o_ref.dtype)

    if nk == 1:
        finish(_dot(x_ref[...], w_ref[...]))
    else:
        acc_ref = scr[0]
        k = pl.program_id(2)

        @pl.when(k == 0)
        def _():
            acc_ref[...] = jnp.zeros(acc_ref.shape, F32)

        acc_ref[...] += _dot(x_ref[...], w_ref[...])

        @pl.when(k == nk - 1)
        def _():
            finish(acc_ref[...])


def matmul(x, w, *, act=None, res=None, out_dtype=F32, name="matmul"):
    m, k = x.shape
    n = w.shape[1]
    tm = _pick(m, (MM_ROW_TILE, ROW_TILE, 512, 256, 128))
    tn = _pick(n, (512, 256, 128))
    tk = k if k <= 2048 else 2048
    nk = k // tk
    in_specs = [pl.BlockSpec((tm, tk), lambda i, j, kk: (i, kk)), pl.BlockSpec((tk, tn), lambda i, j, kk: (kk, j))]
    args = [x, w]
    if res is not None:
        in_specs.append(pl.BlockSpec((tm, tn), lambda i, j, kk: (i, j)))
        args.append(res)
    return pl.pallas_call(
        functools.partial(_mm_kernel, nk, act, res is not None),
        grid=(m // tm, n // tn, nk),
        in_specs=in_specs,
        out_specs=pl.BlockSpec((tm, tn), lambda i, j, kk: (i, j)),
        out_shape=jax.ShapeDtypeStruct((m, n), out_dtype),
        scratch_shapes=[pltpu.VMEM((tm, tn), F32)] if nk > 1 else [],
        compiler_params=_cparams(("parallel", "parallel", "arbitrary")),
        name=name,
    )(*args)


def _gdn_kernel(C, NC, first_valid, has_conv, qkv_ref, z_ref, ps_ref, cw_ref, alog_ref, dt_ref, nw_ref, s0_ref,
                *rest):
    conv_ref = rest[0] if has_conv else None
    o_ref, sout_ref, S, ext = rest[1:] if has_conv else rest
    c = pl.program_id(1)

    @pl.when(c == 0)
    def _():
        S[...] = s0_ref[0]
        ext[0:SUBLANES, :] = jnp.zeros((SUBLANES, GDN_QKV), F32)

    x = qkv_ref[...]
    ext[SUBLANES:SUBLANES + C, :] = x
    if has_conv:
        assert NC == 1 and 3 <= first_valid < C
        ext[SUBLANES + first_valid - 3:SUBLANES + first_valid, :] = conv_ref[0, 0]
    cw = cw_ref[...]
    y = ((ext[5:5 + C, :] * cw[0:1] + ext[6:6 + C, :] * cw[1:2]) + ext[7:7 + C, :] * cw[2:3]) + ext[8:8 + C, :] * cw[3:4]
    ext[0:SUBLANES, :] = x[C - SUBLANES:C]
    y = y * _sigmoid(y)

    rows = c * C + _iota2((C, 1), 0)
    valid = rows >= first_valid
    ps = ps_ref[...]
    g_all = jnp.where(valid, -jnp.exp(alog_ref[...]) * _softplus(ps + dt_ref[...]), 0.0)
    beta_all = jnp.where(valid, _sigmoid(ps), 0.0)
    ri = _iota2((C, C), 0)
    ci = _iota2((C, C), 1)
    causal = ri >= ci
    strict = ri > ci
    gc = _dot_lhs01(causal.astype(F32), g_all)
    gct = _dot_rhs01(g_all.T, (ri <= ci).astype(F32))
    nw = nw_ref[...]

    def head_steps(h):
        sl = slice(h * GDN_D, (h + 1) * GDN_D)
        qh = y[:, sl]
        kh = y[:, GDN_QK + h * GDN_D:GDN_QK + (h + 1) * GDN_D]
        vh = jnp.where(valid, y[:, 2 * GDN_QK + h * GDN_D:2 * GDN_QK + (h + 1) * GDN_D], 0.0)
        qh = jnp.where(valid, qh * lax.rsqrt(jnp.sum(qh * qh, axis=-1, keepdims=True) + L2_EPS) * GDN_D ** -0.5, 0.0)
        kh = jnp.where(valid, kh * lax.rsqrt(jnp.sum(kh * kh, axis=-1, keepdims=True) + L2_EPS), 0.0)
        bcol = beta_all[:, LANE_B + h:LANE_B + h + 1]
        gcol = gc[:, LANE_A + h:LANE_A + h + 1]
        grow = gct[LANE_A + h:LANE_A + h + 1, :]
        glast = gc[C - 1:C, LANE_A + h:LANE_A + h + 1]
        diff = gcol - grow
        decay = jnp.where(causal, jnp.exp(jnp.where(causal, diff, 0.0)), 0.0)
        kb = kh * bcol
        a_mat = jnp.where(strict, _dot_nt(_bf(kb), _bf(kh)) * decay, 0.0)
        qk = _dot_nt(_bf(qh), _bf(kh)) * decay
        eg = jnp.exp(gcol)
        rhs = jnp.concatenate([vh * bcol, kb * eg], axis=1)
        q_dec = qh * eg
        k_dec = kh * jnp.exp(glast - gcol)
        yield
        t_mat = yield from _tri_inv_steps(-a_mat, C)
        sol = _dot3(t_mat, rhs)
        u = sol[:, :GDN_D]
        w = sol[:, GDN_D:]
        yield
        s_h = S[h]
        s_bf = _bf(s_h)
        v_new = u - _dot(_bf(w), s_bf)
        o_state = _dot(_bf(q_dec), s_bf)
        yield
        o = o_state + _dot(_bf(qk), _bf(v_new))
        S[h] = s_h * jnp.exp(glast) + _dot_tn(_bf(k_dec), _bf(v_new))
        yield
        on = o * lax.rsqrt(jnp.mean(o * o, axis=-1, keepdims=True) + NORM_EPS) * nw
        zh = z_ref[:, sl]
        o_ref[:, sl] = (on * (zh * _sigmoid(zh))).astype(o_ref.dtype)

    _run_interleaved([head_steps(h) for h in range(GDN_HEADS)])

    @pl.when(c == NC - 1)
    def _():
        sout_ref[0] = S[...]


def gdn(qkv_arr, z_arr, ps_arr, z_col, conv_w, alog_pad, dt_pad, norm_w, s0, conv0, *, B, NC, C, first_valid, name):
    rows = B * NC * C
    has_conv = conv0 is not None
    extra_specs = [pl.BlockSpec((1, 1, 3, GDN_QKV), lambda b, c: (0, b, 0, 0))] if has_conv else []
    extra_args = [conv0] if has_conv else []
    return pl.pallas_call(
        functools.partial(_gdn_kernel, C, NC, first_valid, has_conv),
        grid=(B, NC),
        in_specs=[
            pl.BlockSpec((C, GDN_QKV), lambda b, c: (b * NC + c, 0)),
            pl.BlockSpec((C, GDN_QK), lambda b, c: (b * NC + c, z_col)),
            pl.BlockSpec((C, LANES), lambda b, c: (b * NC + c, 0)),
            pl.BlockSpec((4, GDN_QKV), lambda b, c: (0, 0)),
            pl.BlockSpec((1, LANES), lambda b, c: (0, 0)),
            pl.BlockSpec((1, LANES), lambda b, c: (0, 0)),
            pl.BlockSpec((1, GDN_D), lambda b, c: (0, 0)),
            pl.BlockSpec((1, GDN_HEADS, GDN_D, GDN_D), lambda b, c: (b, 0, 0, 0)),
        ] + extra_specs,
        out_specs=[
            pl.BlockSpec((C, GDN_QK), lambda b, c: (b * NC + c, 0)),
            pl.BlockSpec((1, GDN_HEADS, GDN_D, GDN_D), lambda b, c: (b, 0, 0, 0)),
        ],
        out_shape=[
            jax.ShapeDtypeStruct((rows, GDN_QK), BF16),
            jax.ShapeDtypeStruct((B, GDN_HEADS, GDN_D, GDN_D), F32),
        ],
        scratch_shapes=[pltpu.VMEM((GDN_HEADS, GDN_D, GDN_D), F32), pltpu.VMEM((C + SUBLANES, GDN_QKV), F32)],
        compiler_params=_cparams(("parallel", "arbitrary")),
        name=name,
    )(qkv_arr, z_arr, ps_arr, conv_w, alog_pad, dt_pad, norm_w, s0, *extra_args)


def _fox_prep_kernel(tb, first_valid, ps_ref, bf_ref, lf_ref, cq_ref, ck_ref, carry):
    i = pl.program_id(0)

    @pl.when(i == 0)
    def _():
        carry[...] = jnp.zeros(carry.shape, F32)

    x = ps_ref[...] + bf_ref[...]
    rows = i * tb + _iota2((tb, 1), 0)
    lf = jnp.where(rows >= first_valid, jnp.minimum(x, 0.0) - jnp.log(1.0 + jnp.exp(-jnp.abs(x))), 0.0)
    tri = (_iota2((tb, tb), 0) >= _iota2((tb, tb), 1)).astype(F32)
    c = _dot_lhs01(tri, lf) + carry[0:1, :]
    carry[...] = jnp.broadcast_to(c[tb - 1:tb, :], carry.shape)
    lf_ref[...] = lf
    c2 = c * LOG2E
    for h in range(FOX_HEADS):
        cq_ref[h] = jnp.broadcast_to(c2[:, LANE_F + h:LANE_F + h + 1], (tb, LANES))
    ck_ref[...] = c2.T[LANE_F:LANE_F + FOX_HEADS, :]


def fox_prep(ps, bf_pad, first_valid):
    r = ps.shape[0]
    tb = _pick(r, (ROW_TILE, 512, 256, 128))
    return pl.pallas_call(
        functools.partial(_fox_prep_kernel, tb, first_valid),
        grid=(r // tb,),
        in_specs=[pl.BlockSpec((tb, LANES), lambda i: (i, 0)), pl.BlockSpec((1, LANES), lambda i: (0, 0))],
        out_specs=[
            pl.BlockSpec((tb, LANES), lambda i: (i, 0)),
            pl.BlockSpec((FOX_HEADS, tb, LANES), lambda i: (0, i, 0)),
            pl.BlockSpec((FOX_HEADS, tb), lambda i: (0, i)),
        ],
        out_shape=[
            jax.ShapeDtypeStruct((r, LANES), F32),
            jax.ShapeDtypeStruct((FOX_HEADS, r, LANES), F32),
            jax.ShapeDtypeStruct((FOX_HEADS, r), F32),
        ],
        scratch_shapes=[pltpu.VMEM((SUBLANES, LANES), F32)],
        compiler_params=_cparams(("arbitrary",)),
        name="fox_prep",
    )(ps, bf_pad)


def _fox_flash_kernel(tq, first_valid, qi_ref, ki_ref, qt_ref, k_ref, vt_ref, ck_ref, cq_ref, fz_ref, o_ref,
                      m_s, l_s, acc_s, t_s):
    step = pl.program_id(1)
    qi = qi_ref[step]
    ki = ki_ref[step]
    reps = tq // LANES
    heads = qt_ref.shape[0]

    @pl.when(ki == 0)
    def _():
        m_s[...] = jnp.full(m_s.shape, NEG_INF, F32)
        l_s[...] = jnp.zeros(l_s.shape, F32)
        acc_s[...] = jnp.zeros(acc_s.shape, F32)

    def head_steps(h, masked):
        cols = slice(h * FOX_DH, (h + 1) * FOX_DH)
        qt = _bf(qt_ref[h] * (FOX_DH ** -0.5 * LOG2E))
        cq = cq_ref[h]
        m_prev = m_s[h, 0:1, :]
        kb = LANES
        groups = kb // SUBLANES
        mx = None
        for b in range(tq // kb):
            rs = slice(b * kb, (b + 1) * kb)
            t = _dot(_bf(k_ref[rs, cols]), qt) - jnp.concatenate([ck_ref[h, rs, :]] * reps, axis=1)
            if masked:
                kpos = ki * tq + b * kb + _iota2((kb, tq), 0)
                qpos = qi * tq + _iota2((kb, tq), 1)
                t = jnp.where((kpos <= qpos) & (kpos >= first_valid), t, NEG_INF)
            t_s[h, rs, :] = t
            part = jnp.max(t.reshape(groups, SUBLANES, tq), axis=0)
            mx = part if mx is None else jnp.maximum(mx, part)
            yield
        m_new = jnp.maximum(m_prev, jnp.max(mx, axis=0, keepdims=True) + cq)
        shift = cq - m_new
        alpha = jnp.exp2(m_prev - m_new)
        lsum = None
        pv = None
        for b in range(tq // kb):
            rs = slice(b * kb, (b + 1) * kb)
            p = jnp.exp2(t_s[h, rs, :] + shift)
            part = jnp.sum(p.reshape(groups, SUBLANES, tq), axis=0)
            lsum = part if lsum is None else lsum + part
            d = _dot(_bf(vt_ref[h, :, rs]), _bf(p))
            pv = d if pv is None else pv + d
            yield
        l_new = alpha * l_s[h, 0:1, :] + jnp.sum(lsum, axis=0, keepdims=True)
        l_s[h] = jnp.broadcast_to(l_new, (SUBLANES, tq))
        acc_s[h] = alpha * acc_s[h] + pv
        m_s[h] = jnp.broadcast_to(m_new, (SUBLANES, tq))

    def accumulate(masked):
        _run_interleaved([head_steps(h, masked) for h in range(heads)])

    edge = (ki == qi) | (ki == 0)
    pl.when(edge)(lambda: accumulate(True))
    pl.when(jnp.logical_not(edge))(lambda: accumulate(False))

    @pl.when(ki == qi)
    def _():
        rows = qi * tq + _iota2((tq, 1), 0)
        for h in range(heads):
            cols = slice(h * FOX_DH, (h + 1) * FOX_DH)
            o = (acc_s[h] / l_s[h, 0:1, :]).T * _sigmoid(fz_ref[:, cols])
            o_ref[:, cols] = jnp.where(rows >= first_valid, o, 0.0).astype(o_ref.dtype)


def fox_prompt(p, c_rep, c_row, tp, first_valid):
    tq = _pick(tp, (ROW_TILE, 512, 384, 256, 128))
    nq = tp // tq
    assert first_valid < tq
    pairs = [(qi, ki) for qi in range(nq) for ki in range(qi + 1)]
    qi_arr = jnp.array([a for a, _ in pairs], jnp.int32)
    ki_arr = jnp.array([b for _, b in pairs], jnp.int32)
    hp = FOX_HEADS_PER_STEP
    wid = hp * FOX_DH
    cb = lambda col, g: col // wid + g

    def heads_t(col):
        return jnp.transpose(p[:tp, col:col + FOX_W].reshape(tp, FOX_HEADS, FOX_DH), (1, 2, 0))

    return pl.pallas_call(
        functools.partial(_fox_flash_kernel, tq, first_valid),
        grid_spec=pltpu.PrefetchScalarGridSpec(
            num_scalar_prefetch=2,
            grid=(FOX_HEADS // hp, len(pairs)),
            in_specs=[
                pl.BlockSpec((hp, FOX_DH, tq), lambda g, t, qa, ka: (g, 0, qa[t])),
                pl.BlockSpec((tq, wid), lambda g, t, qa, ka: (ka[t], cb(COL_FK, g))),
                pl.BlockSpec((hp, FOX_DH, tq), lambda g, t, qa, ka: (g, 0, ka[t])),
                pl.BlockSpec((hp, tq, LANES), lambda g, t, qa, ka: (g, ka[t], 0)),
                pl.BlockSpec((hp, 1, tq), lambda g, t, qa, ka: (g, 0, qa[t])),
                pl.BlockSpec((tq, wid), lambda g, t, qa, ka: (qa[t], cb(COL_FZ, g))),
            ],
            out_specs=pl.BlockSpec((tq, wid), lambda g, t, qa, ka: (qa[t], g)),
            scratch_shapes=[pltpu.VMEM((hp, SUBLANES, tq), F32), pltpu.VMEM((hp, SUBLANES, tq), F32),
                            pltpu.VMEM((hp, FOX_DH, tq), F32), pltpu.VMEM((hp, tq, tq), F32)],
        ),
        out_shape=jax.ShapeDtypeStruct((tp, FOX_W), BF16),
        compiler_params=_cparams(("parallel", "arbitrary")),
        name="fox_prompt",
    )(qi_arr, ki_arr, heads_t(COL_FQ), p, heads_t(COL_FV), c_rep, c_row.reshape(FOX_HEADS, 1, -1), p)


def _fox_sample_kernel(G, ngroups, pt_ref, q_ref, *refs):
    k_refs, v_refs, lft_refs = refs[0:G], refs[G:2 * G], refs[2 * G:3 * G]
    lfn_ref, kn_ref, vn_ref, fz_ref, o_ref, cq_s, cn_s, carry_s, m_s, l_s, acc_s = refs[3 * G:]
    jg = pl.program_id(1)
    rows = q_ref.shape[1]
    page = lft_refs[0].shape[2]
    flat = page * FOX_HEADS
    nn = lfn_ref.shape[2]
    scale = FOX_DH ** -0.5

    @pl.when(jg == 0)
    def _():
        upper = (_iota2((nn, nn), 0) <= _iota2((nn, nn), 1)).astype(F32)
        cn = _dot_rhs01(lfn_ref[0], upper)
        cn_s[...] = cn
        own_q = _iota2((rows, nn), 1) == _div(_iota2((rows, nn), 0), FOX_HEADS)
        cq_s[...] = jnp.broadcast_to(jnp.sum(jnp.where(own_q, cn, 0.0), axis=-1, keepdims=True), cq_s.shape)
        carry_s[...] = jnp.zeros(carry_s.shape, F32)
        m_s[...] = jnp.full(m_s.shape, NEG_INF, F32)
        l_s[...] = jnp.zeros(l_s.shape, F32)
        acc_s[...] = jnp.zeros(acc_s.shape, F32)

    def update(ts, vals_bf):
        cq = cq_s[:, 0:1]
        m_prev = m_s[:, 0:1]
        t_max = functools.reduce(jnp.maximum, ts)
        m_new = jnp.maximum(m_prev, jnp.max(t_max, axis=-1, keepdims=True) + cq)
        shift = cq - m_new
        ps = [jnp.exp(t + shift) for t in ts]
        alpha = jnp.exp(m_prev - m_new)
        l_s[...] = alpha * l_s[...] + jnp.sum(functools.reduce(jnp.add, ps), axis=-1, keepdims=True)
        pv = functools.reduce(jnp.add, [_dot(_bf(p), v) for p, v in zip(ps, vals_bf)])
        acc_s[...] = alpha * acc_s[...] + pv
        m_s[...] = jnp.broadcast_to(m_new, m_s.shape)

    own = (_iota2((rows, flat), 0) & (FOX_HEADS - 1)) == (_iota2((rows, flat), 1) & (FOX_HEADS - 1))
    later = (_iota2((page, flat), 0) > _div(_iota2((page, flat), 1), FOX_HEADS)).astype(BF16)
    head_col = _iota2((FOX_HEADS, flat), 0) == (_iota2((FOX_HEADS, flat), 1) & (FOX_HEADS - 1))
    carry = carry_s[:, 0:1]
    q_bf = _bf(q_ref[0])
    lfts = [lft_refs[g][0] for g in range(G)]
    suffixes = _dot_rhs01(jnp.concatenate(lfts, axis=0), later)
    ts = []
    for g in range(G):
        suffix = suffixes[g * FOX_HEADS:(g + 1) * FOX_HEADS]
        d = jnp.sum(jnp.where(head_col, suffix + carry, 0.0), axis=0, keepdims=True)
        carry = carry + jnp.sum(lfts[g], axis=-1, keepdims=True)
        s = _dot_nt(q_bf, _bf(k_refs[g][0])) * scale
        ts.append(jnp.where(own, s + d, NEG_INF))
    update(ts, [_bf(v_refs[g][0]) for g in range(G)])
    carry_s[...] = jnp.broadcast_to(carry, carry_s.shape)

    @pl.when(jg == ngroups - 1)
    def _():
        nflat = kn_ref.shape[1]
        spread = (_iota2((nn, nflat), 0) == _div(_iota2((nn, nflat), 1), FOX_HEADS)).astype(F32)
        cn_cols = _dot_rhs01(cn_s[...], spread)
        ri = _iota2((rows, nflat), 0)
        ci = _iota2((rows, nflat), 1)
        ok = ((ri & (FOX_HEADS - 1)) == (ci & (FOX_HEADS - 1))) & (_div(ci, FOX_HEADS) <= _div(ri, FOX_HEADS))
        sn = _dot_nt(_bf(q_ref[0]), _bf(kn_ref[0])) * scale
        update([jnp.where(ok, sn - cn_cols, NEG_INF)], [_bf(vn_ref[0])])
        o_ref[0] = acc_s[...] / l_s[...] * _sigmoid(fz_ref[0])


def fox_sample(page_table_flat, q_rows, cache_k, cache_v, cache_lft, lfn, kn_flat, vn_flat, fz_rows, nb, npages):
    flat = cache_k.shape[1]
    page = cache_lft.shape[2]
    rows = q_rows.shape[1]
    nn = lfn.shape[2]
    G = _pick(npages, (FOX_GROUP, 2, 1))
    ngroups = npages // G

    def page_map(g):
        return lambda b, jg, pt: (pt[b * npages + (npages - 1 - (jg * G + g))], 0, 0)

    seq = lambda b, jg, pt: (b, 0, 0)
    row_blk = pl.BlockSpec((1, rows, FOX_DH), seq)
    new_blk = pl.BlockSpec((1, kn_flat.shape[1], FOX_DH), seq)
    in_specs = [row_blk]
    in_specs += [pl.BlockSpec((1, flat, FOX_DH), page_map(g)) for g in range(G)]
    in_specs += [pl.BlockSpec((1, flat, FOX_DH), page_map(g)) for g in range(G)]
    in_specs += [pl.BlockSpec((1, FOX_HEADS, page), page_map(g)) for g in range(G)]
    in_specs += [pl.BlockSpec((1, rows, nn), seq), new_blk, new_blk, row_blk]
    return pl.pallas_call(
        functools.partial(_fox_sample_kernel, G, ngroups),
        grid_spec=pltpu.PrefetchScalarGridSpec(
            num_scalar_prefetch=1,
            grid=(nb, ngroups),
            in_specs=in_specs,
            out_specs=row_blk,
            scratch_shapes=[
                pltpu.VMEM((rows, LANES), F32),
                pltpu.VMEM((rows, nn), F32),
                pltpu.VMEM((FOX_HEADS, LANES), F32),
                pltpu.VMEM((rows, LANES), F32),
                pltpu.VMEM((rows, LANES), F32),
                pltpu.VMEM((rows, FOX_DH), F32),
            ],
        ),
        out_shape=jax.ShapeDtypeStruct((nb, rows, FOX_DH), F32),
        compiler_params=_cparams(("parallel", "arbitrary")),
        name="fox_sample",
    )(page_table_flat, q_rows, *([cache_k] * G), *([cache_v] * G), *([cache_lft] * G), lfn, kn_flat, vn_flat, fz_rows)


def _rwkv_mix_kernel(tm, tp, ns, nq, h_ref, hb_ref, st_ref, mu_ref, *o_refs):
    i = pl.program_id(0)
    h = h_ref[...]
    above = jnp.concatenate([hb_ref[SUBLANES - 1:SUBLANES, :], h[:tm - 1]], axis=0)
    row = i * tm + _iota2((tm, 1), 0)
    seq_start = (row >= tp) & (row < tp + ns) & (((row - tp) & (nq - 1)) == 0)
    prev = jnp.where(seq_start, st_ref[...], jnp.where(row == 0, 0.0, above))
    xx = prev - h
    mu = mu_ref[...]
    for j, o_ref in enumerate(o_refs):
        o_ref[...] = (h + xx * mu[j:j + 1]).astype(o_ref.dtype)


def rwkv_mix(h, shift_rows, mu, tp, ns, nq):
    m, d = h.shape
    assert nq & (nq - 1) == 0
    tm = _pick(math.gcd(m, tp), (320, 256, 128))
    first = tp // tm
    spec = pl.BlockSpec((tm, d), lambda i: (i, 0))
    return pl.pallas_call(
        functools.partial(_rwkv_mix_kernel, tm, tp, ns, nq),
        grid=(m // tm,),
        in_specs=[spec,
                  pl.BlockSpec((SUBLANES, d), lambda i: (jnp.maximum(i * (tm // SUBLANES) - 1, 0), 0)),
                  pl.BlockSpec((tm, d), lambda i: (jnp.maximum(i - first, 0), 0)),
                  pl.BlockSpec((6, d), lambda i: (0, 0))],
        out_specs=[spec] * 6,
        out_shape=[jax.ShapeDtypeStruct((m, d), BF16)] * 6,
        compiler_params=_cparams(("parallel",)),
        name="rwkv_mix",
    )(h, h, shift_rows, mu)


def _head_ones():
    return (_div(_iota2((LANES, LANES), 0), RWKV_HEAD) == _div(_iota2((LANES, LANES), 1), RWKV_HEAD)).astype(F32)


def _rwkv_chunk_terms(C, valid, r, k, v, wl, al, prm):
    HD = RWKV_HEAD
    m0 = _iota2((1, LANES), 1) < HD
    bones = _head_ones()
    w0, a0, k_k, k_a, r_k = (prm[i:i + 1] for i in (PRM_W0, PRM_A0, PRM_KK, PRM_KA, PRM_RK))
    wlog = -_softplus(-(w0 + wl)) - 0.5
    lw = jnp.where(valid, -jnp.exp(wlog), 0.0)
    a = _sigmoid(a0 + al)
    kkr = k * k_k
    kk = kkr * lax.rsqrt(_dot_rhs01(kkr * kkr, bones) + L2_EPS)
    k2 = k * (1.0 + (a - 1.0) * k_a)
    bonus = _dot_rhs01(r * k2 * r_k, bones) * v
    rm = jnp.where(valid, r, 0.0)
    k2 = jnp.where(valid, k2, 0.0)
    vm = jnp.where(valid, v, 0.0)
    av = jnp.where(valid, -kk, 0.0)
    bv = jnp.where(valid, kk * a, 0.0)

    ri = _iota2((C, C), 0)
    ci = _iota2((C, C), 1)
    yield
    cum = _dot_lhs01((ri >= ci).astype(F32), lw)
    cum_last = cum[C - 1:C, :]
    inv = jnp.exp(-cum)
    rt = rm * jnp.exp(cum)
    at = av * jnp.exp(cum - lw)
    bt = bv * inv
    kt = k2 * inv
    to_end = jnp.exp(cum_last - cum)
    b_end = bv * to_end
    k_end = k2 * to_end

    def split(x):
        return jnp.concatenate([jnp.where(m0, x, 0.0), jnp.where(m0, 0.0, x)], axis=0)

    def halves(x):
        return x[0:C] + x[C:2 * C]

    yield
    at_s = split(at)
    gram = _dot_nt(_bf(jnp.concatenate([at_s, split(rt)], axis=0)), _bf(jnp.concatenate([bt, kt], axis=0)))
    r2 = _iota2((2 * C, 2 * C), 0)
    c2 = _iota2((2 * C, 2 * C), 1)
    same = _div(r2, C) == _div(c2, C)
    bd_strict = same & (r2 > c2)
    bd_incl = same & (r2 >= c2)

    def bd(block, mask):
        return jnp.where(mask, jnp.concatenate([block, block], axis=1), 0.0)

    a_ab = bd(gram[0:2 * C, 0:C], bd_strict)
    a_ak = bd(gram[0:2 * C, C:2 * C], bd_strict)
    r_b = bd(gram[2 * C:4 * C, 0:C], bd_incl)
    r_k2 = bd(gram[2 * C:4 * C, C:2 * C], bd_incl)
    vs = split(vm)
    akv = halves(_dot(_bf(a_ak), _bf(vs)))
    yield
    t_bd = yield from _tri_inv_steps(a_ab, C)
    tw = _dot(_bf(t_bd), _bf(jnp.concatenate([at_s, split(akv)], axis=1)))
    wt = halves(tw[:, :LANES])
    ut = halves(tw[:, LANES:])
    yield
    rp = rt + halves(_dot(_bf(r_b), _bf(split(wt))))
    y0 = halves(_dot(_bf(jnp.concatenate([r_b, r_k2], axis=1)), _bf(jnp.concatenate([split(ut), vs], axis=0))))
    eye = (_iota2((LANES, LANES), 0) == _iota2((LANES, LANES), 1)).astype(F32)
    m_mat = eye * jnp.exp(cum_last) + bones * _dot_tn(_bf(b_end), _bf(wt))
    n_mat = bones * _dot_tn(_bf(jnp.concatenate([b_end, k_end], axis=0)), _bf(jnp.concatenate([ut, vm], axis=0)))
    return rp, y0, bonus, m_mat, n_mat


def _rwkv_prep_kernel(C, chunk_of_step, first_valid, PG, r_ref, k_ref, v_ref, wl_ref, al_ref, prm_ref,
                      rp_ref, y0_ref, bo_ref, m_ref, n_ref):
    c = pl.program_id(0) if chunk_of_step else 0
    valid = (c * C + _iota2((C, 1), 0)) >= first_valid
    sls = [slice(i * LANES, (i + 1) * LANES) for i in range(PG)]
    terms = _run_interleaved([
        _rwkv_chunk_terms(C, valid, r_ref[:, sl], k_ref[:, sl], v_ref[:, sl], wl_ref[:, sl], al_ref[:, sl], prm_ref[:, sl])
        for sl in sls])
    for i, sl in enumerate(sls):
        rp, y0, bonus, m_mat, n_mat = terms[i]
        rp_ref[:, sl] = rp
        y0_ref[:, sl] = y0
        bo_ref[:, sl] = bonus
        m_ref[0, i] = m_mat
        n_ref[0, i] = n_mat


def _rwkv_scan_kernel(NC, rp_ref, y0_ref, bo_ref, g_ref, m_ref, n_ref, prm_ref, s0_ref, o_ref, sout_ref, H):
    c = pl.program_id(1)
    HD = RWKV_HEAD
    bones = _head_ones()

    @pl.when(c == 0)
    def _():
        zero = jnp.zeros((HD, HD), F32)
        for i in range(RWKV_PAIRS):
            top = jnp.concatenate([s0_ref[0, 2 * i], zero], axis=1)
            bot = jnp.concatenate([zero, s0_ref[0, 2 * i + 1]], axis=1)
            H[i] = jnp.concatenate([top, bot], axis=0).T

    def pair_steps(i):
        sl = slice(i * LANES, (i + 1) * LANES)
        h_bd = H[i]
        y = _dot(_bf(rp_ref[:, sl]), _bf(h_bd)) + y0_ref[:, sl]
        H[i] = _dot3(m_ref[0, i], h_bd) + n_ref[0, i]
        yield
        mean = _dot_rhs01(y, bones) * (1.0 / HD)
        dlt = y - mean
        yield
        var = _dot_rhs01(dlt * dlt, bones) * (1.0 / HD)
        yn = dlt * lax.rsqrt(var + GN_EPS) * prm_ref[PRM_LNW:PRM_LNW + 1, sl] + prm_ref[PRM_LNB:PRM_LNB + 1, sl]
        o_ref[:, sl] = ((yn + bo_ref[:, sl]) * g_ref[:, sl]).astype(o_ref.dtype)

    _run_interleaved([pair_steps(i) for i in range(RWKV_PAIRS)])

    @pl.when(c == NC - 1)
    def _():
        for i in range(RWKV_PAIRS):
            ht = H[i].T
            sout_ref[0, 2 * i] = ht[0:HD, 0:HD]
            sout_ref[0, 2 * i + 1] = ht[HD:2 * HD, HD:2 * HD]


def rwkv(r, k, v, wl, al, g, prm, s0, *, B, NC, C, first_valid, name):
    assert B == 1 or NC == 1
    nblk = B * NC
    rows = nblk * C
    PG = RWKV_PREP_PAIRS
    wide = pl.BlockSpec((C, PG * LANES), lambda blk, pg: (blk, pg))
    mat = pl.BlockSpec((1, PG, LANES, LANES), lambda blk, pg: (blk, pg, 0, 0))
    mat_shape = jax.ShapeDtypeStruct((nblk, RWKV_PAIRS, LANES, LANES), F32)
    row_shape = jax.ShapeDtypeStruct((rows, D_MODEL), F32)
    rp, y0, bonus, m_all, n_all = pl.pallas_call(
        functools.partial(_rwkv_prep_kernel, C, B == 1, first_valid, PG),
        grid=(nblk, RWKV_PAIRS // PG),
        in_specs=[wide] * 5 + [pl.BlockSpec((SUBLANES, PG * LANES), lambda blk, pg: (0, pg))],
        out_specs=[wide, wide, wide, mat, mat],
        out_shape=[row_shape, row_shape, row_shape, mat_shape, mat_shape],
        compiler_params=_cparams(("parallel", "parallel")),
        name=name + "_terms",
    )(r, k, v, wl, al, prm)

    full = pl.BlockSpec((C, D_MODEL), lambda b, c: (b * NC + c, 0))
    mats = pl.BlockSpec((1, RWKV_PAIRS, LANES, LANES), lambda b, c: (b * NC + c, 0, 0, 0))
    st_spec = pl.BlockSpec((1, 2 * RWKV_PAIRS, RWKV_HEAD, RWKV_HEAD), lambda b, c: (b, 0, 0, 0))
    return pl.pallas_call(
        functools.partial(_rwkv_scan_kernel, NC),
        grid=(B, NC),
        in_specs=[full] * 4 + [mats, mats, pl.BlockSpec((SUBLANES, D_MODEL), lambda b, c: (0, 0)), st_spec],
        out_specs=[full, st_spec],
        out_shape=[
            jax.ShapeDtypeStruct((rows, D_MODEL), BF16),
            jax.ShapeDtypeStruct((B, 2 * RWKV_PAIRS, RWKV_HEAD, RWKV_HEAD), F32),
        ],
        scratch_shapes=[pltpu.VMEM((RWKV_PAIRS, LANES, LANES), F32)],
        compiler_params=_cparams(("parallel", "arbitrary")),
        name=name + "_scan",
    )(rp, y0, bonus, g, m_all, n_all, prm, s0)


def _pad_lanes(vec, offset):
    out = jnp.zeros((1, LANES), F32)
    return lax.dynamic_update_slice(out, vec.reshape(1, -1).astype(F32), (0, offset))


def _sample_rows(arr, row0, nb, nq, front):
    cols = arr.shape[1]
    s = arr[row0:row0 + nb * nq].reshape(nb, nq, cols)
    s = jnp.pad(s, ((0, 0), (front, 0), (0, 0)))
    return s.reshape(nb * (front + nq), cols)


def _only(x):
    assert x.shape[0] == 1
    return x.reshape(x.shape[1:])


def kernel(x_prompt, x_sample, cache_fox_k, cache_fox_v, cache_fox_logf, state_gdn_conv, state_gdn_S,
           state_rwkv_shift, state_rwkv_S, page_table, meta_tokens, ln_mix, ln_mlp, ln_final,
           w_in0, gdn_conv_w, gdn_A_log, gdn_dt_bias, gdn_norm_w, fox_b_f, w_out0,
           rwkv_mu, rwkv_w0, rwkv_w1, rwkv_w2, rwkv_a0, rwkv_a1, rwkv_a2, rwkv_g1, rwkv_g2,
           rwkv_k_k, rwkv_k_a, rwkv_r_k, rwkv_w_r, rwkv_w_k, rwkv_w_v, rwkv_w_o, rwkv_ln_w, rwkv_ln_b,
           w_up, w_down):
    D = D_MODEL
    assert x_prompt.shape[0] == 1 and x_prompt.shape[2] == D
    seq = x_prompt.shape[1]
    nb, nq = x_sample.shape[0], x_sample.shape[1]
    npages = page_table.shape[1]
    tprompt = N_META + seq
    pad = (-tprompt) % LANES
    tp = tprompt + pad
    ns = nb * nq
    R = -(-(tp + ns) // ROW_TILE) * ROW_TILE
    CS = SUBLANES
    front = CS - nq
    assert 3 <= front

    x0 = jnp.concatenate([jnp.zeros((pad, D), F32), meta_tokens.astype(F32), x_prompt[0],
                          x_sample.reshape(ns, D), jnp.zeros((R - tp - ns, D), F32)], axis=0)

    w_in = w_in0[0]
    o_z = GDN_QKV
    o_a = o_z + GDN_QK
    o_b = o_a + GDN_HEADS
    o_fq = o_b + GDN_HEADS
    o_fk = o_fq + FOX_W
    o_fv = o_fk + FOX_W
    o_ff = o_fv + FOX_W
    o_fz = o_ff + FOX_HEADS
    w_big = jnp.concatenate([w_in[:, :o_a], w_in[:, o_fq:o_ff], w_in[:, o_fz:]], axis=1).astype(BF16)
    w_small = jnp.concatenate([w_in[:, o_a:o_fq], w_in[:, o_ff:o_fz],
                               jnp.zeros((D, LANES - 3 * GDN_HEADS), F32)], axis=1).astype(BF16)

    h0 = rmsnorm(x0, ln_mix[0], BF16)
    p = matmul(h0, w_big, name="in_proj")
    ps = matmul(h0, w_small, name="in_proj_small")

    alog_pad = _pad_lanes(gdn_A_log[0], LANE_A)
    dt_pad = _pad_lanes(gdn_dt_bias[0], LANE_A)
    bf_pad = _pad_lanes(fox_b_f[0], LANE_F)
    conv_w = gdn_conv_w[0]
    norm_w = gdn_norm_w[0].reshape(1, GDN_D)

    GC = 64
    o_gdn_p, s_gdn_p = gdn(p, p, ps, COL_Z // GDN_QK, conv_w, alog_pad, dt_pad, norm_w,
                           jnp.zeros((1, GDN_HEADS, GDN_D, GDN_D), F32), None,
                           B=1, NC=tp // GC, C=GC, first_valid=pad, name="gdn_prompt")
    qkv_ext = _sample_rows(p[:, :GDN_QKV], tp, nb, nq, front)
    z_ext = _sample_rows(p[:, COL_Z:COL_Z + GDN_QK], tp, nb, nq, front)
    ps_ext = _sample_rows(ps, tp, nb, nq, front)
    o_gdn_s, s_gdn_s = gdn(qkv_ext, z_ext, ps_ext, 0, conv_w, alog_pad, dt_pad, norm_w,
                           _only(state_gdn_S), state_gdn_conv, B=nb, NC=1, C=CS, first_valid=front, name="gdn_sample")
    o_gdn_s = o_gdn_s.reshape(nb, CS, GDN_QK)[:, front:].reshape(ns, GDN_QK)

    lf, cq, ck = fox_prep(ps, bf_pad, pad)
    o_fox_p = fox_prompt(p, cq, ck, tp, pad)
    pt_flat = page_table.reshape(-1).astype(jnp.int32)
    lf_s = lf[tp:tp + ns, LANE_F:LANE_F + FOX_HEADS].reshape(nb, nq, FOX_HEADS)
    lfn = jnp.tile(jnp.swapaxes(lf_s, 1, 2), (1, nq, 1))
    lfn = jnp.pad(lfn, ((0, 0), (0, 0), (0, SUBLANES - nq)))
    pool = cache_fox_k.shape[1]
    page = cache_fox_k.shape[2]
    cache_k = _only(cache_fox_k).reshape(pool, page * FOX_HEADS, FOX_DH)
    cache_v = _only(cache_fox_v).reshape(pool, page * FOX_HEADS, FOX_DH)
    cache_lft = jnp.swapaxes(_only(cache_fox_logf), 1, 2)
    def sample_heads(col):
        return p[tp:tp + ns, col:col + FOX_W].reshape(nb, nq * FOX_HEADS, FOX_DH)

    o_fox_s = fox_sample(pt_flat, sample_heads(COL_FQ), cache_k, cache_v, cache_lft, lfn,
                         sample_heads(COL_FK), sample_heads(COL_FV), sample_heads(COL_FZ), nb, npages)
    o_fox_s = o_fox_s.reshape(ns, FOX_W).astype(BF16)

    mix = jnp.concatenate([
        jnp.concatenate([o_gdn_p, o_fox_p], axis=1),
        jnp.concatenate([o_gdn_s, o_fox_s], axis=1),
        jnp.zeros((R - tp - ns, 2 * GDN_QK), BF16)], axis=0)
    x1 = matmul(mix, w_out0[0].astype(BF16), res=x0, name="out_proj")
    u0 = matmul(rmsnorm(x1, ln_mlp[0], BF16), w_up[0].astype(BF16), act="relu2", out_dtype=BF16, name="mlp_up0")
    x2 = matmul(u0, w_down[0].astype(BF16), res=x1, name="mlp_down0")

    h1 = rmsnorm(x2, ln_mix[1], F32)
    h1_s = h1[tp:tp + ns].reshape(nb, nq, D)
    shift_rows = jnp.concatenate([jnp.repeat(state_rwkv_shift[0].astype(F32), nq, axis=0),
                                  jnp.zeros((R - tp - ns, D), F32)], axis=0)
    xr, xw, xk, xv, xa, xg = rwkv_mix(h1, shift_rows, rwkv_mu[0], tp, ns, nq)

    def pad_cols(w):
        return jnp.pad(w, ((0, 0), (0, LANES - w.shape[1]))).astype(BF16)

    def pad_rows(w):
        return jnp.pad(w, ((0, LANES - w.shape[0]), (0, 0))).astype(BF16)

    r_ = matmul(xr, rwkv_w_r[0].astype(BF16), name="rwkv_r")
    k_ = matmul(xk, rwkv_w_k[0].astype(BF16), name="rwkv_k")
    v_ = matmul(xv, rwkv_w_v[0].astype(BF16), name="rwkv_v")
    wl = matmul(matmul(xw, pad_cols(rwkv_w1[0]), act="tanh", out_dtype=BF16, name="rwkv_w1"), pad_rows(rwkv_w2[0]), name="rwkv_w2")
    al = matmul(matmul(xa, pad_cols(rwkv_a1[0]), out_dtype=BF16, name="rwkv_a1"), pad_rows(rwkv_a2[0]), name="rwkv_a2")
    g_ = matmul(matmul(xg, rwkv_g1[0].astype(BF16), act="sigmoid", out_dtype=BF16, name="rwkv_g1"), rwkv_g2[0].astype(BF16), name="rwkv_g2")

    prm = jnp.stack([rwkv_w0[0], rwkv_a0[0], rwkv_k_k[0], rwkv_k_a[0], rwkv_r_k[0].reshape(D), rwkv_ln_w[0], rwkv_ln_b[0],
                     jnp.zeros((D,), F32)], axis=0).astype(F32)
    RC = 64
    o_rw_p, s_rw_p = rwkv(r_, k_, v_, wl, al, g_, prm, jnp.zeros((1, 2 * RWKV_PAIRS, RWKV_HEAD, RWKV_HEAD), F32),
                          B=1, NC=tp // RC, C=RC, first_valid=pad, name="rwkv_prompt")
    sx = [_sample_rows(t, tp, nb, nq, front) for t in (r_, k_, v_, wl, al, g_)]
    o_rw_s, s_rw_s = rwkv(*sx, prm, _only(state_rwkv_S), B=nb, NC=1, C=CS, first_valid=front, name="rwkv_sample")
    o_rw_s = o_rw_s.reshape(nb, CS, D)[:, front:].reshape(ns, D)
    o_rw = jnp.concatenate([o_rw_p, o_rw_s, jnp.zeros((R - tp - ns, D), BF16)], axis=0)
    x3 = matmul(o_rw, rwkv_w_o[0].astype(BF16), res=x2, name="rwkv_o")
    u1 = matmul(rmsnorm(x3, ln_mlp[1], BF16), w_up[1].astype(BF16), act="relu2", out_dtype=BF16, name="mlp_up1")
    x4 = matmul(u1, w_down[1].astype(BF16), res=x3, name="mlp_down1")
    y = rmsnorm(x4, ln_final, F32)

    r0 = pad
    y_prompt = y[r0 + N_META:tp].reshape(1, seq, D)
    y_sample = y[tp:tp + ns].reshape(nb, nq, D)

    def kv_rows(col):
        blk = p[:, col:col + FOX_W]
        return (blk[r0:tp].reshape(1, 1, tprompt, FOX_HEADS, FOX_DH),
                blk[tp:tp + ns].reshape(1, nb, nq, FOX_HEADS, FOX_DH))

    fk_p, fk_s = kv_rows(COL_FK)
    fv_p, fv_s = kv_rows(COL_FV)
    lf8 = lf[:, LANE_F:LANE_F + FOX_HEADS]
    lf_p = lf8[r0:tp].reshape(1, 1, tprompt, FOX_HEADS)
    lf_sm = lf8[tp:tp + ns].reshape(1, nb, nq, FOX_HEADS)
    cb_p = p[tp - 3:tp, :GDN_QKV].reshape(1, 1, 3, GDN_QKV)
    cb_s = p[tp:tp + ns, :GDN_QKV].reshape(nb, nq, GDN_QKV)[:, nq - 3:][None]
    gs_p = s_gdn_p[None]
    gs_s = s_gdn_s[None]
    sh_p = h1[tp - 1].reshape(1, 1, D)
    sh_s = h1_s[:, nq - 1][None]
    rs_p = s_rw_p[None]
    rs_s = s_rw_s[None]
    return (y_prompt, y_sample, fk_p, fk_s, fv_p, fv_s, lf_p, lf_sm, cb_p, cb_s, gs_p, gs_s, sh_p, sh_s, rs_p, rs_s)
```

```python
import functools
import math

import jax
import jax.numpy as jnp
from jax import lax
from jax.experimental import pallas as pl
from jax.experimental.pallas import tpu as pltpu

F32 = jnp.float32
BF16 = jnp.bfloat16

D_MODEL = 2048
N_META = 16
GDN_HEADS = 8
GDN_D = 128
GDN_QK = GDN_HEADS * GDN_D
GDN_QKV = 3 * GDN_QK
FOX_HEADS = 8
FOX_DH = 128
FOX_W = FOX_HEADS * FOX_DH
RWKV_HEAD = 64
RWKV_PAIRS = D_MODEL // (2 * RWKV_HEAD)
NORM_EPS = 1e-6
L2_EPS = 1e-6
GN_EPS = 64e-5
NEG_INF = -1e30

LANES = 128
SUBLANES = 8
ROW_TILE = 640
MM_ROW_TILE = 1792
LOG2E = 1.4426950408889634
VMEM_LIMIT = 48 * 1024 * 1024
FOX_HEADS_PER_STEP = 4
FOX_GROUP = 16
RWKV_PREP_PAIRS = 8

COL_Z = GDN_QKV
COL_FQ = COL_Z + GDN_QK
COL_FK = COL_FQ + FOX_W
COL_FV = COL_FK + FOX_W
COL_FZ = COL_FV + FOX_W
LANE_A = 0
LANE_B = 8
LANE_F = 16
PRM_W0, PRM_A0, PRM_KK, PRM_KA, PRM_RK, PRM_LNW, PRM_LNB = range(7)


def _pick(n, cands):
    for c in cands:
        if n % c == 0:
            return c
    raise ValueError(f"no tile for {n}")


def _cparams(sem):
    return pltpu.CompilerParams(dimension_semantics=sem, vmem_limit_bytes=VMEM_LIMIT)


def _dot(a, b):
    return jnp.dot(a, b, preferred_element_type=F32)


def _dot_nt(a, b):
    return lax.dot_general(a, b, (((1,), (1,)), ((), ())), preferred_element_type=F32)


def _dot_tn(a, b):
    return lax.dot_general(a, b, (((0,), (0,)), ((), ())), preferred_element_type=F32)


def _bf(x):
    return x.astype(BF16)


def _softplus(x):
    return jnp.maximum(x, 0.0) + jnp.log(1.0 + jnp.exp(-jnp.abs(x)))


def _sigmoid(x):
    return 1.0 / (1.0 + jnp.exp(-x))


def _iota2(shape, dim):
    return lax.broadcasted_iota(jnp.int32, shape, dim)


def _div(x, n):
    assert n & (n - 1) == 0
    return x >> (n.bit_length() - 1)


def _split2(x):
    hi = x.astype(BF16)
    return hi, (x - hi.astype(F32)).astype(BF16)


def _split3(x):
    p1 = x.astype(BF16)
    rem = x - p1.astype(F32)
    p2 = rem.astype(BF16)
    return p1, p2, (rem - p2.astype(F32)).astype(BF16)


def _dot3(a, b):
    m = a.shape[0]
    ah, al = _split2(a)
    bh, bl = _split2(b)
    top = _dot(jnp.concatenate([ah, al], axis=0), bh)
    return (top[:m] + top[m:]) + _dot(ah, bl)


def _dot_lhs01(a01, b):
    a = a01.astype(BF16)
    b1, b2, b3 = _split3(b)
    return _dot(a, b1) + (_dot(a, b2) + _dot(a, b3))


def _dot_rhs01(a, b01):
    b = b01.astype(BF16)
    a1, a2, a3 = _split3(a)
    m = a.shape[0]
    out = _dot(jnp.concatenate([a1, a2, a3], axis=0), b)
    return out[:m] + (out[m:2 * m] + out[2 * m:])


def _tri_inv_steps(n_mat, n):
    size = n_mat.shape[0]
    eye = (_iota2((size, size), 0) == _iota2((size, size), 1)).astype(F32)
    t = eye + n_mat
    if n <= 2:
        return t
    p = _dot(_bf(n_mat), _bf(n_mat))
    yield
    m = 2
    while m < n:
        p_bf = _bf(p)
        if 2 * m < n:
            both = _dot(_bf(jnp.concatenate([p, t], axis=0)), p_bf)
            p = both[:size]
            t = t + both[size:]
        else:
            t = t + _dot(_bf(t), p_bf)
        m *= 2
        yield
    return t


def _run_interleaved(gens):
    results = [None] * len(gens)
    live = list(enumerate(gens))
    while live:
        nxt = []
        for idx, gen in live:
            try:
                next(gen)
                nxt.append((idx, gen))
            except StopIteration as stop:
                results[idx] = stop.value
        live = nxt
    return results


def _rmsnorm_kernel(x_ref, w_ref, o_ref):
    x = x_ref[...]
    ms = jnp.mean(x * x, axis=-1, keepdims=True)
    o_ref[...] = (x * lax.rsqrt(ms + NORM_EPS) * w_ref[...]).astype(o_ref.dtype)


def rmsnorm(x, w, out_dtype):
    m, d = x.shape
    tm = _pick(m, (320, 256, 128))
    return pl.pallas_call(
        _rmsnorm_kernel,
        grid=(m // tm,),
        in_specs=[pl.BlockSpec((tm, d), lambda i: (i, 0)), pl.BlockSpec((1, d), lambda i: (0, 0))],
        out_specs=pl.BlockSpec((tm, d), lambda i: (i, 0)),
        out_shape=jax.ShapeDtypeStruct((m, d), out_dtype),
        compiler_params=_cparams(("parallel",)),
        name="rmsnorm",
    )(x, w.reshape(1, d))


def _mm_kernel(nk, nx, act, has_res, *refs):
    x_refs, w_ref = refs[:nx], refs[nx]
    r_ref = refs[nx + 1] if has_res else None
    o_ref = refs[nx + 1 + has_res]
    scr = refs[nx + 2 + has_res:]

    def product():
        if nx == 1:
            return _dot(x_refs[0][...], w_ref[...])
        acc, k0 = None, 0
        for x_ref in x_refs:
            kw = x_ref.shape[1]
            part = _dot(x_ref[...], w_ref[k0:k0 + kw, :])
            acc = part if acc is None else acc + part
            k0 += kw
        return acc

    def finish(acc):
        if act == "relu2":
            acc = jnp.square(jnp.maximum(acc, 0.0))
        elif act == "tanh":
            acc = jnp.tanh(acc)
        elif act == "sigmoid":
            acc = _sigmoid(acc)
        if has_res:
            acc = r_ref[...] + acc
        o_ref[...] = acc.astype(o_ref.dtype)

    if nk == 1:
        finish(product())
    else:
        acc_ref = scr[0]
        k = pl.program_id(2)

        @pl.when(k == 0)
        def _():
            acc_ref[...] = jnp.zeros(acc_ref.shape, F32)

        acc_ref[...] += product()

        @pl.when(k == nk - 1)
        def _():
            finish(acc_ref[...])


def matmul(x, w, *, layer=None, act=None, res=None, out_dtype=F32, name="matmul"):
    xs = list(x) if isinstance(x, (list, tuple)) else [x]
    m = xs[0].shape[0]
    k = sum(xi.shape[1] for xi in xs)
    n = w.shape[-1]
    tm = _pick(m, (MM_ROW_TILE, ROW_TILE, 512, 256, 128))
    tn = _pick(n, (512, 256, 128))
    tk = k if k <= 2048 else 2048
    nk = k // tk
    assert len(xs) == 1 or nk == 1
    in_specs = [pl.BlockSpec((tm, tk if len(xs) == 1 else xi.shape[1]), lambda i, j, kk: (i, kk)) for xi in xs]
    if w.ndim == 3:
        in_specs.append(pl.BlockSpec((None, tk, tn), lambda i, j, kk: (layer, kk, j)))
    else:
        in_specs.append(pl.BlockSpec((tk, tn), lambda i, j, kk: (kk, j)))
    args = xs + [w]
    if res is not None:
        in_specs.append(pl.BlockSpec((tm, tn), lambda i, j, kk: (i, j)))
        args.append(res)
    return pl.pallas_call(
        functools.partial(_mm_kernel, nk, len(xs), act, res is not None),
        grid=(m // tm, n // tn, nk),
        in_specs=in_specs,
        out_specs=pl.BlockSpec((tm, tn), lambda i, j, kk: (i, j)),
        out_shape=jax.ShapeDtypeStruct((m, n), out_dtype),
        scratch_shapes=[pltpu.VMEM((tm, tn), F32)] if nk > 1 else [],
        compiler_params=_cparams(("parallel", "parallel", "arbitrary")),
        name=name,
    )(*args)


def _gdn_kernel(C, NC, first_valid, has_conv, qkv_ref, z_ref, ps_ref, cw_ref, alog_ref, dt_ref, nw_ref, s0_ref,
                *rest):
    conv_ref = rest[0] if has_conv else None
    o_ref, sout_ref, S, ext = rest[1:] if has_conv else rest
    c = pl.program_id(1)

    @pl.when(c == 0)
    def _():
        S[...] = s0_ref[0]
        ext[0:SUBLANES, :] = jnp.zeros((SUBLANES, GDN_QKV), F32)

    x = qkv_ref[...]
    ext[SUBLANES:SUBLANES + C, :] = x
    if has_conv:
        assert NC == 1 and 3 <= first_valid < C
        ext[SUBLANES + first_valid - 3:SUBLANES + first_valid, :] = conv_ref[0, 0]
    cw = cw_ref[...]
    y = ((ext[5:5 + C, :] * cw[0:1] + ext[6:6 + C, :] * cw[1:2]) + ext[7:7 + C, :] * cw[2:3]) + ext[8:8 + C, :] * cw[3:4]
    ext[0:SUBLANES, :] = x[C - SUBLANES:C]
    y = y * _sigmoid(y)

    rows = c * C + _iota2((C, 1), 0)
    valid = rows >= first_valid
    ps = ps_ref[...]
    g_all = jnp.where(valid, -jnp.exp(alog_ref[...]) * _softplus(ps + dt_ref[...]), 0.0)
    beta_all = jnp.where(valid, _sigmoid(ps), 0.0)
    ri = _iota2((C, C), 0)
    ci = _iota2((C, C), 1)
    causal = ri >= ci
    strict = ri > ci
    gc = _dot_lhs01(causal.astype(F32), g_all)
    gct = _dot_rhs01(g_all.T, (ri <= ci).astype(F32))
    nw = nw_ref[...]

    def head_steps(h):
        sl = slice(h * GDN_D, (h + 1) * GDN_D)
        qh = y[:, sl]
        kh = y[:, GDN_QK + h * GDN_D:GDN_QK + (h + 1) * GDN_D]
        vh = jnp.where(valid, y[:, 2 * GDN_QK + h * GDN_D:2 * GDN_QK + (h + 1) * GDN_D], 0.0)
        qh = jnp.where(valid, qh * lax.rsqrt(jnp.sum(qh * qh, axis=-1, keepdims=True) + L2_EPS) * GDN_D ** -0.5, 0.0)
        kh = jnp.where(valid, kh * lax.rsqrt(jnp.sum(kh * kh, axis=-1, keepdims=True) + L2_EPS), 0.0)
        bcol = beta_all[:, LANE_B + h:LANE_B + h + 1]
        gcol = gc[:, LANE_A + h:LANE_A + h + 1]
        grow = gct[LANE_A + h:LANE_A + h + 1, :]
        glast = gc[C - 1:C, LANE_A + h:LANE_A + h + 1]
        diff = gcol - grow
        decay = jnp.where(causal, jnp.exp(jnp.where(causal, diff, 0.0)), 0.0)
        kb = kh * bcol
        a_mat = jnp.where(strict, _dot_nt(_bf(kb), _bf(kh)) * decay, 0.0)
        qk = _dot_nt(_bf(qh), _bf(kh)) * decay
        eg = jnp.exp(gcol)
        rhs = jnp.concatenate([vh * bcol, kb * eg], axis=1)
        q_dec = qh * eg
        k_dec = kh * jnp.exp(glast - gcol)
        yield
        t_mat = yield from _tri_inv_steps(-a_mat, C)
        sol = _dot3(t_mat, rhs)
        u = sol[:, :GDN_D]
        w = sol[:, GDN_D:]
        yield
        s_h = S[h]
        s_bf = _bf(s_h)
        v_new = u - _dot(_bf(w), s_bf)
        o_state = _dot(_bf(q_dec), s_bf)
        yield
        o = o_state + _dot(_bf(qk), _bf(v_new))
        S[h] = s_h * jnp.exp(glast) + _dot_tn(_bf(k_dec), _bf(v_new))
        yield
        on = o * lax.rsqrt(jnp.mean(o * o, axis=-1, keepdims=True) + NORM_EPS) * nw
        zh = z_ref[:, sl]
        o_ref[:, sl] = (on * (zh * _sigmoid(zh))).astype(o_ref.dtype)

    _run_interleaved([head_steps(h) for h in range(GDN_HEADS)])

    @pl.when(c == NC - 1)
    def _():
        sout_ref[0] = S[...]


def gdn(qkv_arr, z_arr, ps_arr, z_col, conv_w, alog_pad, dt_pad, norm_w, s0, conv0, *, B, NC, C, first_valid, name,
        out_rows=None):
    rows = B * NC * C
    has_conv = conv0 is not None
    extra_specs = [pl.BlockSpec((1, 1, 3, GDN_QKV), lambda b, c: (0, b, 0, 0))] if has_conv else []
    extra_args = [conv0] if has_conv else []
    return pl.pallas_call(
        functools.partial(_gdn_kernel, C, NC, first_valid, has_conv),
        grid=(B, NC),
        in_specs=[
            pl.BlockSpec((C, GDN_QKV), lambda b, c: (b * NC + c, 0)),
            pl.BlockSpec((C, GDN_QK), lambda b, c: (b * NC + c, z_col)),
            pl.BlockSpec((C, LANES), lambda b, c: (b * NC + c, 0)),
            pl.BlockSpec((4, GDN_QKV), lambda b, c: (0, 0)),
            pl.BlockSpec((1, LANES), lambda b, c: (0, 0)),
            pl.BlockSpec((1, LANES), lambda b, c: (0, 0)),
            pl.BlockSpec((1, GDN_D), lambda b, c: (0, 0)),
            pl.BlockSpec((1, GDN_HEADS, GDN_D, GDN_D), lambda b, c: (b, 0, 0, 0)),
        ] + extra_specs,
        out_specs=[
            pl.BlockSpec((C, GDN_QK), lambda b, c: (b * NC + c, 0)),
            pl.BlockSpec((1, GDN_HEADS, GDN_D, GDN_D), lambda b, c: (b, 0, 0, 0)),
        ],
        out_shape=[
            jax.ShapeDtypeStruct((out_rows or rows, GDN_QK), BF16),
            jax.ShapeDtypeStruct((B, GDN_HEADS, GDN_D, GDN_D), F32),
        ],
        scratch_shapes=[pltpu.VMEM((GDN_HEADS, GDN_D, GDN_D), F32), pltpu.VMEM((C + SUBLANES, GDN_QKV), F32)],
        compiler_params=_cparams(("parallel", "arbitrary")),
        name=name,
    )(qkv_arr, z_arr, ps_arr, conv_w, alog_pad, dt_pad, norm_w, s0, *extra_args)


def _fox_prep_kernel(tb, first_valid, ps_ref, bf_ref, lf_ref, cq_ref, ck_ref, carry):
    i = pl.program_id(0)

    @pl.when(i == 0)
    def _():
        carry[...] = jnp.zeros(carry.shape, F32)

    x = ps_ref[...] + bf_ref[...]
    rows = i * tb + _iota2((tb, 1), 0)
    lf = jnp.where(rows >= first_valid, jnp.minimum(x, 0.0) - jnp.log(1.0 + jnp.exp(-jnp.abs(x))), 0.0)
    tri = (_iota2((tb, tb), 0) >= _iota2((tb, tb), 1)).astype(F32)
    c = _dot_lhs01(tri, lf) + carry[0:1, :]
    carry[...] = jnp.broadcast_to(c[tb - 1:tb, :], carry.shape)
    lf_ref[...] = lf
    c2 = c * LOG2E
    for h in range(FOX_HEADS):
        cq_ref[h] = jnp.broadcast_to(c2[:, LANE_F + h:LANE_F + h + 1], (tb, LANES))
    ck_ref[...] = c2.T[LANE_F:LANE_F + FOX_HEADS, :]


def fox_prep(ps, bf_pad, first_valid):
    r = ps.shape[0]
    tb = _pick(r, (ROW_TILE, 512, 256, 128))
    return pl.pallas_call(
        functools.partial(_fox_prep_kernel, tb, first_valid),
        grid=(r // tb,),
        in_specs=[pl.BlockSpec((tb, LANES), lambda i: (i, 0)), pl.BlockSpec((1, LANES), lambda i: (0, 0))],
        out_specs=[
            pl.BlockSpec((tb, LANES), lambda i: (i, 0)),
            pl.BlockSpec((FOX_HEADS, tb, LANES), lambda i: (0, i, 0)),
            pl.BlockSpec((FOX_HEADS, tb), lambda i: (0, i)),
        ],
        out_shape=[
            jax.ShapeDtypeStruct((r, LANES), F32),
            jax.ShapeDtypeStruct((FOX_HEADS, r, LANES), F32),
            jax.ShapeDtypeStruct((FOX_HEADS, r), F32),
        ],
        scratch_shapes=[pltpu.VMEM((SUBLANES, LANES), F32)],
        compiler_params=_cparams(("arbitrary",)),
        name="fox_prep",
    )(ps, bf_pad)


def _fox_flash_kernel(tq, first_valid, qi_ref, ki_ref, qt_ref, k_ref, vt_ref, ck_ref, cq_ref, fz_ref, o_ref,
                      m_s, l_s, acc_s, t_s):
    step = pl.program_id(1)
    qi = qi_ref[step]
    ki = ki_ref[step]
    reps = tq // LANES
    heads = qt_ref.shape[0]

    @pl.when(ki == 0)
    def _():
        m_s[...] = jnp.full(m_s.shape, NEG_INF, F32)
        l_s[...] = jnp.zeros(l_s.shape, F32)
        acc_s[...] = jnp.zeros(acc_s.shape, F32)

    def head_steps(h, masked):
        cols = slice(h * FOX_DH, (h + 1) * FOX_DH)
        qt = _bf(qt_ref[h] * (FOX_DH ** -0.5 * LOG2E))
        cq = cq_ref[h]
        m_prev = m_s[h, 0:1, :]
        kb = LANES
        groups = kb // SUBLANES
        mx = None
        for b in range(tq // kb):
            rs = slice(b * kb, (b + 1) * kb)
            t = _dot(_bf(k_ref[rs, cols]), qt) - jnp.concatenate([ck_ref[h, rs, :]] * reps, axis=1)
            if masked:
                kpos = ki * tq + b * kb + _iota2((kb, tq), 0)
                qpos = qi * tq + _iota2((kb, tq), 1)
                t = jnp.where((kpos <= qpos) & (kpos >= first_valid), t, NEG_INF)
            t_s[h, rs, :] = t
            part = jnp.max(t.reshape(groups, SUBLANES, tq), axis=0)
            mx = part if mx is None else jnp.maximum(mx, part)
            yield
        m_new = jnp.maximum(m_prev, jnp.max(mx, axis=0, keepdims=True) + cq)
        shift = cq - m_new
        alpha = jnp.exp2(m_prev - m_new)
        lsum = None
        pv = None
        for b in range(tq // kb):
            rs = slice(b * kb, (b + 1) * kb)
            p = jnp.exp2(t_s[h, rs, :] + shift)
            part = jnp.sum(p.reshape(groups, SUBLANES, tq), axis=0)
            lsum = part if lsum is None else lsum + part
            d = _dot(_bf(vt_ref[h, :, rs]), _bf(p))
            pv = d if pv is None else pv + d
            yield
        l_new = alpha * l_s[h, 0:1, :] + jnp.sum(lsum, axis=0, keepdims=True)
        l_s[h] = jnp.broadcast_to(l_new, (SUBLANES, tq))
        acc_s[h] = alpha * acc_s[h] + pv
        m_s[h] = jnp.broadcast_to(m_new, (SUBLANES, tq))

    def accumulate(masked):
        _run_interleaved([head_steps(h, masked) for h in range(heads)])

    edge = (ki == qi) | (ki == 0)
    pl.when(edge)(lambda: accumulate(True))
    pl.when(jnp.logical_not(edge))(lambda: accumulate(False))

    @pl.when(ki == qi)
    def _():
        rows = qi * tq + _iota2((tq, 1), 0)
        for h in range(heads):
            cols = slice(h * FOX_DH, (h + 1) * FOX_DH)
            o = (acc_s[h] / l_s[h, 0:1, :]).T * _sigmoid(fz_ref[:, cols])
            o_ref[:, cols] = jnp.where(rows >= first_valid, o, 0.0).astype(o_ref.dtype)


def fox_prompt(p, c_rep, c_row, tp, first_valid, out_rows):
    tq = _pick(tp, (ROW_TILE, 512, 384, 256, 128))
    nq = tp // tq
    assert first_valid < tq
    pairs = [(qi, ki) for qi in range(nq) for ki in range(qi + 1)]
    qi_arr = jnp.array([a for a, _ in pairs], jnp.int32)
    ki_arr = jnp.array([b for _, b in pairs], jnp.int32)
    hp = FOX_HEADS_PER_STEP
    wid = hp * FOX_DH
    cb = lambda col, g: col // wid + g

    def heads_t(col):
        return jnp.transpose(p[:tp, col:col + FOX_W].reshape(tp, FOX_HEADS, FOX_DH), (1, 2, 0))

    return pl.pallas_call(
        functools.partial(_fox_flash_kernel, tq, first_valid),
        grid_spec=pltpu.PrefetchScalarGridSpec(
            num_scalar_prefetch=2,
            grid=(FOX_HEADS // hp, len(pairs)),
            in_specs=[
                pl.BlockSpec((hp, FOX_DH, tq), lambda g, t, qa, ka: (g, 0, qa[t])),
                pl.BlockSpec((tq, wid), lambda g, t, qa, ka: (ka[t], cb(COL_FK, g))),
                pl.BlockSpec((hp, FOX_DH, tq), lambda g, t, qa, ka: (g, 0, ka[t])),
                pl.BlockSpec((hp, tq, LANES), lambda g, t, qa, ka: (g, ka[t], 0)),
                pl.BlockSpec((hp, 1, tq), lambda g, t, qa, ka: (g, 0, qa[t])),
                pl.BlockSpec((tq, wid), lambda g, t, qa, ka: (qa[t], cb(COL_FZ, g))),
            ],
            out_specs=pl.BlockSpec((tq, wid), lambda g, t, qa, ka: (qa[t], g)),
            scratch_shapes=[pltpu.VMEM((hp, SUBLANES, tq), F32), pltpu.VMEM((hp, SUBLANES, tq), F32),
                            pltpu.VMEM((hp, FOX_DH, tq), F32), pltpu.VMEM((hp, tq, tq), F32)],
        ),
        out_shape=jax.ShapeDtypeStruct((out_rows, FOX_W), BF16),
        compiler_params=_cparams(("parallel", "arbitrary")),
        name="fox_prompt",
    )(qi_arr, ki_arr, heads_t(COL_FQ), p, heads_t(COL_FV), c_rep, c_row.reshape(FOX_HEADS, 1, -1), p)


def _fox_sample_kernel(G, ngroups, pt_ref, q_ref, *refs):
    k_refs, v_refs, lft_refs = refs[0:G], refs[G:2 * G], refs[2 * G:3 * G]
    lfn_ref, kn_ref, vn_ref, fz_ref, o_ref, cq_s, cn_s, carry_s, m_s, l_s, acc_s = refs[3 * G:]
    jg = pl.program_id(1)
    rows = q_ref.shape[1]
    page = lft_refs[0].shape[2]
    flat = page * FOX_HEADS
    nn = lfn_ref.shape[2]
    scale = FOX_DH ** -0.5

    @pl.when(jg == 0)
    def _():
        upper = (_iota2((nn, nn), 0) <= _iota2((nn, nn), 1)).astype(F32)
        cn = _dot_rhs01(lfn_ref[0], upper)
        cn_s[...] = cn
        own_q = _iota2((rows, nn), 1) == _div(_iota2((rows, nn), 0), FOX_HEADS)
        cq_s[...] = jnp.broadcast_to(jnp.sum(jnp.where(own_q, cn, 0.0), axis=-1, keepdims=True), cq_s.shape)
        carry_s[...] = jnp.zeros(carry_s.shape, F32)
        m_s[...] = jnp.full(m_s.shape, NEG_INF, F32)
        l_s[...] = jnp.zeros(l_s.shape, F32)
        acc_s[...] = jnp.zeros(acc_s.shape, F32)

    def update(ts, vals_bf):
        cq = cq_s[:, 0:1]
        m_prev = m_s[:, 0:1]
        t_max = functools.reduce(jnp.maximum, ts)
        m_new = jnp.maximum(m_prev, jnp.max(t_max, axis=-1, keepdims=True) + cq)
        shift = cq - m_new
        ps = [jnp.exp(t + shift) for t in ts]
        alpha = jnp.exp(m_prev - m_new)
        l_s[...] = alpha * l_s[...] + jnp.sum(functools.reduce(jnp.add, ps), axis=-1, keepdims=True)
        pv = functools.reduce(jnp.add, [_dot(_bf(p), v) for p, v in zip(ps, vals_bf)])
        acc_s[...] = alpha * acc_s[...] + pv
        m_s[...] = jnp.broadcast_to(m_new, m_s.shape)

    own = (_iota2((rows, flat), 0) & (FOX_HEADS - 1)) == (_iota2((rows, flat), 1) & (FOX_HEADS - 1))
    later = (_iota2((page, flat), 0) > _div(_iota2((page, flat), 1), FOX_HEADS)).astype(BF16)
    head_col = _iota2((FOX_HEADS, flat), 0) == (_iota2((FOX_HEADS, flat), 1) & (FOX_HEADS - 1))
    carry = carry_s[:, 0:1]
    q_bf = _bf(q_ref[0])
    lfts = [lft_refs[g][0] for g in range(G)]
    suffixes = _dot_rhs01(jnp.concatenate(lfts, axis=0), later)
    ts = []
    for g in range(G):
        suffix = suffixes[g * FOX_HEADS:(g + 1) * FOX_HEADS]
        d = jnp.sum(jnp.where(head_col, suffix + carry, 0.0), axis=0, keepdims=True)
        carry = carry + jnp.sum(lfts[g], axis=-1, keepdims=True)
        s = _dot_nt(q_bf, _bf(k_refs[g][0])) * scale
        ts.append(jnp.where(own, s + d, NEG_INF))
    update(ts, [_bf(v_refs[g][0]) for g in range(G)])
    carry_s[...] = jnp.broadcast_to(carry, carry_s.shape)

    @pl.when(jg == ngroups - 1)
    def _():
        nflat = kn_ref.shape[1]
        spread = (_iota2((nn, nflat), 0) == _div(_iota2((nn, nflat), 1), FOX_HEADS)).astype(F32)
        cn_cols = _dot_rhs01(cn_s[...], spread)
        ri = _iota2((rows, nflat), 0)
        ci = _iota2((rows, nflat), 1)
        ok = ((ri & (FOX_HEADS - 1)) == (ci & (FOX_HEADS - 1))) & (_div(ci, FOX_HEADS) <= _div(ri, FOX_HEADS))
        sn = _dot_nt(_bf(q_ref[0]), _bf(kn_ref[0])) * scale
        update([jnp.where(ok, sn - cn_cols, NEG_INF)], [_bf(vn_ref[0])])
        o_ref[0] = acc_s[...] / l_s[...] * _sigmoid(fz_ref[0])


def fox_sample(page_table_flat, q_rows, cache_k, cache_v, cache_lft, lfn, kn_flat, vn_flat, fz_rows, nb, npages):
    flat = cache_k.shape[1]
    page = cache_lft.shape[2]
    rows = q_rows.shape[1]
    nn = lfn.shape[2]
    G = _pick(npages, (FOX_GROUP, 2, 1))
    ngroups = npages // G

    def page_map(g):
        return lambda b, jg, pt: (pt[b * npages + (npages - 1 - (jg * G + g))], 0, 0)

    seq = lambda b, jg, pt: (b, 0, 0)
    row_blk = pl.BlockSpec((1, rows, FOX_DH), seq)
    new_blk = pl.BlockSpec((1, kn_flat.shape[1], FOX_DH), seq)
    in_specs = [row_blk]
    in_specs += [pl.BlockSpec((1, flat, FOX_DH), page_map(g)) for g in range(G)]
    in_specs += [pl.BlockSpec((1, flat, FOX_DH), page_map(g)) for g in range(G)]
    in_specs += [pl.BlockSpec((1, FOX_HEADS, page), page_map(g)) for g in range(G)]
    in_specs += [pl.BlockSpec((1, rows, nn), seq), new_blk, new_blk, row_blk]
    return pl.pallas_call(
        functools.partial(_fox_sample_kernel, G, ngroups),
        grid_spec=pltpu.PrefetchScalarGridSpec(
            num_scalar_prefetch=1,
            grid=(nb, ngroups),
            in_specs=in_specs,
            out_specs=row_blk,
            scratch_shapes=[
                pltpu.VMEM((rows, LANES), F32),
                pltpu.VMEM((rows, nn), F32),
                pltpu.VMEM((FOX_HEADS, LANES), F32),
                pltpu.VMEM((rows, LANES), F32),
                pltpu.VMEM((rows, LANES), F32),
                pltpu.VMEM((rows, FOX_DH), F32),
            ],
        ),
        out_shape=jax.ShapeDtypeStruct((nb, rows, FOX_DH), F32),
        compiler_params=_cparams(("parallel", "arbitrary")),
        name="fox_sample",
    )(page_table_flat, q_rows, *([cache_k] * G), *([cache_v] * G), *([cache_lft] * G), lfn, kn_flat, vn_flat, fz_rows)


def _rwkv_mix_kernel(tm, tp, ns, nq, h_ref, hb_ref, st_ref, mu_ref, *o_refs):
    i = pl.program_id(0)
    h = h_ref[...]
    above = jnp.concatenate([hb_ref[SUBLANES - 1:SUBLANES, :], h[:tm - 1]], axis=0)
    row = i * tm + _iota2((tm, 1), 0)
    seq_start = (row >= tp) & (row < tp + ns) & (((row - tp) & (nq - 1)) == 0)
    prev = jnp.where(seq_start, st_ref[...], jnp.where(row == 0, 0.0, above))
    xx = prev - h
    mu = mu_ref[...]
    for j, o_ref in enumerate(o_refs):
        o_ref[...] = (h + xx * mu[j:j + 1]).astype(o_ref.dtype)


def rwkv_mix(h, shift_rows, mu, tp, ns, nq):
    m, d = h.shape
    assert nq & (nq - 1) == 0
    tm = _pick(math.gcd(m, tp), (320, 256, 128))
    first = tp // tm
    spec = pl.BlockSpec((tm, d), lambda i: (i, 0))
    return pl.pallas_call(
        functools.partial(_rwkv_mix_kernel, tm, tp, ns, nq),
        grid=(m // tm,),
        in_specs=[spec,
                  pl.BlockSpec((SUBLANES, d), lambda i: (jnp.maximum(i * (tm // SUBLANES) - 1, 0), 0)),
                  pl.BlockSpec((tm, d), lambda i: (jnp.maximum(i - first, 0), 0)),
                  pl.BlockSpec((6, d), lambda i: (0, 0))],
        out_specs=[spec] * 6,
        out_shape=[jax.ShapeDtypeStruct((m, d), BF16)] * 6,
        compiler_params=_cparams(("parallel",)),
        name="rwkv_mix",
    )(h, h, shift_rows, mu)


def _head_ones():
    return (_div(_iota2((LANES, LANES), 0), RWKV_HEAD) == _div(_iota2((LANES, LANES), 1), RWKV_HEAD)).astype(F32)


def _rwkv_chunk_terms(C, valid, r, k, v, wl, al, prm):
    HD = RWKV_HEAD
    m0 = _iota2((1, LANES), 1) < HD
    bones = _head_ones()
    w0, a0, k_k, k_a, r_k = (prm[i:i + 1] for i in (PRM_W0, PRM_A0, PRM_KK, PRM_KA, PRM_RK))
    wlog = -_softplus(-(w0 + wl)) - 0.5
    lw = jnp.where(valid, -jnp.exp(wlog), 0.0)
    a = _sigmoid(a0 + al)
    kkr = k * k_k
    kk = kkr * lax.rsqrt(_dot_rhs01(kkr * kkr, bones) + L2_EPS)
    k2 = k * (1.0 + (a - 1.0) * k_a)
    bonus = _dot_rhs01(r * k2 * r_k, bones) * v
    rm = jnp.where(valid, r, 0.0)
    k2 = jnp.where(valid, k2, 0.0)
    vm = jnp.where(valid, v, 0.0)
    av = jnp.where(valid, -kk, 0.0)
    bv = jnp.where(valid, kk * a, 0.0)

    ri = _iota2((C, C), 0)
    ci = _iota2((C, C), 1)
    yield
    cum = _dot_lhs01((ri >= ci).astype(F32), lw)
    cum_last = cum[C - 1:C, :]
    inv = jnp.exp(-cum)
    rt = rm * jnp.exp(cum)
    at = av * jnp.exp(cum - lw)
    bt = bv * inv
    kt = k2 * inv
    to_end = jnp.exp(cum_last - cum)
    b_end = bv * to_end
    k_end = k2 * to_end

    def split(x):
        return jnp.concatenate([jnp.where(m0, x, 0.0), jnp.where(m0, 0.0, x)], axis=0)

    def halves(x):
        return x[0:C] + x[C:2 * C]

    yield
    at_s = split(at)
    gram = _dot_nt(_bf(jnp.concatenate([at_s, split(rt)], axis=0)), _bf(jnp.concatenate([bt, kt], axis=0)))
    r2 = _iota2((2 * C, 2 * C), 0)
    c2 = _iota2((2 * C, 2 * C), 1)
    same = _div(r2, C) == _div(c2, C)
    bd_strict = same & (r2 > c2)
    bd_incl = same & (r2 >= c2)

    def bd(block, mask):
        return jnp.where(mask, jnp.concatenate([block, block], axis=1), 0.0)

    a_ab = bd(gram[0:2 * C, 0:C], bd_strict)
    a_ak = bd(gram[0:2 * C, C:2 * C], bd_strict)
    r_b = bd(gram[2 * C:4 * C, 0:C], bd_incl)
    r_k2 = bd(gram[2 * C:4 * C, C:2 * C], bd_incl)
    vs = split(vm)
    akv = halves(_dot(_bf(a_ak), _bf(vs)))
    yield
    t_bd = yield from _tri_inv_steps(a_ab, C)
    tw = _dot(_bf(t_bd), _bf(jnp.concatenate([at_s, split(akv)], axis=1)))
    wt = halves(tw[:, :LANES])
    ut = halves(tw[:, LANES:])
    yield
    rp = rt + halves(_dot(_bf(r_b), _bf(split(wt))))
    y0 = halves(_dot(_bf(jnp.concatenate([r_b, r_k2], axis=1)), _bf(jnp.concatenate([split(ut), vs], axis=0))))
    eye = (_iota2((LANES, LANES), 0) == _iota2((LANES, LANES), 1)).astype(F32)
    m_mat = eye * jnp.exp(cum_last) + bones * _dot_tn(_bf(b_end), _bf(wt))
    n_mat = bones * _dot_tn(_bf(jnp.concatenate([b_end, k_end], axis=0)), _bf(jnp.concatenate([ut, vm], axis=0)))
    return rp, y0, bonus, m_mat, n_mat


def _rwkv_prep_kernel(C, chunk_of_step, first_valid, PG, r_ref, k_ref, v_ref, wl_ref, al_ref, prm_ref,
                      rp_ref, y0_ref, bo_ref, m_ref, n_ref):
    c = pl.program_id(0) if chunk_of_step else 0
    valid = (c * C + _iota2((C, 1), 0)) >= first_valid
    sls = [slice(i * LANES, (i + 1) * LANES) for i in range(PG)]
    terms = _run_interleaved([
        _rwkv_chunk_terms(C, valid, r_ref[:, sl], k_ref[:, sl], v_ref[:, sl], wl_ref[:, sl], al_ref[:, sl], prm_ref[:, sl])
        for sl in sls])
    for i, sl in enumerate(sls):
        rp, y0, bonus, m_mat, n_mat = terms[i]
        rp_ref[:, sl] = rp
        y0_ref[:, sl] = y0
        bo_ref[:, sl] = bonus
        m_ref[0, i] = m_mat
        n_ref[0, i] = n_mat


def _rwkv_scan_kernel(NC, rp_ref, y0_ref, bo_ref, g_ref, m_ref, n_ref, prm_ref, s0_ref, o_ref, sout_ref, H):
    c = pl.program_id(1)
    HD = RWKV_HEAD
    bones = _head_ones()

    @pl.when(c == 0)
    def _():
        zero = jnp.zeros((HD, HD), F32)
        for i in range(RWKV_PAIRS):
            top = jnp.concatenate([s0_ref[0, 2 * i], zero], axis=1)
            bot = jnp.concatenate([zero, s0_ref[0, 2 * i + 1]], axis=1)
            H[i] = jnp.concatenate([top, bot], axis=0).T

    def pair_steps(i):
        sl = slice(i * LANES, (i + 1) * LANES)
        h_bd = H[i]
        y = _dot(_bf(rp_ref[:, sl]), _bf(h_bd)) + y0_ref[:, sl]
        H[i] = _dot3(m_ref[0, i], h_bd) + n_ref[0, i]
        yield
        mean = _dot_rhs01(y, bones) * (1.0 / HD)
        dlt = y - mean
        yield
        var = _dot_rhs01(dlt * dlt, bones) * (1.0 / HD)
        yn = dlt * lax.rsqrt(var + GN_EPS) * prm_ref[PRM_LNW:PRM_LNW + 1, sl] + prm_ref[PRM_LNB:PRM_LNB + 1, sl]
        o_ref[:, sl] = ((yn + bo_ref[:, sl]) * g_ref[:, sl]).astype(o_ref.dtype)

    _run_interleaved([pair_steps(i) for i in range(RWKV_PAIRS)])

    @pl.when(c == NC - 1)
    def _():
        for i in range(RWKV_PAIRS):
            ht = H[i].T
            sout_ref[0, 2 * i] = ht[0:HD, 0:HD]
            sout_ref[0, 2 * i + 1] = ht[HD:2 * HD, HD:2 * HD]


def rwkv(r, k, v, wl, al, g, prm, s0, *, B, NC, C, first_valid, name, out_rows=None):
    assert B == 1 or NC == 1
    nblk = B * NC
    rows = nblk * C
    PG = RWKV_PREP_PAIRS
    wide = pl.BlockSpec((C, PG * LANES), lambda blk, pg: (blk, pg))
    mat = pl.BlockSpec((1, PG, LANES, LANES), lambda blk, pg: (blk, pg, 0, 0))
    mat_shape = jax.ShapeDtypeStruct((nblk, RWKV_PAIRS, LANES, LANES), F32)
    row_shape = jax.ShapeDtypeStruct((rows, D_MODEL), F32)
    rp, y0, bonus, m_all, n_all = pl.pallas_call(
        functools.partial(_rwkv_prep_kernel, C, B == 1, first_valid, PG),
        grid=(nblk, RWKV_PAIRS // PG),
        in_specs=[wide] * 5 + [pl.BlockSpec((SUBLANES, PG * LANES), lambda blk, pg: (0, pg))],
        out_specs=[wide, wide, wide, mat, mat],
        out_shape=[row_shape, row_shape, row_shape, mat_shape, mat_shape],
        compiler_params=_cparams(("parallel", "parallel")),
        name=name + "_terms",
    )(r, k, v, wl, al, prm)

    full = pl.BlockSpec((C, D_MODEL), lambda b, c: (b * NC + c, 0))
    mats = pl.BlockSpec((1, RWKV_PAIRS, LANES, LANES), lambda b, c: (b * NC + c, 0, 0, 0))
    st_spec = pl.BlockSpec((1, 2 * RWKV_PAIRS, RWKV_HEAD, RWKV_HEAD), lambda b, c: (b, 0, 0, 0))
    return pl.pallas_call(
        functools.partial(_rwkv_scan_kernel, NC),
        grid=(B, NC),
        in_specs=[full] * 4 + [mats, mats, pl.BlockSpec((SUBLANES, D_MODEL), lambda b, c: (0, 0)), st_spec],
        out_specs=[full, st_spec],
        out_shape=[
            jax.ShapeDtypeStruct((out_rows or rows, D_MODEL), BF16),
            jax.ShapeDtypeStruct((B, 2 * RWKV_PAIRS, RWKV_HEAD, RWKV_HEAD), F32),
        ],
        scratch_shapes=[pltpu.VMEM((RWKV_PAIRS, LANES, LANES), F32)],
        compiler_params=_cparams(("parallel", "arbitrary")),
        name=name + "_scan",
    )(rp, y0, bonus, g, m_all, n_all, prm, s0)


def _pad_lanes(vec, offset):
    out = jnp.zeros((1, LANES), F32)
    return lax.dynamic_update_slice(out, vec.reshape(1, -1).astype(F32), (0, offset))


def _sample_rows(arr, row0, nb, nq, front):
    cols = arr.shape[1]
    s = arr[row0:row0 + nb * nq].reshape(nb, nq, cols)
    s = jnp.pad(s, ((0, 0), (front, 0), (0, 0)))
    return s.reshape(nb * (front + nq), cols)


def _only(x):
    assert x.shape[0] == 1
    return x.reshape(x.shape[1:])


def kernel(x_prompt, x_sample, cache_fox_k, cache_fox_v, cache_fox_logf, state_gdn_conv, state_gdn_S,
           state_rwkv_shift, state_rwkv_S, page_table, meta_tokens, ln_mix, ln_mlp, ln_final,
           w_in0, gdn_conv_w, gdn_A_log, gdn_dt_bias, gdn_norm_w, fox_b_f, w_out0,
           rwkv_mu, rwkv_w0, rwkv_w1, rwkv_w2, rwkv_a0, rwkv_a1, rwkv_a2, rwkv_g1, rwkv_g2,
           rwkv_k_k, rwkv_k_a, rwkv_r_k, rwkv_w_r, rwkv_w_k, rwkv_w_v, rwkv_w_o, rwkv_ln_w, rwkv_ln_b,
           w_up, w_down):
    D = D_MODEL
    assert x_prompt.shape[0] == 1 and x_prompt.shape[2] == D
    seq = x_prompt.shape[1]
    nb, nq = x_sample.shape[0], x_sample.shape[1]
    npages = page_table.shape[1]
    tprompt = N_META + seq
    pad = (-tprompt) % LANES
    tp = tprompt + pad
    ns = nb * nq
    R = -(-(tp + ns) // ROW_TILE) * ROW_TILE
    CS = SUBLANES
    front = CS - nq
    assert 3 <= front

    x0 = jnp.concatenate([jnp.zeros((pad, D), F32), meta_tokens.astype(F32), x_prompt[0],
                          x_sample.reshape(ns, D), jnp.zeros((R - tp - ns, D), F32)], axis=0)

    w_in = w_in0[0]
    o_z = GDN_QKV
    o_a = o_z + GDN_QK
    o_b = o_a + GDN_HEADS
    o_fq = o_b + GDN_HEADS
    o_fk = o_fq + FOX_W
    o_fv = o_fk + FOX_W
    o_ff = o_fv + FOX_W
    o_fz = o_ff + FOX_HEADS
    w_big = jnp.concatenate([w_in[:, :o_a], w_in[:, o_fq:o_ff], w_in[:, o_fz:]], axis=1).astype(BF16)
    w_small = jnp.concatenate([w_in[:, o_a:o_fq], w_in[:, o_ff:o_fz],
                               jnp.zeros((D, LANES - 3 * GDN_HEADS), F32)], axis=1).astype(BF16)

    h0 = rmsnorm(x0, ln_mix[0], BF16)
    p = matmul(h0, w_big, name="in_proj")
    ps = matmul(h0, w_small, name="in_proj_small")

    alog_pad = _pad_lanes(gdn_A_log[0], LANE_A)
    dt_pad = _pad_lanes(gdn_dt_bias[0], LANE_A)
    bf_pad = _pad_lanes(fox_b_f[0], LANE_F)
    conv_w = gdn_conv_w[0]
    norm_w = gdn_norm_w[0].reshape(1, GDN_D)

    GC = 64
    o_gdn_p, s_gdn_p = gdn(p, p, ps, COL_Z // GDN_QK, conv_w, alog_pad, dt_pad, norm_w,
                           jnp.zeros((1, GDN_HEADS, GDN_D, GDN_D), F32), None,
                           B=1, NC=tp // GC, C=GC, first_valid=pad, name="gdn_prompt", out_rows=R)
    qkv_ext = _sample_rows(p[:, :GDN_QKV], tp, nb, nq, front)
    z_ext = _sample_rows(p[:, COL_Z:COL_Z + GDN_QK], tp, nb, nq, front)
    ps_ext = _sample_rows(ps, tp, nb, nq, front)
    o_gdn_s, s_gdn_s = gdn(qkv_ext, z_ext, ps_ext, 0, conv_w, alog_pad, dt_pad, norm_w,
                           _only(state_gdn_S), state_gdn_conv, B=nb, NC=1, C=CS, first_valid=front, name="gdn_sample")
    o_gdn_s = o_gdn_s.reshape(nb, CS, GDN_QK)[:, front:].reshape(ns, GDN_QK)

    lf, cq, ck = fox_prep(ps, bf_pad, pad)
    o_fox_p = fox_prompt(p, cq, ck, tp, pad, R)
    pt_flat = page_table.reshape(-1).astype(jnp.int32)
    lf_s = lf[tp:tp + ns, LANE_F:LANE_F + FOX_HEADS].reshape(nb, nq, FOX_HEADS)
    lfn = jnp.tile(jnp.swapaxes(lf_s, 1, 2), (1, nq, 1))
    lfn = jnp.pad(lfn, ((0, 0), (0, 0), (0, SUBLANES - nq)))
    pool = cache_fox_k.shape[1]
    page = cache_fox_k.shape[2]
    cache_k = _only(cache_fox_k).reshape(pool, page * FOX_HEADS, FOX_DH)
    cache_v = _only(cache_fox_v).reshape(pool, page * FOX_HEADS, FOX_DH)
    cache_lft = jnp.swapaxes(_only(cache_fox_logf), 1, 2)
    def sample_heads(col):
        return p[tp:tp + ns, col:col + FOX_W].reshape(nb, nq * FOX_HEADS, FOX_DH)

    o_fox_s = fox_sample(pt_flat, sample_heads(COL_FQ), cache_k, cache_v, cache_lft, lfn,
                         sample_heads(COL_FK), sample_heads(COL_FV), sample_heads(COL_FZ), nb, npages)
    o_fox_s = o_fox_s.reshape(ns, FOX_W).astype(BF16)

    def with_sample_rows(full, sample):
        full = lax.dynamic_update_slice(full, sample, (tp, 0))
        return lax.dynamic_update_slice(full, jnp.zeros((R - tp - ns, full.shape[1]), full.dtype), (tp + ns, 0))

    w_up_bf = w_up.astype(BF16)
    w_down_bf = w_down.astype(BF16)
    x1 = matmul([with_sample_rows(o_gdn_p, o_gdn_s), with_sample_rows(o_fox_p, o_fox_s)], w_out0[0].astype(BF16),
                res=x0, name="out_proj")
    u0 = matmul(rmsnorm(x1, ln_mlp[0], BF16), w_up_bf, layer=0, act="relu2", out_dtype=BF16, name="mlp_up0")
    x2 = matmul(u0, w_down_bf, layer=0, res=x1, name="mlp_down0")

    h1 = rmsnorm(x2, ln_mix[1], F32)
    h1_s = h1[tp:tp + ns].reshape(nb, nq, D)
    shift_rows = jnp.concatenate([jnp.repeat(state_rwkv_shift[0].astype(F32), nq, axis=0),
                                  jnp.zeros((R - tp - ns, D), F32)], axis=0)
    xr, xw, xk, xv, xa, xg = rwkv_mix(h1, shift_rows, rwkv_mu[0], tp, ns, nq)

    def pad_cols(w):
        return jnp.pad(w, ((0, 0), (0, LANES - w.shape[1]))).astype(BF16)

    def pad_rows(w):
        return jnp.pad(w, ((0, LANES - w.shape[0]), (0, 0))).astype(BF16)

    r_ = matmul(xr, rwkv_w_r[0].astype(BF16), name="rwkv_r")
    k_ = matmul(xk, rwkv_w_k[0].astype(BF16), name="rwkv_k")
    v_ = matmul(xv, rwkv_w_v[0].astype(BF16), name="rwkv_v")
    wl = matmul(matmul(xw, pad_cols(rwkv_w1[0]), act="tanh", out_dtype=BF16, name="rwkv_w1"), pad_rows(rwkv_w2[0]), name="rwkv_w2")
    al = matmul(matmul(xa, pad_cols(rwkv_a1[0]), out_dtype=BF16, name="rwkv_a1"), pad_rows(rwkv_a2[0]), name="rwkv_a2")
    g_ = matmul(matmul(xg, rwkv_g1[0].astype(BF16), act="sigmoid", out_dtype=BF16, name="rwkv_g1"), rwkv_g2[0].astype(BF16), name="rwkv_g2")

    prm = jnp.stack([rwkv_w0[0], rwkv_a0[0], rwkv_k_k[0], rwkv_k_a[0], rwkv_r_k[0].reshape(D), rwkv_ln_w[0], rwkv_ln_b[0],
                     jnp.zeros((D,), F32)], axis=0).astype(F32)
    RC = 64
    o_rw_p, s_rw_p = rwkv(r_, k_, v_, wl, al, g_, prm, jnp.zeros((1, 2 * RWKV_PAIRS, RWKV_HEAD, RWKV_HEAD), F32),
                          B=1, NC=tp // RC, C=RC, first_valid=pad, name="rwkv_prompt", out_rows=R)
    sx = [_sample_rows(t, tp, nb, nq, front) for t in (r_, k_, v_, wl, al, g_)]
    o_rw_s, s_rw_s = rwkv(*sx, prm, _only(state_rwkv_S), B=nb, NC=1, C=CS, first_valid=front, name="rwkv_sample")
    o_rw_s = o_rw_s.reshape(nb, CS, D)[:, front:].reshape(ns, D)
    x3 = matmul(with_sample_rows(o_rw_p, o_rw_s), rwkv_w_o[0].astype(BF16), res=x2, name="rwkv_o")
    u1 = matmul(rmsnorm(x3, ln_mlp[1], BF16), w_up_bf, layer=1, act="relu2", out_dtype=BF16, name="mlp_up1")
    x4 = matmul(u1, w_down_bf, layer=1, res=x3, name="mlp_down1")
    y = rmsnorm(x4, ln_final, F32)

    r0 = pad
    y_prompt = y[r0 + N_META:tp].reshape(1, seq, D)
    y_sample = y[tp:tp + ns].reshape(nb, nq, D)

    def kv_rows(col):
        blk = p[:, col:col + FOX_W]
        return (blk[r0:tp].reshape(1, 1, tprompt, FOX_HEADS, FOX_DH),
                blk[tp:tp + ns].reshape(1, nb, nq, FOX_HEADS, FOX_DH))

    fk_p, fk_s = kv_rows(COL_FK)
    fv_p, fv_s = kv_rows(COL_FV)
    lf8 = lf[:, LANE_F:LANE_F + FOX_HEADS]
    lf_p = lf8[r0:tp].reshape(1, 1, tprompt, FOX_HEADS)
    lf_sm = lf8[tp:tp + ns].reshape(1, nb, nq, FOX_HEADS)
    cb_p = p[tp - 3:tp, :GDN_QKV].reshape(1, 1, 3, GDN_QKV)
    cb_s = p[tp:tp + ns, :GDN_QKV].reshape(nb, nq, GDN_QKV)[:, nq - 3:][None]
    gs_p = s_gdn_p[None]
    gs_s = s_gdn_s[None]
    sh_p = h1[tp - 1].reshape(1, 1, D)
    sh_s = h1_s[:, nq - 1][None]
    rs_p = s_rw_p[None]
    rs_s = s_rw_s[None]
    return (y_prompt, y_sample, fk_p, fk_s, fv_p, fv_s, lf_p, lf_sm, cb_p, cb_s, gs_p, gs_s, sh_p, sh_s, rs_p, rs_s)
```

```python
import functools
import math

import jax
import jax.numpy as jnp
from jax import lax
from jax.experimental import pallas as pl
from jax.experimental.pallas import tpu as pltpu

F32 = jnp.float32
BF16 = jnp.bfloat16

D_MODEL = 2048
N_META = 16
GDN_HEADS = 8
GDN_D = 128
GDN_QK = GDN_HEADS * GDN_D
GDN_QKV = 3 * GDN_QK
FOX_HEADS = 8
FOX_DH = 128
FOX_W = FOX_HEADS * FOX_DH
RWKV_HEAD = 64
RWKV_PAIRS = D_MODEL // (2 * RWKV_HEAD)
NORM_EPS = 1e-6
L2_EPS = 1e-6
GN_EPS = 64e-5
NEG_INF = -1e30

LANES = 128
SUBLANES = 8
ROW_TILE = 640
MM_ROW_TILE = 1792
LOG2E = 1.4426950408889634
VMEM_LIMIT = 48 * 1024 * 1024
FOX_HEADS_PER_STEP = 4
FOX_GROUP = 16
RWKV_PREP_PAIRS = 8

COL_Z = GDN_QKV
COL_FQ = COL_Z + GDN_QK
COL_FK = COL_FQ + FOX_W
COL_FV = COL_FK + FOX_W
COL_FZ = COL_FV + FOX_W
LANE_A = 0
LANE_B = 8
LANE_F = 16
PRM_W0, PRM_A0, PRM_KK, PRM_KA, PRM_RK, PRM_LNW, PRM_LNB = range(7)


def _pick(n, cands):
    for c in cands:
        if n % c == 0:
            return c
    raise ValueError(f"no tile for {n}")


def _cparams(sem):
    return pltpu.CompilerParams(dimension_semantics=sem, vmem_limit_bytes=VMEM_LIMIT)


def _dot(a, b):
    return jnp.dot(a, b, preferred_element_type=F32)


def _dot_nt(a, b):
    return lax.dot_general(a, b, (((1,), (1,)), ((), ())), preferred_element_type=F32)


def _dot_tn(a, b):
    return lax.dot_general(a, b, (((0,), (0,)), ((), ())), preferred_element_type=F32)


def _bf(x):
    return x.astype(BF16)


def _softplus(x):
    return jnp.maximum(x, 0.0) + jnp.log(1.0 + jnp.exp(-jnp.abs(x)))


def _sigmoid(x):
    return 1.0 / (1.0 + jnp.exp(-x))


def _iota2(shape, dim):
    return lax.broadcasted_iota(jnp.int32, shape, dim)


def _div(x, n):
    assert n & (n - 1) == 0
    return x >> (n.bit_length() - 1)


def _split2(x):
    hi = x.astype(BF16)
    return hi, (x - hi.astype(F32)).astype(BF16)


def _split3(x):
    p1 = x.astype(BF16)
    rem = x - p1.astype(F32)
    p2 = rem.astype(BF16)
    return p1, p2, (rem - p2.astype(F32)).astype(BF16)


def _dot3(a, b):
    m = a.shape[0]
    ah, al = _split2(a)
    bh, bl = _split2(b)
    top = _dot(jnp.concatenate([ah, al], axis=0), bh)
    return (top[:m] + top[m:]) + _dot(ah, bl)


def _dot_lhs01(a01, b):
    a = a01.astype(BF16)
    b1, b2, b3 = _split3(b)
    return _dot(a, b1) + (_dot(a, b2) + _dot(a, b3))


def _dot_rhs01(a, b01):
    b = b01.astype(BF16)
    a1, a2, a3 = _split3(a)
    m = a.shape[0]
    out = _dot(jnp.concatenate([a1, a2, a3], axis=0), b)
    return out[:m] + (out[m:2 * m] + out[2 * m:])


def _tri_inv_steps(n_mat, n):
    size = n_mat.shape[0]
    eye = (_iota2((size, size), 0) == _iota2((size, size), 1)).astype(F32)
    t = eye + n_mat
    if n <= 2:
        return t
    p = _dot(_bf(n_mat), _bf(n_mat))
    yield
    m = 2
    while m < n:
        p_bf = _bf(p)
        if 2 * m < n:
            both = _dot(_bf(jnp.concatenate([p, t], axis=0)), p_bf)
            p = both[:size]
            t = t + both[size:]
        else:
            t = t + _dot(_bf(t), p_bf)
        m *= 2
        yield
    return t


def _run_interleaved(gens):
    results = [None] * len(gens)
    live = list(enumerate(gens))
    while live:
        nxt = []
        for idx, gen in live:
            try:
                next(gen)
                nxt.append((idx, gen))
            except StopIteration as stop:
                results[idx] = stop.value
        live = nxt
    return results


def _rmsnorm_kernel(x_ref, w_ref, o_ref):
    x = x_ref[...]
    ms = jnp.mean(x * x, axis=-1, keepdims=True)
    o_ref[...] = (x * lax.rsqrt(ms + NORM_EPS) * w_ref[...]).astype(o_ref.dtype)


def rmsnorm(x, w, out_dtype):
    m, d = x.shape
    tm = _pick(m, (320, 256, 128))
    return pl.pallas_call(
        _rmsnorm_kernel,
        grid=(m // tm,),
        in_specs=[pl.BlockSpec((tm, d), lambda i: (i, 0)), pl.BlockSpec((1, d), lambda i: (0, 0))],
        out_specs=pl.BlockSpec((tm, d), lambda i: (i, 0)),
        out_shape=jax.ShapeDtypeStruct((m, d), out_dtype),
        compiler_params=_cparams(("parallel",)),
        name="rmsnorm",
    )(x, w.reshape(1, d))


def _mm_kernel(nk, nx, act, has_res, *refs):
    x_refs, w_ref = refs[:nx], refs[nx]
    r_ref = refs[nx + 1] if has_res else None
    o_ref = refs[nx + 1 + has_res]
    scr = refs[nx + 2 + has_res:]

    def product():
        if nx == 1:
            return _dot(x_refs[0][...], _bf(w_ref[...]))
        acc, k0 = None, 0
        for x_ref in x_refs:
            kw = x_ref.shape[1]
            part = _dot(x_ref[...], _bf(w_ref[k0:k0 + kw, :]))
            acc = part if acc is None else acc + part
            k0 += kw
        return acc

    def finish(acc):
        if act == "relu2":
            acc = jnp.square(jnp.maximum(acc, 0.0))
        elif act == "tanh":
            acc = jnp.tanh(acc)
        elif act == "sigmoid":
            acc = _sigmoid(acc)
        if has_res:
            acc = r_ref[...] + acc
        o_ref[...] = acc.astype(o_ref.dtype)

    if nk == 1:
        finish(product())
    else:
        acc_ref = scr[0]
        k = pl.program_id(2)

        @pl.when(k == 0)
        def _():
            acc_ref[...] = jnp.zeros(acc_ref.shape, F32)

        acc_ref[...] += product()

        @pl.when(k == nk - 1)
        def _():
            finish(acc_ref[...])


def matmul(x, w, *, layer=None, act=None, res=None, out_dtype=F32, name="matmul"):
    xs = list(x) if isinstance(x, (list, tuple)) else [x]
    m = xs[0].shape[0]
    k = sum(xi.shape[1] for xi in xs)
    n = w.shape[-1]
    tm = _pick(m, (MM_ROW_TILE, ROW_TILE, 512, 256, 128))
    tn = _pick(n, (512, 256, 128))
    tk = k if k <= 2048 else 2048
    nk = k // tk
    assert len(xs) == 1 or nk == 1
    in_specs = [pl.BlockSpec((tm, tk if len(xs) == 1 else xi.shape[1]), lambda i, j, kk: (i, kk)) for xi in xs]
    if w.ndim == 3:
        in_specs.append(pl.BlockSpec((None, tk, tn), lambda i, j, kk: (layer, kk, j)))
    else:
        in_specs.append(pl.BlockSpec((tk, tn), lambda i, j, kk: (kk, j)))
    args = xs + [w]
    if res is not None:
        in_specs.append(pl.BlockSpec((tm, tn), lambda i, j, kk: (i, j)))
        args.append(res)
    return pl.pallas_call(
        functools.partial(_mm_kernel, nk, len(xs), act, res is not None),
        grid=(m // tm, n // tn, nk),
        in_specs=in_specs,
        out_specs=pl.BlockSpec((tm, tn), lambda i, j, kk: (i, j)),
        out_shape=jax.ShapeDtypeStruct((m, n), out_dtype),
        scratch_shapes=[pltpu.VMEM((tm, tn), F32)] if nk > 1 else [],
        compiler_params=_cparams(("parallel", "parallel", "arbitrary")),
        name=name,
    )(*args)


def _gdn_kernel(C, NC, first_valid, has_conv, qkv_ref, z_ref, ps_ref, cw_ref, alog_ref, dt_ref, nw_ref, s0_ref,
                *rest):
    conv_ref = rest[0] if has_conv else None
    o_ref, sout_ref, S, ext = rest[1:] if has_conv else rest
    c = pl.program_id(1)

    @pl.when(c == 0)
    def _():
        S[...] = s0_ref[0]
        ext[0:SUBLANES, :] = jnp.zeros((SUBLANES, GDN_QKV), F32)

    x = qkv_ref[...]
    ext[SUBLANES:SUBLANES + C, :] = x
    if has_conv:
        assert NC == 1 and 3 <= first_valid < C
        ext[SUBLANES + first_valid - 3:SUBLANES + first_valid, :] = conv_ref[0, 0]
    cw = cw_ref[...]
    y = ((ext[5:5 + C, :] * cw[0:1] + ext[6:6 + C, :] * cw[1:2]) + ext[7:7 + C, :] * cw[2:3]) + ext[8:8 + C, :] * cw[3:4]
    ext[0:SUBLANES, :] = x[C - SUBLANES:C]
    y = y * _sigmoid(y)

    rows = c * C + _iota2((C, 1), 0)
    valid = rows >= first_valid
    ps = ps_ref[...]
    g_all = jnp.where(valid, -jnp.exp(alog_ref[...]) * _softplus(ps + dt_ref[...]), 0.0)
    beta_all = jnp.where(valid, _sigmoid(ps), 0.0)
    ri = _iota2((C, C), 0)
    ci = _iota2((C, C), 1)
    causal = ri >= ci
    strict = ri > ci
    gc = _dot_lhs01(causal.astype(F32), g_all)
    gct = _dot_rhs01(g_all.T, (ri <= ci).astype(F32))
    nw = nw_ref[...]

    def head_steps(h):
        sl = slice(h * GDN_D, (h + 1) * GDN_D)
        qh = y[:, sl]
        kh = y[:, GDN_QK + h * GDN_D:GDN_QK + (h + 1) * GDN_D]
        vh = jnp.where(valid, y[:, 2 * GDN_QK + h * GDN_D:2 * GDN_QK + (h + 1) * GDN_D], 0.0)
        qh = jnp.where(valid, qh * lax.rsqrt(jnp.sum(qh * qh, axis=-1, keepdims=True) + L2_EPS) * GDN_D ** -0.5, 0.0)
        kh = jnp.where(valid, kh * lax.rsqrt(jnp.sum(kh * kh, axis=-1, keepdims=True) + L2_EPS), 0.0)
        bcol = beta_all[:, LANE_B + h:LANE_B + h + 1]
        gcol = gc[:, LANE_A + h:LANE_A + h + 1]
        grow = gct[LANE_A + h:LANE_A + h + 1, :]
        glast = gc[C - 1:C, LANE_A + h:LANE_A + h + 1]
        diff = gcol - grow
        decay = jnp.where(causal, jnp.exp(jnp.where(causal, diff, 0.0)), 0.0)
        kb = kh * bcol
        a_mat = jnp.where(strict, _dot_nt(_bf(kb), _bf(kh)) * decay, 0.0)
        qk = _dot_nt(_bf(qh), _bf(kh)) * decay
        eg = jnp.exp(gcol)
        rhs = jnp.concatenate([vh * bcol, kb * eg], axis=1)
        q_dec = qh * eg
        k_dec = kh * jnp.exp(glast - gcol)
        yield
        t_mat = yield from _tri_inv_steps(-a_mat, C)
        sol = _dot3(t_mat, rhs)
        u = sol[:, :GDN_D]
        w = sol[:, GDN_D:]
        yield
        s_h = S[h]
        s_bf = _bf(s_h)
        v_new = u - _dot(_bf(w), s_bf)
        o_state = _dot(_bf(q_dec), s_bf)
        yield
        o = o_state + _dot(_bf(qk), _bf(v_new))
        S[h] = s_h * jnp.exp(glast) + _dot_tn(_bf(k_dec), _bf(v_new))
        yield
        on = o * lax.rsqrt(jnp.mean(o * o, axis=-1, keepdims=True) + NORM_EPS) * nw
        zh = z_ref[:, sl]
        o_ref[:, sl] = (on * (zh * _sigmoid(zh))).astype(o_ref.dtype)

    _run_interleaved([head_steps(h) for h in range(GDN_HEADS)])

    @pl.when(c == NC - 1)
    def _():
        sout_ref[0] = S[...]


def gdn(qkv_arr, z_arr, ps_arr, z_col, conv_w, alog_pad, dt_pad, norm_w, s0, conv0, *, B, NC, C, first_valid, name,
        out_rows=None):
    rows = B * NC * C
    has_conv = conv0 is not None
    extra_specs = [pl.BlockSpec((1, 1, 3, GDN_QKV), lambda b, c: (0, b, 0, 0))] if has_conv else []
    extra_args = [conv0] if has_conv else []
    return pl.pallas_call(
        functools.partial(_gdn_kernel, C, NC, first_valid, has_conv),
        grid=(B, NC),
        in_specs=[
            pl.BlockSpec((C, GDN_QKV), lambda b, c: (b * NC + c, 0)),
            pl.BlockSpec((C, GDN_QK), lambda b, c: (b * NC + c, z_col)),
            pl.BlockSpec((C, LANES), lambda b, c: (b * NC + c, 0)),
            pl.BlockSpec((4, GDN_QKV), lambda b, c: (0, 0)),
            pl.BlockSpec((1, LANES), lambda b, c: (0, 0)),
            pl.BlockSpec((1, LANES), lambda b, c: (0, 0)),
            pl.BlockSpec((1, GDN_D), lambda b, c: (0, 0)),
            pl.BlockSpec((1, GDN_HEADS, GDN_D, GDN_D), lambda b, c: (b, 0, 0, 0)),
        ] + extra_specs,
        out_specs=[
            pl.BlockSpec((C, GDN_QK), lambda b, c: (b * NC + c, 0)),
            pl.BlockSpec((1, GDN_HEADS, GDN_D, GDN_D), lambda b, c: (b, 0, 0, 0)),
        ],
        out_shape=[
            jax.ShapeDtypeStruct((out_rows or rows, GDN_QK), BF16),
            jax.ShapeDtypeStruct((B, GDN_HEADS, GDN_D, GDN_D), F32),
        ],
        scratch_shapes=[pltpu.VMEM((GDN_HEADS, GDN_D, GDN_D), F32), pltpu.VMEM((C + SUBLANES, GDN_QKV), F32)],
        compiler_params=_cparams(("parallel", "arbitrary")),
        name=name,
    )(qkv_arr, z_arr, ps_arr, conv_w, alog_pad, dt_pad, norm_w, s0, *extra_args)


def _fox_prep_kernel(tb, first_valid, ps_ref, bf_ref, lf_ref, cq_ref, ck_ref, carry):
    i = pl.program_id(0)

    @pl.when(i == 0)
    def _():
        carry[...] = jnp.zeros(carry.shape, F32)

    x = ps_ref[...] + bf_ref[...]
    rows = i * tb + _iota2((tb, 1), 0)
    lf = jnp.where(rows >= first_valid, jnp.minimum(x, 0.0) - jnp.log(1.0 + jnp.exp(-jnp.abs(x))), 0.0)
    tri = (_iota2((tb, tb), 0) >= _iota2((tb, tb), 1)).astype(F32)
    c = _dot_lhs01(tri, lf) + carry[0:1, :]
    carry[...] = jnp.broadcast_to(c[tb - 1:tb, :], carry.shape)
    lf_ref[...] = lf
    c2 = c * LOG2E
    for h in range(FOX_HEADS):
        cq_ref[h] = jnp.broadcast_to(c2[:, LANE_F + h:LANE_F + h + 1], (tb, LANES))
    ck_ref[...] = c2.T[LANE_F:LANE_F + FOX_HEADS, :]


def fox_prep(ps, bf_pad, first_valid):
    r = ps.shape[0]
    tb = _pick(r, (ROW_TILE, 512, 256, 128))
    return pl.pallas_call(
        functools.partial(_fox_prep_kernel, tb, first_valid),
        grid=(r // tb,),
        in_specs=[pl.BlockSpec((tb, LANES), lambda i: (i, 0)), pl.BlockSpec((1, LANES), lambda i: (0, 0))],
        out_specs=[
            pl.BlockSpec((tb, LANES), lambda i: (i, 0)),
            pl.BlockSpec((FOX_HEADS, tb, LANES), lambda i: (0, i, 0)),
            pl.BlockSpec((FOX_HEADS, tb), lambda i: (0, i)),
        ],
        out_shape=[
            jax.ShapeDtypeStruct((r, LANES), F32),
            jax.ShapeDtypeStruct((FOX_HEADS, r, LANES), F32),
            jax.ShapeDtypeStruct((FOX_HEADS, r), F32),
        ],
        scratch_shapes=[pltpu.VMEM((SUBLANES, LANES), F32)],
        compiler_params=_cparams(("arbitrary",)),
        name="fox_prep",
    )(ps, bf_pad)


def _fox_flash_kernel(tq, first_valid, qi_ref, ki_ref, qt_ref, k_ref, vt_ref, ck_ref, cq_ref, fz_ref, o_ref,
                      m_s, l_s, acc_s, t_s):
    step = pl.program_id(1)
    qi = qi_ref[step]
    ki = ki_ref[step]
    reps = tq // LANES
    heads = qt_ref.shape[0]

    @pl.when(ki == 0)
    def _():
        m_s[...] = jnp.full(m_s.shape, NEG_INF, F32)
        l_s[...] = jnp.zeros(l_s.shape, F32)
        acc_s[...] = jnp.zeros(acc_s.shape, F32)

    def head_steps(h, masked):
        cols = slice(h * FOX_DH, (h + 1) * FOX_DH)
        qt = _bf(qt_ref[h] * (FOX_DH ** -0.5 * LOG2E))
        cq = cq_ref[h]
        m_prev = m_s[h, 0:1, :]
        kb = LANES
        groups = kb // SUBLANES
        mx = None
        for b in range(tq // kb):
            rs = slice(b * kb, (b + 1) * kb)
            t = _dot(_bf(k_ref[rs, cols]), qt) - jnp.concatenate([ck_ref[h, rs, :]] * reps, axis=1)
            if masked:
                kpos = ki * tq + b * kb + _iota2((kb, tq), 0)
                qpos = qi * tq + _iota2((kb, tq), 1)
                t = jnp.where((kpos <= qpos) & (kpos >= first_valid), t, NEG_INF)
            t_s[h, rs, :] = t
            part = jnp.max(t.reshape(groups, SUBLANES, tq), axis=0)
            mx = part if mx is None else jnp.maximum(mx, part)
            yield
        m_new = jnp.maximum(m_prev, jnp.max(mx, axis=0, keepdims=True) + cq)
        shift = cq - m_new
        alpha = jnp.exp2(m_prev - m_new)
        lsum = None
        pv = None
        for b in range(tq // kb):
            rs = slice(b * kb, (b + 1) * kb)
            p = jnp.exp2(t_s[h, rs, :] + shift)
            part = jnp.sum(p.reshape(groups, SUBLANES, tq), axis=0)
            lsum = part if lsum is None else lsum + part
            d = _dot(_bf(vt_ref[h, :, rs]), _bf(p))
            pv = d if pv is None else pv + d
            yield
        l_new = alpha * l_s[h, 0:1, :] + jnp.sum(lsum, axis=0, keepdims=True)
        l_s[h] = jnp.broadcast_to(l_new, (SUBLANES, tq))
        acc_s[h] = alpha * acc_s[h] + pv
        m_s[h] = jnp.broadcast_to(m_new, (SUBLANES, tq))

    def accumulate(masked):
        _run_interleaved([head_steps(h, masked) for h in range(heads)])

    edge = (ki == qi) | (ki == 0)
    pl.when(edge)(lambda: accumulate(True))
    pl.when(jnp.logical_not(edge))(lambda: accumulate(False))

    @pl.when(ki == qi)
    def _():
        rows = qi * tq + _iota2((tq, 1), 0)
        for h in range(heads):
            cols = slice(h * FOX_DH, (h + 1) * FOX_DH)
            o = (acc_s[h] / l_s[h, 0:1, :]).T * _sigmoid(fz_ref[:, cols])
            o_ref[:, cols] = jnp.where(rows >= first_valid, o, 0.0).astype(o_ref.dtype)


def fox_prompt(p, c_rep, c_row, tp, first_valid, out_rows):
    tq = _pick(tp, (ROW_TILE, 512, 384, 256, 128))
    nq = tp // tq
    assert first_valid < tq
    pairs = [(qi, ki) for qi in range(nq) for ki in range(qi + 1)]
    qi_arr = jnp.array([a for a, _ in pairs], jnp.int32)
    ki_arr = jnp.array([b for _, b in pairs], jnp.int32)
    hp = FOX_HEADS_PER_STEP
    wid = hp * FOX_DH
    cb = lambda col, g: col // wid + g

    def heads_t(col):
        return jnp.transpose(p[:tp, col:col + FOX_W].reshape(tp, FOX_HEADS, FOX_DH), (1, 2, 0))

    return pl.pallas_call(
        functools.partial(_fox_flash_kernel, tq, first_valid),
        grid_spec=pltpu.PrefetchScalarGridSpec(
            num_scalar_prefetch=2,
            grid=(FOX_HEADS // hp, len(pairs)),
            in_specs=[
                pl.BlockSpec((hp, FOX_DH, tq), lambda g, t, qa, ka: (g, 0, qa[t])),
                pl.BlockSpec((tq, wid), lambda g, t, qa, ka: (ka[t], cb(COL_FK, g))),
                pl.BlockSpec((hp, FOX_DH, tq), lambda g, t, qa, ka: (g, 0, ka[t])),
                pl.BlockSpec((hp, tq, LANES), lambda g, t, qa, ka: (g, ka[t], 0)),
                pl.BlockSpec((hp, 1, tq), lambda g, t, qa, ka: (g, 0, qa[t])),
                pl.BlockSpec((tq, wid), lambda g, t, qa, ka: (qa[t], cb(COL_FZ, g))),
            ],
            out_specs=pl.BlockSpec((tq, wid), lambda g, t, qa, ka: (qa[t], g)),
            scratch_shapes=[pltpu.VMEM((hp, SUBLANES, tq), F32), pltpu.VMEM((hp, SUBLANES, tq), F32),
                            pltpu.VMEM((hp, FOX_DH, tq), F32), pltpu.VMEM((hp, tq, tq), F32)],
        ),
        out_shape=jax.ShapeDtypeStruct((out_rows, FOX_W), BF16),
        compiler_params=_cparams(("parallel", "arbitrary")),
        name="fox_prompt",
    )(qi_arr, ki_arr, heads_t(COL_FQ), p, heads_t(COL_FV), c_rep, c_row.reshape(FOX_HEADS, 1, -1), p)


def _fox_sample_kernel(G, ngroups, pt_ref, q_ref, *refs):
    k_refs, v_refs, lft_refs = refs[0:G], refs[G:2 * G], refs[2 * G:3 * G]
    lfn_ref, kn_ref, vn_ref, fz_ref, o_ref, cq_s, cn_s, carry_s, m_s, l_s, acc_s = refs[3 * G:]
    jg = pl.program_id(1)
    rows = q_ref.shape[1]
    page = lft_refs[0].shape[2]
    flat = page * FOX_HEADS
    nn = lfn_ref.shape[2]
    scale = FOX_DH ** -0.5

    @pl.when(jg == 0)
    def _():
        upper = (_iota2((nn, nn), 0) <= _iota2((nn, nn), 1)).astype(F32)
        cn = _dot_rhs01(lfn_ref[0], upper)
        cn_s[...] = cn
        own_q = _iota2((rows, nn), 1) == _div(_iota2((rows, nn), 0), FOX_HEADS)
        cq_s[...] = jnp.broadcast_to(jnp.sum(jnp.where(own_q, cn, 0.0), axis=-1, keepdims=True), cq_s.shape)
        carry_s[...] = jnp.zeros(carry_s.shape, F32)
        m_s[...] = jnp.full(m_s.shape, NEG_INF, F32)
        l_s[...] = jnp.zeros(l_s.shape, F32)
        acc_s[...] = jnp.zeros(acc_s.shape, F32)

    def update(ts, vals_bf):
        cq = cq_s[:, 0:1]
        m_prev = m_s[:, 0:1]
        t_max = functools.reduce(jnp.maximum, ts)
        m_new = jnp.maximum(m_prev, jnp.max(t_max, axis=-1, keepdims=True) + cq)
        shift = cq - m_new
        ps = [jnp.exp(t + shift) for t in ts]
        alpha = jnp.exp(m_prev - m_new)
        l_s[...] = alpha * l_s[...] + jnp.sum(functools.reduce(jnp.add, ps), axis=-1, keepdims=True)
        pv = functools.reduce(jnp.add, [_dot(_bf(p), v) for p, v in zip(ps, vals_bf)])
        acc_s[...] = alpha * acc_s[...] + pv
        m_s[...] = jnp.broadcast_to(m_new, m_s.shape)

    own = (_iota2((rows, flat), 0) & (FOX_HEADS - 1)) == (_iota2((rows, flat), 1) & (FOX_HEADS - 1))
    later = (_iota2((page, flat), 0) > _div(_iota2((page, flat), 1), FOX_HEADS)).astype(BF16)
    head_col = _iota2((FOX_HEADS, flat), 0) == (_iota2((FOX_HEADS, flat), 1) & (FOX_HEADS - 1))
    carry = carry_s[:, 0:1]
    q_bf = _bf(q_ref[0])
    lfts = [lft_refs[g][0] for g in range(G)]
    suffixes = _dot_rhs01(jnp.concatenate(lfts, axis=0), later)
    ts = []
    for g in range(G):
        suffix = suffixes[g * FOX_HEADS:(g + 1) * FOX_HEADS]
        d = jnp.sum(jnp.where(head_col, suffix + carry, 0.0), axis=0, keepdims=True)
        carry = carry + jnp.sum(lfts[g], axis=-1, keepdims=True)
        s = _dot_nt(q_bf, _bf(k_refs[g][0])) * scale
        ts.append(jnp.where(own, s + d, NEG_INF))
    update(ts, [_bf(v_refs[g][0]) for g in range(G)])
    carry_s[...] = jnp.broadcast_to(carry, carry_s.shape)

    @pl.when(jg == ngroups - 1)
    def _():
        nflat = kn_ref.shape[1]
        spread = (_iota2((nn, nflat), 0) == _div(_iota2((nn, nflat), 1), FOX_HEADS)).astype(F32)
        cn_cols = _dot_rhs01(cn_s[...], spread)
        ri = _iota2((rows, nflat), 0)
        ci = _iota2((rows, nflat), 1)
        ok = ((ri & (FOX_HEADS - 1)) == (ci & (FOX_HEADS - 1))) & (_div(ci, FOX_HEADS) <= _div(ri, FOX_HEADS))
        sn = _dot_nt(_bf(q_ref[0]), _bf(kn_ref[0])) * scale
        update([jnp.where(ok, sn - cn_cols, NEG_INF)], [_bf(vn_ref[0])])
        o_ref[0] = acc_s[...] / l_s[...] * _sigmoid(fz_ref[0])


def fox_sample(page_table_flat, q_rows, cache_k, cache_v, cache_lft, lfn, kn_flat, vn_flat, fz_rows, nb, npages):
    flat = cache_k.shape[1]
    page = cache_lft.shape[2]
    rows = q_rows.shape[1]
    nn = lfn.shape[2]
    G = _pick(npages, (FOX_GROUP, 2, 1))
    ngroups = npages // G

    def page_map(g):
        return lambda b, jg, pt: (pt[b * npages + (npages - 1 - (jg * G + g))], 0, 0)

    seq = lambda b, jg, pt: (b, 0, 0)
    row_blk = pl.BlockSpec((1, rows, FOX_DH), seq)
    new_blk = pl.BlockSpec((1, kn_flat.shape[1], FOX_DH), seq)
    in_specs = [row_blk]
    in_specs += [pl.BlockSpec((1, flat, FOX_DH), page_map(g)) for g in range(G)]
    in_specs += [pl.BlockSpec((1, flat, FOX_DH), page_map(g)) for g in range(G)]
    in_specs += [pl.BlockSpec((1, FOX_HEADS, page), page_map(g)) for g in range(G)]
    in_specs += [pl.BlockSpec((1, rows, nn), seq), new_blk, new_blk, row_blk]
    return pl.pallas_call(
        functools.partial(_fox_sample_kernel, G, ngroups),
        grid_spec=pltpu.PrefetchScalarGridSpec(
            num_scalar_prefetch=1,
            grid=(nb, ngroups),
            in_specs=in_specs,
            out_specs=row_blk,
            scratch_shapes=[
                pltpu.VMEM((rows, LANES), F32),
                pltpu.VMEM((rows, nn), F32),
                pltpu.VMEM((FOX_HEADS, LANES), F32),
                pltpu.VMEM((rows, LANES), F32),
                pltpu.VMEM((rows, LANES), F32),
                pltpu.VMEM((rows, FOX_DH), F32),
            ],
        ),
        out_shape=jax.ShapeDtypeStruct((nb, rows, FOX_DH), F32),
        compiler_params=_cparams(("parallel", "arbitrary")),
        name="fox_sample",
    )(page_table_flat, q_rows, *([cache_k] * G), *([cache_v] * G), *([cache_lft] * G), lfn, kn_flat, vn_flat, fz_rows)


def _rwkv_mix_kernel(tm, tp, ns, nq, h_ref, hb_ref, st_ref, mu_ref, *o_refs):
    i = pl.program_id(0)
    h = h_ref[...]
    above = jnp.concatenate([hb_ref[SUBLANES - 1:SUBLANES, :], h[:tm - 1]], axis=0)
    row = i * tm + _iota2((tm, 1), 0)
    seq_start = (row >= tp) & (row < tp + ns) & (((row - tp) & (nq - 1)) == 0)
    prev = jnp.where(seq_start, st_ref[...], jnp.where(row == 0, 0.0, above))
    xx = prev - h
    mu = mu_ref[...]
    for j, o_ref in enumerate(o_refs):
        o_ref[...] = (h + xx * mu[j:j + 1]).astype(o_ref.dtype)


def rwkv_mix(h, shift_rows, mu, tp, ns, nq):
    m, d = h.shape
    assert nq & (nq - 1) == 0
    tm = _pick(math.gcd(m, tp), (320, 256, 128))
    first = tp // tm
    spec = pl.BlockSpec((tm, d), lambda i: (i, 0))
    return pl.pallas_call(
        functools.partial(_rwkv_mix_kernel, tm, tp, ns, nq),
        grid=(m // tm,),
        in_specs=[spec,
                  pl.BlockSpec((SUBLANES, d), lambda i: (jnp.maximum(i * (tm // SUBLANES) - 1, 0), 0)),
                  pl.BlockSpec((tm, d), lambda i: (jnp.maximum(i - first, 0), 0)),
                  pl.BlockSpec((6, d), lambda i: (0, 0))],
        out_specs=[spec] * 6,
        out_shape=[jax.ShapeDtypeStruct((m, d), BF16)] * 6,
        compiler_params=_cparams(("parallel",)),
        name="rwkv_mix",
    )(h, h, shift_rows, mu)


def _head_ones():
    return (_div(_iota2((LANES, LANES), 0), RWKV_HEAD) == _div(_iota2((LANES, LANES), 1), RWKV_HEAD)).astype(F32)


def _rwkv_chunk_terms(C, valid, r, k, v, wl, al, prm):
    HD = RWKV_HEAD
    m0 = _iota2((1, LANES), 1) < HD
    bones = _head_ones()
    w0, a0, k_k, k_a, r_k = (prm[i:i + 1] for i in (PRM_W0, PRM_A0, PRM_KK, PRM_KA, PRM_RK))
    wlog = -_softplus(-(w0 + wl)) - 0.5
    lw = jnp.where(valid, -jnp.exp(wlog), 0.0)
    a = _sigmoid(a0 + al)
    kkr = k * k_k
    kk = kkr * lax.rsqrt(_dot_rhs01(kkr * kkr, bones) + L2_EPS)
    k2 = k * (1.0 + (a - 1.0) * k_a)
    bonus = _dot_rhs01(r * k2 * r_k, bones) * v
    rm = jnp.where(valid, r, 0.0)
    k2 = jnp.where(valid, k2, 0.0)
    vm = jnp.where(valid, v, 0.0)
    av = jnp.where(valid, -kk, 0.0)
    bv = jnp.where(valid, kk * a, 0.0)

    ri = _iota2((C, C), 0)
    ci = _iota2((C, C), 1)
    yield
    cum = _dot_lhs01((ri >= ci).astype(F32), lw)
    cum_last = cum[C - 1:C, :]
    inv = jnp.exp(-cum)
    rt = rm * jnp.exp(cum)
    at = av * jnp.exp(cum - lw)
    bt = bv * inv
    kt = k2 * inv
    to_end = jnp.exp(cum_last - cum)
    b_end = bv * to_end
    k_end = k2 * to_end

    def split(x):
        return jnp.concatenate([jnp.where(m0, x, 0.0), jnp.where(m0, 0.0, x)], axis=0)

    def halves(x):
        return x[0:C] + x[C:2 * C]

    yield
    at_s = split(at)
    gram = _dot_nt(_bf(jnp.concatenate([at_s, split(rt)], axis=0)), _bf(jnp.concatenate([bt, kt], axis=0)))
    r2 = _iota2((2 * C, 2 * C), 0)
    c2 = _iota2((2 * C, 2 * C), 1)
    same = _div(r2, C) == _div(c2, C)
    bd_strict = same & (r2 > c2)
    bd_incl = same & (r2 >= c2)

    def bd(block, mask):
        return jnp.where(mask, jnp.concatenate([block, block], axis=1), 0.0)

    a_ab = bd(gram[0:2 * C, 0:C], bd_strict)
    a_ak = bd(gram[0:2 * C, C:2 * C], bd_strict)
    r_b = bd(gram[2 * C:4 * C, 0:C], bd_incl)
    r_k2 = bd(gram[2 * C:4 * C, C:2 * C], bd_incl)
    vs = split(vm)
    akv = halves(_dot(_bf(a_ak), _bf(vs)))
    yield
    t_bd = yield from _tri_inv_steps(a_ab, C)
    tw = _dot(_bf(t_bd), _bf(jnp.concatenate([at_s, split(akv)], axis=1)))
    wt = halves(tw[:, :LANES])
    ut = halves(tw[:, LANES:])
    yield
    rp = rt + halves(_dot(_bf(r_b), _bf(split(wt))))
    y0 = halves(_dot(_bf(jnp.concatenate([r_b, r_k2], axis=1)), _bf(jnp.concatenate([split(ut), vs], axis=0))))
    eye = (_iota2((LANES, LANES), 0) == _iota2((LANES, LANES), 1)).astype(F32)
    m_mat = eye * jnp.exp(cum_last) + bones * _dot_tn(_bf(b_end), _bf(wt))
    n_mat = bones * _dot_tn(_bf(jnp.concatenate([b_end, k_end], axis=0)), _bf(jnp.concatenate([ut, vm], axis=0)))
    return rp, y0, bonus, m_mat, n_mat


def _rwkv_prep_kernel(C, chunk_of_step, first_valid, PG, r_ref, k_ref, v_ref, wl_ref, al_ref, prm_ref,
                      rp_ref, y0_ref, bo_ref, m_ref, n_ref):
    c = pl.program_id(0) if chunk_of_step else 0
    valid = (c * C + _iota2((C, 1), 0)) >= first_valid
    sls = [slice(i * LANES, (i + 1) * LANES) for i in range(PG)]
    terms = _run_interleaved([
        _rwkv_chunk_terms(C, valid, r_ref[:, sl], k_ref[:, sl], v_ref[:, sl], wl_ref[:, sl], al_ref[:, sl], prm_ref[:, sl])
        for sl in sls])
    for i, sl in enumerate(sls):
        rp, y0, bonus, m_mat, n_mat = terms[i]
        rp_ref[:, sl] = rp
        y0_ref[:, sl] = y0
        bo_ref[:, sl] = bonus
        m_ref[0, i] = m_mat
        n_ref[0, i] = n_mat


def _rwkv_scan_kernel(NC, rp_ref, y0_ref, bo_ref, g_ref, m_ref, n_ref, prm_ref, s0_ref, o_ref, sout_ref, H):
    c = pl.program_id(1)
    HD = RWKV_HEAD
    bones = _head_ones()

    @pl.when(c == 0)
    def _():
        zero = jnp.zeros((HD, HD), F32)
        for i in range(RWKV_PAIRS):
            top = jnp.concatenate([s0_ref[0, 2 * i], zero], axis=1)
            bot = jnp.concatenate([zero, s0_ref[0, 2 * i + 1]], axis=1)
            H[i] = jnp.concatenate([top, bot], axis=0).T

    def pair_steps(i):
        sl = slice(i * LANES, (i + 1) * LANES)
        h_bd = H[i]
        y = _dot(_bf(rp_ref[:, sl]), _bf(h_bd)) + y0_ref[:, sl]
        H[i] = _dot3(m_ref[0, i], h_bd) + n_ref[0, i]
        yield
        mean = _dot_rhs01(y, bones) * (1.0 / HD)
        dlt = y - mean
        yield
        var = _dot_rhs01(dlt * dlt, bones) * (1.0 / HD)
        yn = dlt * lax.rsqrt(var + GN_EPS) * prm_ref[PRM_LNW:PRM_LNW + 1, sl] + prm_ref[PRM_LNB:PRM_LNB + 1, sl]
        o_ref[:, sl] = ((yn + bo_ref[:, sl]) * g_ref[:, sl]).astype(o_ref.dtype)

    _run_interleaved([pair_steps(i) for i in range(RWKV_PAIRS)])

    @pl.when(c == NC - 1)
    def _():
        for i in range(RWKV_PAIRS):
            ht = H[i].T
            sout_ref[0, 2 * i] = ht[0:HD, 0:HD]
            sout_ref[0, 2 * i + 1] = ht[HD:2 * HD, HD:2 * HD]


def rwkv(r, k, v, wl, al, g, prm, s0, *, B, NC, C, first_valid, name, out_rows=None):
    assert B == 1 or NC == 1
    nblk = B * NC
    rows = nblk * C
    PG = RWKV_PREP_PAIRS
    wide = pl.BlockSpec((C, PG * LANES), lambda blk, pg: (blk, pg))
    mat = pl.BlockSpec((1, PG, LANES, LANES), lambda blk, pg: (blk, pg, 0, 0))
    mat_shape = jax.ShapeDtypeStruct((nblk, RWKV_PAIRS, LANES, LANES), F32)
    row_shape = jax.ShapeDtypeStruct((rows, D_MODEL), F32)
    rp, y0, bonus, m_all, n_all = pl.pallas_call(
        functools.partial(_rwkv_prep_kernel, C, B == 1, first_valid, PG),
        grid=(nblk, RWKV_PAIRS // PG),
        in_specs=[wide] * 5 + [pl.BlockSpec((SUBLANES, PG * LANES), lambda blk, pg: (0, pg))],
        out_specs=[wide, wide, wide, mat, mat],
        out_shape=[row_shape, row_shape, row_shape, mat_shape, mat_shape],
        compiler_params=_cparams(("parallel", "parallel")),
        name=name + "_terms",
    )(r, k, v, wl, al, prm)

    full = pl.BlockSpec((C, D_MODEL), lambda b, c: (b * NC + c, 0))
    mats = pl.BlockSpec((1, RWKV_PAIRS, LANES, LANES), lambda b, c: (b * NC + c, 0, 0, 0))
    st_spec = pl.BlockSpec((1, 2 * RWKV_PAIRS, RWKV_HEAD, RWKV_HEAD), lambda b, c: (b, 0, 0, 0))
    return pl.pallas_call(
        functools.partial(_rwkv_scan_kernel, NC),
        grid=(B, NC),
        in_specs=[full] * 4 + [mats, mats, pl.BlockSpec((SUBLANES, D_MODEL), lambda b, c: (0, 0)), st_spec],
        out_specs=[full, st_spec],
        out_shape=[
            jax.ShapeDtypeStruct((out_rows or rows, D_MODEL), BF16),
            jax.ShapeDtypeStruct((B, 2 * RWKV_PAIRS, RWKV_HEAD, RWKV_HEAD), F32),
        ],
        scratch_shapes=[pltpu.VMEM((RWKV_PAIRS, LANES, LANES), F32)],
        compiler_params=_cparams(("parallel", "arbitrary")),
        name=name + "_scan",
    )(rp, y0, bonus, g, m_all, n_all, prm, s0)


def _pad_lanes(vec, offset):
    out = jnp.zeros((1, LANES), F32)
    return lax.dynamic_update_slice(out, vec.reshape(1, -1).astype(F32), (0, offset))


def _sample_rows(arr, row0, nb, nq, front):
    cols = arr.shape[1]
    s = arr[row0:row0 + nb * nq].reshape(nb, nq, cols)
    s = jnp.pad(s, ((0, 0), (front, 0), (0, 0)))
    return s.reshape(nb * (front + nq), cols)


def _only(x):
    assert x.shape[0] == 1
    return x.reshape(x.shape[1:])


def kernel(x_prompt, x_sample, cache_fox_k, cache_fox_v, cache_fox_logf, state_gdn_conv, state_gdn_S,
           state_rwkv_shift, state_rwkv_S, page_table, meta_tokens, ln_mix, ln_mlp, ln_final,
           w_in0, gdn_conv_w, gdn_A_log, gdn_dt_bias, gdn_norm_w, fox_b_f, w_out0,
           rwkv_mu, rwkv_w0, rwkv_w1, rwkv_w2, rwkv_a0, rwkv_a1, rwkv_a2, rwkv_g1, rwkv_g2,
           rwkv_k_k, rwkv_k_a, rwkv_r_k, rwkv_w_r, rwkv_w_k, rwkv_w_v, rwkv_w_o, rwkv_ln_w, rwkv_ln_b,
           w_up, w_down):
    D = D_MODEL
    assert x_prompt.shape[0] == 1 and x_prompt.shape[2] == D
    seq = x_prompt.shape[1]
    nb, nq = x_sample.shape[0], x_sample.shape[1]
    npages = page_table.shape[1]
    tprompt = N_META + seq
    pad = (-tprompt) % LANES
    tp = tprompt + pad
    ns = nb * nq
    R = -(-(tp + ns) // ROW_TILE) * ROW_TILE
    CS = SUBLANES
    front = CS - nq
    assert 3 <= front

    x0 = jnp.concatenate([jnp.zeros((pad, D), F32), meta_tokens.astype(F32), x_prompt[0],
                          x_sample.reshape(ns, D), jnp.zeros((R - tp - ns, D), F32)], axis=0)

    w_in = w_in0[0]
    o_z = GDN_QKV
    o_a = o_z + GDN_QK
    o_b = o_a + GDN_HEADS
    o_fq = o_b + GDN_HEADS
    o_fk = o_fq + FOX_W
    o_fv = o_fk + FOX_W
    o_ff = o_fv + FOX_W
    o_fz = o_ff + FOX_HEADS
    w_big = jnp.concatenate([w_in[:, :o_a], w_in[:, o_fq:o_ff], w_in[:, o_fz:]], axis=1).astype(BF16)
    w_small = jnp.concatenate([w_in[:, o_a:o_fq], w_in[:, o_ff:o_fz],
                               jnp.zeros((D, LANES - 3 * GDN_HEADS), F32)], axis=1).astype(BF16)

    h0 = rmsnorm(x0, ln_mix[0], BF16)
    p = matmul(h0, w_big, name="in_proj")
    ps = matmul(h0, w_small, name="in_proj_small")

    alog_pad = _pad_lanes(gdn_A_log[0], LANE_A)
    dt_pad = _pad_lanes(gdn_dt_bias[0], LANE_A)
    bf_pad = _pad_lanes(fox_b_f[0], LANE_F)
    conv_w = gdn_conv_w[0]
    norm_w = gdn_norm_w[0].reshape(1, GDN_D)

    GC = 64
    o_gdn_p, s_gdn_p = gdn(p, p, ps, COL_Z // GDN_QK, conv_w, alog_pad, dt_pad, norm_w,
                           jnp.zeros((1, GDN_HEADS, GDN_D, GDN_D), F32), None,
                           B=1, NC=tp // GC, C=GC, first_valid=pad, name="gdn_prompt", out_rows=R)
    qkv_ext = _sample_rows(p[:, :GDN_QKV], tp, nb, nq, front)
    z_ext = _sample_rows(p[:, COL_Z:COL_Z + GDN_QK], tp, nb, nq, front)
    ps_ext = _sample_rows(ps, tp, nb, nq, front)
    o_gdn_s, s_gdn_s = gdn(qkv_ext, z_ext, ps_ext, 0, conv_w, alog_pad, dt_pad, norm_w,
                           _only(state_gdn_S), state_gdn_conv, B=nb, NC=1, C=CS, first_valid=front, name="gdn_sample")
    o_gdn_s = o_gdn_s.reshape(nb, CS, GDN_QK)[:, front:].reshape(ns, GDN_QK)

    lf, cq, ck = fox_prep(ps, bf_pad, pad)
    o_fox_p = fox_prompt(p, cq, ck, tp, pad, R)
    pt_flat = page_table.reshape(-1).astype(jnp.int32)
    lf_s = lf[tp:tp + ns, LANE_F:LANE_F + FOX_HEADS].reshape(nb, nq, FOX_HEADS)
    lfn = jnp.tile(jnp.swapaxes(lf_s, 1, 2), (1, nq, 1))
    lfn = jnp.pad(lfn, ((0, 0), (0, 0), (0, SUBLANES - nq)))
    pool = cache_fox_k.shape[1]
    page = cache_fox_k.shape[2]
    cache_k = _only(cache_fox_k).reshape(pool, page * FOX_HEADS, FOX_DH)
    cache_v = _only(cache_fox_v).reshape(pool, page * FOX_HEADS, FOX_DH)
    cache_lft = jnp.swapaxes(_only(cache_fox_logf), 1, 2)
    def sample_heads(col):
        return p[tp:tp + ns, col:col + FOX_W].reshape(nb, nq * FOX_HEADS, FOX_DH)

    o_fox_s = fox_sample(pt_flat, sample_heads(COL_FQ), cache_k, cache_v, cache_lft, lfn,
                         sample_heads(COL_FK), sample_heads(COL_FV), sample_heads(COL_FZ), nb, npages)
    o_fox_s = o_fox_s.reshape(ns, FOX_W).astype(BF16)

    def with_sample_rows(full, sample):
        full = lax.dynamic_update_slice(full, sample, (tp, 0))
        return lax.dynamic_update_slice(full, jnp.zeros((R - tp - ns, full.shape[1]), full.dtype), (tp + ns, 0))

    x1 = matmul([with_sample_rows(o_gdn_p, o_gdn_s), with_sample_rows(o_fox_p, o_fox_s)], w_out0, layer=0,
                res=x0, name="out_proj")
    u0 = matmul(rmsnorm(x1, ln_mlp[0], BF16), w_up, layer=0, act="relu2", out_dtype=BF16, name="mlp_up0")
    x2 = matmul(u0, w_down, layer=0, res=x1, name="mlp_down0")

    h1 = rmsnorm(x2, ln_mix[1], F32)
    h1_s = h1[tp:tp + ns].reshape(nb, nq, D)
    shift_rows = jnp.concatenate([jnp.repeat(state_rwkv_shift[0].astype(F32), nq, axis=0),
                                  jnp.zeros((R - tp - ns, D), F32)], axis=0)
    xr, xw, xk, xv, xa, xg = rwkv_mix(h1, shift_rows, rwkv_mu[0], tp, ns, nq)

    def pad_cols(w):
        return jnp.pad(w, ((0, 0), (0, LANES - w.shape[1]))).astype(BF16)

    def pad_rows(w):
        return jnp.pad(w, ((0, LANES - w.shape[0]), (0, 0))).astype(BF16)

    r_ = matmul(xr, rwkv_w_r, layer=0, name="rwkv_r")
    k_ = matmul(xk, rwkv_w_k, layer=0, name="rwkv_k")
    v_ = matmul(xv, rwkv_w_v, layer=0, name="rwkv_v")
    wl = matmul(matmul(xw, pad_cols(rwkv_w1[0]), act="tanh", out_dtype=BF16, name="rwkv_w1"), pad_rows(rwkv_w2[0]), name="rwkv_w2")
    al = matmul(matmul(xa, pad_cols(rwkv_a1[0]), out_dtype=BF16, name="rwkv_a1"), pad_rows(rwkv_a2[0]), name="rwkv_a2")
    g_ = matmul(matmul(xg, rwkv_g1, layer=0, act="sigmoid", out_dtype=BF16, name="rwkv_g1"), rwkv_g2, layer=0, name="rwkv_g2")

    prm = jnp.stack([rwkv_w0[0], rwkv_a0[0], rwkv_k_k[0], rwkv_k_a[0], rwkv_r_k[0].reshape(D), rwkv_ln_w[0], rwkv_ln_b[0],
                     jnp.zeros((D,), F32)], axis=0).astype(F32)
    RC = 64
    o_rw_p, s_rw_p = rwkv(r_, k_, v_, wl, al, g_, prm, jnp.zeros((1, 2 * RWKV_PAIRS, RWKV_HEAD, RWKV_HEAD), F32),
                          B=1, NC=tp // RC, C=RC, first_valid=pad, name="rwkv_prompt", out_rows=R)
    sx = [_sample_rows(t, tp, nb, nq, front) for t in (r_, k_, v_, wl, al, g_)]
    o_rw_s, s_rw_s = rwkv(*sx, prm, _only(state_rwkv_S), B=nb, NC=1, C=CS, first_valid=front, name="rwkv_sample")
    o_rw_s = o_rw_s.reshape(nb, CS, D)[:, front:].reshape(ns, D)
    x3 = matmul(with_sample_rows(o_rw_p, o_rw_s), rwkv_w_o, layer=0, res=x2, name="rwkv_o")
    u1 = matmul(rmsnorm(x3, ln_mlp[1], BF16), w_up, layer=1, act="relu2", out_dtype=BF16, name="mlp_up1")
    x4 = matmul(u1, w_down, layer=1, res=x3, name="mlp_down1")
    y = rmsnorm(x4, ln_final, F32)

    r0 = pad
    y_prompt = y[r0 + N_META:tp].reshape(1, seq, D)
    y_sample = y[tp:tp + ns].reshape(nb, nq, D)

    def kv_rows(col):
        blk = p[:, col:col + FOX_W]
        return (blk[r0:tp].reshape(1, 1, tprompt, FOX_HEADS, FOX_DH),
                blk[tp:tp + ns].reshape(1, nb, nq, FOX_HEADS, FOX_DH))

    fk_p, fk_s = kv_rows(COL_FK)
    fv_p, fv_s = kv_rows(COL_FV)
    lf8 = lf[:, LANE_F:LANE_F + FOX_HEADS]
    lf_p = lf8[r0:tp].reshape(1, 1, tprompt, FOX_HEADS)
    lf_sm = lf8[tp:tp + ns].reshape(1, nb, nq, FOX_HEADS)
    cb_p = p[tp - 3:tp, :GDN_QKV].reshape(1, 1, 3, GDN_QKV)
    cb_s = p[tp:tp + ns, :GDN_QKV].reshape(nb, nq, GDN_QKV)[:, nq - 3:][None]
    gs_p = s_gdn_p[None]
    gs_s = s_gdn_s[None]
    sh_p = h1[tp - 1].reshape(1, 1, D)
    sh_s = h1_s[:, nq - 1][None]
    rs_p = s_rw_p[None]
    rs_s = s_rw_s[None]
    return (y_prompt, y_sample, fk_p, fk_s, fv_p, fv_s, lf_p, lf_sm, cb_p, cb_s, gs_p, gs_s, sh_p, sh_s, rs_p, rs_s)
```

```python
import functools
import math

import jax
import jax.numpy as jnp
from jax import lax
from jax.experimental import pallas as pl
from jax.experimental.pallas import tpu as pltpu

F32 = jnp.float32
BF16 = jnp.bfloat16

D_MODEL = 2048
N_META = 16
GDN_HEADS = 8
GDN_D = 128
GDN_QK = GDN_HEADS * GDN_D
GDN_QKV = 3 * GDN_QK
FOX_HEADS = 8
FOX_DH = 128
FOX_W = FOX_HEADS * FOX_DH
RWKV_HEAD = 64
RWKV_PAIRS = D_MODEL // (2 * RWKV_HEAD)
NORM_EPS = 1e-6
L2_EPS = 1e-6
GN_EPS = 64e-5
NEG_INF = -1e30

LANES = 128
SUBLANES = 8
ROW_TILE = 640
MM_ROW_TILE = 1792
LOG2E = 1.4426950408889634
VMEM_LIMIT = 48 * 1024 * 1024
FOX_HEADS_PER_STEP = 4
FOX_GROUP = 16
RWKV_PREP_PAIRS = 8

COL_Z = GDN_QKV
COL_FQ = COL_Z + GDN_QK
COL_FK = COL_FQ + FOX_W
COL_FV = COL_FK + FOX_W
COL_FZ = COL_FV + FOX_W
LANE_A = 0
LANE_B = 8
LANE_F = 16
PRM_W0, PRM_A0, PRM_KK, PRM_KA, PRM_RK, PRM_LNW, PRM_LNB = range(7)


def _pick(n, cands):
    for c in cands:
        if n % c == 0:
            return c
    raise ValueError(f"no tile for {n}")


def _cparams(sem):
    return pltpu.CompilerParams(dimension_semantics=sem, vmem_limit_bytes=VMEM_LIMIT)


def _dot(a, b):
    return jnp.dot(a, b, preferred_element_type=F32)


def _dot_nt(a, b):
    return lax.dot_general(a, b, (((1,), (1,)), ((), ())), preferred_element_type=F32)


def _dot_tn(a, b):
    return lax.dot_general(a, b, (((0,), (0,)), ((), ())), preferred_element_type=F32)


def _bf(x):
    return x.astype(BF16)


def _softplus(x):
    return jnp.maximum(x, 0.0) + jnp.log(1.0 + jnp.exp(-jnp.abs(x)))


def _sigmoid(x):
    return 1.0 / (1.0 + jnp.exp(-x))


def _iota2(shape, dim):
    return lax.broadcasted_iota(jnp.int32, shape, dim)


def _div(x, n):
    assert n & (n - 1) == 0
    return x >> (n.bit_length() - 1)


def _split2(x):
    hi = x.astype(BF16)
    return hi, (x - hi.astype(F32)).astype(BF16)


def _split3(x):
    p1 = x.astype(BF16)
    rem = x - p1.astype(F32)
    p2 = rem.astype(BF16)
    return p1, p2, (rem - p2.astype(F32)).astype(BF16)


def _dot3(a, b):
    m = a.shape[0]
    ah, al = _split2(a)
    bh, bl = _split2(b)
    top = _dot(jnp.concatenate([ah, al], axis=0), bh)
    return (top[:m] + top[m:]) + _dot(ah, bl)


def _dot_lhs01(a01, b):
    a = a01.astype(BF16)
    b1, b2, b3 = _split3(b)
    return _dot(a, b1) + (_dot(a, b2) + _dot(a, b3))


def _dot_rhs01(a, b01):
    b = b01.astype(BF16)
    a1, a2, a3 = _split3(a)
    m = a.shape[0]
    out = _dot(jnp.concatenate([a1, a2, a3], axis=0), b)
    return out[:m] + (out[m:2 * m] + out[2 * m:])


def _tri_inv_steps(n_mat, n):
    size = n_mat.shape[0]
    eye = (_iota2((size, size), 0) == _iota2((size, size), 1)).astype(F32)
    t = eye + n_mat
    if n <= 2:
        return t
    p = _dot(_bf(n_mat), _bf(n_mat))
    yield
    m = 2
    while m < n:
        p_bf = _bf(p)
        if 2 * m < n:
            both = _dot(_bf(jnp.concatenate([p, t], axis=0)), p_bf)
            p = both[:size]
            t = t + both[size:]
        else:
            t = t + _dot(_bf(t), p_bf)
        m *= 2
        yield
    return t


def _run_interleaved(gens):
    results = [None] * len(gens)
    live = list(enumerate(gens))
    while live:
        nxt = []
        for idx, gen in live:
            try:
                next(gen)
                nxt.append((idx, gen))
            except StopIteration as stop:
                results[idx] = stop.value
        live = nxt
    return results


def _rmsnorm_kernel(x_ref, w_ref, o_ref):
    x = x_ref[...]
    ms = jnp.mean(x * x, axis=-1, keepdims=True)
    o_ref[...] = (x * lax.rsqrt(ms + NORM_EPS) * w_ref[...]).astype(o_ref.dtype)


def rmsnorm(x, w, out_dtype):
    m, d = x.shape
    tm = _pick(m, (320, 256, 128))
    return pl.pallas_call(
        _rmsnorm_kernel,
        grid=(m // tm,),
        in_specs=[pl.BlockSpec((tm, d), lambda i: (i, 0)), pl.BlockSpec((1, d), lambda i: (0, 0))],
        out_specs=pl.BlockSpec((tm, d), lambda i: (i, 0)),
        out_shape=jax.ShapeDtypeStruct((m, d), out_dtype),
        compiler_params=_cparams(("parallel",)),
        name="rmsnorm",
    )(x, w.reshape(1, d))


def _mm_kernel(nk, nx, act, has_res, *refs):
    x_refs, w_ref = refs[:nx], refs[nx]
    r_ref = refs[nx + 1] if has_res else None
    o_ref = refs[nx + 1 + has_res]
    scr = refs[nx + 2 + has_res:]

    def product():
        if nx == 1:
            return _dot(x_refs[0][...], _bf(w_ref[...]))
        acc, k0 = None, 0
        for x_ref in x_refs:
            kw = x_ref.shape[1]
            part = _dot(x_ref[...], _bf(w_ref[k0:k0 + kw, :]))
            acc = part if acc is None else acc + part
            k0 += kw
        return acc

    def finish(acc):
        if act == "relu2":
            acc = jnp.square(jnp.maximum(acc, 0.0))
        elif act == "tanh":
            acc = jnp.tanh(acc)
        elif act == "sigmoid":
            acc = _sigmoid(acc)
        if has_res:
            acc = r_ref[...] + acc
        o_ref[...] = acc.astype(o_ref.dtype)

    if nk == 1:
        finish(product())
    else:
        acc_ref = scr[0]
        k = pl.program_id(2)

        @pl.when(k == 0)
        def _():
            acc_ref[...] = jnp.zeros(acc_ref.shape, F32)

        acc_ref[...] += product()

        @pl.when(k == nk - 1)
        def _():
            finish(acc_ref[...])


def matmul(x, w, *, layer=None, act=None, res=None, out_dtype=F32, name="matmul"):
    xs = list(x) if isinstance(x, (list, tuple)) else [x]
    m = xs[0].shape[0]
    k = sum(xi.shape[1] for xi in xs)
    n = w.shape[-1]
    tm = _pick(m, (MM_ROW_TILE, ROW_TILE, 512, 256, 128))
    tn = _pick(n, (512, 256, 128))
    tk = k if k <= 2048 else 2048
    nk = k // tk
    assert len(xs) == 1 or nk == 1
    in_specs = [pl.BlockSpec((tm, tk if len(xs) == 1 else xi.shape[1]), lambda i, j, kk: (i, kk)) for xi in xs]
    if w.ndim == 3:
        in_specs.append(pl.BlockSpec((None, tk, tn), lambda i, j, kk: (layer, kk, j)))
    else:
        in_specs.append(pl.BlockSpec((tk, tn), lambda i, j, kk: (kk, j)))
    args = xs + [w]
    if res is not None:
        in_specs.append(pl.BlockSpec((tm, tn), lambda i, j, kk: (i, j)))
        args.append(res)
    return pl.pallas_call(
        functools.partial(_mm_kernel, nk, len(xs), act, res is not None),
        grid=(m // tm, n // tn, nk),
        in_specs=in_specs,
        out_specs=pl.BlockSpec((tm, tn), lambda i, j, kk: (i, j)),
        out_shape=jax.ShapeDtypeStruct((m, n), out_dtype),
        scratch_shapes=[pltpu.VMEM((tm, tn), F32)] if nk > 1 else [],
        compiler_params=_cparams(("parallel", "parallel", "arbitrary")),
        name=name,
    )(*args)


def _gdn_kernel(C, NC, first_valid, has_conv, qkv_ref, z_ref, ps_ref, cw_ref, alog_ref, dt_ref, nw_ref, s0_ref,
                *rest):
    conv_ref = rest[0] if has_conv else None
    o_ref, sout_ref, S, ext = rest[1:] if has_conv else rest
    c = pl.program_id(1)

    @pl.when(c == 0)
    def _():
        S[...] = s0_ref[0]
        ext[0:SUBLANES, :] = jnp.zeros((SUBLANES, GDN_QKV), F32)

    x = qkv_ref[...]
    ext[SUBLANES:SUBLANES + C, :] = x
    if has_conv:
        assert NC == 1 and 3 <= first_valid < C
        ext[SUBLANES + first_valid - 3:SUBLANES + first_valid, :] = conv_ref[0, 0]
    cw = cw_ref[...]
    y = ((ext[5:5 + C, :] * cw[0:1] + ext[6:6 + C, :] * cw[1:2]) + ext[7:7 + C, :] * cw[2:3]) + ext[8:8 + C, :] * cw[3:4]
    ext[0:SUBLANES, :] = x[C - SUBLANES:C]
    y = y * _sigmoid(y)

    rows = c * C + _iota2((C, 1), 0)
    valid = rows >= first_valid
    ps = ps_ref[...]
    g_all = jnp.where(valid, -jnp.exp(alog_ref[...]) * _softplus(ps + dt_ref[...]), 0.0)
    beta_all = jnp.where(valid, _sigmoid(ps), 0.0)
    ri = _iota2((C, C), 0)
    ci = _iota2((C, C), 1)
    causal = ri >= ci
    strict = ri > ci
    gc = _dot_lhs01(causal.astype(F32), g_all)
    gct = _dot_rhs01(g_all.T, (ri <= ci).astype(F32))
    nw = nw_ref[...]

    def head_steps(h):
        sl = slice(h * GDN_D, (h + 1) * GDN_D)
        qh = y[:, sl]
        kh = y[:, GDN_QK + h * GDN_D:GDN_QK + (h + 1) * GDN_D]
        vh = jnp.where(valid, y[:, 2 * GDN_QK + h * GDN_D:2 * GDN_QK + (h + 1) * GDN_D], 0.0)
        qh = jnp.where(valid, qh * lax.rsqrt(jnp.sum(qh * qh, axis=-1, keepdims=True) + L2_EPS) * GDN_D ** -0.5, 0.0)
        kh = jnp.where(valid, kh * lax.rsqrt(jnp.sum(kh * kh, axis=-1, keepdims=True) + L2_EPS), 0.0)
        bcol = beta_all[:, LANE_B + h:LANE_B + h + 1]
        gcol = gc[:, LANE_A + h:LANE_A + h + 1]
        grow = gct[LANE_A + h:LANE_A + h + 1, :]
        glast = gc[C - 1:C, LANE_A + h:LANE_A + h + 1]
        diff = gcol - grow
        decay = jnp.where(causal, jnp.exp(jnp.where(causal, diff, 0.0)), 0.0)
        kb = kh * bcol
        a_mat = jnp.where(strict, _dot_nt(_bf(kb), _bf(kh)) * decay, 0.0)
        qk = _dot_nt(_bf(qh), _bf(kh)) * decay
        eg = jnp.exp(gcol)
        rhs = jnp.concatenate([vh * bcol, kb * eg], axis=1)
        q_dec = qh * eg
        k_dec = kh * jnp.exp(glast - gcol)
        yield
        t_mat = yield from _tri_inv_steps(-a_mat, C)
        sol = _dot3(t_mat, rhs)
        u = sol[:, :GDN_D]
        w = sol[:, GDN_D:]
        yield
        s_h = S[h]
        s_bf = _bf(s_h)
        v_new = u - _dot(_bf(w), s_bf)
        o_state = _dot(_bf(q_dec), s_bf)
        yield
        o = o_state + _dot(_bf(qk), _bf(v_new))
        S[h] = s_h * jnp.exp(glast) + _dot_tn(_bf(k_dec), _bf(v_new))
        yield
        on = o * lax.rsqrt(jnp.mean(o * o, axis=-1, keepdims=True) + NORM_EPS) * nw
        zh = z_ref[:, sl]
        o_ref[:, sl] = (on * (zh * _sigmoid(zh))).astype(o_ref.dtype)

    _run_interleaved([head_steps(h) for h in range(GDN_HEADS)])

    @pl.when(c == NC - 1)
    def _():
        sout_ref[0] = S[...]


def gdn(qkv_arr, z_arr, ps_arr, z_col, conv_w, alog_pad, dt_pad, norm_w, s0, conv0, *, B, NC, C, first_valid, name,
        out_rows=None):
    rows = B * NC * C
    has_conv = conv0 is not None
    extra_specs = [pl.BlockSpec((1, 1, 3, GDN_QKV), lambda b, c: (0, b, 0, 0))] if has_conv else []
    extra_args = [conv0] if has_conv else []
    return pl.pallas_call(
        functools.partial(_gdn_kernel, C, NC, first_valid, has_conv),
        grid=(B, NC),
        in_specs=[
            pl.BlockSpec((C, GDN_QKV), lambda b, c: (b * NC + c, 0)),
            pl.BlockSpec((C, GDN_QK), lambda b, c: (b * NC + c, z_col)),
            pl.BlockSpec((C, LANES), lambda b, c: (b * NC + c, 0)),
            pl.BlockSpec((4, GDN_QKV), lambda b, c: (0, 0)),
            pl.BlockSpec((1, LANES), lambda b, c: (0, 0)),
            pl.BlockSpec((1, LANES), lambda b, c: (0, 0)),
            pl.BlockSpec((1, GDN_D), lambda b, c: (0, 0)),
            pl.BlockSpec((1, GDN_HEADS, GDN_D, GDN_D), lambda b, c: (b, 0, 0, 0)),
        ] + extra_specs,
        out_specs=[
            pl.BlockSpec((C, GDN_QK), lambda b, c: (b * NC + c, 0)),
            pl.BlockSpec((1, GDN_HEADS, GDN_D, GDN_D), lambda b, c: (b, 0, 0, 0)),
        ],
        out_shape=[
            jax.ShapeDtypeStruct((out_rows or rows, GDN_QK), BF16),
            jax.ShapeDtypeStruct((B, GDN_HEADS, GDN_D, GDN_D), F32),
        ],
        scratch_shapes=[pltpu.VMEM((GDN_HEADS, GDN_D, GDN_D), F32), pltpu.VMEM((C + SUBLANES, GDN_QKV), F32)],
        compiler_params=_cparams(("parallel", "arbitrary")),
        name=name,
    )(qkv_arr, z_arr, ps_arr, conv_w, alog_pad, dt_pad, norm_w, s0, *extra_args)


def _fox_prep_kernel(tb, first_valid, ps_ref, bf_ref, lf_ref, cq_ref, ck_ref, carry):
    i = pl.program_id(0)

    @pl.when(i == 0)
    def _():
        carry[...] = jnp.zeros(carry.shape, F32)

    x = ps_ref[...] + bf_ref[...]
    rows = i * tb + _iota2((tb, 1), 0)
    lf = jnp.where(rows >= first_valid, jnp.minimum(x, 0.0) - jnp.log(1.0 + jnp.exp(-jnp.abs(x))), 0.0)
    tri = (_iota2((tb, tb), 0) >= _iota2((tb, tb), 1)).astype(F32)
    c = _dot_lhs01(tri, lf) + carry[0:1, :]
    carry[...] = jnp.broadcast_to(c[tb - 1:tb, :], carry.shape)
    lf_ref[...] = lf
    c2 = c * LOG2E
    for h in range(FOX_HEADS):
        cq_ref[h] = jnp.broadcast_to(c2[:, LANE_F + h:LANE_F + h + 1], (tb, LANES))
    ck_ref[...] = c2.T[LANE_F:LANE_F + FOX_HEADS, :]


def fox_prep(ps, bf_pad, first_valid):
    r = ps.shape[0]
    tb = _pick(r, (ROW_TILE, 512, 256, 128))
    return pl.pallas_call(
        functools.partial(_fox_prep_kernel, tb, first_valid),
        grid=(r // tb,),
        in_specs=[pl.BlockSpec((tb, LANES), lambda i: (i, 0)), pl.BlockSpec((1, LANES), lambda i: (0, 0))],
        out_specs=[
            pl.BlockSpec((tb, LANES), lambda i: (i, 0)),
            pl.BlockSpec((FOX_HEADS, tb, LANES), lambda i: (0, i, 0)),
            pl.BlockSpec((FOX_HEADS, tb), lambda i: (0, i)),
        ],
        out_shape=[
            jax.ShapeDtypeStruct((r, LANES), F32),
            jax.ShapeDtypeStruct((FOX_HEADS, r, LANES), F32),
            jax.ShapeDtypeStruct((FOX_HEADS, r), F32),
        ],
        scratch_shapes=[pltpu.VMEM((SUBLANES, LANES), F32)],
        compiler_params=_cparams(("arbitrary",)),
        name="fox_prep",
    )(ps, bf_pad)


def _fox_flash_kernel(tq, first_valid, qi_ref, ki_ref, qt_ref, k_ref, vt_ref, ck_ref, cq_ref, fz_ref, o_ref,
                      m_s, l_s, acc_s, t_s):
    step = pl.program_id(1)
    qi = qi_ref[step]
    ki = ki_ref[step]
    reps = tq // LANES
    heads = qt_ref.shape[0]

    @pl.when(ki == 0)
    def _():
        m_s[...] = jnp.full(m_s.shape, NEG_INF, F32)
        l_s[...] = jnp.zeros(l_s.shape, F32)
        acc_s[...] = jnp.zeros(acc_s.shape, F32)

    def head_steps(h, masked):
        cols = slice(h * FOX_DH, (h + 1) * FOX_DH)
        qt = _bf(qt_ref[h] * (FOX_DH ** -0.5 * LOG2E))
        cq = cq_ref[h]
        m_prev = m_s[h, 0:1, :]
        kb = LANES
        groups = kb // SUBLANES
        mx = None
        for b in range(tq // kb):
            rs = slice(b * kb, (b + 1) * kb)
            t = _dot(_bf(k_ref[rs, cols]), qt) - jnp.concatenate([ck_ref[h, rs, :]] * reps, axis=1)
            if masked:
                kpos = ki * tq + b * kb + _iota2((kb, tq), 0)
                qpos = qi * tq + _iota2((kb, tq), 1)
                t = jnp.where((kpos <= qpos) & (kpos >= first_valid), t, NEG_INF)
            t_s[h, rs, :] = t
            part = jnp.max(t.reshape(groups, SUBLANES, tq), axis=0)
            mx = part if mx is None else jnp.maximum(mx, part)
            yield
        m_new = jnp.maximum(m_prev, jnp.max(mx, axis=0, keepdims=True) + cq)
        shift = cq - m_new
        alpha = jnp.exp2(m_prev - m_new)
        lsum = None
        pv = None
        for b in range(tq // kb):
            rs = slice(b * kb, (b + 1) * kb)
            p = jnp.exp2(t_s[h, rs, :] + shift)
            part = jnp.sum(p.reshape(groups, SUBLANES, tq), axis=0)
            lsum = part if lsum is None else lsum + part
            d = _dot(_bf(vt_ref[h, :, rs]), _bf(p))
            pv = d if pv is None else pv + d
            yield
        l_new = alpha * l_s[h, 0:1, :] + jnp.sum(lsum, axis=0, keepdims=True)
        l_s[h] = jnp.broadcast_to(l_new, (SUBLANES, tq))
        acc_s[h] = alpha * acc_s[h] + pv
        m_s[h] = jnp.broadcast_to(m_new, (SUBLANES, tq))

    def accumulate(masked):
        _run_interleaved([head_steps(h, masked) for h in range(heads)])

    edge = (ki == qi) | (ki == 0)
    pl.when(edge)(lambda: accumulate(True))
    pl.when(jnp.logical_not(edge))(lambda: accumulate(False))

    @pl.when(ki == qi)
    def _():
        rows = qi * tq + _iota2((tq, 1), 0)
        for h in range(heads):
            cols = slice(h * FOX_DH, (h + 1) * FOX_DH)
            o = (acc_s[h] / l_s[h, 0:1, :]).T * _sigmoid(fz_ref[:, cols])
            o_ref[:, cols] = jnp.where(rows >= first_valid, o, 0.0).astype(o_ref.dtype)


def fox_prompt(p, c_rep, c_row, tp, first_valid, out_rows):
    tq = _pick(tp, (ROW_TILE, 512, 384, 256, 128))
    nq = tp // tq
    assert first_valid < tq
    pairs = [(qi, ki) for qi in range(nq) for ki in range(qi + 1)]
    qi_arr = jnp.array([a for a, _ in pairs], jnp.int32)
    ki_arr = jnp.array([b for _, b in pairs], jnp.int32)
    hp = FOX_HEADS_PER_STEP
    wid = hp * FOX_DH
    cb = lambda col, g: col // wid + g

    def heads_t(col):
        return jnp.transpose(p[:tp, col:col + FOX_W].reshape(tp, FOX_HEADS, FOX_DH), (1, 2, 0))

    return pl.pallas_call(
        functools.partial(_fox_flash_kernel, tq, first_valid),
        grid_spec=pltpu.PrefetchScalarGridSpec(
            num_scalar_prefetch=2,
            grid=(FOX_HEADS // hp, len(pairs)),
            in_specs=[
                pl.BlockSpec((hp, FOX_DH, tq), lambda g, t, qa, ka: (g, 0, qa[t])),
                pl.BlockSpec((tq, wid), lambda g, t, qa, ka: (ka[t], cb(COL_FK, g))),
                pl.BlockSpec((hp, FOX_DH, tq), lambda g, t, qa, ka: (g, 0, ka[t])),
                pl.BlockSpec((hp, tq, LANES), lambda g, t, qa, ka: (g, ka[t], 0)),
                pl.BlockSpec((hp, 1, tq), lambda g, t, qa, ka: (g, 0, qa[t])),
                pl.BlockSpec((tq, wid), lambda g, t, qa, ka: (qa[t], cb(COL_FZ, g))),
            ],
            out_specs=pl.BlockSpec((tq, wid), lambda g, t, qa, ka: (qa[t], g)),
            scratch_shapes=[pltpu.VMEM((hp, SUBLANES, tq), F32), pltpu.VMEM((hp, SUBLANES, tq), F32),
                            pltpu.VMEM((hp, FOX_DH, tq), F32), pltpu.VMEM((hp, tq, tq), F32)],
        ),
        out_shape=jax.ShapeDtypeStruct((out_rows, FOX_W), BF16),
        compiler_params=_cparams(("parallel", "arbitrary")),
        name="fox_prompt",
    )(qi_arr, ki_arr, heads_t(COL_FQ), p, heads_t(COL_FV), c_rep, c_row.reshape(FOX_HEADS, 1, -1), p)


def _fox_sample_kernel(G, ngroups, pt_ref, q_ref, *refs):
    k_refs, v_refs, lft_refs = refs[0:G], refs[G:2 * G], refs[2 * G:3 * G]
    lfn_ref, kn_ref, vn_ref, fz_ref, o_ref, cq_s, cn_s, carry_s, m_s, l_s, acc_s = refs[3 * G:]
    jg = pl.program_id(1)
    rows = q_ref.shape[1]
    page = lft_refs[0].shape[2]
    flat = page * FOX_HEADS
    nn = lfn_ref.shape[2]
    scale = FOX_DH ** -0.5

    @pl.when(jg == 0)
    def _():
        upper = (_iota2((nn, nn), 0) <= _iota2((nn, nn), 1)).astype(F32)
        cn = _dot_rhs01(lfn_ref[0], upper)
        cn_s[...] = cn
        own_q = _iota2((rows, nn), 1) == _div(_iota2((rows, nn), 0), FOX_HEADS)
        cq_s[...] = jnp.broadcast_to(jnp.sum(jnp.where(own_q, cn, 0.0), axis=-1, keepdims=True), cq_s.shape)
        carry_s[...] = jnp.zeros(carry_s.shape, F32)
        m_s[...] = jnp.full(m_s.shape, NEG_INF, F32)
        l_s[...] = jnp.zeros(l_s.shape, F32)
        acc_s[...] = jnp.zeros(acc_s.shape, F32)

    def update(ts, vals_bf):
        cq = cq_s[:, 0:1]
        m_prev = m_s[:, 0:1]
        t_max = functools.reduce(jnp.maximum, ts)
        m_new = jnp.maximum(m_prev, jnp.max(t_max, axis=-1, keepdims=True) + cq)
        shift = cq - m_new
        ps = [jnp.exp(t + shift) for t in ts]
        alpha = jnp.exp(m_prev - m_new)
        l_s[...] = alpha * l_s[...] + jnp.sum(functools.reduce(jnp.add, ps), axis=-1, keepdims=True)
        pv = functools.reduce(jnp.add, [_dot(_bf(p), v) for p, v in zip(ps, vals_bf)])
        acc_s[...] = alpha * acc_s[...] + pv
        m_s[...] = jnp.broadcast_to(m_new, m_s.shape)

    own = (_iota2((rows, flat), 0) & (FOX_HEADS - 1)) == (_iota2((rows, flat), 1) & (FOX_HEADS - 1))
    later = (_iota2((page, flat), 0) > _div(_iota2((page, flat), 1), FOX_HEADS)).astype(BF16)
    head_col = _iota2((FOX_HEADS, flat), 0) == (_iota2((FOX_HEADS, flat), 1) & (FOX_HEADS - 1))
    carry = carry_s[:, 0:1]
    q_bf = _bf(q_ref[0])
    lfts = [lft_refs[g][0] for g in range(G)]
    suffixes = _dot_rhs01(jnp.concatenate(lfts, axis=0), later)
    ts = []
    for g in range(G):
        suffix = suffixes[g * FOX_HEADS:(g + 1) * FOX_HEADS]
        d = jnp.sum(jnp.where(head_col, suffix + carry, 0.0), axis=0, keepdims=True)
        carry = carry + jnp.sum(lfts[g], axis=-1, keepdims=True)
        s = _dot_nt(q_bf, _bf(k_refs[g][0])) * scale
        ts.append(jnp.where(own, s + d, NEG_INF))
    update(ts, [_bf(v_refs[g][0]) for g in range(G)])
    carry_s[...] = jnp.broadcast_to(carry, carry_s.shape)

    @pl.when(jg == ngroups - 1)
    def _():
        nflat = kn_ref.shape[1]
        spread = (_iota2((nn, nflat), 0) == _div(_iota2((nn, nflat), 1), FOX_HEADS)).astype(F32)
        cn_cols = _dot_rhs01(cn_s[...], spread)
        ri = _iota2((rows, nflat), 0)
        ci = _iota2((rows, nflat), 1)
        ok = ((ri & (FOX_HEADS - 1)) == (ci & (FOX_HEADS - 1))) & (_div(ci, FOX_HEADS) <= _div(ri, FOX_HEADS))
        sn = _dot_nt(_bf(q_ref[0]), _bf(kn_ref[0])) * scale
        update([jnp.where(ok, sn - cn_cols, NEG_INF)], [_bf(vn_ref[0])])
        o_ref[0] = acc_s[...] / l_s[...] * _sigmoid(fz_ref[0])


def fox_sample(page_table_flat, q_rows, cache_k, cache_v, cache_lft, lfn, kn_flat, vn_flat, fz_rows, nb, npages):
    flat = cache_k.shape[1]
    page = cache_lft.shape[2]
    rows = q_rows.shape[1]
    nn = lfn.shape[2]
    G = _pick(npages, (FOX_GROUP, 2, 1))
    ngroups = npages // G

    def page_map(g):
        return lambda b, jg, pt: (pt[b * npages + (npages - 1 - (jg * G + g))], 0, 0)

    seq = lambda b, jg, pt: (b, 0, 0)
    row_blk = pl.BlockSpec((1, rows, FOX_DH), seq)
    new_blk = pl.BlockSpec((1, kn_flat.shape[1], FOX_DH), seq)
    in_specs = [row_blk]
    in_specs += [pl.BlockSpec((1, flat, FOX_DH), page_map(g)) for g in range(G)]
    in_specs += [pl.BlockSpec((1, flat, FOX_DH), page_map(g)) for g in range(G)]
    in_specs += [pl.BlockSpec((1, FOX_HEADS, page), page_map(g)) for g in range(G)]
    in_specs += [pl.BlockSpec((1, rows, nn), seq), new_blk, new_blk, row_blk]
    return pl.pallas_call(
        functools.partial(_fox_sample_kernel, G, ngroups),
        grid_spec=pltpu.PrefetchScalarGridSpec(
            num_scalar_prefetch=1,
            grid=(nb, ngroups),
            in_specs=in_specs,
            out_specs=row_blk,
            scratch_shapes=[
                pltpu.VMEM((rows, LANES), F32),
                pltpu.VMEM((rows, nn), F32),
                pltpu.VMEM((FOX_HEADS, LANES), F32),
                pltpu.VMEM((rows, LANES), F32),
                pltpu.VMEM((rows, LANES), F32),
                pltpu.VMEM((rows, FOX_DH), F32),
            ],
        ),
        out_shape=jax.ShapeDtypeStruct((nb, rows, FOX_DH), F32),
        compiler_params=_cparams(("parallel", "arbitrary")),
        name="fox_sample",
    )(page_table_flat, q_rows, *([cache_k] * G), *([cache_v] * G), *([cache_lft] * G), lfn, kn_flat, vn_flat, fz_rows)


def _rwkv_mix_kernel(tm, tp, ns, nq, h_ref, hb_ref, st_ref, mu_ref, *o_refs):
    i = pl.program_id(0)
    h = h_ref[...]
    above = jnp.concatenate([hb_ref[SUBLANES - 1:SUBLANES, :], h[:tm - 1]], axis=0)
    row = i * tm + _iota2((tm, 1), 0)
    seq_start = (row >= tp) & (row < tp + ns) & (((row - tp) & (nq - 1)) == 0)
    prev = jnp.where(seq_start, st_ref[...], jnp.where(row == 0, 0.0, above))
    xx = prev - h
    mu = mu_ref[...]
    for j, o_ref in enumerate(o_refs):
        o_ref[...] = (h + xx * mu[j:j + 1]).astype(o_ref.dtype)


def rwkv_mix(h, shift_rows, mu, tp, ns, nq):
    m, d = h.shape
    assert nq & (nq - 1) == 0
    tm = _pick(math.gcd(m, tp), (320, 256, 128))
    first = tp // tm
    spec = pl.BlockSpec((tm, d), lambda i: (i, 0))
    return pl.pallas_call(
        functools.partial(_rwkv_mix_kernel, tm, tp, ns, nq),
        grid=(m // tm,),
        in_specs=[spec,
                  pl.BlockSpec((SUBLANES, d), lambda i: (jnp.maximum(i * (tm // SUBLANES) - 1, 0), 0)),
                  pl.BlockSpec((tm, d), lambda i: (jnp.maximum(i - first, 0), 0)),
                  pl.BlockSpec((6, d), lambda i: (0, 0))],
        out_specs=[spec] * 6,
        out_shape=[jax.ShapeDtypeStruct((m, d), BF16)] * 6,
        compiler_params=_cparams(("parallel",)),
        name="rwkv_mix",
    )(h, h, shift_rows, mu)


def _head_ones():
    return (_div(_iota2((LANES, LANES), 0), RWKV_HEAD) == _div(_iota2((LANES, LANES), 1), RWKV_HEAD)).astype(F32)


def _rwkv_chunk_terms(C, valid, r, k, v, wl, al, prm):
    HD = RWKV_HEAD
    m0 = _iota2((1, LANES), 1) < HD
    bones = _head_ones()
    w0, a0, k_k, k_a, r_k = (prm[i:i + 1] for i in (PRM_W0, PRM_A0, PRM_KK, PRM_KA, PRM_RK))
    wlog = -_softplus(-(w0 + wl)) - 0.5
    lw = jnp.where(valid, -jnp.exp(wlog), 0.0)
    a = _sigmoid(a0 + al)
    kkr = k * k_k
    kk = kkr * lax.rsqrt(_dot_rhs01(kkr * kkr, bones) + L2_EPS)
    k2 = k * (1.0 + (a - 1.0) * k_a)
    bonus = _dot_rhs01(r * k2 * r_k, bones) * v
    rm = jnp.where(valid, r, 0.0)
    k2 = jnp.where(valid, k2, 0.0)
    vm = jnp.where(valid, v, 0.0)
    av = jnp.where(valid, -kk, 0.0)
    bv = jnp.where(valid, kk * a, 0.0)

    ri = _iota2((C, C), 0)
    ci = _iota2((C, C), 1)
    yield
    cum = _dot_lhs01((ri >= ci).astype(F32), lw)
    cum_last = cum[C - 1:C, :]
    inv = jnp.exp(-cum)
    rt = rm * jnp.exp(cum)
    at = av * jnp.exp(cum - lw)
    bt = bv * inv
    kt = k2 * inv
    to_end = jnp.exp(cum_last - cum)
    b_end = bv * to_end
    k_end = k2 * to_end

    def split(x):
        return jnp.concatenate([jnp.where(m0, x, 0.0), jnp.where(m0, 0.0, x)], axis=0)

    def halves(x):
        return x[0:C] + x[C:2 * C]

    yield
    at_s = split(at)
    gram = _dot_nt(_bf(jnp.concatenate([at_s, split(rt)], axis=0)), _bf(jnp.concatenate([bt, kt], axis=0)))
    r2 = _iota2((2 * C, 2 * C), 0)
    c2 = _iota2((2 * C, 2 * C), 1)
    same = _div(r2, C) == _div(c2, C)
    bd_strict = same & (r2 > c2)
    bd_incl = same & (r2 >= c2)

    def bd(block, mask):
        return jnp.where(mask, jnp.concatenate([block, block], axis=1), 0.0)

    a_ab = bd(gram[0:2 * C, 0:C], bd_strict)
    a_ak = bd(gram[0:2 * C, C:2 * C], bd_strict)
    r_b = bd(gram[2 * C:4 * C, 0:C], bd_incl)
    r_k2 = bd(gram[2 * C:4 * C, C:2 * C], bd_incl)
    vs = split(vm)
    akv = halves(_dot(_bf(a_ak), _bf(vs)))
    yield
    t_bd = yield from _tri_inv_steps(a_ab, C)
    tw = _dot(_bf(t_bd), _bf(jnp.concatenate([at_s, split(akv)], axis=1)))
    wt = halves(tw[:, :LANES])
    ut = halves(tw[:, LANES:])
    yield
    rp = rt + halves(_dot(_bf(r_b), _bf(split(wt))))
    y0 = halves(_dot(_bf(jnp.concatenate([r_b, r_k2], axis=1)), _bf(jnp.concatenate([split(ut), vs], axis=0))))
    eye = (_iota2((LANES, LANES), 0) == _iota2((LANES, LANES), 1)).astype(F32)
    m_mat = eye * jnp.exp(cum_last) + bones * _dot_tn(_bf(b_end), _bf(wt))
    n_mat = bones * _dot_tn(_bf(jnp.concatenate([b_end, k_end], axis=0)), _bf(jnp.concatenate([ut, vm], axis=0)))
    return rp, y0, bonus, m_mat, n_mat


def _rwkv_prep_kernel(C, chunk_of_step, first_valid, PG, r_ref, k_ref, v_ref, wl_ref, al_ref, prm_ref,
                      rp_ref, y0_ref, bo_ref, m_ref, n_ref):
    c = pl.program_id(0) if chunk_of_step else 0
    valid = (c * C + _iota2((C, 1), 0)) >= first_valid
    sls = [slice(i * LANES, (i + 1) * LANES) for i in range(PG)]
    terms = _run_interleaved([
        _rwkv_chunk_terms(C, valid, r_ref[:, sl], k_ref[:, sl], v_ref[:, sl], wl_ref[:, sl], al_ref[:, sl], prm_ref[:, sl])
        for sl in sls])
    for i, sl in enumerate(sls):
        rp, y0, bonus, m_mat, n_mat = terms[i]
        rp_ref[:, sl] = rp
        y0_ref[:, sl] = y0
        bo_ref[:, sl] = bonus
        m_ref[0, i] = m_mat
        n_ref[0, i] = n_mat


def _rwkv_scan_kernel(NC, rp_ref, y0_ref, bo_ref, g_ref, m_ref, n_ref, prm_ref, s0_ref, o_ref, sout_ref, H):
    c = pl.program_id(1)
    HD = RWKV_HEAD
    bones = _head_ones()

    @pl.when(c == 0)
    def _():
        zero = jnp.zeros((HD, HD), F32)
        for i in range(RWKV_PAIRS):
            top = jnp.concatenate([s0_ref[0, 2 * i], zero], axis=1)
            bot = jnp.concatenate([zero, s0_ref[0, 2 * i + 1]], axis=1)
            H[i] = jnp.concatenate([top, bot], axis=0).T

    def pair_steps(i):
        sl = slice(i * LANES, (i + 1) * LANES)
        h_bd = H[i]
        y = _dot(_bf(rp_ref[:, sl]), _bf(h_bd)) + y0_ref[:, sl]
        H[i] = _dot3(m_ref[0, i], h_bd) + n_ref[0, i]
        yield
        mean = _dot_rhs01(y, bones) * (1.0 / HD)
        dlt = y - mean
        yield
        var = _dot_rhs01(dlt * dlt, bones) * (1.0 / HD)
        yn = dlt * lax.rsqrt(var + GN_EPS) * prm_ref[PRM_LNW:PRM_LNW + 1, sl] + prm_ref[PRM_LNB:PRM_LNB + 1, sl]
        o_ref[:, sl] = ((yn + bo_ref[:, sl]) * g_ref[:, sl]).astype(o_ref.dtype)

    _run_interleaved([pair_steps(i) for i in range(RWKV_PAIRS)])

    @pl.when(c == NC - 1)
    def _():
        for i in range(RWKV_PAIRS):
            ht = H[i].T
            sout_ref[0, 2 * i] = ht[0:HD, 0:HD]
            sout_ref[0, 2 * i + 1] = ht[HD:2 * HD, HD:2 * HD]


def rwkv(r, k, v, wl, al, g, prm, s0, *, B, NC, C, first_valid, name, out_rows=None):
    assert B == 1 or NC == 1
    nblk = B * NC
    rows = nblk * C
    PG = RWKV_PREP_PAIRS
    wide = pl.BlockSpec((C, PG * LANES), lambda blk, pg: (blk, pg))
    mat = pl.BlockSpec((1, PG, LANES, LANES), lambda blk, pg: (blk, pg, 0, 0))
    mat_shape = jax.ShapeDtypeStruct((nblk, RWKV_PAIRS, LANES, LANES), F32)
    row_shape = jax.ShapeDtypeStruct((rows, D_MODEL), F32)
    rp, y0, bonus, m_all, n_all = pl.pallas_call(
        functools.partial(_rwkv_prep_kernel, C, B == 1, first_valid, PG),
        grid=(nblk, RWKV_PAIRS // PG),
        in_specs=[wide] * 5 + [pl.BlockSpec((SUBLANES, PG * LANES), lambda blk, pg: (0, pg))],
        out_specs=[wide, wide, wide, mat, mat],
        out_shape=[row_shape, row_shape, row_shape, mat_shape, mat_shape],
        compiler_params=_cparams(("parallel", "parallel")),
        name=name + "_terms",
    )(r, k, v, wl, al, prm)

    full = pl.BlockSpec((C, D_MODEL), lambda b, c: (b * NC + c, 0))
    mats = pl.BlockSpec((1, RWKV_PAIRS, LANES, LANES), lambda b, c: (b * NC + c, 0, 0, 0))
    st_spec = pl.BlockSpec((1, 2 * RWKV_PAIRS, RWKV_HEAD, RWKV_HEAD), lambda b, c: (b, 0, 0, 0))
    return pl.pallas_call(
        functools.partial(_rwkv_scan_kernel, NC),
        grid=(B, NC),
        in_specs=[full] * 4 + [mats, mats, pl.BlockSpec((SUBLANES, D_MODEL), lambda b, c: (0, 0)), st_spec],
        out_specs=[full, st_spec],
        out_shape=[
            jax.ShapeDtypeStruct((out_rows or rows, D_MODEL), BF16),
            jax.ShapeDtypeStruct((B, 2 * RWKV_PAIRS, RWKV_HEAD, RWKV_HEAD), F32),
        ],
        scratch_shapes=[pltpu.VMEM((RWKV_PAIRS, LANES, LANES), F32)],
        compiler_params=_cparams(("parallel", "arbitrary")),
        name=name + "_scan",
    )(rp, y0, bonus, g, m_all, n_all, prm, s0)


def _rwkv_lanes_kernel(nq, unroll, x_ref, prm_ref, s0_ref, o_ref, sout_ref, y_s):
    HD = RWKV_HEAD
    w0, a0, k_k, k_a, r_k, ln_w, ln_b = (prm_ref[i] for i in (PRM_W0, PRM_A0, PRM_KK, PRM_KA, PRM_RK,
                                                               PRM_LNW, PRM_LNB))
    sout_ref[...] = s0_ref[...]
    for t in range(nq):
        r, k, v, wl, al, g = (x_ref[i, t] for i in range(6))
        wlog = -_softplus(-(w0 + wl)) - 0.5
        w = jnp.exp(-jnp.exp(wlog))
        a = _sigmoid(a0 + al)
        kkr = k * k_k
        kk = kkr * lax.rsqrt(jnp.sum(kkr * kkr, axis=0, keepdims=True) + L2_EPS)
        k2 = k * (1.0 + (a - 1.0) * k_a)
        av = -kk
        bv = kk * a

        def value_row(i, carry):
            s = sout_ref[0, i]
            sa = jnp.sum(s * av, axis=0, keepdims=True)
            s = s * w + sa * bv + x_ref[2, t, pl.ds(i, 1), :] * k2
            sout_ref[0, i] = s
            y_s[pl.ds(i, 1), :] = jnp.sum(s * r, axis=0, keepdims=True)
            return carry

        lax.fori_loop(0, HD, value_row, 0, unroll=unroll)
        y = y_s[...]
        mean = jnp.mean(y, axis=0, keepdims=True)
        dlt = y - mean
        var = jnp.mean(dlt * dlt, axis=0, keepdims=True)
        yn = dlt * lax.rsqrt(var + GN_EPS) * ln_w + ln_b
        bonus = jnp.sum(r * k2 * r_k, axis=0, keepdims=True) * v
        o_ref[t] = ((yn + bonus) * g).astype(o_ref.dtype)


def rwkv_lanes(xs, prm_col, s0):
    _, nq, d, nb = xs.shape
    heads = d // RWKV_HEAD
    st = pl.BlockSpec((1, RWKV_HEAD, RWKV_HEAD, nb), lambda h: (h, 0, 0, 0))
    return pl.pallas_call(
        functools.partial(_rwkv_lanes_kernel, nq, SUBLANES),
        grid=(heads,),
        in_specs=[pl.BlockSpec((6, nq, RWKV_HEAD, nb), lambda h: (0, 0, h, 0)),
                  pl.BlockSpec((SUBLANES, RWKV_HEAD, 1), lambda h: (0, h, 0)), st],
        out_specs=[pl.BlockSpec((nq, RWKV_HEAD, nb), lambda h: (0, h, 0)), st],
        out_shape=[jax.ShapeDtypeStruct((nq, d, nb), F32), jax.ShapeDtypeStruct(s0.shape, F32)],
        scratch_shapes=[pltpu.VMEM((RWKV_HEAD, nb), F32)],
        compiler_params=_cparams(("parallel",)),
        name="rwkv_sample",
    )(xs, prm_col, s0)


def _pad_lanes(vec, offset):
    out = jnp.zeros((1, LANES), F32)
    return lax.dynamic_update_slice(out, vec.reshape(1, -1).astype(F32), (0, offset))


def _sample_rows(arr, row0, nb, nq, front):
    cols = arr.shape[1]
    s = arr[row0:row0 + nb * nq].reshape(nb, nq, cols)
    s = jnp.pad(s, ((0, 0), (front, 0), (0, 0)))
    return s.reshape(nb * (front + nq), cols)


def _only(x):
    assert x.shape[0] == 1
    return x.reshape(x.shape[1:])


def kernel(x_prompt, x_sample, cache_fox_k, cache_fox_v, cache_fox_logf, state_gdn_conv, state_gdn_S,
           state_rwkv_shift, state_rwkv_S, page_table, meta_tokens, ln_mix, ln_mlp, ln_final,
           w_in0, gdn_conv_w, gdn_A_log, gdn_dt_bias, gdn_norm_w, fox_b_f, w_out0,
           rwkv_mu, rwkv_w0, rwkv_w1, rwkv_w2, rwkv_a0, rwkv_a1, rwkv_a2, rwkv_g1, rwkv_g2,
           rwkv_k_k, rwkv_k_a, rwkv_r_k, rwkv_w_r, rwkv_w_k, rwkv_w_v, rwkv_w_o, rwkv_ln_w, rwkv_ln_b,
           w_up, w_down):
    D = D_MODEL
    assert x_prompt.shape[0] == 1 and x_prompt.shape[2] == D
    seq = x_prompt.shape[1]
    nb, nq = x_sample.shape[0], x_sample.shape[1]
    npages = page_table.shape[1]
    tprompt = N_META + seq
    pad = (-tprompt) % LANES
    tp = tprompt + pad
    ns = nb * nq
    R = -(-(tp + ns) // ROW_TILE) * ROW_TILE
    CS = SUBLANES
    front = CS - nq
    assert 3 <= front

    x0 = jnp.concatenate([jnp.zeros((pad, D), F32), meta_tokens.astype(F32), x_prompt[0],
                          x_sample.reshape(ns, D), jnp.zeros((R - tp - ns, D), F32)], axis=0)

    w_in = w_in0[0]
    o_z = GDN_QKV
    o_a = o_z + GDN_QK
    o_b = o_a + GDN_HEADS
    o_fq = o_b + GDN_HEADS
    o_fk = o_fq + FOX_W
    o_fv = o_fk + FOX_W
    o_ff = o_fv + FOX_W
    o_fz = o_ff + FOX_HEADS
    w_big = jnp.concatenate([w_in[:, :o_a], w_in[:, o_fq:o_ff], w_in[:, o_fz:]], axis=1).astype(BF16)
    w_small = jnp.concatenate([w_in[:, o_a:o_fq], w_in[:, o_ff:o_fz],
                               jnp.zeros((D, LANES - 3 * GDN_HEADS), F32)], axis=1).astype(BF16)

    h0 = rmsnorm(x0, ln_mix[0], BF16)
    p = matmul(h0, w_big, name="in_proj")
    ps = matmul(h0, w_small, name="in_proj_small")

    alog_pad = _pad_lanes(gdn_A_log[0], LANE_A)
    dt_pad = _pad_lanes(gdn_dt_bias[0], LANE_A)
    bf_pad = _pad_lanes(fox_b_f[0], LANE_F)
    conv_w = gdn_conv_w[0]
    norm_w = gdn_norm_w[0].reshape(1, GDN_D)

    GC = 64
    o_gdn_p, s_gdn_p = gdn(p, p, ps, COL_Z // GDN_QK, conv_w, alog_pad, dt_pad, norm_w,
                           jnp.zeros((1, GDN_HEADS, GDN_D, GDN_D), F32), None,
                           B=1, NC=tp // GC, C=GC, first_valid=pad, name="gdn_prompt", out_rows=R)
    qkv_ext = _sample_rows(p[:, :GDN_QKV], tp, nb, nq, front)
    z_ext = _sample_rows(p[:, COL_Z:COL_Z + GDN_QK], tp, nb, nq, front)
    ps_ext = _sample_rows(ps, tp, nb, nq, front)
    o_gdn_s, s_gdn_s = gdn(qkv_ext, z_ext, ps_ext, 0, conv_w, alog_pad, dt_pad, norm_w,
                           _only(state_gdn_S), state_gdn_conv, B=nb, NC=1, C=CS, first_valid=front, name="gdn_sample")
    o_gdn_s = o_gdn_s.reshape(nb, CS, GDN_QK)[:, front:].reshape(ns, GDN_QK)

    lf, cq, ck = fox_prep(ps, bf_pad, pad)
    o_fox_p = fox_prompt(p, cq, ck, tp, pad, R)
    pt_flat = page_table.reshape(-1).astype(jnp.int32)
    lf_s = lf[tp:tp + ns, LANE_F:LANE_F + FOX_HEADS].reshape(nb, nq, FOX_HEADS)
    lfn = jnp.tile(jnp.swapaxes(lf_s, 1, 2), (1, nq, 1))
    lfn = jnp.pad(lfn, ((0, 0), (0, 0), (0, SUBLANES - nq)))
    pool = cache_fox_k.shape[1]
    page = cache_fox_k.shape[2]
    cache_k = _only(cache_fox_k).reshape(pool, page * FOX_HEADS, FOX_DH)
    cache_v = _only(cache_fox_v).reshape(pool, page * FOX_HEADS, FOX_DH)
    cache_lft = jnp.swapaxes(_only(cache_fox_logf), 1, 2)
    def sample_heads(col):
        return p[tp:tp + ns, col:col + FOX_W].reshape(nb, nq * FOX_HEADS, FOX_DH)

    o_fox_s = fox_sample(pt_flat, sample_heads(COL_FQ), cache_k, cache_v, cache_lft, lfn,
                         sample_heads(COL_FK), sample_heads(COL_FV), sample_heads(COL_FZ), nb, npages)
    o_fox_s = o_fox_s.reshape(ns, FOX_W).astype(BF16)

    def with_sample_rows(full, sample):
        full = lax.dynamic_update_slice(full, sample, (tp, 0))
        return lax.dynamic_update_slice(full, jnp.zeros((R - tp - ns, full.shape[1]), full.dtype), (tp + ns, 0))

    x1 = matmul([with_sample_rows(o_gdn_p, o_gdn_s), with_sample_rows(o_fox_p, o_fox_s)], w_out0, layer=0,
                res=x0, name="out_proj")
    u0 = matmul(rmsnorm(x1, ln_mlp[0], BF16), w_up, layer=0, act="relu2", out_dtype=BF16, name="mlp_up0")
    x2 = matmul(u0, w_down, layer=0, res=x1, name="mlp_down0")

    h1 = rmsnorm(x2, ln_mix[1], F32)
    h1_s = h1[tp:tp + ns].reshape(nb, nq, D)
    shift_rows = jnp.concatenate([jnp.repeat(state_rwkv_shift[0].astype(F32), nq, axis=0),
                                  jnp.zeros((R - tp - ns, D), F32)], axis=0)
    xr, xw, xk, xv, xa, xg = rwkv_mix(h1, shift_rows, rwkv_mu[0], tp, ns, nq)

    def pad_cols(w):
        return jnp.pad(w, ((0, 0), (0, LANES - w.shape[1]))).astype(BF16)

    def pad_rows(w):
        return jnp.pad(w, ((0, LANES - w.shape[0]), (0, 0))).astype(BF16)

    r_ = matmul(xr, rwkv_w_r, layer=0, name="rwkv_r")
    k_ = matmul(xk, rwkv_w_k, layer=0, name="rwkv_k")
    v_ = matmul(xv, rwkv_w_v, layer=0, name="rwkv_v")
    wl = matmul(matmul(xw, pad_cols(rwkv_w1[0]), act="tanh", out_dtype=BF16, name="rwkv_w1"), pad_rows(rwkv_w2[0]), name="rwkv_w2")
    al = matmul(matmul(xa, pad_cols(rwkv_a1[0]), out_dtype=BF16, name="rwkv_a1"), pad_rows(rwkv_a2[0]), name="rwkv_a2")
    g_ = matmul(matmul(xg, rwkv_g1, layer=0, act="sigmoid", out_dtype=BF16, name="rwkv_g1"), rwkv_g2, layer=0, name="rwkv_g2")

    prm = jnp.stack([rwkv_w0[0], rwkv_a0[0], rwkv_k_k[0], rwkv_k_a[0], rwkv_r_k[0].reshape(D), rwkv_ln_w[0], rwkv_ln_b[0],
                     jnp.zeros((D,), F32)], axis=0).astype(F32)
    RC = 64
    o_rw_p, s_rw_p = rwkv(r_, k_, v_, wl, al, g_, prm, jnp.zeros((1, 2 * RWKV_PAIRS, RWKV_HEAD, RWKV_HEAD), F32),
                          B=1, NC=tp // RC, C=RC, first_valid=pad, name="rwkv_prompt", out_rows=R)
    xs = jnp.stack([t[tp:tp + ns] for t in (r_, k_, v_, wl, al, g_)])
    xs = jnp.transpose(xs.reshape(6, nb, nq, D), (0, 2, 3, 1))
    s0_l = jnp.transpose(_only(state_rwkv_S), (1, 2, 3, 0))
    o_l, s_l = rwkv_lanes(xs, prm.reshape(SUBLANES, D, 1), s0_l)
    o_rw_s = jnp.transpose(o_l, (2, 0, 1)).reshape(ns, D).astype(BF16)
    s_rw_s = jnp.transpose(s_l, (3, 0, 1, 2))
    x3 = matmul(with_sample_rows(o_rw_p, o_rw_s), rwkv_w_o, layer=0, res=x2, name="rwkv_o")
    u1 = matmul(rmsnorm(x3, ln_mlp[1], BF16), w_up, layer=1, act="relu2", out_dtype=BF16, name="mlp_up1")
    x4 = matmul(u1, w_down, layer=1, res=x3, name="mlp_down1")
    y = rmsnorm(x4, ln_final, F32)

    r0 = pad
    y_prompt = y[r0 + N_META:tp].reshape(1, seq, D)
    y_sample = y[tp:tp + ns].reshape(nb, nq, D)

    def kv_rows(col):
        blk = p[:, col:col + FOX_W]
        return (blk[r0:tp].reshape(1, 1, tprompt, FOX_HEADS, FOX_DH),
                blk[tp:tp + ns].reshape(1, nb, nq, FOX_HEADS, FOX_DH))

    fk_p, fk_s = kv_rows(COL_FK)
    fv_p, fv_s = kv_rows(COL_FV)
    lf8 = lf[:, LANE_F:LANE_F + FOX_HEADS]
    lf_p = lf8[r0:tp].reshape(1, 1, tprompt, FOX_HEADS)
    lf_sm = lf8[tp:tp + ns].reshape(1, nb, nq, FOX_HEADS)
    cb_p = p[tp - 3:tp, :GDN_QKV].reshape(1, 1, 3, GDN_QKV)
    cb_s = p[tp:tp + ns, :GDN_QKV].reshape(nb, nq, GDN_QKV)[:, nq - 3:][None]
    gs_p = s_gdn_p[None]
    gs_s = s_gdn_s[None]
    sh_p = h1[tp - 1].reshape(1, 1, D)
    sh_s = h1_s[:, nq - 1][None]
    rs_p = s_rw_p[None]
    rs_s = s_rw_s[None]
    return (y_prompt, y_sample, fk_p, fk_s, fv_p, fv_s, lf_p, lf_sm, cb_p, cb_s, gs_p, gs_s, sh_p, sh_s, rs_p, rs_s)
```

```python
import functools
import math

import jax
import jax.numpy as jnp
from jax import lax
from jax.experimental import pallas as pl
from jax.experimental.pallas import tpu as pltpu

F32 = jnp.float32
BF16 = jnp.bfloat16

D_MODEL = 2048
N_META = 16
GDN_HEADS = 8
GDN_D = 128
GDN_QK = GDN_HEADS * GDN_D
GDN_QKV = 3 * GDN_QK
FOX_HEADS = 8
FOX_DH = 128
FOX_W = FOX_HEADS * FOX_DH
RWKV_HEAD = 64
RWKV_PAIRS = D_MODEL // (2 * RWKV_HEAD)
NORM_EPS = 1e-6
L2_EPS = 1e-6
GN_EPS = 64e-5
NEG_INF = -1e30

LANES = 128
SUBLANES = 8
ROW_TILE = 640
MM_ROW_TILE = 1792
MM_WIDE_MIN_COLS = 8192
LOG2E = 1.4426950408889634
VMEM_LIMIT = 48 * 1024 * 1024
FOX_HEADS_PER_STEP = 4
FOX_GROUP = 16
RWKV_PREP_PAIRS = 16

COL_Z = GDN_QKV
COL_FQ = COL_Z + GDN_QK
COL_FK = COL_FQ + FOX_W
COL_FV = COL_FK + FOX_W
COL_FZ = COL_FV + FOX_W
LANE_A = 0
LANE_B = 8
LANE_F = 16
PRM_W0, PRM_A0, PRM_KK, PRM_KA, PRM_RK, PRM_LNW, PRM_LNB = range(7)


def _pick(n, cands):
    for c in cands:
        if n % c == 0:
            return c
    raise ValueError(f"no tile for {n}")


def _cparams(sem):
    return pltpu.CompilerParams(dimension_semantics=sem, vmem_limit_bytes=VMEM_LIMIT)


def _dot(a, b):
    return jnp.dot(a, b, preferred_element_type=F32)


def _dot_nt(a, b):
    return lax.dot_general(a, b, (((1,), (1,)), ((), ())), preferred_element_type=F32)


def _dot_tn(a, b):
    return lax.dot_general(a, b, (((0,), (0,)), ((), ())), preferred_element_type=F32)


def _bf(x):
    return x.astype(BF16)


def _softplus(x):
    return jnp.maximum(x, 0.0) + jnp.log(1.0 + jnp.exp(-jnp.abs(x)))


def _sigmoid(x):
    return 1.0 / (1.0 + jnp.exp(-x))


def _iota2(shape, dim):
    return lax.broadcasted_iota(jnp.int32, shape, dim)


def _div(x, n):
    assert n & (n - 1) == 0
    return x >> (n.bit_length() - 1)


def _split2(x):
    hi = x.astype(BF16)
    return hi, (x - hi.astype(F32)).astype(BF16)


def _split3(x):
    p1 = x.astype(BF16)
    rem = x - p1.astype(F32)
    p2 = rem.astype(BF16)
    return p1, p2, (rem - p2.astype(F32)).astype(BF16)


def _dot3(a, b):
    m = a.shape[0]
    ah, al = _split2(a)
    bh, bl = _split2(b)
    top = _dot(jnp.concatenate([ah, al], axis=0), bh)
    return (top[:m] + top[m:]) + _dot(ah, bl)


def _dot_lhs01(a01, b):
    a = a01.astype(BF16)
    b1, b2, b3 = _split3(b)
    return _dot(a, b1) + (_dot(a, b2) + _dot(a, b3))


def _dot_rhs01(a, b01):
    b = b01.astype(BF16)
    a1, a2, a3 = _split3(a)
    m = a.shape[0]
    out = _dot(jnp.concatenate([a1, a2, a3], axis=0), b)
    return out[:m] + (out[m:2 * m] + out[2 * m:])


def _tri_inv_steps(n_mat, n):
    size = n_mat.shape[0]
    eye = (_iota2((size, size), 0) == _iota2((size, size), 1)).astype(F32)
    t = eye + n_mat
    if n <= 2:
        return t
    p = _dot(_bf(n_mat), _bf(n_mat))
    yield
    m = 2
    while m < n:
        p_bf = _bf(p)
        if 2 * m < n:
            both = _dot(_bf(jnp.concatenate([p, t], axis=0)), p_bf)
            p = both[:size]
            t = t + both[size:]
        else:
            t = t + _dot(_bf(t), p_bf)
        m *= 2
        yield
    return t


def _run_interleaved(gens):
    results = [None] * len(gens)
    live = list(enumerate(gens))
    while live:
        nxt = []
        for idx, gen in live:
            try:
                next(gen)
                nxt.append((idx, gen))
            except StopIteration as stop:
                results[idx] = stop.value
        live = nxt
    return results


def _rmsnorm_kernel(x_ref, w_ref, o_ref):
    x = x_ref[...]
    ms = jnp.mean(x * x, axis=-1, keepdims=True)
    o_ref[...] = (x * lax.rsqrt(ms + NORM_EPS) * w_ref[...]).astype(o_ref.dtype)


def rmsnorm(x, w, out_dtype):
    m, d = x.shape
    tm = _pick(m, (320, 256, 128))
    return pl.pallas_call(
        _rmsnorm_kernel,
        grid=(m // tm,),
        in_specs=[pl.BlockSpec((tm, d), lambda i: (i, 0)), pl.BlockSpec((1, d), lambda i: (0, 0))],
        out_specs=pl.BlockSpec((tm, d), lambda i: (i, 0)),
        out_shape=jax.ShapeDtypeStruct((m, d), out_dtype),
        compiler_params=_cparams(("parallel",)),
        name="rmsnorm",
    )(x, w.reshape(1, d))


def _mm_kernel(nk, nx, act, has_res, *refs):
    x_refs, w_ref = refs[:nx], refs[nx]
    r_ref = refs[nx + 1] if has_res else None
    o_ref = refs[nx + 1 + has_res]
    scr = refs[nx + 2 + has_res:]

    def product():
        if nx == 1:
            return _dot(x_refs[0][...], _bf(w_ref[...]))
        acc, k0 = None, 0
        for x_ref in x_refs:
            kw = x_ref.shape[1]
            part = _dot(x_ref[...], _bf(w_ref[k0:k0 + kw, :]))
            acc = part if acc is None else acc + part
            k0 += kw
        return acc

    def finish(acc):
        if act == "relu2":
            acc = jnp.square(jnp.maximum(acc, 0.0))
        elif act == "tanh":
            acc = jnp.tanh(acc)
        elif act == "sigmoid":
            acc = _sigmoid(acc)
        if has_res:
            acc = r_ref[...] + acc
        o_ref[...] = acc.astype(o_ref.dtype)

    if nk == 1:
        finish(product())
    else:
        acc_ref = scr[0]
        k = pl.program_id(2)

        @pl.when(k == 0)
        def _():
            acc_ref[...] = jnp.zeros(acc_ref.shape, F32)

        acc_ref[...] += product()

        @pl.when(k == nk - 1)
        def _():
            finish(acc_ref[...])


def matmul(x, w, *, layer=None, act=None, res=None, out_dtype=F32, name="matmul"):
    xs = list(x) if isinstance(x, (list, tuple)) else [x]
    m = xs[0].shape[0]
    k = sum(xi.shape[1] for xi in xs)
    n = w.shape[-1]
    tm = _pick(m, (MM_ROW_TILE, ROW_TILE, 512, 256, 128))
    tk = k if k <= 2048 else 2048
    nk = k // tk
    wide_ok = nk == 1 and res is None and n >= MM_WIDE_MIN_COLS
    tn = _pick(n, ((1024,) if wide_ok else ()) + (512, 256, 128))
    assert len(xs) == 1 or nk == 1
    in_specs = [pl.BlockSpec((tm, tk if len(xs) == 1 else xi.shape[1]), lambda i, j, kk: (i, kk)) for xi in xs]
    if w.ndim == 3:
        in_specs.append(pl.BlockSpec((None, tk, tn), lambda i, j, kk: (layer, kk, j)))
    else:
        in_specs.append(pl.BlockSpec((tk, tn), lambda i, j, kk: (kk, j)))
    args = xs + [w]
    if res is not None:
        in_specs.append(pl.BlockSpec((tm, tn), lambda i, j, kk: (i, j)))
        args.append(res)
    return pl.pallas_call(
        functools.partial(_mm_kernel, nk, len(xs), act, res is not None),
        grid=(m // tm, n // tn, nk),
        in_specs=in_specs,
        out_specs=pl.BlockSpec((tm, tn), lambda i, j, kk: (i, j)),
        out_shape=jax.ShapeDtypeStruct((m, n), out_dtype),
        scratch_shapes=[pltpu.VMEM((tm, tn), F32)] if nk > 1 else [],
        compiler_params=_cparams(("parallel", "parallel", "arbitrary")),
        name=name,
    )(*args)


def _gdn_kernel(C, NC, first_valid, has_conv, qkv_ref, z_ref, ps_ref, cw_ref, alog_ref, dt_ref, nw_ref, s0_ref,
                *rest):
    conv_ref = rest[0] if has_conv else None
    o_ref, sout_ref, S, ext = rest[1:] if has_conv else rest
    c = pl.program_id(1)

    @pl.when(c == 0)
    def _():
        S[...] = s0_ref[0]
        ext[0:SUBLANES, :] = jnp.zeros((SUBLANES, GDN_QKV), F32)

    x = qkv_ref[...]
    ext[SUBLANES:SUBLANES + C, :] = x
    if has_conv:
        assert NC == 1 and 3 <= first_valid < C
        ext[SUBLANES + first_valid - 3:SUBLANES + first_valid, :] = conv_ref[0, 0]
    cw = cw_ref[...]
    y = ((ext[5:5 + C, :] * cw[0:1] + ext[6:6 + C, :] * cw[1:2]) + ext[7:7 + C, :] * cw[2:3]) + ext[8:8 + C, :] * cw[3:4]
    ext[0:SUBLANES, :] = x[C - SUBLANES:C]
    y = y * _sigmoid(y)

    rows = c * C + _iota2((C, 1), 0)
    valid = rows >= first_valid
    ps = ps_ref[...]
    g_all = jnp.where(valid, -jnp.exp(alog_ref[...]) * _softplus(ps + dt_ref[...]), 0.0)
    beta_all = jnp.where(valid, _sigmoid(ps), 0.0)
    ri = _iota2((C, C), 0)
    ci = _iota2((C, C), 1)
    causal = ri >= ci
    strict = ri > ci
    gc = _dot_lhs01(causal.astype(F32), g_all)
    gct = _dot_rhs01(g_all.T, (ri <= ci).astype(F32))
    nw = nw_ref[...]

    def head_steps(h):
        sl = slice(h * GDN_D, (h + 1) * GDN_D)
        qh = y[:, sl]
        kh = y[:, GDN_QK + h * GDN_D:GDN_QK + (h + 1) * GDN_D]
        vh = jnp.where(valid, y[:, 2 * GDN_QK + h * GDN_D:2 * GDN_QK + (h + 1) * GDN_D], 0.0)
        qh = jnp.where(valid, qh * lax.rsqrt(jnp.sum(qh * qh, axis=-1, keepdims=True) + L2_EPS) * GDN_D ** -0.5, 0.0)
        kh = jnp.where(valid, kh * lax.rsqrt(jnp.sum(kh * kh, axis=-1, keepdims=True) + L2_EPS), 0.0)
        bcol = beta_all[:, LANE_B + h:LANE_B + h + 1]
        gcol = gc[:, LANE_A + h:LANE_A + h + 1]
        grow = gct[LANE_A + h:LANE_A + h + 1, :]
        glast = gc[C - 1:C, LANE_A + h:LANE_A + h + 1]
        diff = gcol - grow
        decay = jnp.where(causal, jnp.exp(jnp.where(causal, diff, 0.0)), 0.0)
        kb = kh * bcol
        a_mat = jnp.where(strict, _dot_nt(_bf(kb), _bf(kh)) * decay, 0.0)
        qk = _dot_nt(_bf(qh), _bf(kh)) * decay
        eg = jnp.exp(gcol)
        rhs = jnp.concatenate([vh * bcol, kb * eg], axis=1)
        q_dec = qh * eg
        k_dec = kh * jnp.exp(glast - gcol)
        yield
        t_mat = yield from _tri_inv_steps(-a_mat, C)
        sol = _dot3(t_mat, rhs)
        u = sol[:, :GDN_D]
        w = sol[:, GDN_D:]
        yield
        s_h = S[h]
        s_bf = _bf(s_h)
        v_new = u - _dot(_bf(w), s_bf)
        o_state = _dot(_bf(q_dec), s_bf)
        yield
        o = o_state + _dot(_bf(qk), _bf(v_new))
        S[h] = s_h * jnp.exp(glast) + _dot_tn(_bf(k_dec), _bf(v_new))
        yield
        on = o * lax.rsqrt(jnp.mean(o * o, axis=-1, keepdims=True) + NORM_EPS) * nw
        zh = z_ref[:, sl]
        o_ref[:, sl] = (on * (zh * _sigmoid(zh))).astype(o_ref.dtype)

    _run_interleaved([head_steps(h) for h in range(GDN_HEADS)])

    @pl.when(c == NC - 1)
    def _():
        sout_ref[0] = S[...]


def gdn(qkv_arr, z_arr, ps_arr, z_col, conv_w, alog_pad, dt_pad, norm_w, s0, conv0, *, B, NC, C, first_valid, name,
        out_rows=None):
    rows = B * NC * C
    has_conv = conv0 is not None
    extra_specs = [pl.BlockSpec((1, 1, 3, GDN_QKV), lambda b, c: (0, b, 0, 0))] if has_conv else []
    extra_args = [conv0] if has_conv else []
    return pl.pallas_call(
        functools.partial(_gdn_kernel, C, NC, first_valid, has_conv),
        grid=(B, NC),
        in_specs=[
            pl.BlockSpec((C, GDN_QKV), lambda b, c: (b * NC + c, 0)),
            pl.BlockSpec((C, GDN_QK), lambda b, c: (b * NC + c, z_col)),
            pl.BlockSpec((C, LANES), lambda b, c: (b * NC + c, 0)),
            pl.BlockSpec((4, GDN_QKV), lambda b, c: (0, 0)),
            pl.BlockSpec((1, LANES), lambda b, c: (0, 0)),
            pl.BlockSpec((1, LANES), lambda b, c: (0, 0)),
            pl.BlockSpec((1, GDN_D), lambda b, c: (0, 0)),
            pl.BlockSpec((1, GDN_HEADS, GDN_D, GDN_D), lambda b, c: (b, 0, 0, 0)),
        ] + extra_specs,
        out_specs=[
            pl.BlockSpec((C, GDN_QK), lambda b, c: (b * NC + c, 0)),
            pl.BlockSpec((1, GDN_HEADS, GDN_D, GDN_D), lambda b, c: (b, 0, 0, 0)),
        ],
        out_shape=[
            jax.ShapeDtypeStruct((out_rows or rows, GDN_QK), BF16),
            jax.ShapeDtypeStruct((B, GDN_HEADS, GDN_D, GDN_D), F32),
        ],
        scratch_shapes=[pltpu.VMEM((GDN_HEADS, GDN_D, GDN_D), F32), pltpu.VMEM((C + SUBLANES, GDN_QKV), F32)],
        compiler_params=_cparams(("parallel", "arbitrary")),
        name=name,
    )(qkv_arr, z_arr, ps_arr, conv_w, alog_pad, dt_pad, norm_w, s0, *extra_args)


def _fox_prep_kernel(tb, first_valid, ps_ref, bf_ref, lf_ref, cq_ref, ck_ref, carry):
    i = pl.program_id(0)

    @pl.when(i == 0)
    def _():
        carry[...] = jnp.zeros(carry.shape, F32)

    x = ps_ref[...] + bf_ref[...]
    rows = i * tb + _iota2((tb, 1), 0)
    lf = jnp.where(rows >= first_valid, jnp.minimum(x, 0.0) - jnp.log(1.0 + jnp.exp(-jnp.abs(x))), 0.0)
    tri = (_iota2((tb, tb), 0) >= _iota2((tb, tb), 1)).astype(F32)
    c = _dot_lhs01(tri, lf) + carry[0:1, :]
    carry[...] = jnp.broadcast_to(c[tb - 1:tb, :], carry.shape)
    lf_ref[...] = lf
    c2 = c * LOG2E
    for h in range(FOX_HEADS):
        cq_ref[h] = jnp.broadcast_to(c2[:, LANE_F + h:LANE_F + h + 1], (tb, LANES))
    ck_ref[...] = c2.T[LANE_F:LANE_F + FOX_HEADS, :]


def fox_prep(ps, bf_pad, first_valid):
    r = ps.shape[0]
    tb = _pick(r, (ROW_TILE, 512, 256, 128))
    return pl.pallas_call(
        functools.partial(_fox_prep_kernel, tb, first_valid),
        grid=(r // tb,),
        in_specs=[pl.BlockSpec((tb, LANES), lambda i: (i, 0)), pl.BlockSpec((1, LANES), lambda i: (0, 0))],
        out_specs=[
            pl.BlockSpec((tb, LANES), lambda i: (i, 0)),
            pl.BlockSpec((FOX_HEADS, tb, LANES), lambda i: (0, i, 0)),
            pl.BlockSpec((FOX_HEADS, tb), lambda i: (0, i)),
        ],
        out_shape=[
            jax.ShapeDtypeStruct((r, LANES), F32),
            jax.ShapeDtypeStruct((FOX_HEADS, r, LANES), F32),
            jax.ShapeDtypeStruct((FOX_HEADS, r), F32),
        ],
        scratch_shapes=[pltpu.VMEM((SUBLANES, LANES), F32)],
        compiler_params=_cparams(("arbitrary",)),
        name="fox_prep",
    )(ps, bf_pad)


def _fox_flash_kernel(tq, first_valid, qi_ref, ki_ref, qt_ref, k_ref, vt_ref, ck_ref, cq_ref, fz_ref, o_ref,
                      m_s, l_s, acc_s, t_s):
    step = pl.program_id(1)
    qi = qi_ref[step]
    ki = ki_ref[step]
    reps = tq // LANES
    heads = qt_ref.shape[0]

    @pl.when(ki == 0)
    def _():
        m_s[...] = jnp.full(m_s.shape, NEG_INF, F32)
        l_s[...] = jnp.zeros(l_s.shape, F32)
        acc_s[...] = jnp.zeros(acc_s.shape, F32)

    def head_steps(h, masked):
        cols = slice(h * FOX_DH, (h + 1) * FOX_DH)
        qt = _bf(qt_ref[h] * (FOX_DH ** -0.5 * LOG2E))
        cq = cq_ref[h]
        m_prev = m_s[h, 0:1, :]
        kb = LANES
        groups = kb // SUBLANES
        mx = None
        for b in range(tq // kb):
            rs = slice(b * kb, (b + 1) * kb)
            t = _dot(_bf(k_ref[rs, cols]), qt) - jnp.concatenate([ck_ref[h, rs, :]] * reps, axis=1)
            if masked:
                kpos = ki * tq + b * kb + _iota2((kb, tq), 0)
                qpos = qi * tq + _iota2((kb, tq), 1)
                t = jnp.where((kpos <= qpos) & (kpos >= first_valid), t, NEG_INF)
            t_s[h, rs, :] = t
            part = jnp.max(t.reshape(groups, SUBLANES, tq), axis=0)
            mx = part if mx is None else jnp.maximum(mx, part)
            yield
        m_new = jnp.maximum(m_prev, jnp.max(mx, axis=0, keepdims=True) + cq)
        shift = cq - m_new
        alpha = jnp.exp2(m_prev - m_new)
        lsum = None
        pv = None
        for b in range(tq // kb):
            rs = slice(b * kb, (b + 1) * kb)
            p = jnp.exp2(t_s[h, rs, :] + shift)
            part = jnp.sum(p.reshape(groups, SUBLANES, tq), axis=0)
            lsum = part if lsum is None else lsum + part
            d = _dot(_bf(vt_ref[h, :, rs]), _bf(p))
            pv = d if pv is None else pv + d
            yield
        l_new = alpha * l_s[h, 0:1, :] + jnp.sum(lsum, axis=0, keepdims=True)
        l_s[h] = jnp.broadcast_to(l_new, (SUBLANES, tq))
        acc_s[h] = alpha * acc_s[h] + pv
        m_s[h] = jnp.broadcast_to(m_new, (SUBLANES, tq))

    def accumulate(masked):
        _run_interleaved([head_steps(h, masked) for h in range(heads)])

    edge = (ki == qi) | (ki == 0)
    pl.when(edge)(lambda: accumulate(True))
    pl.when(jnp.logical_not(edge))(lambda: accumulate(False))

    @pl.when(ki == qi)
    def _():
        rows = qi * tq + _iota2((tq, 1), 0)
        for h in range(heads):
            cols = slice(h * FOX_DH, (h + 1) * FOX_DH)
            o = (acc_s[h] / l_s[h, 0:1, :]).T * _sigmoid(fz_ref[:, cols])
            o_ref[:, cols] = jnp.where(rows >= first_valid, o, 0.0).astype(o_ref.dtype)


def fox_prompt(p, c_rep, c_row, tp, first_valid, out_rows):
    tq = _pick(tp, (ROW_TILE, 512, 384, 256, 128))
    nq = tp // tq
    assert first_valid < tq
    pairs = [(qi, ki) for qi in range(nq) for ki in range(qi + 1)]
    qi_arr = jnp.array([a for a, _ in pairs], jnp.int32)
    ki_arr = jnp.array([b for _, b in pairs], jnp.int32)
    hp = FOX_HEADS_PER_STEP
    wid = hp * FOX_DH
    cb = lambda col, g: col // wid + g

    def heads_t(col):
        return jnp.transpose(p[:tp, col:col + FOX_W].reshape(tp, FOX_HEADS, FOX_DH), (1, 2, 0))

    return pl.pallas_call(
        functools.partial(_fox_flash_kernel, tq, first_valid),
        grid_spec=pltpu.PrefetchScalarGridSpec(
            num_scalar_prefetch=2,
            grid=(FOX_HEADS // hp, len(pairs)),
            in_specs=[
                pl.BlockSpec((hp, FOX_DH, tq), lambda g, t, qa, ka: (g, 0, qa[t])),
                pl.BlockSpec((tq, wid), lambda g, t, qa, ka: (ka[t], cb(COL_FK, g))),
                pl.BlockSpec((hp, FOX_DH, tq), lambda g, t, qa, ka: (g, 0, ka[t])),
                pl.BlockSpec((hp, tq, LANES), lambda g, t, qa, ka: (g, ka[t], 0)),
                pl.BlockSpec((hp, 1, tq), lambda g, t, qa, ka: (g, 0, qa[t])),
                pl.BlockSpec((tq, wid), lambda g, t, qa, ka: (qa[t], cb(COL_FZ, g))),
            ],
            out_specs=pl.BlockSpec((tq, wid), lambda g, t, qa, ka: (qa[t], g)),
            scratch_shapes=[pltpu.VMEM((hp, SUBLANES, tq), F32), pltpu.VMEM((hp, SUBLANES, tq), F32),
                            pltpu.VMEM((hp, FOX_DH, tq), F32), pltpu.VMEM((hp, tq, tq), F32)],
        ),
        out_shape=jax.ShapeDtypeStruct((out_rows, FOX_W), BF16),
        compiler_params=_cparams(("parallel", "arbitrary")),
        name="fox_prompt",
    )(qi_arr, ki_arr, heads_t(COL_FQ), p, heads_t(COL_FV), c_rep, c_row.reshape(FOX_HEADS, 1, -1), p)


def _fox_sample_kernel(G, ngroups, pt_ref, q_ref, *refs):
    k_refs, v_refs, lft_refs = refs[0:G], refs[G:2 * G], refs[2 * G:3 * G]
    lfn_ref, kn_ref, vn_ref, fz_ref, o_ref, cq_s, cn_s, carry_s, m_s, l_s, acc_s = refs[3 * G:]
    jg = pl.program_id(1)
    rows = q_ref.shape[1]
    page = lft_refs[0].shape[2]
    flat = page * FOX_HEADS
    nn = lfn_ref.shape[2]
    scale = FOX_DH ** -0.5

    @pl.when(jg == 0)
    def _():
        upper = (_iota2((nn, nn), 0) <= _iota2((nn, nn), 1)).astype(F32)
        cn = _dot_rhs01(lfn_ref[0], upper)
        cn_s[...] = cn
        own_q = _iota2((rows, nn), 1) == _div(_iota2((rows, nn), 0), FOX_HEADS)
        cq_s[...] = jnp.broadcast_to(jnp.sum(jnp.where(own_q, cn, 0.0), axis=-1, keepdims=True), cq_s.shape)
        carry_s[...] = jnp.zeros(carry_s.shape, F32)
        m_s[...] = jnp.full(m_s.shape, NEG_INF, F32)
        l_s[...] = jnp.zeros(l_s.shape, F32)
        acc_s[...] = jnp.zeros(acc_s.shape, F32)

    def update(ts, vals_bf):
        cq = cq_s[:, 0:1]
        m_prev = m_s[:, 0:1]
        t_max = functools.reduce(jnp.maximum, ts)
        m_new = jnp.maximum(m_prev, jnp.max(t_max, axis=-1, keepdims=True) + cq)
        shift = cq - m_new
        ps = [jnp.exp(t + shift) for t in ts]
        alpha = jnp.exp(m_prev - m_new)
        l_s[...] = alpha * l_s[...] + jnp.sum(functools.reduce(jnp.add, ps), axis=-1, keepdims=True)
        pv = functools.reduce(jnp.add, [_dot(_bf(p), v) for p, v in zip(ps, vals_bf)])
        acc_s[...] = alpha * acc_s[...] + pv
        m_s[...] = jnp.broadcast_to(m_new, m_s.shape)

    own = (_iota2((rows, flat), 0) & (FOX_HEADS - 1)) == (_iota2((rows, flat), 1) & (FOX_HEADS - 1))
    later = (_iota2((page, flat), 0) > _div(_iota2((page, flat), 1), FOX_HEADS)).astype(BF16)
    head_col = _iota2((FOX_HEADS, flat), 0) == (_iota2((FOX_HEADS, flat), 1) & (FOX_HEADS - 1))
    carry = carry_s[:, 0:1]
    q_bf = _bf(q_ref[0])
    lfts = [lft_refs[g][0] for g in range(G)]
    suffixes = _dot_rhs01(jnp.concatenate(lfts, axis=0), later)
    ts = []
    for g in range(G):
        suffix = suffixes[g * FOX_HEADS:(g + 1) * FOX_HEADS]
        d = jnp.sum(jnp.where(head_col, suffix + carry, 0.0), axis=0, keepdims=True)
        carry = carry + jnp.sum(lfts[g], axis=-1, keepdims=True)
        s = _dot_nt(q_bf, _bf(k_refs[g][0])) * scale
        ts.append(jnp.where(own, s + d, NEG_INF))
    update(ts, [_bf(v_refs[g][0]) for g in range(G)])
    carry_s[...] = jnp.broadcast_to(carry, carry_s.shape)

    @pl.when(jg == ngroups - 1)
    def _():
        nflat = kn_ref.shape[1]
        spread = (_iota2((nn, nflat), 0) == _div(_iota2((nn, nflat), 1), FOX_HEADS)).astype(F32)
        cn_cols = _dot_rhs01(cn_s[...], spread)
        ri = _iota2((rows, nflat), 0)
        ci = _iota2((rows, nflat), 1)
        ok = ((ri & (FOX_HEADS - 1)) == (ci & (FOX_HEADS - 1))) & (_div(ci, FOX_HEADS) <= _div(ri, FOX_HEADS))
        sn = _dot_nt(_bf(q_ref[0]), _bf(kn_ref[0])) * scale
        update([jnp.where(ok, sn - cn_cols, NEG_INF)], [_bf(vn_ref[0])])
        o_ref[0] = acc_s[...] / l_s[...] * _sigmoid(fz_ref[0])


def fox_sample(page_table_flat, q_rows, cache_k, cache_v, cache_lft, lfn, kn_flat, vn_flat, fz_rows, nb, npages):
    flat = cache_k.shape[1]
    page = cache_lft.shape[2]
    rows = q_rows.shape[1]
    nn = lfn.shape[2]
    G = _pick(npages, (FOX_GROUP, 2, 1))
    ngroups = npages // G

    def page_map(g):
        return lambda b, jg, pt: (pt[b * npages + (npages - 1 - (jg * G + g))], 0, 0)

    seq = lambda b, jg, pt: (b, 0, 0)
    row_blk = pl.BlockSpec((1, rows, FOX_DH), seq)
    new_blk = pl.BlockSpec((1, kn_flat.shape[1], FOX_DH), seq)
    in_specs = [row_blk]
    in_specs += [pl.BlockSpec((1, flat, FOX_DH), page_map(g)) for g in range(G)]
    in_specs += [pl.BlockSpec((1, flat, FOX_DH), page_map(g)) for g in range(G)]
    in_specs += [pl.BlockSpec((1, FOX_HEADS, page), page_map(g)) for g in range(G)]
    in_specs += [pl.BlockSpec((1, rows, nn), seq), new_blk, new_blk, row_blk]
    return pl.pallas_call(
        functools.partial(_fox_sample_kernel, G, ngroups),
        grid_spec=pltpu.PrefetchScalarGridSpec(
            num_scalar_prefetch=1,
            grid=(nb, ngroups),
            in_specs=in_specs,
            out_specs=row_blk,
            scratch_shapes=[
                pltpu.VMEM((rows, LANES), F32),
                pltpu.VMEM((rows, nn), F32),
                pltpu.VMEM((FOX_HEADS, LANES), F32),
                pltpu.VMEM((rows, LANES), F32),
                pltpu.VMEM((rows, LANES), F32),
                pltpu.VMEM((rows, FOX_DH), F32),
            ],
        ),
        out_shape=jax.ShapeDtypeStruct((nb, rows, FOX_DH), F32),
        compiler_params=_cparams(("parallel", "arbitrary")),
        name="fox_sample",
    )(page_table_flat, q_rows, *([cache_k] * G), *([cache_v] * G), *([cache_lft] * G), lfn, kn_flat, vn_flat, fz_rows)


def _rwkv_mix_kernel(tm, tp, ns, nq, h_ref, hb_ref, st_ref, mu_ref, *o_refs):
    i = pl.program_id(0)
    h = h_ref[...]
    above = jnp.concatenate([hb_ref[SUBLANES - 1:SUBLANES, :], h[:tm - 1]], axis=0)
    row = i * tm + _iota2((tm, 1), 0)
    seq_start = (row >= tp) & (row < tp + ns) & (((row - tp) & (nq - 1)) == 0)
    prev = jnp.where(seq_start, st_ref[...], jnp.where(row == 0, 0.0, above))
    xx = prev - h
    mu = mu_ref[...]
    for j, o_ref in enumerate(o_refs):
        o_ref[...] = (h + xx * mu[j:j + 1]).astype(o_ref.dtype)


def rwkv_mix(h, shift_rows, mu, tp, ns, nq):
    m, d = h.shape
    assert nq & (nq - 1) == 0
    tm = _pick(math.gcd(m, tp), (320, 256, 128))
    first = tp // tm
    spec = pl.BlockSpec((tm, d), lambda i: (i, 0))
    return pl.pallas_call(
        functools.partial(_rwkv_mix_kernel, tm, tp, ns, nq),
        grid=(m // tm,),
        in_specs=[spec,
                  pl.BlockSpec((SUBLANES, d), lambda i: (jnp.maximum(i * (tm // SUBLANES) - 1, 0), 0)),
                  pl.BlockSpec((tm, d), lambda i: (jnp.maximum(i - first, 0), 0)),
                  pl.BlockSpec((6, d), lambda i: (0, 0))],
        out_specs=[spec] * 6,
        out_shape=[jax.ShapeDtypeStruct((m, d), BF16)] * 6,
        compiler_params=_cparams(("parallel",)),
        name="rwkv_mix",
    )(h, h, shift_rows, mu)


def _head_ones():
    return (_div(_iota2((LANES, LANES), 0), RWKV_HEAD) == _div(_iota2((LANES, LANES), 1), RWKV_HEAD)).astype(F32)


def _rwkv_chunk_terms(C, valid, r, k, v, wl, al, prm):
    HD = RWKV_HEAD
    m0 = _iota2((1, LANES), 1) < HD
    bones = _head_ones()
    w0, a0, k_k, k_a, r_k = (prm[i:i + 1] for i in (PRM_W0, PRM_A0, PRM_KK, PRM_KA, PRM_RK))
    wlog = -_softplus(-(w0 + wl)) - 0.5
    lw = jnp.where(valid, -jnp.exp(wlog), 0.0)
    a = _sigmoid(a0 + al)
    kkr = k * k_k
    kk = kkr * lax.rsqrt(_dot_rhs01(kkr * kkr, bones) + L2_EPS)
    k2 = k * (1.0 + (a - 1.0) * k_a)
    bonus = _dot_rhs01(r * k2 * r_k, bones) * v
    rm = jnp.where(valid, r, 0.0)
    k2 = jnp.where(valid, k2, 0.0)
    vm = jnp.where(valid, v, 0.0)
    av = jnp.where(valid, -kk, 0.0)
    bv = jnp.where(valid, kk * a, 0.0)

    ri = _iota2((C, C), 0)
    ci = _iota2((C, C), 1)
    yield
    cum = _dot_lhs01((ri >= ci).astype(F32), lw)
    cum_last = cum[C - 1:C, :]
    inv = jnp.exp(-cum)
    rt = rm * jnp.exp(cum)
    at = av * jnp.exp(cum - lw)
    bt = bv * inv
    kt = k2 * inv
    to_end = jnp.exp(cum_last - cum)
    b_end = bv * to_end
    k_end = k2 * to_end

    def split(x):
        return jnp.concatenate([jnp.where(m0, x, 0.0), jnp.where(m0, 0.0, x)], axis=0)

    def halves(x):
        return x[0:C] + x[C:2 * C]

    yield
    at_s = split(at)
    gram = _dot_nt(_bf(jnp.concatenate([at_s, split(rt)], axis=0)), _bf(jnp.concatenate([bt, kt], axis=0)))
    r2 = _iota2((2 * C, 2 * C), 0)
    c2 = _iota2((2 * C, 2 * C), 1)
    same = _div(r2, C) == _div(c2, C)
    bd_strict = same & (r2 > c2)
    bd_incl = same & (r2 >= c2)

    def bd(block, mask):
        return jnp.where(mask, jnp.concatenate([block, block], axis=1), 0.0)

    a_ab = bd(gram[0:2 * C, 0:C], bd_strict)
    a_ak = bd(gram[0:2 * C, C:2 * C], bd_strict)
    r_b = bd(gram[2 * C:4 * C, 0:C], bd_incl)
    r_k2 = bd(gram[2 * C:4 * C, C:2 * C], bd_incl)
    vs = split(vm)
    akv = halves(_dot(_bf(a_ak), _bf(vs)))
    yield
    t_bd = yield from _tri_inv_steps(a_ab, C)
    tw = _dot(_bf(t_bd), _bf(jnp.concatenate([at_s, split(akv)], axis=1)))
    wt = halves(tw[:, :LANES])
    ut = halves(tw[:, LANES:])
    yield
    rp = rt + halves(_dot(_bf(r_b), _bf(split(wt))))
    y0 = halves(_dot(_bf(jnp.concatenate([r_b, r_k2], axis=1)), _bf(jnp.concatenate([split(ut), vs], axis=0))))
    eye = (_iota2((LANES, LANES), 0) == _iota2((LANES, LANES), 1)).astype(F32)
    m_mat = eye * jnp.exp(cum_last) + bones * _dot_tn(_bf(b_end), _bf(wt))
    n_mat = bones * _dot_tn(_bf(jnp.concatenate([b_end, k_end], axis=0)), _bf(jnp.concatenate([ut, vm], axis=0)))
    return rp, y0, bonus, m_mat, n_mat


def _rwkv_prep_kernel(C, chunk_of_step, first_valid, PG, r_ref, k_ref, v_ref, wl_ref, al_ref, prm_ref,
                      rp_ref, y0_ref, bo_ref, m_ref, n_ref):
    c = pl.program_id(0) if chunk_of_step else 0
    valid = (c * C + _iota2((C, 1), 0)) >= first_valid
    sls = [slice(i * LANES, (i + 1) * LANES) for i in range(PG)]
    terms = _run_interleaved([
        _rwkv_chunk_terms(C, valid, r_ref[:, sl], k_ref[:, sl], v_ref[:, sl], wl_ref[:, sl], al_ref[:, sl], prm_ref[:, sl])
        for sl in sls])
    for i, sl in enumerate(sls):
        rp, y0, bonus, m_mat, n_mat = terms[i]
        rp_ref[:, sl] = rp
        y0_ref[:, sl] = y0
        bo_ref[:, sl] = bonus
        m_ref[0, i] = m_mat
        n_ref[0, i] = n_mat


def _rwkv_scan_kernel(NC, rp_ref, y0_ref, bo_ref, g_ref, m_ref, n_ref, prm_ref, s0_ref, o_ref, sout_ref, H):
    c = pl.program_id(1)
    HD = RWKV_HEAD
    bones = _head_ones()

    @pl.when(c == 0)
    def _():
        zero = jnp.zeros((HD, HD), F32)
        for i in range(RWKV_PAIRS):
            top = jnp.concatenate([s0_ref[0, 2 * i], zero], axis=1)
            bot = jnp.concatenate([zero, s0_ref[0, 2 * i + 1]], axis=1)
            H[i] = jnp.concatenate([top, bot], axis=0).T

    def pair_steps(i):
        sl = slice(i * LANES, (i + 1) * LANES)
        h_bd = H[i]
        y = _dot(_bf(rp_ref[:, sl]), _bf(h_bd)) + y0_ref[:, sl]
        H[i] = _dot3(m_ref[0, i], h_bd) + n_ref[0, i]
        yield
        mean = _dot_rhs01(y, bones) * (1.0 / HD)
        dlt = y - mean
        yield
        var = _dot_rhs01(dlt * dlt, bones) * (1.0 / HD)
        yn = dlt * lax.rsqrt(var + GN_EPS) * prm_ref[PRM_LNW:PRM_LNW + 1, sl] + prm_ref[PRM_LNB:PRM_LNB + 1, sl]
        o_ref[:, sl] = ((yn + bo_ref[:, sl]) * g_ref[:, sl]).astype(o_ref.dtype)

    _run_interleaved([pair_steps(i) for i in range(RWKV_PAIRS)])

    @pl.when(c == NC - 1)
    def _():
        for i in range(RWKV_PAIRS):
            ht = H[i].T
            sout_ref[0, 2 * i] = ht[0:HD, 0:HD]
            sout_ref[0, 2 * i + 1] = ht[HD:2 * HD, HD:2 * HD]


def rwkv(r, k, v, wl, al, g, prm, s0, *, B, NC, C, first_valid, name, out_rows=None):
    assert B == 1 or NC == 1
    nblk = B * NC
    rows = nblk * C
    PG = RWKV_PREP_PAIRS
    wide = pl.BlockSpec((C, PG * LANES), lambda blk, pg: (blk, pg))
    mat = pl.BlockSpec((1, PG, LANES, LANES), lambda blk, pg: (blk, pg, 0, 0))
    mat_shape = jax.ShapeDtypeStruct((nblk, RWKV_PAIRS, LANES, LANES), F32)
    row_shape = jax.ShapeDtypeStruct((rows, D_MODEL), F32)
    rp, y0, bonus, m_all, n_all = pl.pallas_call(
        functools.partial(_rwkv_prep_kernel, C, B == 1, first_valid, PG),
        grid=(nblk, RWKV_PAIRS // PG),
        in_specs=[wide] * 5 + [pl.BlockSpec((SUBLANES, PG * LANES), lambda blk, pg: (0, pg))],
        out_specs=[wide, wide, wide, mat, mat],
        out_shape=[row_shape, row_shape, row_shape, mat_shape, mat_shape],
        compiler_params=_cparams(("parallel", "parallel")),
        name=name + "_terms",
    )(r, k, v, wl, al, prm)

    full = pl.BlockSpec((C, D_MODEL), lambda b, c: (b * NC + c, 0))
    mats = pl.BlockSpec((1, RWKV_PAIRS, LANES, LANES), lambda b, c: (b * NC + c, 0, 0, 0))
    st_spec = pl.BlockSpec((1, 2 * RWKV_PAIRS, RWKV_HEAD, RWKV_HEAD), lambda b, c: (b, 0, 0, 0))
    return pl.pallas_call(
        functools.partial(_rwkv_scan_kernel, NC),
        grid=(B, NC),
        in_specs=[full] * 4 + [mats, mats, pl.BlockSpec((SUBLANES, D_MODEL), lambda b, c: (0, 0)), st_spec],
        out_specs=[full, st_spec],
        out_shape=[
            jax.ShapeDtypeStruct((out_rows or rows, D_MODEL), BF16),
            jax.ShapeDtypeStruct((B, 2 * RWKV_PAIRS, RWKV_HEAD, RWKV_HEAD), F32),
        ],
        scratch_shapes=[pltpu.VMEM((RWKV_PAIRS, LANES, LANES), F32)],
        compiler_params=_cparams(("parallel", "arbitrary")),
        name=name + "_scan",
    )(rp, y0, bonus, g, m_all, n_all, prm, s0)


def _rwkv_lanes_kernel(nq, unroll, x_ref, prm_ref, s0_ref, o_ref, sout_ref, y_s):
    HD = RWKV_HEAD
    w0, a0, k_k, k_a, r_k, ln_w, ln_b = (prm_ref[i] for i in (PRM_W0, PRM_A0, PRM_KK, PRM_KA, PRM_RK,
                                                               PRM_LNW, PRM_LNB))
    sout_ref[...] = s0_ref[...]
    for t in range(nq):
        r, k, v, wl, al, g = (x_ref[i, t] for i in range(6))
        wlog = -_softplus(-(w0 + wl)) - 0.5
        w = jnp.exp(-jnp.exp(wlog))
        a = _sigmoid(a0 + al)
        kkr = k * k_k
        kk = kkr * lax.rsqrt(jnp.sum(kkr * kkr, axis=0, keepdims=True) + L2_EPS)
        k2 = k * (1.0 + (a - 1.0) * k_a)
        av = -kk
        bv = kk * a

        def value_row(i, carry):
            s = sout_ref[0, i]
            sa = jnp.sum(s * av, axis=0, keepdims=True)
            s = s * w + sa * bv + x_ref[2, t, pl.ds(i, 1), :] * k2
            sout_ref[0, i] = s
            y_s[pl.ds(i, 1), :] = jnp.sum(s * r, axis=0, keepdims=True)
            return carry

        lax.fori_loop(0, HD, value_row, 0, unroll=unroll)
        y = y_s[...]
        mean = jnp.mean(y, axis=0, keepdims=True)
        dlt = y - mean
        var = jnp.mean(dlt * dlt, axis=0, keepdims=True)
        yn = dlt * lax.rsqrt(var + GN_EPS) * ln_w + ln_b
        bonus = jnp.sum(r * k2 * r_k, axis=0, keepdims=True) * v
        o_ref[t] = ((yn + bonus) * g).astype(o_ref.dtype)


def rwkv_lanes(xs, prm_col, s0):
    _, nq, d, nb = xs.shape
    heads = d // RWKV_HEAD
    st = pl.BlockSpec((1, RWKV_HEAD, RWKV_HEAD, nb), lambda h: (h, 0, 0, 0))
    return pl.pallas_call(
        functools.partial(_rwkv_lanes_kernel, nq, SUBLANES),
        grid=(heads,),
        in_specs=[pl.BlockSpec((6, nq, RWKV_HEAD, nb), lambda h: (0, 0, h, 0)),
                  pl.BlockSpec((SUBLANES, RWKV_HEAD, 1), lambda h: (0, h, 0)), st],
        out_specs=[pl.BlockSpec((nq, RWKV_HEAD, nb), lambda h: (0, h, 0)), st],
        out_shape=[jax.ShapeDtypeStruct((nq, d, nb), F32), jax.ShapeDtypeStruct(s0.shape, F32)],
        scratch_shapes=[pltpu.VMEM((RWKV_HEAD, nb), F32)],
        compiler_params=_cparams(("parallel",)),
        name="rwkv_sample",
    )(xs, prm_col, s0)


def _pad_lanes(vec, offset):
    out = jnp.zeros((1, LANES), F32)
    return lax.dynamic_update_slice(out, vec.reshape(1, -1).astype(F32), (0, offset))


def _sample_rows(arr, row0, nb, nq, front):
    cols = arr.shape[1]
    s = arr[row0:row0 + nb * nq].reshape(nb, nq, cols)
    s = jnp.pad(s, ((0, 0), (front, 0), (0, 0)))
    return s.reshape(nb * (front + nq), cols)


def _only(x):
    assert x.shape[0] == 1
    return x.reshape(x.shape[1:])


def kernel(x_prompt, x_sample, cache_fox_k, cache_fox_v, cache_fox_logf, state_gdn_conv, state_gdn_S,
           state_rwkv_shift, state_rwkv_S, page_table, meta_tokens, ln_mix, ln_mlp, ln_final,
           w_in0, gdn_conv_w, gdn_A_log, gdn_dt_bias, gdn_norm_w, fox_b_f, w_out0,
           rwkv_mu, rwkv_w0, rwkv_w1, rwkv_w2, rwkv_a0, rwkv_a1, rwkv_a2, rwkv_g1, rwkv_g2,
           rwkv_k_k, rwkv_k_a, rwkv_r_k, rwkv_w_r, rwkv_w_k, rwkv_w_v, rwkv_w_o, rwkv_ln_w, rwkv_ln_b,
           w_up, w_down):
    D = D_MODEL
    assert x_prompt.shape[0] == 1 and x_prompt.shape[2] == D
    seq = x_prompt.shape[1]
    nb, nq = x_sample.shape[0], x_sample.shape[1]
    npages = page_table.shape[1]
    tprompt = N_META + seq
    pad = (-tprompt) % LANES
    tp = tprompt + pad
    ns = nb * nq
    R = -(-(tp + ns) // ROW_TILE) * ROW_TILE
    CS = SUBLANES
    front = CS - nq
    assert 3 <= front

    x0 = jnp.concatenate([jnp.zeros((pad, D), F32), meta_tokens.astype(F32), x_prompt[0],
                          x_sample.reshape(ns, D), jnp.zeros((R - tp - ns, D), F32)], axis=0)

    w_in = w_in0[0]
    o_z = GDN_QKV
    o_a = o_z + GDN_QK
    o_b = o_a + GDN_HEADS
    o_fq = o_b + GDN_HEADS
    o_fk = o_fq + FOX_W
    o_fv = o_fk + FOX_W
    o_ff = o_fv + FOX_W
    o_fz = o_ff + FOX_HEADS
    w_big = jnp.concatenate([w_in[:, :o_a], w_in[:, o_fq:o_ff], w_in[:, o_fz:]], axis=1).astype(BF16)
    w_small = jnp.concatenate([w_in[:, o_a:o_fq], w_in[:, o_ff:o_fz],
                               jnp.zeros((D, LANES - 3 * GDN_HEADS), F32)], axis=1).astype(BF16)

    h0 = rmsnorm(x0, ln_mix[0], BF16)
    p = matmul(h0, w_big, name="in_proj")
    ps = matmul(h0, w_small, name="in_proj_small")

    alog_pad = _pad_lanes(gdn_A_log[0], LANE_A)
    dt_pad = _pad_lanes(gdn_dt_bias[0], LANE_A)
    bf_pad = _pad_lanes(fox_b_f[0], LANE_F)
    conv_w = gdn_conv_w[0]
    norm_w = gdn_norm_w[0].reshape(1, GDN_D)

    GC = 64
    o_gdn_p, s_gdn_p = gdn(p, p, ps, COL_Z // GDN_QK, conv_w, alog_pad, dt_pad, norm_w,
                           jnp.zeros((1, GDN_HEADS, GDN_D, GDN_D), F32), None,
                           B=1, NC=tp // GC, C=GC, first_valid=pad, name="gdn_prompt", out_rows=R)
    qkv_ext = _sample_rows(p[:, :GDN_QKV], tp, nb, nq, front)
    z_ext = _sample_rows(p[:, COL_Z:COL_Z + GDN_QK], tp, nb, nq, front)
    ps_ext = _sample_rows(ps, tp, nb, nq, front)
    o_gdn_s, s_gdn_s = gdn(qkv_ext, z_ext, ps_ext, 0, conv_w, alog_pad, dt_pad, norm_w,
                           _only(state_gdn_S), state_gdn_conv, B=nb, NC=1, C=CS, first_valid=front, name="gdn_sample")
    o_gdn_s = o_gdn_s.reshape(nb, CS, GDN_QK)[:, front:].reshape(ns, GDN_QK)

    lf, cq, ck = fox_prep(ps, bf_pad, pad)
    o_fox_p = fox_prompt(p, cq, ck, tp, pad, R)
    pt_flat = page_table.reshape(-1).astype(jnp.int32)
    lf_s = lf[tp:tp + ns, LANE_F:LANE_F + FOX_HEADS].reshape(nb, nq, FOX_HEADS)
    lfn = jnp.tile(jnp.swapaxes(lf_s, 1, 2), (1, nq, 1))
    lfn = jnp.pad(lfn, ((0, 0), (0, 0), (0, SUBLANES - nq)))
    pool = cache_fox_k.shape[1]
    page = cache_fox_k.shape[2]
    cache_k = _only(cache_fox_k).reshape(pool, page * FOX_HEADS, FOX_DH)
    cache_v = _only(cache_fox_v).reshape(pool, page * FOX_HEADS, FOX_DH)
    cache_lft = jnp.swapaxes(_only(cache_fox_logf), 1, 2)
    def sample_heads(col):
        return p[tp:tp + ns, col:col + FOX_W].reshape(nb, nq * FOX_HEADS, FOX_DH)

    o_fox_s = fox_sample(pt_flat, sample_heads(COL_FQ), cache_k, cache_v, cache_lft, lfn,
                         sample_heads(COL_FK), sample_heads(COL_FV), sample_heads(COL_FZ), nb, npages)
    o_fox_s = o_fox_s.reshape(ns, FOX_W).astype(BF16)

    def with_sample_rows(full, sample):
        full = lax.dynamic_update_slice(full, sample, (tp, 0))
        return lax.dynamic_update_slice(full, jnp.zeros((R - tp - ns, full.shape[1]), full.dtype), (tp + ns, 0))

    x1 = matmul([with_sample_rows(o_gdn_p, o_gdn_s), with_sample_rows(o_fox_p, o_fox_s)], w_out0, layer=0,
                res=x0, name="out_proj")
    u0 = matmul(rmsnorm(x1, ln_mlp[0], BF16), w_up, layer=0, act="relu2", out_dtype=BF16, name="mlp_up0")
    x2 = matmul(u0, w_down, layer=0, res=x1, name="mlp_down0")

    h1 = rmsnorm(x2, ln_mix[1], F32)
    h1_s = h1[tp:tp + ns].reshape(nb, nq, D)
    shift_rows = jnp.concatenate([jnp.repeat(state_rwkv_shift[0].astype(F32), nq, axis=0),
                                  jnp.zeros((R - tp - ns, D), F32)], axis=0)
    xr, xw, xk, xv, xa, xg = rwkv_mix(h1, shift_rows, rwkv_mu[0], tp, ns, nq)

    def pad_cols(w):
        return jnp.pad(w, ((0, 0), (0, LANES - w.shape[1]))).astype(BF16)

    def pad_rows(w):
        return jnp.pad(w, ((0, LANES - w.shape[0]), (0, 0))).astype(BF16)

    r_ = matmul(xr, rwkv_w_r, layer=0, name="rwkv_r")
    k_ = matmul(xk, rwkv_w_k, layer=0, name="rwkv_k")
    v_ = matmul(xv, rwkv_w_v, layer=0, name="rwkv_v")
    wl = matmul(matmul(xw, pad_cols(rwkv_w1[0]), act="tanh", out_dtype=BF16, name="rwkv_w1"), pad_rows(rwkv_w2[0]), name="rwkv_w2")
    al = matmul(matmul(xa, pad_cols(rwkv_a1[0]), out_dtype=BF16, name="rwkv_a1"), pad_rows(rwkv_a2[0]), name="rwkv_a2")
    g_ = matmul(matmul(xg, rwkv_g1, layer=0, act="sigmoid", out_dtype=BF16, name="rwkv_g1"), rwkv_g2, layer=0, name="rwkv_g2")

    prm = jnp.stack([rwkv_w0[0], rwkv_a0[0], rwkv_k_k[0], rwkv_k_a[0], rwkv_r_k[0].reshape(D), rwkv_ln_w[0], rwkv_ln_b[0],
                     jnp.zeros((D,), F32)], axis=0).astype(F32)
    RC = 64
    o_rw_p, s_rw_p = rwkv(r_, k_, v_, wl, al, g_, prm, jnp.zeros((1, 2 * RWKV_PAIRS, RWKV_HEAD, RWKV_HEAD), F32),
                          B=1, NC=tp // RC, C=RC, first_valid=pad, name="rwkv_prompt", out_rows=R)
    xs = jnp.stack([t[tp:tp + ns] for t in (r_, k_, v_, wl, al, g_)])
    xs = jnp.transpose(xs.reshape(6, nb, nq, D), (0, 2, 3, 1))
    s0_l = jnp.transpose(_only(state_rwkv_S), (1, 2, 3, 0))
    o_l, s_l = rwkv_lanes(xs, prm.reshape(SUBLANES, D, 1), s0_l)
    o_rw_s = jnp.transpose(o_l, (2, 0, 1)).reshape(ns, D).astype(BF16)
    s_rw_s = jnp.transpose(s_l, (3, 0, 1, 2))
    x3 = matmul(with_sample_rows(o_rw_p, o_rw_s), rwkv_w_o, layer=0, res=x2, name="rwkv_o")
    u1 = matmul(rmsnorm(x3, ln_mlp[1], BF16), w_up, layer=1, act="relu2", out_dtype=BF16, name="mlp_up1")
    x4 = matmul(u1, w_down, layer=1, res=x3, name="mlp_down1")
    y = rmsnorm(x4, ln_final, F32)

    r0 = pad
    y_prompt = y[r0 + N_META:tp].reshape(1, seq, D)
    y_sample = y[tp:tp + ns].reshape(nb, nq, D)

    def kv_rows(col):
        blk = p[:, col:col + FOX_W]
        return (blk[r0:tp].reshape(1, 1, tprompt, FOX_HEADS, FOX_DH),
                blk[tp:tp + ns].reshape(1, nb, nq, FOX_HEADS, FOX_DH))

    fk_p, fk_s = kv_rows(COL_FK)
    fv_p, fv_s = kv_rows(COL_FV)
    lf8 = lf[:, LANE_F:LANE_F + FOX_HEADS]
    lf_p = lf8[r0:tp].reshape(1, 1, tprompt, FOX_HEADS)
    lf_sm = lf8[tp:tp + ns].reshape(1, nb, nq, FOX_HEADS)
    cb_p = p[tp - 3:tp, :GDN_QKV].reshape(1, 1, 3, GDN_QKV)
    cb_s = p[tp:tp + ns, :GDN_QKV].reshape(nb, nq, GDN_QKV)[:, nq - 3:][None]
    gs_p = s_gdn_p[None]
    gs_s = s_gdn_s[None]
    sh_p = h1[tp - 1].reshape(1, 1, D)
    sh_s = h1_s[:, nq - 1][None]
    rs_p = s_rw_p[None]
    rs_s = s_rw_s[None]
    return (y_prompt, y_sample, fk_p, fk_s, fv_p, fv_s, lf_p, lf_sm, cb_p, cb_s, gs_p, gs_s, sh_p, sh_s, rs_p, rs_s)
```

```python
import functools
import math

import jax
import jax.numpy as jnp
from jax import lax
from jax.experimental import pallas as pl
from jax.experimental.pallas import tpu as pltpu

F32 = jnp.float32
BF16 = jnp.bfloat16

D_MODEL = 2048
N_META = 16
GDN_HEADS = 8
GDN_D = 128
GDN_QK = GDN_HEADS * GDN_D
GDN_QKV = 3 * GDN_QK
FOX_HEADS = 8
FOX_DH = 128
FOX_W = FOX_HEADS * FOX_DH
RWKV_HEAD = 64
RWKV_PAIRS = D_MODEL // (2 * RWKV_HEAD)
NORM_EPS = 1e-6
L2_EPS = 1e-6
GN_EPS = 64e-5
NEG_INF = -1e30

LANES = 128
SUBLANES = 8
ROW_TILE = 640
MM_ROW_TILE = 1792
MM_WIDE_MIN_COLS = 8192
LOG2E = 1.4426950408889634
VMEM_LIMIT = 48 * 1024 * 1024
FOX_HEADS_PER_STEP = 4
FOX_GROUP = 16
RWKV_PREP_PAIRS = 16

COL_Z = GDN_QKV
COL_FQ = COL_Z + GDN_QK
COL_FK = COL_FQ + FOX_W
COL_FV = COL_FK + FOX_W
COL_FZ = COL_FV + FOX_W
LANE_A = 0
LANE_B = 8
LANE_F = 16
PRM_W0, PRM_A0, PRM_KK, PRM_KA, PRM_RK, PRM_LNW, PRM_LNB = range(7)


def _pick(n, cands):
    for c in cands:
        if n % c == 0:
            return c
    raise ValueError(f"no tile for {n}")


def _cparams(sem):
    return pltpu.CompilerParams(dimension_semantics=sem, vmem_limit_bytes=VMEM_LIMIT)


def _dot(a, b):
    return jnp.dot(a, b, preferred_element_type=F32)


def _dot_nt(a, b):
    return lax.dot_general(a, b, (((1,), (1,)), ((), ())), preferred_element_type=F32)


def _dot_tn(a, b):
    return lax.dot_general(a, b, (((0,), (0,)), ((), ())), preferred_element_type=F32)


def _bf(x):
    return x.astype(BF16)


def _softplus(x):
    return jnp.maximum(x, 0.0) + jnp.log(1.0 + jnp.exp(-jnp.abs(x)))


def _sigmoid(x):
    return 1.0 / (1.0 + jnp.exp(-x))


def _iota2(shape, dim):
    return lax.broadcasted_iota(jnp.int32, shape, dim)


def _div(x, n):
    assert n & (n - 1) == 0
    return x >> (n.bit_length() - 1)


def _split2(x):
    hi = x.astype(BF16)
    return hi, (x - hi.astype(F32)).astype(BF16)


def _split3(x):
    p1 = x.astype(BF16)
    rem = x - p1.astype(F32)
    p2 = rem.astype(BF16)
    return p1, p2, (rem - p2.astype(F32)).astype(BF16)


def _dot3(a, b):
    m = a.shape[0]
    ah, al = _split2(a)
    bh, bl = _split2(b)
    top = _dot(jnp.concatenate([ah, al], axis=0), bh)
    return (top[:m] + top[m:]) + _dot(ah, bl)


def _dot_lhs01(a01, b):
    a = a01.astype(BF16)
    b1, b2, b3 = _split3(b)
    return _dot(a, b1) + (_dot(a, b2) + _dot(a, b3))


def _dot_rhs01(a, b01):
    b = b01.astype(BF16)
    a1, a2, a3 = _split3(a)
    m = a.shape[0]
    out = _dot(jnp.concatenate([a1, a2, a3], axis=0), b)
    return out[:m] + (out[m:2 * m] + out[2 * m:])


def _tri_inv_steps(n_mat, n):
    size = n_mat.shape[0]
    eye = (_iota2((size, size), 0) == _iota2((size, size), 1)).astype(F32)
    t = eye + n_mat
    if n <= 2:
        return t
    p = _dot(_bf(n_mat), _bf(n_mat))
    yield
    m = 2
    while m < n:
        p_bf = _bf(p)
        if 2 * m < n:
            both = _dot(_bf(jnp.concatenate([p, t], axis=0)), p_bf)
            p = both[:size]
            t = t + both[size:]
        else:
            t = t + _dot(_bf(t), p_bf)
        m *= 2
        yield
    return t


def _run_interleaved(gens):
    results = [None] * len(gens)
    live = list(enumerate(gens))
    while live:
        nxt = []
        for idx, gen in live:
            try:
                next(gen)
                nxt.append((idx, gen))
            except StopIteration as stop:
                results[idx] = stop.value
        live = nxt
    return results


def _rmsnorm_kernel(x_ref, w_ref, o_ref):
    x = x_ref[...]
    ms = jnp.mean(x * x, axis=-1, keepdims=True)
    o_ref[...] = (x * lax.rsqrt(ms + NORM_EPS) * w_ref[...]).astype(o_ref.dtype)


def rmsnorm(x, w, out_dtype):
    m, d = x.shape
    tm = _pick(m, (640, 320, 256, 128))
    return pl.pallas_call(
        _rmsnorm_kernel,
        grid=(m // tm,),
        in_specs=[pl.BlockSpec((tm, d), lambda i: (i, 0)), pl.BlockSpec((1, d), lambda i: (0, 0))],
        out_specs=pl.BlockSpec((tm, d), lambda i: (i, 0)),
        out_shape=jax.ShapeDtypeStruct((m, d), out_dtype),
        compiler_params=_cparams(("parallel",)),
        name="rmsnorm",
    )(x, w.reshape(1, d))


def _mm_kernel(nk, nx, act, has_res, *refs):
    x_refs, w_ref = refs[:nx], refs[nx]
    r_ref = refs[nx + 1] if has_res else None
    o_ref = refs[nx + 1 + has_res]
    scr = refs[nx + 2 + has_res:]

    def product():
        if nx == 1:
            return _dot(x_refs[0][...], _bf(w_ref[...]))
        acc, k0 = None, 0
        for x_ref in x_refs:
            kw = x_ref.shape[1]
            part = _dot(x_ref[...], _bf(w_ref[k0:k0 + kw, :]))
            acc = part if acc is None else acc + part
            k0 += kw
        return acc

    def finish(acc):
        if act == "relu2":
            acc = jnp.square(jnp.maximum(acc, 0.0))
        elif act == "tanh":
            acc = jnp.tanh(acc)
        elif act == "sigmoid":
            acc = _sigmoid(acc)
        if has_res:
            acc = r_ref[...] + acc
        o_ref[...] = acc.astype(o_ref.dtype)

    if nk == 1:
        finish(product())
    else:
        acc_ref = scr[0]
        k = pl.program_id(2)

        @pl.when(k == 0)
        def _():
            acc_ref[...] = jnp.zeros(acc_ref.shape, F32)

        acc_ref[...] += product()

        @pl.when(k == nk - 1)
        def _():
            finish(acc_ref[...])


def matmul(x, w, *, layer=None, act=None, res=None, out_dtype=F32, name="matmul"):
    xs = list(x) if isinstance(x, (list, tuple)) else [x]
    m = xs[0].shape[0]
    k = sum(xi.shape[1] for xi in xs)
    n = w.shape[-1]
    tm = _pick(m, (MM_ROW_TILE, ROW_TILE, 512, 256, 128))
    tk = k if k <= 2048 else 2048
    nk = k // tk
    wide_ok = nk == 1 and res is None and n >= MM_WIDE_MIN_COLS
    tn = _pick(n, ((1024,) if wide_ok else ()) + (512, 256, 128))
    assert len(xs) == 1 or nk == 1
    in_specs = [pl.BlockSpec((tm, tk if len(xs) == 1 else xi.shape[1]), lambda i, j, kk: (i, kk)) for xi in xs]
    if w.ndim == 3:
        in_specs.append(pl.BlockSpec((None, tk, tn), lambda i, j, kk: (layer, kk, j)))
    else:
        in_specs.append(pl.BlockSpec((tk, tn), lambda i, j, kk: (kk, j)))
    args = xs + [w]
    if res is not None:
        in_specs.append(pl.BlockSpec((tm, tn), lambda i, j, kk: (i, j)))
        args.append(res)
    return pl.pallas_call(
        functools.partial(_mm_kernel, nk, len(xs), act, res is not None),
        grid=(m // tm, n // tn, nk),
        in_specs=in_specs,
        out_specs=pl.BlockSpec((tm, tn), lambda i, j, kk: (i, j)),
        out_shape=jax.ShapeDtypeStruct((m, n), out_dtype),
        scratch_shapes=[pltpu.VMEM((tm, tn), F32)] if nk > 1 else [],
        compiler_params=_cparams(("parallel", "parallel", "arbitrary")),
        name=name,
    )(*args)


def _gdn_kernel(C, NC, first_valid, has_conv, qkv_ref, z_ref, ps_ref, cw_ref, alog_ref, dt_ref, nw_ref, s0_ref,
                *rest):
    conv_ref = rest[0] if has_conv else None
    o_ref, sout_ref, S, ext = rest[1:] if has_conv else rest
    c = pl.program_id(1)

    @pl.when(c == 0)
    def _():
        S[...] = s0_ref[0]
        ext[0:SUBLANES, :] = jnp.zeros((SUBLANES, GDN_QKV), F32)

    x = qkv_ref[...]
    ext[SUBLANES:SUBLANES + C, :] = x
    if has_conv:
        assert NC == 1 and 3 <= first_valid < C
        ext[SUBLANES + first_valid - 3:SUBLANES + first_valid, :] = conv_ref[0, 0]
    cw = cw_ref[...]
    y = ((ext[5:5 + C, :] * cw[0:1] + ext[6:6 + C, :] * cw[1:2]) + ext[7:7 + C, :] * cw[2:3]) + ext[8:8 + C, :] * cw[3:4]
    ext[0:SUBLANES, :] = x[C - SUBLANES:C]
    y = y * _sigmoid(y)

    rows = c * C + _iota2((C, 1), 0)
    valid = rows >= first_valid
    ps = ps_ref[...]
    g_all = jnp.where(valid, -jnp.exp(alog_ref[...]) * _softplus(ps + dt_ref[...]), 0.0)
    beta_all = jnp.where(valid, _sigmoid(ps), 0.0)
    ri = _iota2((C, C), 0)
    ci = _iota2((C, C), 1)
    causal = ri >= ci
    strict = ri > ci
    gc = _dot_lhs01(causal.astype(F32), g_all)
    gct = _dot_rhs01(g_all.T, (ri <= ci).astype(F32))
    nw = nw_ref[...]

    def head_steps(h):
        sl = slice(h * GDN_D, (h + 1) * GDN_D)
        qh = y[:, sl]
        kh = y[:, GDN_QK + h * GDN_D:GDN_QK + (h + 1) * GDN_D]
        vh = jnp.where(valid, y[:, 2 * GDN_QK + h * GDN_D:2 * GDN_QK + (h + 1) * GDN_D], 0.0)
        qh = jnp.where(valid, qh * lax.rsqrt(jnp.sum(qh * qh, axis=-1, keepdims=True) + L2_EPS) * GDN_D ** -0.5, 0.0)
        kh = jnp.where(valid, kh * lax.rsqrt(jnp.sum(kh * kh, axis=-1, keepdims=True) + L2_EPS), 0.0)
        bcol = beta_all[:, LANE_B + h:LANE_B + h + 1]
        gcol = gc[:, LANE_A + h:LANE_A + h + 1]
        grow = gct[LANE_A + h:LANE_A + h + 1, :]
        glast = gc[C - 1:C, LANE_A + h:LANE_A + h + 1]
        diff = gcol - grow
        decay = jnp.where(causal, jnp.exp(jnp.where(causal, diff, 0.0)), 0.0)
        kb = kh * bcol
        a_mat = jnp.where(strict, _dot_nt(_bf(kb), _bf(kh)) * decay, 0.0)
        qk = _dot_nt(_bf(qh), _bf(kh)) * decay
        eg = jnp.exp(gcol)
        rhs = jnp.concatenate([vh * bcol, kb * eg], axis=1)
        q_dec = qh * eg
        k_dec = kh * jnp.exp(glast - gcol)
        yield
        t_mat = yield from _tri_inv_steps(-a_mat, C)
        sol = _dot3(t_mat, rhs)
        u = sol[:, :GDN_D]
        w = sol[:, GDN_D:]
        yield
        s_h = S[h]
        s_bf = _bf(s_h)
        v_new = u - _dot(_bf(w), s_bf)
        o_state = _dot(_bf(q_dec), s_bf)
        yield
        o = o_state + _dot(_bf(qk), _bf(v_new))
        S[h] = s_h * jnp.exp(glast) + _dot_tn(_bf(k_dec), _bf(v_new))
        yield
        on = o * lax.rsqrt(jnp.mean(o * o, axis=-1, keepdims=True) + NORM_EPS) * nw
        zh = z_ref[:, sl]
        o_ref[:, sl] = (on * (zh * _sigmoid(zh))).astype(o_ref.dtype)

    _run_interleaved([head_steps(h) for h in range(GDN_HEADS)])

    @pl.when(c == NC - 1)
    def _():
        sout_ref[0] = S[...]


def gdn(qkv_arr, z_arr, ps_arr, z_col, conv_w, alog_pad, dt_pad, norm_w, s0, conv0, *, B, NC, C, first_valid, name,
        out_rows=None):
    rows = B * NC * C
    has_conv = conv0 is not None
    extra_specs = [pl.BlockSpec((1, 1, 3, GDN_QKV), lambda b, c: (0, b, 0, 0))] if has_conv else []
    extra_args = [conv0] if has_conv else []
    return pl.pallas_call(
        functools.partial(_gdn_kernel, C, NC, first_valid, has_conv),
        grid=(B, NC),
        in_specs=[
            pl.BlockSpec((C, GDN_QKV), lambda b, c: (b * NC + c, 0)),
            pl.BlockSpec((C, GDN_QK), lambda b, c: (b * NC + c, z_col)),
            pl.BlockSpec((C, LANES), lambda b, c: (b * NC + c, 0)),
            pl.BlockSpec((4, GDN_QKV), lambda b, c: (0, 0)),
            pl.BlockSpec((1, LANES), lambda b, c: (0, 0)),
            pl.BlockSpec((1, LANES), lambda b, c: (0, 0)),
            pl.BlockSpec((1, GDN_D), lambda b, c: (0, 0)),
            pl.BlockSpec((1, GDN_HEADS, GDN_D, GDN_D), lambda b, c: (b, 0, 0, 0)),
        ] + extra_specs,
        out_specs=[
            pl.BlockSpec((C, GDN_QK), lambda b, c: (b * NC + c, 0)),
            pl.BlockSpec((1, GDN_HEADS, GDN_D, GDN_D), lambda b, c: (b, 0, 0, 0)),
        ],
        out_shape=[
            jax.ShapeDtypeStruct((out_rows or rows, GDN_QK), BF16),
            jax.ShapeDtypeStruct((B, GDN_HEADS, GDN_D, GDN_D), F32),
        ],
        scratch_shapes=[pltpu.VMEM((GDN_HEADS, GDN_D, GDN_D), F32), pltpu.VMEM((C + SUBLANES, GDN_QKV), F32)],
        compiler_params=_cparams(("parallel", "arbitrary")),
        name=name,
    )(qkv_arr, z_arr, ps_arr, conv_w, alog_pad, dt_pad, norm_w, s0, *extra_args)


def _fox_prep_kernel(tb, first_valid, ps_ref, bf_ref, lf_ref, cq_ref, ck_ref, carry):
    i = pl.program_id(0)

    @pl.when(i == 0)
    def _():
        carry[...] = jnp.zeros(carry.shape, F32)

    x = ps_ref[...] + bf_ref[...]
    rows = i * tb + _iota2((tb, 1), 0)
    lf = jnp.where(rows >= first_valid, jnp.minimum(x, 0.0) - jnp.log(1.0 + jnp.exp(-jnp.abs(x))), 0.0)
    tri = (_iota2((tb, tb), 0) >= _iota2((tb, tb), 1)).astype(F32)
    c = _dot_lhs01(tri, lf) + carry[0:1, :]
    carry[...] = jnp.broadcast_to(c[tb - 1:tb, :], carry.shape)
    lf_ref[...] = lf
    c2 = c * LOG2E
    for h in range(FOX_HEADS):
        cq_ref[h] = jnp.broadcast_to(c2[:, LANE_F + h:LANE_F + h + 1], (tb, LANES))
    ck_ref[...] = c2.T[LANE_F:LANE_F + FOX_HEADS, :]


def fox_prep(ps, bf_pad, first_valid):
    r = ps.shape[0]
    tb = _pick(r, (ROW_TILE, 512, 256, 128))
    return pl.pallas_call(
        functools.partial(_fox_prep_kernel, tb, first_valid),
        grid=(r // tb,),
        in_specs=[pl.BlockSpec((tb, LANES), lambda i: (i, 0)), pl.BlockSpec((1, LANES), lambda i: (0, 0))],
        out_specs=[
            pl.BlockSpec((tb, LANES), lambda i: (i, 0)),
            pl.BlockSpec((FOX_HEADS, tb, LANES), lambda i: (0, i, 0)),
            pl.BlockSpec((FOX_HEADS, tb), lambda i: (0, i)),
        ],
        out_shape=[
            jax.ShapeDtypeStruct((r, LANES), F32),
            jax.ShapeDtypeStruct((FOX_HEADS, r, LANES), F32),
            jax.ShapeDtypeStruct((FOX_HEADS, r), F32),
        ],
        scratch_shapes=[pltpu.VMEM((SUBLANES, LANES), F32)],
        compiler_params=_cparams(("arbitrary",)),
        name="fox_prep",
    )(ps, bf_pad)


def _fox_flash_kernel(tq, first_valid, qi_ref, ki_ref, qt_ref, k_ref, vt_ref, ck_ref, cq_ref, fz_ref, o_ref,
                      m_s, l_s, acc_s, t_s):
    step = pl.program_id(1)
    qi = qi_ref[step]
    ki = ki_ref[step]
    reps = tq // LANES
    heads = qt_ref.shape[0]

    @pl.when(ki == 0)
    def _():
        m_s[...] = jnp.full(m_s.shape, NEG_INF, F32)
        l_s[...] = jnp.zeros(l_s.shape, F32)
        acc_s[...] = jnp.zeros(acc_s.shape, F32)

    def head_steps(h, masked):
        cols = slice(h * FOX_DH, (h + 1) * FOX_DH)
        qt = _bf(qt_ref[h] * (FOX_DH ** -0.5 * LOG2E))
        cq = cq_ref[h]
        m_prev = m_s[h, 0:1, :]
        kb = LANES
        groups = kb // SUBLANES
        mx = None
        for b in range(tq // kb):
            rs = slice(b * kb, (b + 1) * kb)
            t = _dot(_bf(k_ref[rs, cols]), qt) - jnp.concatenate([ck_ref[h, rs, :]] * reps, axis=1)
            if masked:
                kpos = ki * tq + b * kb + _iota2((kb, tq), 0)
                qpos = qi * tq + _iota2((kb, tq), 1)
                t = jnp.where((kpos <= qpos) & (kpos >= first_valid), t, NEG_INF)
            t_s[h, rs, :] = t
            part = jnp.max(t.reshape(groups, SUBLANES, tq), axis=0)
            mx = part if mx is None else jnp.maximum(mx, part)
            yield
        m_new = jnp.maximum(m_prev, jnp.max(mx, axis=0, keepdims=True) + cq)
        shift = cq - m_new
        alpha = jnp.exp2(m_prev - m_new)
        lsum = None
        pv = None
        for b in range(tq // kb):
            rs = slice(b * kb, (b + 1) * kb)
            p = jnp.exp2(t_s[h, rs, :] + shift)
            part = jnp.sum(p.reshape(groups, SUBLANES, tq), axis=0)
            lsum = part if lsum is None else lsum + part
            d = _dot(_bf(vt_ref[h, :, rs]), _bf(p))
            pv = d if pv is None else pv + d
            yield
        l_new = alpha * l_s[h, 0:1, :] + jnp.sum(lsum, axis=0, keepdims=True)
        l_s[h] = jnp.broadcast_to(l_new, (SUBLANES, tq))
        acc_s[h] = alpha * acc_s[h] + pv
        m_s[h] = jnp.broadcast_to(m_new, (SUBLANES, tq))

    def accumulate(masked):
        _run_interleaved([head_steps(h, masked) for h in range(heads)])

    edge = (ki == qi) | (ki == 0)
    pl.when(edge)(lambda: accumulate(True))
    pl.when(jnp.logical_not(edge))(lambda: accumulate(False))

    @pl.when(ki == qi)
    def _():
        rows = qi * tq + _iota2((tq, 1), 0)
        for h in range(heads):
            cols = slice(h * FOX_DH, (h + 1) * FOX_DH)
            o = (acc_s[h] / l_s[h, 0:1, :]).T * _sigmoid(fz_ref[:, cols])
            o_ref[:, cols] = jnp.where(rows >= first_valid, o, 0.0).astype(o_ref.dtype)


def fox_prompt(p, c_rep, c_row, tp, first_valid, out_rows):
    tq = _pick(tp, (ROW_TILE, 512, 384, 256, 128))
    nq = tp // tq
    assert first_valid < tq
    pairs = [(qi, ki) for qi in range(nq) for ki in range(qi + 1)]
    qi_arr = jnp.array([a for a, _ in pairs], jnp.int32)
    ki_arr = jnp.array([b for _, b in pairs], jnp.int32)
    hp = FOX_HEADS_PER_STEP
    wid = hp * FOX_DH
    cb = lambda col, g: col // wid + g

    def heads_t(col, dtype):
        return jnp.transpose(p[:tp, col:col + FOX_W].reshape(tp, FOX_HEADS, FOX_DH), (1, 2, 0)).astype(dtype)

    return pl.pallas_call(
        functools.partial(_fox_flash_kernel, tq, first_valid),
        grid_spec=pltpu.PrefetchScalarGridSpec(
            num_scalar_prefetch=2,
            grid=(FOX_HEADS // hp, len(pairs)),
            in_specs=[
                pl.BlockSpec((hp, FOX_DH, tq), lambda g, t, qa, ka: (g, 0, qa[t])),
                pl.BlockSpec((tq, wid), lambda g, t, qa, ka: (ka[t], cb(COL_FK, g))),
                pl.BlockSpec((hp, FOX_DH, tq), lambda g, t, qa, ka: (g, 0, ka[t])),
                pl.BlockSpec((hp, tq, LANES), lambda g, t, qa, ka: (g, ka[t], 0)),
                pl.BlockSpec((hp, 1, tq), lambda g, t, qa, ka: (g, 0, qa[t])),
                pl.BlockSpec((tq, wid), lambda g, t, qa, ka: (qa[t], cb(COL_FZ, g))),
            ],
            out_specs=pl.BlockSpec((tq, wid), lambda g, t, qa, ka: (qa[t], g)),
            scratch_shapes=[pltpu.VMEM((hp, SUBLANES, tq), F32), pltpu.VMEM((hp, SUBLANES, tq), F32),
                            pltpu.VMEM((hp, FOX_DH, tq), F32), pltpu.VMEM((hp, tq, tq), F32)],
        ),
        out_shape=jax.ShapeDtypeStruct((out_rows, FOX_W), BF16),
        compiler_params=_cparams(("parallel", "arbitrary")),
        name="fox_prompt",
    )(qi_arr, ki_arr, heads_t(COL_FQ, F32), p, heads_t(COL_FV, BF16), c_rep, c_row.reshape(FOX_HEADS, 1, -1), p)


def _fox_sample_kernel(G, ngroups, pt_ref, q_ref, *refs):
    k_refs, v_refs, lft_refs = refs[0:G], refs[G:2 * G], refs[2 * G:3 * G]
    lfn_ref, kn_ref, vn_ref, fz_ref, o_ref, cq_s, cn_s, carry_s, m_s, l_s, acc_s = refs[3 * G:]
    jg = pl.program_id(1)
    rows = q_ref.shape[1]
    page = lft_refs[0].shape[2]
    flat = page * FOX_HEADS
    nn = lfn_ref.shape[2]
    scale = FOX_DH ** -0.5

    @pl.when(jg == 0)
    def _():
        upper = (_iota2((nn, nn), 0) <= _iota2((nn, nn), 1)).astype(F32)
        cn = _dot_rhs01(lfn_ref[0], upper)
        cn_s[...] = cn
        own_q = _iota2((rows, nn), 1) == _div(_iota2((rows, nn), 0), FOX_HEADS)
        cq_s[...] = jnp.broadcast_to(jnp.sum(jnp.where(own_q, cn, 0.0), axis=-1, keepdims=True), cq_s.shape)
        carry_s[...] = jnp.zeros(carry_s.shape, F32)
        m_s[...] = jnp.full(m_s.shape, NEG_INF, F32)
        l_s[...] = jnp.zeros(l_s.shape, F32)
        acc_s[...] = jnp.zeros(acc_s.shape, F32)

    def update(ts, vals_bf):
        cq = cq_s[:, 0:1]
        m_prev = m_s[:, 0:1]
        t_max = functools.reduce(jnp.maximum, ts)
        m_new = jnp.maximum(m_prev, jnp.max(t_max, axis=-1, keepdims=True) + cq)
        shift = cq - m_new
        ps = [jnp.exp(t + shift) for t in ts]
        alpha = jnp.exp(m_prev - m_new)
        l_s[...] = alpha * l_s[...] + jnp.sum(functools.reduce(jnp.add, ps), axis=-1, keepdims=True)
        pv = functools.reduce(jnp.add, [_dot(_bf(p), v) for p, v in zip(ps, vals_bf)])
        acc_s[...] = alpha * acc_s[...] + pv
        m_s[...] = jnp.broadcast_to(m_new, m_s.shape)

    own = (_iota2((rows, flat), 0) & (FOX_HEADS - 1)) == (_iota2((rows, flat), 1) & (FOX_HEADS - 1))
    later = (_iota2((page, flat), 0) > _div(_iota2((page, flat), 1), FOX_HEADS)).astype(BF16)
    head_col = _iota2((FOX_HEADS, flat), 0) == (_iota2((FOX_HEADS, flat), 1) & (FOX_HEADS - 1))
    carry = carry_s[:, 0:1]
    q_bf = _bf(q_ref[0])
    lfts = [lft_refs[g][0] for g in range(G)]
    suffixes = _dot_rhs01(jnp.concatenate(lfts, axis=0), later)
    ts = []
    for g in range(G):
        suffix = suffixes[g * FOX_HEADS:(g + 1) * FOX_HEADS]
        d = jnp.sum(jnp.where(head_col, suffix + carry, 0.0), axis=0, keepdims=True)
        carry = carry + jnp.sum(lfts[g], axis=-1, keepdims=True)
        s = _dot_nt(q_bf, _bf(k_refs[g][0])) * scale
        ts.append(jnp.where(own, s + d, NEG_INF))
    update(ts, [_bf(v_refs[g][0]) for g in range(G)])
    carry_s[...] = jnp.broadcast_to(carry, carry_s.shape)

    @pl.when(jg == ngroups - 1)
    def _():
        nflat = kn_ref.shape[1]
        spread = (_iota2((nn, nflat), 0) == _div(_iota2((nn, nflat), 1), FOX_HEADS)).astype(F32)
        cn_cols = _dot_rhs01(cn_s[...], spread)
        ri = _iota2((rows, nflat), 0)
        ci = _iota2((rows, nflat), 1)
        ok = ((ri & (FOX_HEADS - 1)) == (ci & (FOX_HEADS - 1))) & (_div(ci, FOX_HEADS) <= _div(ri, FOX_HEADS))
        sn = _dot_nt(_bf(q_ref[0]), _bf(kn_ref[0])) * scale
        update([jnp.where(ok, sn - cn_cols, NEG_INF)], [_bf(vn_ref[0])])
        o_ref[0] = acc_s[...] / l_s[...] * _sigmoid(fz_ref[0])


def fox_sample(page_table_flat, q_rows, cache_k, cache_v, cache_lft, lfn, kn_flat, vn_flat, fz_rows, nb, npages):
    flat = cache_k.shape[1]
    page = cache_lft.shape[2]
    rows = q_rows.shape[1]
    nn = lfn.shape[2]
    G = _pick(npages, (FOX_GROUP, 2, 1))
    ngroups = npages // G

    def page_map(g):
        return lambda b, jg, pt: (pt[b * npages + (npages - 1 - (jg * G + g))], 0, 0)

    seq = lambda b, jg, pt: (b, 0, 0)
    row_blk = pl.BlockSpec((1, rows, FOX_DH), seq)
    new_blk = pl.BlockSpec((1, kn_flat.shape[1], FOX_DH), seq)
    in_specs = [row_blk]
    in_specs += [pl.BlockSpec((1, flat, FOX_DH), page_map(g)) for g in range(G)]
    in_specs += [pl.BlockSpec((1, flat, FOX_DH), page_map(g)) for g in range(G)]
    in_specs += [pl.BlockSpec((1, FOX_HEADS, page), page_map(g)) for g in range(G)]
    in_specs += [pl.BlockSpec((1, rows, nn), seq), new_blk, new_blk, row_blk]
    return pl.pallas_call(
        functools.partial(_fox_sample_kernel, G, ngroups),
        grid_spec=pltpu.PrefetchScalarGridSpec(
            num_scalar_prefetch=1,
            grid=(nb, ngroups),
            in_specs=in_specs,
            out_specs=row_blk,
            scratch_shapes=[
                pltpu.VMEM((rows, LANES), F32),
                pltpu.VMEM((rows, nn), F32),
                pltpu.VMEM((FOX_HEADS, LANES), F32),
                pltpu.VMEM((rows, LANES), F32),
                pltpu.VMEM((rows, LANES), F32),
                pltpu.VMEM((rows, FOX_DH), F32),
            ],
        ),
        out_shape=jax.ShapeDtypeStruct((nb, rows, FOX_DH), F32),
        compiler_params=_cparams(("parallel", "arbitrary")),
        name="fox_sample",
    )(page_table_flat, q_rows, *([cache_k] * G), *([cache_v] * G), *([cache_lft] * G), lfn, kn_flat, vn_flat, fz_rows)


def _rwkv_mix_kernel(tm, tp, ns, nq, h_ref, hb_ref, st_ref, mu_ref, *o_refs):
    i = pl.program_id(0)
    h = h_ref[...]
    above = jnp.concatenate([hb_ref[SUBLANES - 1:SUBLANES, :], h[:tm - 1]], axis=0)
    row = i * tm + _iota2((tm, 1), 0)
    seq_start = (row >= tp) & (row < tp + ns) & (((row - tp) & (nq - 1)) == 0)
    prev = jnp.where(seq_start, st_ref[...], jnp.where(row == 0, 0.0, above))
    xx = prev - h
    mu = mu_ref[...]
    for j, o_ref in enumerate(o_refs):
        o_ref[...] = (h + xx * mu[j:j + 1]).astype(o_ref.dtype)


def rwkv_mix(h, shift_rows, mu, tp, ns, nq):
    m, d = h.shape
    assert nq & (nq - 1) == 0
    tm = _pick(math.gcd(m, tp), (320, 256, 128))
    first = tp // tm
    spec = pl.BlockSpec((tm, d), lambda i: (i, 0))
    return pl.pallas_call(
        functools.partial(_rwkv_mix_kernel, tm, tp, ns, nq),
        grid=(m // tm,),
        in_specs=[spec,
                  pl.BlockSpec((SUBLANES, d), lambda i: (jnp.maximum(i * (tm // SUBLANES) - 1, 0), 0)),
                  pl.BlockSpec((tm, d), lambda i: (jnp.maximum(i - first, 0), 0)),
                  pl.BlockSpec((6, d), lambda i: (0, 0))],
        out_specs=[spec] * 6,
        out_shape=[jax.ShapeDtypeStruct((m, d), BF16)] * 6,
        compiler_params=_cparams(("parallel",)),
        name="rwkv_mix",
    )(h, h, shift_rows, mu)


def _head_ones():
    return (_div(_iota2((LANES, LANES), 0), RWKV_HEAD) == _div(_iota2((LANES, LANES), 1), RWKV_HEAD)).astype(F32)


def _rwkv_chunk_terms(C, valid, r, k, v, wl, al, prm):
    HD = RWKV_HEAD
    m0 = _iota2((1, LANES), 1) < HD
    bones = _head_ones()
    w0, a0, k_k, k_a, r_k = (prm[i:i + 1] for i in (PRM_W0, PRM_A0, PRM_KK, PRM_KA, PRM_RK))
    wlog = -_softplus(-(w0 + wl)) - 0.5
    lw = jnp.where(valid, -jnp.exp(wlog), 0.0)
    a = _sigmoid(a0 + al)
    kkr = k * k_k
    kk = kkr * lax.rsqrt(_dot_rhs01(kkr * kkr, bones) + L2_EPS)
    k2 = k * (1.0 + (a - 1.0) * k_a)
    bonus = _dot_rhs01(r * k2 * r_k, bones) * v
    rm = jnp.where(valid, r, 0.0)
    k2 = jnp.where(valid, k2, 0.0)
    vm = jnp.where(valid, v, 0.0)
    av = jnp.where(valid, -kk, 0.0)
    bv = jnp.where(valid, kk * a, 0.0)

    ri = _iota2((C, C), 0)
    ci = _iota2((C, C), 1)
    yield
    cum = _dot_lhs01((ri >= ci).astype(F32), lw)
    cum_last = cum[C - 1:C, :]
    inv = jnp.exp(-cum)
    rt = rm * jnp.exp(cum)
    at = av * jnp.exp(cum - lw)
    bt = bv * inv
    kt = k2 * inv
    to_end = jnp.exp(cum_last - cum)
    b_end = bv * to_end
    k_end = k2 * to_end

    def split(x):
        return jnp.concatenate([jnp.where(m0, x, 0.0), jnp.where(m0, 0.0, x)], axis=0)

    def halves(x):
        return x[0:C] + x[C:2 * C]

    yield
    at_s = split(at)
    gram = _dot_nt(_bf(jnp.concatenate([at_s, split(rt)], axis=0)), _bf(jnp.concatenate([bt, kt], axis=0)))
    r2 = _iota2((2 * C, 2 * C), 0)
    c2 = _iota2((2 * C, 2 * C), 1)
    same = _div(r2, C) == _div(c2, C)
    bd_strict = same & (r2 > c2)
    bd_incl = same & (r2 >= c2)

    def bd(block, mask):
        return jnp.where(mask, jnp.concatenate([block, block], axis=1), 0.0)

    a_ab = bd(gram[0:2 * C, 0:C], bd_strict)
    a_ak = bd(gram[0:2 * C, C:2 * C], bd_strict)
    r_b = bd(gram[2 * C:4 * C, 0:C], bd_incl)
    r_k2 = bd(gram[2 * C:4 * C, C:2 * C], bd_incl)
    vs = split(vm)
    akv = halves(_dot(_bf(a_ak), _bf(vs)))
    yield
    t_bd = yield from _tri_inv_steps(a_ab, C)
    tw = _dot(_bf(t_bd), _bf(jnp.concatenate([at_s, split(akv)], axis=1)))
    wt = halves(tw[:, :LANES])
    ut = halves(tw[:, LANES:])
    yield
    rp = rt + halves(_dot(_bf(r_b), _bf(split(wt))))
    y0 = halves(_dot(_bf(jnp.concatenate([r_b, r_k2], axis=1)), _bf(jnp.concatenate([split(ut), vs], axis=0))))
    eye = (_iota2((LANES, LANES), 0) == _iota2((LANES, LANES), 1)).astype(F32)
    m_mat = eye * jnp.exp(cum_last) + bones * _dot_tn(_bf(b_end), _bf(wt))
    n_mat = bones * _dot_tn(_bf(jnp.concatenate([b_end, k_end], axis=0)), _bf(jnp.concatenate([ut, vm], axis=0)))
    return rp, y0, bonus, m_mat, n_mat


def _rwkv_prep_kernel(C, chunk_of_step, first_valid, PG, r_ref, k_ref, v_ref, wl_ref, al_ref, prm_ref,
                      rp_ref, y0_ref, bo_ref, m_ref, n_ref):
    c = pl.program_id(0) if chunk_of_step else 0
    valid = (c * C + _iota2((C, 1), 0)) >= first_valid
    sls = [slice(i * LANES, (i + 1) * LANES) for i in range(PG)]
    terms = _run_interleaved([
        _rwkv_chunk_terms(C, valid, r_ref[:, sl], k_ref[:, sl], v_ref[:, sl], wl_ref[:, sl], al_ref[:, sl], prm_ref[:, sl])
        for sl in sls])
    for i, sl in enumerate(sls):
        rp, y0, bonus, m_mat, n_mat = terms[i]
        rp_ref[:, sl] = rp
        y0_ref[:, sl] = y0
        bo_ref[:, sl] = bonus
        m_ref[0, i] = m_mat
        n_ref[0, i] = n_mat


def _rwkv_scan_kernel(NC, rp_ref, y0_ref, bo_ref, g_ref, m_ref, n_ref, prm_ref, s0_ref, o_ref, sout_ref, H):
    c = pl.program_id(1)
    HD = RWKV_HEAD
    bones = _head_ones()

    @pl.when(c == 0)
    def _():
        zero = jnp.zeros((HD, HD), F32)
        for i in range(RWKV_PAIRS):
            top = jnp.concatenate([s0_ref[0, 2 * i], zero], axis=1)
            bot = jnp.concatenate([zero, s0_ref[0, 2 * i + 1]], axis=1)
            H[i] = jnp.concatenate([top, bot], axis=0).T

    def pair_steps(i):
        sl = slice(i * LANES, (i + 1) * LANES)
        h_bd = H[i]
        y = _dot(_bf(rp_ref[:, sl]), _bf(h_bd)) + y0_ref[:, sl]
        H[i] = _dot3(m_ref[0, i], h_bd) + n_ref[0, i]
        yield
        mean = _dot_rhs01(y, bones) * (1.0 / HD)
        dlt = y - mean
        yield
        var = _dot_rhs01(dlt * dlt, bones) * (1.0 / HD)
        yn = dlt * lax.rsqrt(var + GN_EPS) * prm_ref[PRM_LNW:PRM_LNW + 1, sl] + prm_ref[PRM_LNB:PRM_LNB + 1, sl]
        o_ref[:, sl] = ((yn + bo_ref[:, sl]) * g_ref[:, sl]).astype(o_ref.dtype)

    _run_interleaved([pair_steps(i) for i in range(RWKV_PAIRS)])

    @pl.when(c == NC - 1)
    def _():
        for i in range(RWKV_PAIRS):
            ht = H[i].T
            sout_ref[0, 2 * i] = ht[0:HD, 0:HD]
            sout_ref[0, 2 * i + 1] = ht[HD:2 * HD, HD:2 * HD]


def rwkv(r, k, v, wl, al, g, prm, s0, *, B, NC, C, first_valid, name, out_rows=None):
    assert B == 1 or NC == 1
    nblk = B * NC
    rows = nblk * C
    PG = RWKV_PREP_PAIRS
    wide = pl.BlockSpec((C, PG * LANES), lambda blk, pg: (blk, pg))
    mat = pl.BlockSpec((1, PG, LANES, LANES), lambda blk, pg: (blk, pg, 0, 0))
    mat_shape = jax.ShapeDtypeStruct((nblk, RWKV_PAIRS, LANES, LANES), F32)
    row_shape = jax.ShapeDtypeStruct((rows, D_MODEL), F32)
    rp, y0, bonus, m_all, n_all = pl.pallas_call(
        functools.partial(_rwkv_prep_kernel, C, B == 1, first_valid, PG),
        grid=(nblk, RWKV_PAIRS // PG),
        in_specs=[wide] * 5 + [pl.BlockSpec((SUBLANES, PG * LANES), lambda blk, pg: (0, pg))],
        out_specs=[wide, wide, wide, mat, mat],
        out_shape=[row_shape, row_shape, row_shape, mat_shape, mat_shape],
        compiler_params=_cparams(("parallel", "parallel")),
        name=name + "_terms",
    )(r, k, v, wl, al, prm)

    full = pl.BlockSpec((C, D_MODEL), lambda b, c: (b * NC + c, 0))
    mats = pl.BlockSpec((1, RWKV_PAIRS, LANES, LANES), lambda b, c: (b * NC + c, 0, 0, 0))
    st_spec = pl.BlockSpec((1, 2 * RWKV_PAIRS, RWKV_HEAD, RWKV_HEAD), lambda b, c: (b, 0, 0, 0))
    return pl.pallas_call(
        functools.partial(_rwkv_scan_kernel, NC),
        grid=(B, NC),
        in_specs=[full] * 4 + [mats, mats, pl.BlockSpec((SUBLANES, D_MODEL), lambda b, c: (0, 0)), st_spec],
        out_specs=[full, st_spec],
        out_shape=[
            jax.ShapeDtypeStruct((out_rows or rows, D_MODEL), BF16),
            jax.ShapeDtypeStruct((B, 2 * RWKV_PAIRS, RWKV_HEAD, RWKV_HEAD), F32),
        ],
        scratch_shapes=[pltpu.VMEM((RWKV_PAIRS, LANES, LANES), F32)],
        compiler_params=_cparams(("parallel", "arbitrary")),
        name=name + "_scan",
    )(rp, y0, bonus, g, m_all, n_all, prm, s0)


def _rwkv_lanes_kernel(nq, unroll, x_ref, prm_ref, s0_ref, o_ref, sout_ref, y_s):
    HD = RWKV_HEAD
    w0, a0, k_k, k_a, r_k, ln_w, ln_b = (prm_ref[i] for i in (PRM_W0, PRM_A0, PRM_KK, PRM_KA, PRM_RK,
                                                               PRM_LNW, PRM_LNB))
    sout_ref[...] = s0_ref[...]
    for t in range(nq):
        r, k, v, wl, al, g = (x_ref[i, t] for i in range(6))
        wlog = -_softplus(-(w0 + wl)) - 0.5
        w = jnp.exp(-jnp.exp(wlog))
        a = _sigmoid(a0 + al)
        kkr = k * k_k
        kk = kkr * lax.rsqrt(jnp.sum(kkr * kkr, axis=0, keepdims=True) + L2_EPS)
        k2 = k * (1.0 + (a - 1.0) * k_a)
        av = -kk
        bv = kk * a

        def value_row(i, carry):
            s = sout_ref[0, i]
            sa = jnp.sum(s * av, axis=0, keepdims=True)
            s = s * w + sa * bv + x_ref[2, t, pl.ds(i, 1), :] * k2
            sout_ref[0, i] = s
            y_s[pl.ds(i, 1), :] = jnp.sum(s * r, axis=0, keepdims=True)
            return carry

        lax.fori_loop(0, HD, value_row, 0, unroll=unroll)
        y = y_s[...]
        mean = jnp.mean(y, axis=0, keepdims=True)
        dlt = y - mean
        var = jnp.mean(dlt * dlt, axis=0, keepdims=True)
        yn = dlt * lax.rsqrt(var + GN_EPS) * ln_w + ln_b
        bonus = jnp.sum(r * k2 * r_k, axis=0, keepdims=True) * v
        o_ref[t] = ((yn + bonus) * g).astype(o_ref.dtype)


def rwkv_lanes(xs, prm_col, s0):
    _, nq, d, nb = xs.shape
    heads = d // RWKV_HEAD
    st = pl.BlockSpec((1, RWKV_HEAD, RWKV_HEAD, nb), lambda h: (h, 0, 0, 0))
    return pl.pallas_call(
        functools.partial(_rwkv_lanes_kernel, nq, SUBLANES),
        grid=(heads,),
        in_specs=[pl.BlockSpec((6, nq, RWKV_HEAD, nb), lambda h: (0, 0, h, 0)),
                  pl.BlockSpec((SUBLANES, RWKV_HEAD, 1), lambda h: (0, h, 0)), st],
        out_specs=[pl.BlockSpec((nq, RWKV_HEAD, nb), lambda h: (0, h, 0)), st],
        out_shape=[jax.ShapeDtypeStruct((nq, d, nb), F32), jax.ShapeDtypeStruct(s0.shape, F32)],
        scratch_shapes=[pltpu.VMEM((RWKV_HEAD, nb), F32)],
        compiler_params=_cparams(("parallel",)),
        name="rwkv_sample",
    )(xs, prm_col, s0)


def _pad_lanes(vec, offset):
    out = jnp.zeros((1, LANES), F32)
    return lax.dynamic_update_slice(out, vec.reshape(1, -1).astype(F32), (0, offset))


def _sample_rows(arr, row0, nb, nq, front):
    cols = arr.shape[1]
    s = arr[row0:row0 + nb * nq].reshape(nb, nq, cols)
    s = jnp.pad(s, ((0, 0), (front, 0), (0, 0)))
    return s.reshape(nb * (front + nq), cols)


def _only(x):
    assert x.shape[0] == 1
    return x.reshape(x.shape[1:])


def kernel(x_prompt, x_sample, cache_fox_k, cache_fox_v, cache_fox_logf, state_gdn_conv, state_gdn_S,
           state_rwkv_shift, state_rwkv_S, page_table, meta_tokens, ln_mix, ln_mlp, ln_final,
           w_in0, gdn_conv_w, gdn_A_log, gdn_dt_bias, gdn_norm_w, fox_b_f, w_out0,
           rwkv_mu, rwkv_w0, rwkv_w1, rwkv_w2, rwkv_a0, rwkv_a1, rwkv_a2, rwkv_g1, rwkv_g2,
           rwkv_k_k, rwkv_k_a, rwkv_r_k, rwkv_w_r, rwkv_w_k, rwkv_w_v, rwkv_w_o, rwkv_ln_w, rwkv_ln_b,
           w_up, w_down):
    D = D_MODEL
    assert x_prompt.shape[0] == 1 and x_prompt.shape[2] == D
    seq = x_prompt.shape[1]
    nb, nq = x_sample.shape[0], x_sample.shape[1]
    npages = page_table.shape[1]
    tprompt = N_META + seq
    pad = (-tprompt) % LANES
    tp = tprompt + pad
    ns = nb * nq
    R = -(-(tp + ns) // ROW_TILE) * ROW_TILE
    CS = SUBLANES
    front = CS - nq
    assert 3 <= front

    x0 = jnp.concatenate([jnp.zeros((pad, D), F32), meta_tokens.astype(F32), x_prompt[0],
                          x_sample.reshape(ns, D), jnp.zeros((R - tp - ns, D), F32)], axis=0)

    w_in = w_in0[0]
    o_z = GDN_QKV
    o_a = o_z + GDN_QK
    o_b = o_a + GDN_HEADS
    o_fq = o_b + GDN_HEADS
    o_fk = o_fq + FOX_W
    o_fv = o_fk + FOX_W
    o_ff = o_fv + FOX_W
    o_fz = o_ff + FOX_HEADS
    w_big = jnp.concatenate([w_in[:, :o_a], w_in[:, o_fq:o_ff], w_in[:, o_fz:]], axis=1).astype(BF16)
    w_small = jnp.concatenate([w_in[:, o_a:o_fq], w_in[:, o_ff:o_fz],
                               jnp.zeros((D, LANES - 3 * GDN_HEADS), F32)], axis=1).astype(BF16)

    h0 = rmsnorm(x0, ln_mix[0], BF16)
    p = matmul(h0, w_big, name="in_proj")
    ps = matmul(h0, w_small, name="in_proj_small")

    alog_pad = _pad_lanes(gdn_A_log[0], LANE_A)
    dt_pad = _pad_lanes(gdn_dt_bias[0], LANE_A)
    bf_pad = _pad_lanes(fox_b_f[0], LANE_F)
    conv_w = gdn_conv_w[0]
    norm_w = gdn_norm_w[0].reshape(1, GDN_D)

    GC = 64
    o_gdn_p, s_gdn_p = gdn(p, p, ps, COL_Z // GDN_QK, conv_w, alog_pad, dt_pad, norm_w,
                           jnp.zeros((1, GDN_HEADS, GDN_D, GDN_D), F32), None,
                           B=1, NC=tp // GC, C=GC, first_valid=pad, name="gdn_prompt", out_rows=R)
    qkv_ext = _sample_rows(p[:, :GDN_QKV], tp, nb, nq, front)
    z_ext = _sample_rows(p[:, COL_Z:COL_Z + GDN_QK], tp, nb, nq, front)
    ps_ext = _sample_rows(ps, tp, nb, nq, front)
    o_gdn_s, s_gdn_s = gdn(qkv_ext, z_ext, ps_ext, 0, conv_w, alog_pad, dt_pad, norm_w,
                           _only(state_gdn_S), state_gdn_conv, B=nb, NC=1, C=CS, first_valid=front, name="gdn_sample")
    o_gdn_s = o_gdn_s.reshape(nb, CS, GDN_QK)[:, front:].reshape(ns, GDN_QK)

    lf, cq, ck = fox_prep(ps, bf_pad, pad)
    o_fox_p = fox_prompt(p, cq, ck, tp, pad, R)
    pt_flat = page_table.reshape(-1).astype(jnp.int32)
    lf_s = lf[tp:tp + ns, LANE_F:LANE_F + FOX_HEADS].reshape(nb, nq, FOX_HEADS)
    lfn = jnp.tile(jnp.swapaxes(lf_s, 1, 2), (1, nq, 1))
    lfn = jnp.pad(lfn, ((0, 0), (0, 0), (0, SUBLANES - nq)))
    pool = cache_fox_k.shape[1]
    page = cache_fox_k.shape[2]
    cache_k = _only(cache_fox_k).reshape(pool, page * FOX_HEADS, FOX_DH)
    cache_v = _only(cache_fox_v).reshape(pool, page * FOX_HEADS, FOX_DH)
    cache_lft = jnp.swapaxes(_only(cache_fox_logf), 1, 2)
    def sample_heads(col):
        return p[tp:tp + ns, col:col + FOX_W].reshape(nb, nq * FOX_HEADS, FOX_DH)

    o_fox_s = fox_sample(pt_flat, sample_heads(COL_FQ), cache_k, cache_v, cache_lft, lfn,
                         sample_heads(COL_FK), sample_heads(COL_FV), sample_heads(COL_FZ), nb, npages)
    o_fox_s = o_fox_s.reshape(ns, FOX_W).astype(BF16)

    def with_sample_rows(full, sample):
        full = lax.dynamic_update_slice(full, sample, (tp, 0))
        return lax.dynamic_update_slice(full, jnp.zeros((R - tp - ns, full.shape[1]), full.dtype), (tp + ns, 0))

    x1 = matmul([with_sample_rows(o_gdn_p, o_gdn_s), with_sample_rows(o_fox_p, o_fox_s)], w_out0, layer=0,
                res=x0, name="out_proj")
    u0 = matmul(rmsnorm(x1, ln_mlp[0], BF16), w_up, layer=0, act="relu2", out_dtype=BF16, name="mlp_up0")
    x2 = matmul(u0, w_down, layer=0, res=x1, name="mlp_down0")

    h1 = rmsnorm(x2, ln_mix[1], F32)
    h1_s = h1[tp:tp + ns].reshape(nb, nq, D)
    shift_rows = jnp.concatenate([jnp.repeat(state_rwkv_shift[0].astype(F32), nq, axis=0),
                                  jnp.zeros((R - tp - ns, D), F32)], axis=0)
    xr, xw, xk, xv, xa, xg = rwkv_mix(h1, shift_rows, rwkv_mu[0], tp, ns, nq)

    def pad_cols(w):
        return jnp.pad(w, ((0, 0), (0, LANES - w.shape[1]))).astype(BF16)

    def pad_rows(w):
        return jnp.pad(w, ((0, LANES - w.shape[0]), (0, 0))).astype(BF16)

    r_ = matmul(xr, rwkv_w_r, layer=0, name="rwkv_r")
    k_ = matmul(xk, rwkv_w_k, layer=0, name="rwkv_k")
    v_ = matmul(xv, rwkv_w_v, layer=0, name="rwkv_v")
    wl = matmul(matmul(xw, pad_cols(rwkv_w1[0]), act="tanh", out_dtype=BF16, name="rwkv_w1"), pad_rows(rwkv_w2[0]), name="rwkv_w2")
    al = matmul(matmul(xa, pad_cols(rwkv_a1[0]), out_dtype=BF16, name="rwkv_a1"), pad_rows(rwkv_a2[0]), name="rwkv_a2")
    g_ = matmul(matmul(xg, rwkv_g1, layer=0, act="sigmoid", out_dtype=BF16, name="rwkv_g1"), rwkv_g2, layer=0, name="rwkv_g2")

    prm = jnp.stack([rwkv_w0[0], rwkv_a0[0], rwkv_k_k[0], rwkv_k_a[0], rwkv_r_k[0].reshape(D), rwkv_ln_w[0], rwkv_ln_b[0],
                     jnp.zeros((D,), F32)], axis=0).astype(F32)
    RC = 64
    o_rw_p, s_rw_p = rwkv(r_, k_, v_, wl, al, g_, prm, jnp.zeros((1, 2 * RWKV_PAIRS, RWKV_HEAD, RWKV_HEAD), F32),
                          B=1, NC=tp // RC, C=RC, first_valid=pad, name="rwkv_prompt", out_rows=R)
    xs = jnp.stack([t[tp:tp + ns] for t in (r_, k_, v_, wl, al, g_)])
    xs = jnp.transpose(xs.reshape(6, nb, nq, D), (0, 2, 3, 1))
    s0_l = jnp.transpose(_only(state_rwkv_S), (1, 2, 3, 0))
    o_l, s_l = rwkv_lanes(xs, prm.reshape(SUBLANES, D, 1), s0_l)
    o_rw_s = jnp.transpose(o_l, (2, 0, 1)).reshape(ns, D).astype(BF16)
    s_rw_s = jnp.transpose(s_l, (3, 0, 1, 2))
    x3 = matmul(with_sample_rows(o_rw_p, o_rw_s), rwkv_w_o, layer=0, res=x2, name="rwkv_o")
    u1 = matmul(rmsnorm(x3, ln_mlp[1], BF16), w_up, layer=1, act="relu2", out_dtype=BF16, name="mlp_up1")
    x4 = matmul(u1, w_down, layer=1, res=x3, name="mlp_down1")
    y = rmsnorm(x4, ln_final, F32)

    r0 = pad
    y_prompt = y[r0 + N_META:tp].reshape(1, seq, D)
    y_sample = y[tp:tp + ns].reshape(nb, nq, D)

    def kv_rows(col):
        blk = p[:, col:col + FOX_W]
        return (blk[r0:tp].reshape(1, 1, tprompt, FOX_HEADS, FOX_DH),
                blk[tp:tp + ns].reshape(1, nb, nq, FOX_HEADS, FOX_DH))

    fk_p, fk_s = kv_rows(COL_FK)
    fv_p, fv_s = kv_rows(COL_FV)
    lf8 = lf[:, LANE_F:LANE_F + FOX_HEADS]
    lf_p = lf8[r0:tp].reshape(1, 1, tprompt, FOX_HEADS)
    lf_sm = lf8[tp:tp + ns].reshape(1, nb, nq, FOX_HEADS)
    cb_p = p[tp - 3:tp, :GDN_QKV].reshape(1, 1, 3, GDN_QKV)
    cb_s = p[tp:tp + ns, :GDN_QKV].reshape(nb, nq, GDN_QKV)[:, nq - 3:][None]
    gs_p = s_gdn_p[None]
    gs_s = s_gdn_s[None]
    sh_p = h1[tp - 1].reshape(1, 1, D)
    sh_s = h1_s[:, nq - 1][None]
    rs_p = s_rw_p[None]
    rs_s = s_rw_s[None]
    return (y_prompt, y_sample, fk_p, fk_s, fv_p, fv_s, lf_p, lf_sm, cb_p, cb_s, gs_p, gs_s, sh_p, sh_s, rs_p, rs_s)
```

```python
import functools
import math

import jax
import jax.numpy as jnp
from jax import lax
from jax.experimental import pallas as pl
from jax.experimental.pallas import tpu as pltpu

F32 = jnp.float32
BF16 = jnp.bfloat16

D_MODEL = 2048
N_META = 16
GDN_HEADS = 8
GDN_D = 128
GDN_QK = GDN_HEADS * GDN_D
GDN_QKV = 3 * GDN_QK
FOX_HEADS = 8
FOX_DH = 128
FOX_W = FOX_HEADS * FOX_DH
RWKV_HEAD = 64
RWKV_PAIRS = D_MODEL // (2 * RWKV_HEAD)
NORM_EPS = 1e-6
L2_EPS = 1e-6
GN_EPS = 64e-5
NEG_INF = -1e30

LANES = 128
SUBLANES = 8
ROW_TILE = 640
MM_ROW_TILE = 1792
MM_WIDE_MIN_COLS = 8192
LOG2E = 1.4426950408889634
VMEM_LIMIT = 48 * 1024 * 1024
FOX_HEADS_PER_STEP = 4
FOX_GROUP = 16
RWKV_PREP_PAIRS = 16

COL_Z = GDN_QKV
COL_FQ = COL_Z + GDN_QK
COL_FK = COL_FQ + FOX_W
COL_FV = COL_FK + FOX_W
COL_FZ = COL_FV + FOX_W
LANE_A = 0
LANE_B = 8
LANE_F = 16
PRM_W0, PRM_A0, PRM_KK, PRM_KA, PRM_RK, PRM_LNW, PRM_LNB = range(7)


def _pick(n, cands):
    for c in cands:
        if n % c == 0:
            return c
    raise ValueError(f"no tile for {n}")


def _cparams(sem):
    return pltpu.CompilerParams(dimension_semantics=sem, vmem_limit_bytes=VMEM_LIMIT)


def _dot(a, b):
    return jnp.dot(a, b, preferred_element_type=F32)


def _dot_nt(a, b):
    return lax.dot_general(a, b, (((1,), (1,)), ((), ())), preferred_element_type=F32)


def _dot_tn(a, b):
    return lax.dot_general(a, b, (((0,), (0,)), ((), ())), preferred_element_type=F32)


def _bf(x):
    return x.astype(BF16)


def _softplus(x):
    return jnp.maximum(x, 0.0) + jnp.log(1.0 + jnp.exp(-jnp.abs(x)))


def _sigmoid(x):
    return 1.0 / (1.0 + jnp.exp(-x))


def _iota2(shape, dim):
    return lax.broadcasted_iota(jnp.int32, shape, dim)


def _div(x, n):
    assert n & (n - 1) == 0
    return x >> (n.bit_length() - 1)


def _split2(x):
    hi = x.astype(BF16)
    return hi, (x - hi.astype(F32)).astype(BF16)


def _split3(x):
    p1 = x.astype(BF16)
    rem = x - p1.astype(F32)
    p2 = rem.astype(BF16)
    return p1, p2, (rem - p2.astype(F32)).astype(BF16)


def _dot3(a, b):
    m = a.shape[0]
    ah, al = _split2(a)
    bh, bl = _split2(b)
    top = _dot(jnp.concatenate([ah, al], axis=0), bh)
    return (top[:m] + top[m:]) + _dot(ah, bl)


def _dot_lhs01(a01, b):
    a = a01.astype(BF16)
    b1, b2, b3 = _split3(b)
    return _dot(a, b1) + (_dot(a, b2) + _dot(a, b3))


def _dot_rhs01(a, b01):
    b = b01.astype(BF16)
    a1, a2, a3 = _split3(a)
    m = a.shape[0]
    out = _dot(jnp.concatenate([a1, a2, a3], axis=0), b)
    return out[:m] + (out[m:2 * m] + out[2 * m:])


def _tri_inv_steps(n_mat, n):
    size = n_mat.shape[0]
    eye = (_iota2((size, size), 0) == _iota2((size, size), 1)).astype(F32)
    t = eye + n_mat
    if n <= 2:
        return t
    p = _dot(_bf(n_mat), _bf(n_mat))
    yield
    m = 2
    while m < n:
        p_bf = _bf(p)
        if 2 * m < n:
            both = _dot(_bf(jnp.concatenate([p, t], axis=0)), p_bf)
            p = both[:size]
            t = t + both[size:]
        else:
            t = t + _dot(_bf(t), p_bf)
        m *= 2
        yield
    return t


def _run_interleaved(gens):
    results = [None] * len(gens)
    live = list(enumerate(gens))
    while live:
        nxt = []
        for idx, gen in live:
            try:
                next(gen)
                nxt.append((idx, gen))
            except StopIteration as stop:
                results[idx] = stop.value
        live = nxt
    return results


def _rmsnorm_kernel(x_ref, w_ref, o_ref):
    x = x_ref[...]
    ms = jnp.mean(x * x, axis=-1, keepdims=True)
    o_ref[...] = (x * lax.rsqrt(ms + NORM_EPS) * w_ref[...]).astype(o_ref.dtype)


def rmsnorm(x, w, out_dtype):
    m, d = x.shape
    tm = _pick(m, (640, 320, 256, 128))
    return pl.pallas_call(
        _rmsnorm_kernel,
        grid=(m // tm,),
        in_specs=[pl.BlockSpec((tm, d), lambda i: (i, 0)), pl.BlockSpec((1, d), lambda i: (0, 0))],
        out_specs=pl.BlockSpec((tm, d), lambda i: (i, 0)),
        out_shape=jax.ShapeDtypeStruct((m, d), out_dtype),
        compiler_params=_cparams(("parallel",)),
        name="rmsnorm",
    )(x, w.reshape(1, d))


def _mm_kernel(nk, nx, act, has_res, *refs):
    x_refs, w_ref = refs[:nx], refs[nx]
    r_ref = refs[nx + 1] if has_res else None
    o_ref = refs[nx + 1 + has_res]

    def product():
        if nx == 1:
            return _dot(x_refs[0][...], _bf(w_ref[...]))
        acc, k0 = None, 0
        for x_ref in x_refs:
            kw = x_ref.shape[1]
            part = _dot(x_ref[...], _bf(w_ref[k0:k0 + kw, :]))
            acc = part if acc is None else acc + part
            k0 += kw
        return acc

    def finish(acc):
        if act == "relu2":
            acc = jnp.square(jnp.maximum(acc, 0.0))
        elif act == "tanh":
            acc = jnp.tanh(acc)
        elif act == "sigmoid":
            acc = _sigmoid(acc)
        if has_res:
            acc = r_ref[...] + acc
        o_ref[...] = acc.astype(o_ref.dtype)

    if nk == 1:
        finish(product())
    else:
        assert act is None and o_ref.dtype == F32
        k = pl.program_id(2)

        @pl.when(k == 0)
        def _():
            o_ref[...] = r_ref[...] if has_res else jnp.zeros(o_ref.shape, F32)

        o_ref[...] += product()


def matmul(x, w, *, layer=None, act=None, res=None, out_dtype=F32, name="matmul"):
    xs = list(x) if isinstance(x, (list, tuple)) else [x]
    m = xs[0].shape[0]
    k = sum(xi.shape[1] for xi in xs)
    n = w.shape[-1]
    tm = _pick(m, (MM_ROW_TILE, ROW_TILE, 512, 256, 128))
    tk = k if k <= 2048 else 2048
    nk = k // tk
    wide_ok = nk == 1 and res is None and n >= MM_WIDE_MIN_COLS
    tn = _pick(n, ((1024,) if wide_ok else ()) + (512, 256, 128))
    assert len(xs) == 1 or nk == 1
    in_specs = [pl.BlockSpec((tm, tk if len(xs) == 1 else xi.shape[1]), lambda i, j, kk: (i, kk)) for xi in xs]
    if w.ndim == 3:
        in_specs.append(pl.BlockSpec((None, tk, tn), lambda i, j, kk: (layer, kk, j)))
    else:
        in_specs.append(pl.BlockSpec((tk, tn), lambda i, j, kk: (kk, j)))
    args = xs + [w]
    if res is not None:
        in_specs.append(pl.BlockSpec((tm, tn), lambda i, j, kk: (i, j)))
        args.append(res)
    return pl.pallas_call(
        functools.partial(_mm_kernel, nk, len(xs), act, res is not None),
        grid=(m // tm, n // tn, nk),
        in_specs=in_specs,
        out_specs=pl.BlockSpec((tm, tn), lambda i, j, kk: (i, j)),
        out_shape=jax.ShapeDtypeStruct((m, n), out_dtype),
        compiler_params=_cparams(("parallel", "parallel", "arbitrary")),
        name=name,
    )(*args)


def _gdn_kernel(C, NC, first_valid, has_conv, qkv_ref, z_ref, ps_ref, cw_ref, alog_ref, dt_ref, nw_ref, s0_ref,
                *rest):
    conv_ref = rest[0] if has_conv else None
    o_ref, sout_ref, S, ext = rest[1:] if has_conv else rest
    c = pl.program_id(1)

    @pl.when(c == 0)
    def _():
        S[...] = s0_ref[0]
        ext[0:SUBLANES, :] = jnp.zeros((SUBLANES, GDN_QKV), F32)

    x = qkv_ref[...]
    ext[SUBLANES:SUBLANES + C, :] = x
    if has_conv:
        assert NC == 1 and 3 <= first_valid < C
        ext[SUBLANES + first_valid - 3:SUBLANES + first_valid, :] = conv_ref[0, 0]
    cw = cw_ref[...]
    y = ((ext[5:5 + C, :] * cw[0:1] + ext[6:6 + C, :] * cw[1:2]) + ext[7:7 + C, :] * cw[2:3]) + ext[8:8 + C, :] * cw[3:4]
    ext[0:SUBLANES, :] = x[C - SUBLANES:C]
    y = y * _sigmoid(y)

    rows = c * C + _iota2((C, 1), 0)
    valid = rows >= first_valid
    ps = ps_ref[...]
    g_all = jnp.where(valid, -jnp.exp(alog_ref[...]) * _softplus(ps + dt_ref[...]), 0.0)
    beta_all = jnp.where(valid, _sigmoid(ps), 0.0)
    ri = _iota2((C, C), 0)
    ci = _iota2((C, C), 1)
    causal = ri >= ci
    strict = ri > ci
    gc = _dot_lhs01(causal.astype(F32), g_all)
    gct = _dot_rhs01(g_all.T, (ri <= ci).astype(F32))
    nw = nw_ref[...]

    def head_steps(h):
        sl = slice(h * GDN_D, (h + 1) * GDN_D)
        qh = y[:, sl]
        kh = y[:, GDN_QK + h * GDN_D:GDN_QK + (h + 1) * GDN_D]
        vh = jnp.where(valid, y[:, 2 * GDN_QK + h * GDN_D:2 * GDN_QK + (h + 1) * GDN_D], 0.0)
        qh = jnp.where(valid, qh * lax.rsqrt(jnp.sum(qh * qh, axis=-1, keepdims=True) + L2_EPS) * GDN_D ** -0.5, 0.0)
        kh = jnp.where(valid, kh * lax.rsqrt(jnp.sum(kh * kh, axis=-1, keepdims=True) + L2_EPS), 0.0)
        bcol = beta_all[:, LANE_B + h:LANE_B + h + 1]
        gcol = gc[:, LANE_A + h:LANE_A + h + 1]
        grow = gct[LANE_A + h:LANE_A + h + 1, :]
        glast = gc[C - 1:C, LANE_A + h:LANE_A + h + 1]
        diff = gcol - grow
        decay = jnp.where(causal, jnp.exp(jnp.where(causal, diff, 0.0)), 0.0)
        kb = kh * bcol
        a_mat = jnp.where(strict, _dot_nt(_bf(kb), _bf(kh)) * decay, 0.0)
        qk = _dot_nt(_bf(qh), _bf(kh)) * decay
        eg = jnp.exp(gcol)
        rhs = jnp.concatenate([vh * bcol, kb * eg], axis=1)
        q_dec = qh * eg
        k_dec = kh * jnp.exp(glast - gcol)
        yield
        t_mat = yield from _tri_inv_steps(-a_mat, C)
        sol = _dot3(t_mat, rhs)
        u = sol[:, :GDN_D]
        w = sol[:, GDN_D:]
        yield
        s_h = S[h]
        s_bf = _bf(s_h)
        v_new = u - _dot(_bf(w), s_bf)
        o_state = _dot(_bf(q_dec), s_bf)
        yield
        o = o_state + _dot(_bf(qk), _bf(v_new))
        S[h] = s_h * jnp.exp(glast) + _dot_tn(_bf(k_dec), _bf(v_new))
        yield
        on = o * lax.rsqrt(jnp.mean(o * o, axis=-1, keepdims=True) + NORM_EPS) * nw
        zh = z_ref[:, sl]
        o_ref[:, sl] = (on * (zh * _sigmoid(zh))).astype(o_ref.dtype)

    _run_interleaved([head_steps(h) for h in range(GDN_HEADS)])

    @pl.when(c == NC - 1)
    def _():
        sout_ref[0] = S[...]


def gdn(qkv_arr, z_arr, ps_arr, z_col, conv_w, alog_pad, dt_pad, norm_w, s0, conv0, *, B, NC, C, first_valid, name,
        out_rows=None):
    rows = B * NC * C
    has_conv = conv0 is not None
    extra_specs = [pl.BlockSpec((1, 1, 3, GDN_QKV), lambda b, c: (0, b, 0, 0))] if has_conv else []
    extra_args = [conv0] if has_conv else []
    return pl.pallas_call(
        functools.partial(_gdn_kernel, C, NC, first_valid, has_conv),
        grid=(B, NC),
        in_specs=[
            pl.BlockSpec((C, GDN_QKV), lambda b, c: (b * NC + c, 0)),
            pl.BlockSpec((C, GDN_QK), lambda b, c: (b * NC + c, z_col)),
            pl.BlockSpec((C, LANES), lambda b, c: (b * NC + c, 0)),
            pl.BlockSpec((4, GDN_QKV), lambda b, c: (0, 0)),
            pl.BlockSpec((1, LANES), lambda b, c: (0, 0)),
            pl.BlockSpec((1, LANES), lambda b, c: (0, 0)),
            pl.BlockSpec((1, GDN_D), lambda b, c: (0, 0)),
            pl.BlockSpec((1, GDN_HEADS, GDN_D, GDN_D), lambda b, c: (b, 0, 0, 0)),
        ] + extra_specs,
        out_specs=[
            pl.BlockSpec((C, GDN_QK), lambda b, c: (b * NC + c, 0)),
            pl.BlockSpec((1, GDN_HEADS, GDN_D, GDN_D), lambda b, c: (b, 0, 0, 0)),
        ],
        out_shape=[
            jax.ShapeDtypeStruct((out_rows or rows, GDN_QK), BF16),
            jax.ShapeDtypeStruct((B, GDN_HEADS, GDN_D, GDN_D), F32),
        ],
        scratch_shapes=[pltpu.VMEM((GDN_HEADS, GDN_D, GDN_D), F32), pltpu.VMEM((C + SUBLANES, GDN_QKV), F32)],
        compiler_params=_cparams(("parallel", "arbitrary")),
        name=name,
    )(qkv_arr, z_arr, ps_arr, conv_w, alog_pad, dt_pad, norm_w, s0, *extra_args)


def _fox_prep_kernel(tb, first_valid, ps_ref, bf_ref, lf_ref, cq_ref, ck_ref, carry):
    i = pl.program_id(0)

    @pl.when(i == 0)
    def _():
        carry[...] = jnp.zeros(carry.shape, F32)

    x = ps_ref[...] + bf_ref[...]
    rows = i * tb + _iota2((tb, 1), 0)
    lf = jnp.where(rows >= first_valid, jnp.minimum(x, 0.0) - jnp.log(1.0 + jnp.exp(-jnp.abs(x))), 0.0)
    tri = (_iota2((tb, tb), 0) >= _iota2((tb, tb), 1)).astype(F32)
    c = _dot_lhs01(tri, lf) + carry[0:1, :]
    carry[...] = jnp.broadcast_to(c[tb - 1:tb, :], carry.shape)
    lf_ref[...] = lf
    c2 = c * LOG2E
    for h in range(FOX_HEADS):
        cq_ref[h] = jnp.broadcast_to(c2[:, LANE_F + h:LANE_F + h + 1], (tb, LANES))
    ck_ref[...] = c2.T[LANE_F:LANE_F + FOX_HEADS, :]


def fox_prep(ps, bf_pad, first_valid):
    r = ps.shape[0]
    tb = _pick(r, (ROW_TILE, 512, 256, 128))
    return pl.pallas_call(
        functools.partial(_fox_prep_kernel, tb, first_valid),
        grid=(r // tb,),
        in_specs=[pl.BlockSpec((tb, LANES), lambda i: (i, 0)), pl.BlockSpec((1, LANES), lambda i: (0, 0))],
        out_specs=[
            pl.BlockSpec((tb, LANES), lambda i: (i, 0)),
            pl.BlockSpec((FOX_HEADS, tb, LANES), lambda i: (0, i, 0)),
            pl.BlockSpec((FOX_HEADS, tb), lambda i: (0, i)),
        ],
        out_shape=[
            jax.ShapeDtypeStruct((r, LANES), F32),
            jax.ShapeDtypeStruct((FOX_HEADS, r, LANES), F32),
            jax.ShapeDtypeStruct((FOX_HEADS, r), F32),
        ],
        scratch_shapes=[pltpu.VMEM((SUBLANES, LANES), F32)],
        compiler_params=_cparams(("arbitrary",)),
        name="fox_prep",
    )(ps, bf_pad)


def _fox_flash_kernel(tq, first_valid, qi_ref, ki_ref, qt_ref, k_ref, vt_ref, ck_ref, cq_ref, fz_ref, o_ref,
                      m_s, l_s, acc_s, t_s):
    step = pl.program_id(1)
    qi = qi_ref[step]
    ki = ki_ref[step]
    reps = tq // LANES
    heads = qt_ref.shape[0]

    @pl.when(ki == 0)
    def _():
        m_s[...] = jnp.full(m_s.shape, NEG_INF, F32)
        l_s[...] = jnp.zeros(l_s.shape, F32)
        acc_s[...] = jnp.zeros(acc_s.shape, F32)

    def head_steps(h, masked):
        cols = slice(h * FOX_DH, (h + 1) * FOX_DH)
        qt = _bf(qt_ref[h] * (FOX_DH ** -0.5 * LOG2E))
        cq = cq_ref[h]
        m_prev = m_s[h, 0:1, :]
        kb = LANES
        groups = kb // SUBLANES
        mx = None
        for b in range(tq // kb):
            rs = slice(b * kb, (b + 1) * kb)
            t = _dot(_bf(k_ref[rs, cols]), qt) - jnp.concatenate([ck_ref[h, rs, :]] * reps, axis=1)
            if masked:
                kpos = ki * tq + b * kb + _iota2((kb, tq), 0)
                qpos = qi * tq + _iota2((kb, tq), 1)
                t = jnp.where((kpos <= qpos) & (kpos >= first_valid), t, NEG_INF)
            t_s[h, rs, :] = t
            part = jnp.max(t.reshape(groups, SUBLANES, tq), axis=0)
            mx = part if mx is None else jnp.maximum(mx, part)
            yield
        m_new = jnp.maximum(m_prev, jnp.max(mx, axis=0, keepdims=True) + cq)
        shift = cq - m_new
        alpha = jnp.exp2(m_prev - m_new)
        lsum = None
        pv = None
        for b in range(tq // kb):
            rs = slice(b * kb, (b + 1) * kb)
            p = jnp.exp2(t_s[h, rs, :] + shift)
            part = jnp.sum(p.reshape(groups, SUBLANES, tq), axis=0)
            lsum = part if lsum is None else lsum + part
            d = _dot(_bf(vt_ref[h, :, rs]), _bf(p))
            pv = d if pv is None else pv + d
            yield
        l_new = alpha * l_s[h, 0:1, :] + jnp.sum(lsum, axis=0, keepdims=True)
        l_s[h] = jnp.broadcast_to(l_new, (SUBLANES, tq))
        acc_s[h] = alpha * acc_s[h] + pv
        m_s[h] = jnp.broadcast_to(m_new, (SUBLANES, tq))

    def accumulate(masked):
        _run_interleaved([head_steps(h, masked) for h in range(heads)])

    edge = (ki == qi) | (ki == 0)
    pl.when(edge)(lambda: accumulate(True))
    pl.when(jnp.logical_not(edge))(lambda: accumulate(False))

    @pl.when(ki == qi)
    def _():
        rows = qi * tq + _iota2((tq, 1), 0)
        for h in range(heads):
            cols = slice(h * FOX_DH, (h + 1) * FOX_DH)
            o = (acc_s[h] / l_s[h, 0:1, :]).T * _sigmoid(fz_ref[:, cols])
            o_ref[:, cols] = jnp.where(rows >= first_valid, o, 0.0).astype(o_ref.dtype)


def fox_prompt(p, c_rep, c_row, tp, first_valid, out_rows):
    tq = _pick(tp, (ROW_TILE, 512, 384, 256, 128))
    nq = tp // tq
    assert first_valid < tq
    pairs = [(qi, ki) for qi in range(nq) for ki in range(qi + 1)]
    qi_arr = jnp.array([a for a, _ in pairs], jnp.int32)
    ki_arr = jnp.array([b for _, b in pairs], jnp.int32)
    hp = FOX_HEADS_PER_STEP
    wid = hp * FOX_DH
    cb = lambda col, g: col // wid + g

    def heads_t(col, dtype):
        return jnp.transpose(p[:tp, col:col + FOX_W].reshape(tp, FOX_HEADS, FOX_DH), (1, 2, 0)).astype(dtype)

    return pl.pallas_call(
        functools.partial(_fox_flash_kernel, tq, first_valid),
        grid_spec=pltpu.PrefetchScalarGridSpec(
            num_scalar_prefetch=2,
            grid=(FOX_HEADS // hp, len(pairs)),
            in_specs=[
                pl.BlockSpec((hp, FOX_DH, tq), lambda g, t, qa, ka: (g, 0, qa[t])),
                pl.BlockSpec((tq, wid), lambda g, t, qa, ka: (ka[t], cb(COL_FK, g))),
                pl.BlockSpec((hp, FOX_DH, tq), lambda g, t, qa, ka: (g, 0, ka[t])),
                pl.BlockSpec((hp, tq, LANES), lambda g, t, qa, ka: (g, ka[t], 0)),
                pl.BlockSpec((hp, 1, tq), lambda g, t, qa, ka: (g, 0, qa[t])),
                pl.BlockSpec((tq, wid), lambda g, t, qa, ka: (qa[t], cb(COL_FZ, g))),
            ],
            out_specs=pl.BlockSpec((tq, wid), lambda g, t, qa, ka: (qa[t], g)),
            scratch_shapes=[pltpu.VMEM((hp, SUBLANES, tq), F32), pltpu.VMEM((hp, SUBLANES, tq), F32),
                            pltpu.VMEM((hp, FOX_DH, tq), F32), pltpu.VMEM((hp, tq, tq), F32)],
        ),
        out_shape=jax.ShapeDtypeStruct((out_rows, FOX_W), BF16),
        compiler_params=_cparams(("parallel", "arbitrary")),
        name="fox_prompt",
    )(qi_arr, ki_arr, heads_t(COL_FQ, F32), p, heads_t(COL_FV, BF16), c_rep, c_row.reshape(FOX_HEADS, 1, -1), p)


def _fox_sample_kernel(G, ngroups, pt_ref, q_ref, *refs):
    k_refs, v_refs, lft_refs = refs[0:G], refs[G:2 * G], refs[2 * G:3 * G]
    lfn_ref, kn_ref, vn_ref, fz_ref, o_ref, cq_s, cn_s, carry_s, m_s, l_s, acc_s = refs[3 * G:]
    jg = pl.program_id(1)
    rows = q_ref.shape[1]
    page = lft_refs[0].shape[2]
    flat = page * FOX_HEADS
    nn = lfn_ref.shape[2]
    scale = FOX_DH ** -0.5

    @pl.when(jg == 0)
    def _():
        upper = (_iota2((nn, nn), 0) <= _iota2((nn, nn), 1)).astype(F32)
        cn = _dot_rhs01(lfn_ref[0], upper)
        cn_s[...] = cn
        own_q = _iota2((rows, nn), 1) == _div(_iota2((rows, nn), 0), FOX_HEADS)
        cq_s[...] = jnp.broadcast_to(jnp.sum(jnp.where(own_q, cn, 0.0), axis=-1, keepdims=True), cq_s.shape)
        carry_s[...] = jnp.zeros(carry_s.shape, F32)
        m_s[...] = jnp.full(m_s.shape, NEG_INF, F32)
        l_s[...] = jnp.zeros(l_s.shape, F32)
        acc_s[...] = jnp.zeros(acc_s.shape, F32)

    def update(ts, vals_bf):
        cq = cq_s[:, 0:1]
        m_prev = m_s[:, 0:1]
        t_max = functools.reduce(jnp.maximum, ts)
        m_new = jnp.maximum(m_prev, jnp.max(t_max, axis=-1, keepdims=True) + cq)
        shift = cq - m_new
        ps = [jnp.exp(t + shift) for t in ts]
        alpha = jnp.exp(m_prev - m_new)
        l_s[...] = alpha * l_s[...] + jnp.sum(functools.reduce(jnp.add, ps), axis=-1, keepdims=True)
        pv = functools.reduce(jnp.add, [_dot(_bf(p), v) for p, v in zip(ps, vals_bf)])
        acc_s[...] = alpha * acc_s[...] + pv
        m_s[...] = jnp.broadcast_to(m_new, m_s.shape)

    own = (_iota2((rows, flat), 0) & (FOX_HEADS - 1)) == (_iota2((rows, flat), 1) & (FOX_HEADS - 1))
    later = (_iota2((page, flat), 0) > _div(_iota2((page, flat), 1), FOX_HEADS)).astype(BF16)
    head_col = _iota2((FOX_HEADS, flat), 0) == (_iota2((FOX_HEADS, flat), 1) & (FOX_HEADS - 1))
    carry = carry_s[:, 0:1]
    q_bf = _bf(q_ref[0])
    lfts = [lft_refs[g][0] for g in range(G)]
    suffixes = _dot_rhs01(jnp.concatenate(lfts, axis=0), later)
    ts = []
    for g in range(G):
        suffix = suffixes[g * FOX_HEADS:(g + 1) * FOX_HEADS]
        d = jnp.sum(jnp.where(head_col, suffix + carry, 0.0), axis=0, keepdims=True)
        carry = carry + jnp.sum(lfts[g], axis=-1, keepdims=True)
        s = _dot_nt(q_bf, _bf(k_refs[g][0])) * scale
        ts.append(jnp.where(own, s + d, NEG_INF))
    update(ts, [_bf(v_refs[g][0]) for g in range(G)])
    carry_s[...] = jnp.broadcast_to(carry, carry_s.shape)

    @pl.when(jg == ngroups - 1)
    def _():
        nflat = kn_ref.shape[1]
        spread = (_iota2((nn, nflat), 0) == _div(_iota2((nn, nflat), 1), FOX_HEADS)).astype(F32)
        cn_cols = _dot_rhs01(cn_s[...], spread)
        ri = _iota2((rows, nflat), 0)
        ci = _iota2((rows, nflat), 1)
        ok = ((ri & (FOX_HEADS - 1)) == (ci & (FOX_HEADS - 1))) & (_div(ci, FOX_HEADS) <= _div(ri, FOX_HEADS))
        sn = _dot_nt(_bf(q_ref[0]), _bf(kn_ref[0])) * scale
        update([jnp.where(ok, sn - cn_cols, NEG_INF)], [_bf(vn_ref[0])])
        o_ref[0] = acc_s[...] / l_s[...] * _sigmoid(fz_ref[0])


def fox_sample(page_table_flat, q_rows, cache_k, cache_v, cache_lft, lfn, kn_flat, vn_flat, fz_rows, nb, npages):
    flat = cache_k.shape[1]
    page = cache_lft.shape[2]
    rows = q_rows.shape[1]
    nn = lfn.shape[2]
    G = _pick(npages, (FOX_GROUP, 2, 1))
    ngroups = npages // G

    def page_map(g):
        return lambda b, jg, pt: (pt[b * npages + (npages - 1 - (jg * G + g))], 0, 0)

    seq = lambda b, jg, pt: (b, 0, 0)
    row_blk = pl.BlockSpec((1, rows, FOX_DH), seq)
    new_blk = pl.BlockSpec((1, kn_flat.shape[1], FOX_DH), seq)
    in_specs = [row_blk]
    in_specs += [pl.BlockSpec((1, flat, FOX_DH), page_map(g)) for g in range(G)]
    in_specs += [pl.BlockSpec((1, flat, FOX_DH), page_map(g)) for g in range(G)]
    in_specs += [pl.BlockSpec((1, FOX_HEADS, page), page_map(g)) for g in range(G)]
    in_specs += [pl.BlockSpec((1, rows, nn), seq), new_blk, new_blk, row_blk]
    return pl.pallas_call(
        functools.partial(_fox_sample_kernel, G, ngroups),
        grid_spec=pltpu.PrefetchScalarGridSpec(
            num_scalar_prefetch=1,
            grid=(nb, ngroups),
            in_specs=in_specs,
            out_specs=row_blk,
            scratch_shapes=[
                pltpu.VMEM((rows, LANES), F32),
                pltpu.VMEM((rows, nn), F32),
                pltpu.VMEM((FOX_HEADS, LANES), F32),
                pltpu.VMEM((rows, LANES), F32),
                pltpu.VMEM((rows, LANES), F32),
                pltpu.VMEM((rows, FOX_DH), F32),
            ],
        ),
        out_shape=jax.ShapeDtypeStruct((nb, rows, FOX_DH), F32),
        compiler_params=_cparams(("parallel", "arbitrary")),
        name="fox_sample",
    )(page_table_flat, q_rows, *([cache_k] * G), *([cache_v] * G), *([cache_lft] * G), lfn, kn_flat, vn_flat, fz_rows)


def _rwkv_mix_kernel(tm, tp, ns, nq, h_ref, hb_ref, st_ref, mu_ref, *o_refs):
    i = pl.program_id(0)
    h = h_ref[...]
    above = jnp.concatenate([hb_ref[SUBLANES - 1:SUBLANES, :], h[:tm - 1]], axis=0)
    row = i * tm + _iota2((tm, 1), 0)
    seq_start = (row >= tp) & (row < tp + ns) & (((row - tp) & (nq - 1)) == 0)
    prev = jnp.where(seq_start, st_ref[...], jnp.where(row == 0, 0.0, above))
    xx = prev - h
    mu = mu_ref[...]
    for j, o_ref in enumerate(o_refs):
        o_ref[...] = (h + xx * mu[j:j + 1]).astype(o_ref.dtype)


def rwkv_mix(h, shift_rows, mu, tp, ns, nq):
    m, d = h.shape
    assert nq & (nq - 1) == 0
    tm = _pick(math.gcd(m, tp), (320, 256, 128))
    first = tp // tm
    spec = pl.BlockSpec((tm, d), lambda i: (i, 0))
    return pl.pallas_call(
        functools.partial(_rwkv_mix_kernel, tm, tp, ns, nq),
        grid=(m // tm,),
        in_specs=[spec,
                  pl.BlockSpec((SUBLANES, d), lambda i: (jnp.maximum(i * (tm // SUBLANES) - 1, 0), 0)),
                  pl.BlockSpec((tm, d), lambda i: (jnp.maximum(i - first, 0), 0)),
                  pl.BlockSpec((6, d), lambda i: (0, 0))],
        out_specs=[spec] * 6,
        out_shape=[jax.ShapeDtypeStruct((m, d), BF16)] * 6,
        compiler_params=_cparams(("parallel",)),
        name="rwkv_mix",
    )(h, h, shift_rows, mu)


def _head_ones():
    return (_div(_iota2((LANES, LANES), 0), RWKV_HEAD) == _div(_iota2((LANES, LANES), 1), RWKV_HEAD)).astype(F32)


def _rwkv_chunk_terms(C, valid, r, k, v, wl, al, prm):
    HD = RWKV_HEAD
    m0 = _iota2((1, LANES), 1) < HD
    bones = _head_ones()
    w0, a0, k_k, k_a, r_k = (prm[i:i + 1] for i in (PRM_W0, PRM_A0, PRM_KK, PRM_KA, PRM_RK))
    wlog = -_softplus(-(w0 + wl)) - 0.5
    lw = jnp.where(valid, -jnp.exp(wlog), 0.0)
    a = _sigmoid(a0 + al)
    kkr = k * k_k
    kk = kkr * lax.rsqrt(_dot_rhs01(kkr * kkr, bones) + L2_EPS)
    k2 = k * (1.0 + (a - 1.0) * k_a)
    bonus = _dot_rhs01(r * k2 * r_k, bones) * v
    rm = jnp.where(valid, r, 0.0)
    k2 = jnp.where(valid, k2, 0.0)
    vm = jnp.where(valid, v, 0.0)
    av = jnp.where(valid, -kk, 0.0)
    bv = jnp.where(valid, kk * a, 0.0)

    ri = _iota2((C, C), 0)
    ci = _iota2((C, C), 1)
    yield
    cum = _dot_lhs01((ri >= ci).astype(F32), lw)
    cum_last = cum[C - 1:C, :]
    inv = jnp.exp(-cum)
    rt = rm * jnp.exp(cum)
    at = av * jnp.exp(cum - lw)
    bt = bv * inv
    kt = k2 * inv
    to_end = jnp.exp(cum_last - cum)
    b_end = bv * to_end
    k_end = k2 * to_end

    def split(x):
        return jnp.concatenate([jnp.where(m0, x, 0.0), jnp.where(m0, 0.0, x)], axis=0)

    def halves(x):
        return x[0:C] + x[C:2 * C]

    yield
    at_s = split(at)
    gram = _dot_nt(_bf(jnp.concatenate([at_s, split(rt)], axis=0)), _bf(jnp.concatenate([bt, kt], axis=0)))
    r2 = _iota2((2 * C, 2 * C), 0)
    c2 = _iota2((2 * C, 2 * C), 1)
    same = _div(r2, C) == _div(c2, C)
    bd_strict = same & (r2 > c2)
    bd_incl = same & (r2 >= c2)

    def bd(block, mask):
        return jnp.where(mask, jnp.concatenate([block, block], axis=1), 0.0)

    a_ab = bd(gram[0:2 * C, 0:C], bd_strict)
    a_ak = bd(gram[0:2 * C, C:2 * C], bd_strict)
    r_b = bd(gram[2 * C:4 * C, 0:C], bd_incl)
    r_k2 = bd(gram[2 * C:4 * C, C:2 * C], bd_incl)
    vs = split(vm)
    akv = halves(_dot(_bf(a_ak), _bf(vs)))
    yield
    t_bd = yield from _tri_inv_steps(a_ab, C)
    tw = _dot(_bf(t_bd), _bf(jnp.concatenate([at_s, split(akv)], axis=1)))
    wt = halves(tw[:, :LANES])
    ut = halves(tw[:, LANES:])
    yield
    rp = rt + halves(_dot(_bf(r_b), _bf(split(wt))))
    y0 = halves(_dot(_bf(jnp.concatenate([r_b, r_k2], axis=1)), _bf(jnp.concatenate([split(ut), vs], axis=0))))
    eye = (_iota2((LANES, LANES), 0) == _iota2((LANES, LANES), 1)).astype(F32)
    m_mat = eye * jnp.exp(cum_last) + bones * _dot_tn(_bf(b_end), _bf(wt))
    n_mat = bones * _dot_tn(_bf(jnp.concatenate([b_end, k_end], axis=0)), _bf(jnp.concatenate([ut, vm], axis=0)))
    return rp, y0, bonus, m_mat, n_mat


def _rwkv_prep_kernel(C, chunk_of_step, first_valid, PG, r_ref, k_ref, v_ref, wl_ref, al_ref, prm_ref,
                      rp_ref, y0_ref, bo_ref, m_ref, n_ref):
    c = pl.program_id(0) if chunk_of_step else 0
    valid = (c * C + _iota2((C, 1), 0)) >= first_valid
    sls = [slice(i * LANES, (i + 1) * LANES) for i in range(PG)]
    terms = _run_interleaved([
        _rwkv_chunk_terms(C, valid, r_ref[:, sl], k_ref[:, sl], v_ref[:, sl], wl_ref[:, sl], al_ref[:, sl], prm_ref[:, sl])
        for sl in sls])
    for i, sl in enumerate(sls):
        rp, y0, bonus, m_mat, n_mat = terms[i]
        rp_ref[:, sl] = rp
        y0_ref[:, sl] = y0
        bo_ref[:, sl] = bonus
        m_ref[0, i] = m_mat
        n_ref[0, i] = n_mat


def _rwkv_scan_kernel(NC, rp_ref, y0_ref, bo_ref, g_ref, m_ref, n_ref, prm_ref, s0_ref, o_ref, sout_ref, H):
    c = pl.program_id(1)
    HD = RWKV_HEAD
    bones = _head_ones()

    @pl.when(c == 0)
    def _():
        zero = jnp.zeros((HD, HD), F32)
        for i in range(RWKV_PAIRS):
            top = jnp.concatenate([s0_ref[0, 2 * i], zero], axis=1)
            bot = jnp.concatenate([zero, s0_ref[0, 2 * i + 1]], axis=1)
            H[i] = jnp.concatenate([top, bot], axis=0).T

    def pair_steps(i):
        sl = slice(i * LANES, (i + 1) * LANES)
        h_bd = H[i]
        y = _dot(_bf(rp_ref[:, sl]), _bf(h_bd)) + y0_ref[:, sl]
        H[i] = _dot3(m_ref[0, i], h_bd) + n_ref[0, i]
        yield
        mean = _dot_rhs01(y, bones) * (1.0 / HD)
        dlt = y - mean
        yield
        var = _dot_rhs01(dlt * dlt, bones) * (1.0 / HD)
        yn = dlt * lax.rsqrt(var + GN_EPS) * prm_ref[PRM_LNW:PRM_LNW + 1, sl] + prm_ref[PRM_LNB:PRM_LNB + 1, sl]
        o_ref[:, sl] = ((yn + bo_ref[:, sl]) * g_ref[:, sl]).astype(o_ref.dtype)

    _run_interleaved([pair_steps(i) for i in range(RWKV_PAIRS)])

    @pl.when(c == NC - 1)
    def _():
        for i in range(RWKV_PAIRS):
            ht = H[i].T
            sout_ref[0, 2 * i] = ht[0:HD, 0:HD]
            sout_ref[0, 2 * i + 1] = ht[HD:2 * HD, HD:2 * HD]


def rwkv(r, k, v, wl, al, g, prm, s0, *, B, NC, C, first_valid, name, out_rows=None):
    assert B == 1 or NC == 1
    nblk = B * NC
    rows = nblk * C
    PG = RWKV_PREP_PAIRS
    wide = pl.BlockSpec((C, PG * LANES), lambda blk, pg: (blk, pg))
    mat = pl.BlockSpec((1, PG, LANES, LANES), lambda blk, pg: (blk, pg, 0, 0))
    mat_shape = jax.ShapeDtypeStruct((nblk, RWKV_PAIRS, LANES, LANES), F32)
    row_shape = jax.ShapeDtypeStruct((rows, D_MODEL), F32)
    rp, y0, bonus, m_all, n_all = pl.pallas_call(
        functools.partial(_rwkv_prep_kernel, C, B == 1, first_valid, PG),
        grid=(nblk, RWKV_PAIRS // PG),
        in_specs=[wide] * 5 + [pl.BlockSpec((SUBLANES, PG * LANES), lambda blk, pg: (0, pg))],
        out_specs=[wide, wide, wide, mat, mat],
        out_shape=[row_shape, row_shape, row_shape, mat_shape, mat_shape],
        compiler_params=_cparams(("parallel", "parallel")),
        name=name + "_terms",
    )(r, k, v, wl, al, prm)

    full = pl.BlockSpec((C, D_MODEL), lambda b, c: (b * NC + c, 0))
    mats = pl.BlockSpec((1, RWKV_PAIRS, LANES, LANES), lambda b, c: (b * NC + c, 0, 0, 0))
    st_spec = pl.BlockSpec((1, 2 * RWKV_PAIRS, RWKV_HEAD, RWKV_HEAD), lambda b, c: (b, 0, 0, 0))
    return pl.pallas_call(
        functools.partial(_rwkv_scan_kernel, NC),
        grid=(B, NC),
        in_specs=[full] * 4 + [mats, mats, pl.BlockSpec((SUBLANES, D_MODEL), lambda b, c: (0, 0)), st_spec],
        out_specs=[full, st_spec],
        out_shape=[
            jax.ShapeDtypeStruct((out_rows or rows, D_MODEL), BF16),
            jax.ShapeDtypeStruct((B, 2 * RWKV_PAIRS, RWKV_HEAD, RWKV_HEAD), F32),
        ],
        scratch_shapes=[pltpu.VMEM((RWKV_PAIRS, LANES, LANES), F32)],
        compiler_params=_cparams(("parallel", "arbitrary")),
        name=name + "_scan",
    )(rp, y0, bonus, g, m_all, n_all, prm, s0)


def _rwkv_lanes_kernel(nq, unroll, x_ref, prm_ref, s0_ref, o_ref, sout_ref, y_s):
    HD = RWKV_HEAD
    w0, a0, k_k, k_a, r_k, ln_w, ln_b = (prm_ref[i] for i in (PRM_W0, PRM_A0, PRM_KK, PRM_KA, PRM_RK,
                                                               PRM_LNW, PRM_LNB))
    sout_ref[...] = s0_ref[...]
    for t in range(nq):
        r, k, v, wl, al, g = (x_ref[i, t] for i in range(6))
        wlog = -_softplus(-(w0 + wl)) - 0.5
        w = jnp.exp(-jnp.exp(wlog))
        a = _sigmoid(a0 + al)
        kkr = k * k_k
        kk = kkr * lax.rsqrt(jnp.sum(kkr * kkr, axis=0, keepdims=True) + L2_EPS)
        k2 = k * (1.0 + (a - 1.0) * k_a)
        av = -kk
        bv = kk * a

        def value_row(i, carry):
            s = sout_ref[0, i]
            sa = jnp.sum(s * av, axis=0, keepdims=True)
            s = s * w + sa * bv + x_ref[2, t, pl.ds(i, 1), :] * k2
            sout_ref[0, i] = s
            y_s[pl.ds(i, 1), :] = jnp.sum(s * r, axis=0, keepdims=True)
            return carry

        lax.fori_loop(0, HD, value_row, 0, unroll=unroll)
        y = y_s[...]
        mean = jnp.mean(y, axis=0, keepdims=True)
        dlt = y - mean
        var = jnp.mean(dlt * dlt, axis=0, keepdims=True)
        yn = dlt * lax.rsqrt(var + GN_EPS) * ln_w + ln_b
        bonus = jnp.sum(r * k2 * r_k, axis=0, keepdims=True) * v
        o_ref[t] = ((yn + bonus) * g).astype(o_ref.dtype)


def rwkv_lanes(xs, prm_col, s0):
    _, nq, d, nb = xs.shape
    heads = d // RWKV_HEAD
    st = pl.BlockSpec((1, RWKV_HEAD, RWKV_HEAD, nb), lambda h: (h, 0, 0, 0))
    return pl.pallas_call(
        functools.partial(_rwkv_lanes_kernel, nq, SUBLANES),
        grid=(heads,),
        in_specs=[pl.BlockSpec((6, nq, RWKV_HEAD, nb), lambda h: (0, 0, h, 0)),
                  pl.BlockSpec((SUBLANES, RWKV_HEAD, 1), lambda h: (0, h, 0)), st],
        out_specs=[pl.BlockSpec((nq, RWKV_HEAD, nb), lambda h: (0, h, 0)), st],
        out_shape=[jax.ShapeDtypeStruct((nq, d, nb), F32), jax.ShapeDtypeStruct(s0.shape, F32)],
        scratch_shapes=[pltpu.VMEM((RWKV_HEAD, nb), F32)],
        compiler_params=_cparams(("parallel",)),
        name="rwkv_sample",
    )(xs, prm_col, s0)


def _pad_lanes(vec, offset):
    out = jnp.zeros((1, LANES), F32)
    return lax.dynamic_update_slice(out, vec.reshape(1, -1).astype(F32), (0, offset))


def _sample_rows(arr, row0, nb, nq, front):
    cols = arr.shape[1]
    s = arr[row0:row0 + nb * nq].reshape(nb, nq, cols)
    s = jnp.pad(s, ((0, 0), (front, 0), (0, 0)))
    return s.reshape(nb * (front + nq), cols)


def _only(x):
    assert x.shape[0] == 1
    return x.reshape(x.shape[1:])


def kernel(x_prompt, x_sample, cache_fox_k, cache_fox_v, cache_fox_logf, state_gdn_conv, state_gdn_S,
           state_rwkv_shift, state_rwkv_S, page_table, meta_tokens, ln_mix, ln_mlp, ln_final,
           w_in0, gdn_conv_w, gdn_A_log, gdn_dt_bias, gdn_norm_w, fox_b_f, w_out0,
           rwkv_mu, rwkv_w0, rwkv_w1, rwkv_w2, rwkv_a0, rwkv_a1, rwkv_a2, rwkv_g1, rwkv_g2,
           rwkv_k_k, rwkv_k_a, rwkv_r_k, rwkv_w_r, rwkv_w_k, rwkv_w_v, rwkv_w_o, rwkv_ln_w, rwkv_ln_b,
           w_up, w_down):
    D = D_MODEL
    assert x_prompt.shape[0] == 1 and x_prompt.shape[2] == D
    seq = x_prompt.shape[1]
    nb, nq = x_sample.shape[0], x_sample.shape[1]
    npages = page_table.shape[1]
    tprompt = N_META + seq
    pad = (-tprompt) % LANES
    tp = tprompt + pad
    ns = nb * nq
    R = -(-(tp + ns) // ROW_TILE) * ROW_TILE
    CS = SUBLANES
    front = CS - nq
    assert 3 <= front

    x0 = jnp.concatenate([jnp.zeros((pad, D), F32), meta_tokens.astype(F32), x_prompt[0],
                          x_sample.reshape(ns, D), jnp.zeros((R - tp - ns, D), F32)], axis=0)

    w_in = w_in0[0]
    o_z = GDN_QKV
    o_a = o_z + GDN_QK
    o_b = o_a + GDN_HEADS
    o_fq = o_b + GDN_HEADS
    o_fk = o_fq + FOX_W
    o_fv = o_fk + FOX_W
    o_ff = o_fv + FOX_W
    o_fz = o_ff + FOX_HEADS
    w_big = jnp.concatenate([w_in[:, :o_a], w_in[:, o_fq:o_ff], w_in[:, o_fz:]], axis=1).astype(BF16)
    w_small = jnp.concatenate([w_in[:, o_a:o_fq], w_in[:, o_ff:o_fz],
                               jnp.zeros((D, LANES - 3 * GDN_HEADS), F32)], axis=1).astype(BF16)

    h0 = rmsnorm(x0, ln_mix[0], BF16)
    p = matmul(h0, w_big, name="in_proj")
    ps = matmul(h0, w_small, name="in_proj_small")

    alog_pad = _pad_lanes(gdn_A_log[0], LANE_A)
    dt_pad = _pad_lanes(gdn_dt_bias[0], LANE_A)
    bf_pad = _pad_lanes(fox_b_f[0], LANE_F)
    conv_w = gdn_conv_w[0]
    norm_w = gdn_norm_w[0].reshape(1, GDN_D)

    GC = 64
    o_gdn_p, s_gdn_p = gdn(p, p, ps, COL_Z // GDN_QK, conv_w, alog_pad, dt_pad, norm_w,
                           jnp.zeros((1, GDN_HEADS, GDN_D, GDN_D), F32), None,
                           B=1, NC=tp // GC, C=GC, first_valid=pad, name="gdn_prompt", out_rows=R)
    qkv_ext = _sample_rows(p[:, :GDN_QKV], tp, nb, nq, front)
    z_ext = _sample_rows(p[:, COL_Z:COL_Z + GDN_QK], tp, nb, nq, front)
    ps_ext = _sample_rows(ps, tp, nb, nq, front)
    o_gdn_s, s_gdn_s = gdn(qkv_ext, z_ext, ps_ext, 0, conv_w, alog_pad, dt_pad, norm_w,
                           _only(state_gdn_S), state_gdn_conv, B=nb, NC=1, C=CS, first_valid=front, name="gdn_sample")
    o_gdn_s = o_gdn_s.reshape(nb, CS, GDN_QK)[:, front:].reshape(ns, GDN_QK)

    lf, cq, ck = fox_prep(ps, bf_pad, pad)
    o_fox_p = fox_prompt(p, cq, ck, tp, pad, R)
    pt_flat = page_table.reshape(-1).astype(jnp.int32)
    lf_s = lf[tp:tp + ns, LANE_F:LANE_F + FOX_HEADS].reshape(nb, nq, FOX_HEADS)
    lfn = jnp.tile(jnp.swapaxes(lf_s, 1, 2), (1, nq, 1))
    lfn = jnp.pad(lfn, ((0, 0), (0, 0), (0, SUBLANES - nq)))
    pool = cache_fox_k.shape[1]
    page = cache_fox_k.shape[2]
    cache_k = _only(cache_fox_k).reshape(pool, page * FOX_HEADS, FOX_DH)
    cache_v = _only(cache_fox_v).reshape(pool, page * FOX_HEADS, FOX_DH)
    cache_lft = jnp.swapaxes(_only(cache_fox_logf), 1, 2)
    def sample_heads(col):
        return p[tp:tp + ns, col:col + FOX_W].reshape(nb, nq * FOX_HEADS, FOX_DH)

    o_fox_s = fox_sample(pt_flat, sample_heads(COL_FQ), cache_k, cache_v, cache_lft, lfn,
                         sample_heads(COL_FK), sample_heads(COL_FV), sample_heads(COL_FZ), nb, npages)
    o_fox_s = o_fox_s.reshape(ns, FOX_W).astype(BF16)

    def with_sample_rows(full, sample):
        full = lax.dynamic_update_slice(full, sample, (tp, 0))
        return lax.dynamic_update_slice(full, jnp.zeros((R - tp - ns, full.shape[1]), full.dtype), (tp + ns, 0))

    x1 = matmul([with_sample_rows(o_gdn_p, o_gdn_s), with_sample_rows(o_fox_p, o_fox_s)], w_out0, layer=0,
                res=x0, name="out_proj")
    u0 = matmul(rmsnorm(x1, ln_mlp[0], BF16), w_up, layer=0, act="relu2", out_dtype=BF16, name="mlp_up0")
    x2 = matmul(u0, w_down, layer=0, res=x1, name="mlp_down0")

    h1 = rmsnorm(x2, ln_mix[1], F32)
    h1_s = h1[tp:tp + ns].reshape(nb, nq, D)
    shift_rows = jnp.concatenate([jnp.repeat(state_rwkv_shift[0].astype(F32), nq, axis=0),
                                  jnp.zeros((R - tp - ns, D), F32)], axis=0)
    xr, xw, xk, xv, xa, xg = rwkv_mix(h1, shift_rows, rwkv_mu[0], tp, ns, nq)

    def pad_cols(w):
        return jnp.pad(w, ((0, 0), (0, LANES - w.shape[1]))).astype(BF16)

    def pad_rows(w):
        return jnp.pad(w, ((0, LANES - w.shape[0]), (0, 0))).astype(BF16)

    r_ = matmul(xr, rwkv_w_r, layer=0, name="rwkv_r")
    k_ = matmul(xk, rwkv_w_k, layer=0, name="rwkv_k")
    v_ = matmul(xv, rwkv_w_v, layer=0, name="rwkv_v")
    wl = matmul(matmul(xw, pad_cols(rwkv_w1[0]), act="tanh", out_dtype=BF16, name="rwkv_w1"), pad_rows(rwkv_w2[0]), name="rwkv_w2")
    al = matmul(matmul(xa, pad_cols(rwkv_a1[0]), out_dtype=BF16, name="rwkv_a1"), pad_rows(rwkv_a2[0]), name="rwkv_a2")
    g_ = matmul(matmul(xg, rwkv_g1, layer=0, act="sigmoid", out_dtype=BF16, name="rwkv_g1"), rwkv_g2, layer=0, name="rwkv_g2")

    prm = jnp.stack([rwkv_w0[0], rwkv_a0[0], rwkv_k_k[0], rwkv_k_a[0], rwkv_r_k[0].reshape(D), rwkv_ln_w[0], rwkv_ln_b[0],
                     jnp.zeros((D,), F32)], axis=0).astype(F32)
    RC = 64
    o_rw_p, s_rw_p = rwkv(r_, k_, v_, wl, al, g_, prm, jnp.zeros((1, 2 * RWKV_PAIRS, RWKV_HEAD, RWKV_HEAD), F32),
                          B=1, NC=tp // RC, C=RC, first_valid=pad, name="rwkv_prompt", out_rows=R)
    xs = jnp.stack([t[tp:tp + ns] for t in (r_, k_, v_, wl, al, g_)])
    xs = jnp.transpose(xs.reshape(6, nb, nq, D), (0, 2, 3, 1))
    s0_l = jnp.transpose(_only(state_rwkv_S), (1, 2, 3, 0))
    o_l, s_l = rwkv_lanes(xs, prm.reshape(SUBLANES, D, 1), s0_l)
    o_rw_s = jnp.transpose(o_l, (2, 0, 1)).reshape(ns, D).astype(BF16)
    s_rw_s = jnp.transpose(s_l, (3, 0, 1, 2))
    x3 = matmul(with_sample_rows(o_rw_p, o_rw_s), rwkv_w_o, layer=0, res=x2, name="rwkv_o")
    u1 = matmul(rmsnorm(x3, ln_mlp[1], BF16), w_up, layer=1, act="relu2", out_dtype=BF16, name="mlp_up1")
    x4 = matmul(u1, w_down, layer=1, res=x3, name="mlp_down1")
    y = rmsnorm(x4, ln_final, F32)

    r0 = pad
    y_prompt = y[r0 + N_META:tp].reshape(1, seq, D)
    y_sample = y[tp:tp + ns].reshape(nb, nq, D)

    def kv_rows(col):
        blk = p[:, col:col + FOX_W]
        return (blk[r0:tp].reshape(1, 1, tprompt, FOX_HEADS, FOX_DH),
                blk[tp:tp + ns].reshape(1, nb, nq, FOX_HEADS, FOX_DH))

    fk_p, fk_s = kv_rows(COL_FK)
    fv_p, fv_s = kv_rows(COL_FV)
    lf8 = lf[:, LANE_F:LANE_F + FOX_HEADS]
    lf_p = lf8[r0:tp].reshape(1, 1, tprompt, FOX_HEADS)
    lf_sm = lf8[tp:tp + ns].reshape(1, nb, nq, FOX_HEADS)
    cb_p = p[tp - 3:tp, :GDN_QKV].reshape(1, 1, 3, GDN_QKV)
    cb_s = p[tp:tp + ns, :GDN_QKV].reshape(nb, nq, GDN_QKV)[:, nq - 3:][None]
    gs_p = s_gdn_p[None]
    gs_s = s_gdn_s[None]
    sh_p = h1[tp - 1].reshape(1, 1, D)
    sh_s = h1_s[:, nq - 1][None]
    rs_p = s_rw_p[None]
    rs_s = s_rw_s[None]
    return (y_prompt, y_sample, fk_p, fk_s, fv_p, fv_s, lf_p, lf_sm, cb_p, cb_s, gs_p, gs_s, sh_p, sh_s, rs_p, rs_s)
```

```python
import functools
import math

import jax
import jax.numpy as jnp
from jax import lax
from jax.experimental import pallas as pl
from jax.experimental.pallas import tpu as pltpu

F32 = jnp.float32
BF16 = jnp.bfloat16

D_MODEL = 2048
N_META = 16
GDN_HEADS = 8
GDN_D = 128
GDN_QK = GDN_HEADS * GDN_D
GDN_QKV = 3 * GDN_QK
FOX_HEADS = 8
FOX_DH = 128
FOX_W = FOX_HEADS * FOX_DH
RWKV_HEAD = 64
RWKV_PAIRS = D_MODEL // (2 * RWKV_HEAD)
NORM_EPS = 1e-6
L2_EPS = 1e-6
GN_EPS = 64e-5
NEG_INF = -1e30

LANES = 128
SUBLANES = 8
ROW_TILE = 640
MM_ROW_TILE = 1792
MM_WIDE_MIN_COLS = 8192
LOG2E = 1.4426950408889634
VMEM_LIMIT = 48 * 1024 * 1024
FOX_HEADS_PER_STEP = 4
FOX_GROUP = 16
RWKV_PREP_PAIRS = 16

COL_Z = GDN_QKV
COL_FQ = COL_Z + GDN_QK
COL_FK = COL_FQ + FOX_W
COL_FV = COL_FK + FOX_W
COL_FZ = COL_FV + FOX_W
LANE_A = 0
LANE_B = 8
LANE_F = 16
PRM_W0, PRM_A0, PRM_KK, PRM_KA, PRM_RK, PRM_LNW, PRM_LNB = range(7)


def _pick(n, cands):
    for c in cands:
        if n % c == 0:
            return c
    raise ValueError(f"no tile for {n}")


def _cparams(sem):
    return pltpu.CompilerParams(dimension_semantics=sem, vmem_limit_bytes=VMEM_LIMIT)


def _dot(a, b):
    return jnp.dot(a, b, preferred_element_type=F32)


def _dot_nt(a, b):
    return lax.dot_general(a, b, (((1,), (1,)), ((), ())), preferred_element_type=F32)


def _dot_tn(a, b):
    return lax.dot_general(a, b, (((0,), (0,)), ((), ())), preferred_element_type=F32)


def _bf(x):
    return x.astype(BF16)


def _softplus(x):
    return jnp.maximum(x, 0.0) + jnp.log(1.0 + jnp.exp(-jnp.abs(x)))


def _sigmoid(x):
    return 1.0 / (1.0 + jnp.exp(-x))


def _iota2(shape, dim):
    return lax.broadcasted_iota(jnp.int32, shape, dim)


def _div(x, n):
    assert n & (n - 1) == 0
    return x >> (n.bit_length() - 1)


def _split2(x):
    hi = x.astype(BF16)
    return hi, (x - hi.astype(F32)).astype(BF16)


def _split3(x):
    p1 = x.astype(BF16)
    rem = x - p1.astype(F32)
    p2 = rem.astype(BF16)
    return p1, p2, (rem - p2.astype(F32)).astype(BF16)


def _dot3(a, b):
    m = a.shape[0]
    ah, al = _split2(a)
    bh, bl = _split2(b)
    top = _dot(jnp.concatenate([ah, al], axis=0), bh)
    return (top[:m] + top[m:]) + _dot(ah, bl)


def _dot_lhs01(a01, b):
    a = a01.astype(BF16)
    b1, b2, b3 = _split3(b)
    return _dot(a, b1) + (_dot(a, b2) + _dot(a, b3))


def _dot_rhs01(a, b01):
    b = b01.astype(BF16)
    a1, a2, a3 = _split3(a)
    m = a.shape[0]
    out = _dot(jnp.concatenate([a1, a2, a3], axis=0), b)
    return out[:m] + (out[m:2 * m] + out[2 * m:])


def _tri_inv_steps(n_mat, n):
    size = n_mat.shape[0]
    eye = (_iota2((size, size), 0) == _iota2((size, size), 1)).astype(F32)
    t = eye + n_mat
    if n <= 2:
        return t
    p = _dot(_bf(n_mat), _bf(n_mat))
    yield
    m = 2
    while m < n:
        p_bf = _bf(p)
        if 2 * m < n:
            both = _dot(_bf(jnp.concatenate([p, t], axis=0)), p_bf)
            p = both[:size]
            t = t + both[size:]
        else:
            t = t + _dot(_bf(t), p_bf)
        m *= 2
        yield
    return t


def _run_interleaved(gens):
    results = [None] * len(gens)
    live = list(enumerate(gens))
    while live:
        nxt = []
        for idx, gen in live:
            try:
                next(gen)
                nxt.append((idx, gen))
            except StopIteration as stop:
                results[idx] = stop.value
        live = nxt
    return results


def _rmsnorm_kernel(x_ref, w_ref, o_ref):
    x = x_ref[...]
    ms = jnp.mean(x * x, axis=-1, keepdims=True)
    o_ref[...] = (x * lax.rsqrt(ms + NORM_EPS) * w_ref[...]).astype(o_ref.dtype)


def rmsnorm(x, w, out_dtype):
    m, d = x.shape
    tm = _pick(m, (640, 320, 256, 128))
    return pl.pallas_call(
        _rmsnorm_kernel,
        grid=(m // tm,),
        in_specs=[pl.BlockSpec((tm, d), lambda i: (i, 0)), pl.BlockSpec((1, d), lambda i: (0, 0))],
        out_specs=pl.BlockSpec((tm, d), lambda i: (i, 0)),
        out_shape=jax.ShapeDtypeStruct((m, d), out_dtype),
        compiler_params=_cparams(("parallel",)),
        name="rmsnorm",
    )(x, w.reshape(1, d))


def _mm_kernel(nk, nx, act, has_res, *refs):
    x_refs, w_ref = refs[:nx], refs[nx]
    r_ref = refs[nx + 1] if has_res else None
    o_ref = refs[nx + 1 + has_res]

    def product():
        if nx == 1:
            return _dot(x_refs[0][...], _bf(w_ref[...]))
        acc, k0 = None, 0
        for x_ref in x_refs:
            kw = x_ref.shape[1]
            part = _dot(x_ref[...], _bf(w_ref[k0:k0 + kw, :]))
            acc = part if acc is None else acc + part
            k0 += kw
        return acc

    def finish(acc):
        if act == "relu2":
            acc = jnp.square(jnp.maximum(acc, 0.0))
        elif act == "tanh":
            acc = jnp.tanh(acc)
        elif act == "sigmoid":
            acc = _sigmoid(acc)
        if has_res:
            acc = r_ref[...] + acc
        o_ref[...] = acc.astype(o_ref.dtype)

    if nk == 1:
        finish(product())
    else:
        assert act is None and o_ref.dtype == F32
        k = pl.program_id(2)

        @pl.when(k == 0)
        def _():
            o_ref[...] = r_ref[...] if has_res else jnp.zeros(o_ref.shape, F32)

        o_ref[...] += product()


def matmul(x, w, *, layer=None, act=None, res=None, out_dtype=F32, name="matmul"):
    xs = list(x) if isinstance(x, (list, tuple)) else [x]
    m = xs[0].shape[0]
    k = sum(xi.shape[1] for xi in xs)
    n = w.shape[-1]
    tm = _pick(m, (MM_ROW_TILE, ROW_TILE, 512, 256, 128))
    tk = k if k <= 2048 else 2048
    nk = k // tk
    wide_ok = nk == 1 and res is None and n >= MM_WIDE_MIN_COLS
    tn = _pick(n, ((1024,) if wide_ok else ()) + (512, 256, 128))
    assert len(xs) == 1 or nk == 1
    in_specs = [pl.BlockSpec((tm, tk if len(xs) == 1 else xi.shape[1]), lambda i, j, kk: (i, kk)) for xi in xs]
    if w.ndim == 3:
        in_specs.append(pl.BlockSpec((None, tk, tn), lambda i, j, kk: (layer, kk, j)))
    else:
        in_specs.append(pl.BlockSpec((tk, tn), lambda i, j, kk: (kk, j)))
    args = xs + [w]
    if res is not None:
        in_specs.append(pl.BlockSpec((tm, tn), lambda i, j, kk: (i, j)))
        args.append(res)
    return pl.pallas_call(
        functools.partial(_mm_kernel, nk, len(xs), act, res is not None),
        grid=(m // tm, n // tn, nk),
        in_specs=in_specs,
        out_specs=pl.BlockSpec((tm, tn), lambda i, j, kk: (i, j)),
        out_shape=jax.ShapeDtypeStruct((m, n), out_dtype),
        compiler_params=_cparams(("parallel", "parallel", "arbitrary")),
        name=name,
    )(*args)


def _gdn_kernel(C, NC, first_valid, has_conv, qkv_ref, z_ref, ps_ref, cw_ref, alog_ref, dt_ref, nw_ref, s0_ref,
                *rest):
    conv_ref = rest[0] if has_conv else None
    o_ref, sout_ref, S, ext = rest[1:] if has_conv else rest
    c = pl.program_id(1)

    @pl.when(c == 0)
    def _():
        S[...] = s0_ref[0]
        ext[0:SUBLANES, :] = jnp.zeros((SUBLANES, GDN_QKV), F32)

    x = qkv_ref[...]
    ext[SUBLANES:SUBLANES + C, :] = x
    if has_conv:
        assert NC == 1 and 3 <= first_valid < C
        ext[SUBLANES + first_valid - 3:SUBLANES + first_valid, :] = conv_ref[0, 0]
    cw = cw_ref[...]
    y = ((ext[5:5 + C, :] * cw[0:1] + ext[6:6 + C, :] * cw[1:2]) + ext[7:7 + C, :] * cw[2:3]) + ext[8:8 + C, :] * cw[3:4]
    ext[0:SUBLANES, :] = x[C - SUBLANES:C]
    y = y * _sigmoid(y)

    rows = c * C + _iota2((C, 1), 0)
    valid = rows >= first_valid
    ps = ps_ref[...]
    g_all = jnp.where(valid, -jnp.exp(alog_ref[...]) * _softplus(ps + dt_ref[...]), 0.0)
    beta_all = jnp.where(valid, _sigmoid(ps), 0.0)
    ri = _iota2((C, C), 0)
    ci = _iota2((C, C), 1)
    causal = ri >= ci
    strict = ri > ci
    gc = _dot_lhs01(causal.astype(F32), g_all)
    gct = _dot_rhs01(g_all.T, (ri <= ci).astype(F32))
    nw = nw_ref[...]

    def head_steps(h):
        sl = slice(h * GDN_D, (h + 1) * GDN_D)
        qh = y[:, sl]
        kh = y[:, GDN_QK + h * GDN_D:GDN_QK + (h + 1) * GDN_D]
        vh = jnp.where(valid, y[:, 2 * GDN_QK + h * GDN_D:2 * GDN_QK + (h + 1) * GDN_D], 0.0)
        qh = jnp.where(valid, qh * lax.rsqrt(jnp.sum(qh * qh, axis=-1, keepdims=True) + L2_EPS) * GDN_D ** -0.5, 0.0)
        kh = jnp.where(valid, kh * lax.rsqrt(jnp.sum(kh * kh, axis=-1, keepdims=True) + L2_EPS), 0.0)
        bcol = beta_all[:, LANE_B + h:LANE_B + h + 1]
        gcol = gc[:, LANE_A + h:LANE_A + h + 1]
        grow = gct[LANE_A + h:LANE_A + h + 1, :]
        glast = gc[C - 1:C, LANE_A + h:LANE_A + h + 1]
        diff = gcol - grow
        decay = jnp.where(causal, jnp.exp(jnp.where(causal, diff, 0.0)), 0.0)
        kb = kh * bcol
        a_mat = jnp.where(strict, _dot_nt(_bf(kb), _bf(kh)) * decay, 0.0)
        qk = _dot_nt(_bf(qh), _bf(kh)) * decay
        eg = jnp.exp(gcol)
        rhs = jnp.concatenate([vh * bcol, kb * eg], axis=1)
        q_dec = qh * eg
        k_dec = kh * jnp.exp(glast - gcol)
        yield
        t_mat = yield from _tri_inv_steps(-a_mat, C)
        sol = _dot3(t_mat, rhs)
        u = sol[:, :GDN_D]
        w = sol[:, GDN_D:]
        yield
        s_h = S[h]
        s_bf = _bf(s_h)
        v_new = u - _dot(_bf(w), s_bf)
        o_state = _dot(_bf(q_dec), s_bf)
        yield
        o = o_state + _dot(_bf(qk), _bf(v_new))
        S[h] = s_h * jnp.exp(glast) + _dot_tn(_bf(k_dec), _bf(v_new))
        yield
        on = o * lax.rsqrt(jnp.mean(o * o, axis=-1, keepdims=True) + NORM_EPS) * nw
        zh = z_ref[:, sl]
        o_ref[:, sl] = (on * (zh * _sigmoid(zh))).astype(o_ref.dtype)

    _run_interleaved([head_steps(h) for h in range(GDN_HEADS)])

    @pl.when(c == NC - 1)
    def _():
        sout_ref[0] = S[...]


def gdn(qkv_arr, z_arr, ps_arr, z_col, conv_w, alog_pad, dt_pad, norm_w, s0, conv0, *, B, NC, C, first_valid, name,
        out_rows=None):
    rows = B * NC * C
    has_conv = conv0 is not None
    extra_specs = [pl.BlockSpec((1, 1, 3, GDN_QKV), lambda b, c: (0, b, 0, 0))] if has_conv else []
    extra_args = [conv0] if has_conv else []
    return pl.pallas_call(
        functools.partial(_gdn_kernel, C, NC, first_valid, has_conv),
        grid=(B, NC),
        in_specs=[
            pl.BlockSpec((C, GDN_QKV), lambda b, c: (b * NC + c, 0)),
            pl.BlockSpec((C, GDN_QK), lambda b, c: (b * NC + c, z_col)),
            pl.BlockSpec((C, LANES), lambda b, c: (b * NC + c, 0)),
            pl.BlockSpec((4, GDN_QKV), lambda b, c: (0, 0)),
            pl.BlockSpec((1, LANES), lambda b, c: (0, 0)),
            pl.BlockSpec((1, LANES), lambda b, c: (0, 0)),
            pl.BlockSpec((1, GDN_D), lambda b, c: (0, 0)),
            pl.BlockSpec((1, GDN_HEADS, GDN_D, GDN_D), lambda b, c: (b, 0, 0, 0)),
        ] + extra_specs,
        out_specs=[
            pl.BlockSpec((C, GDN_QK), lambda b, c: (b * NC + c, 0)),
            pl.BlockSpec((1, GDN_HEADS, GDN_D, GDN_D), lambda b, c: (b, 0, 0, 0)),
        ],
        out_shape=[
            jax.ShapeDtypeStruct((out_rows or rows, GDN_QK), BF16),
            jax.ShapeDtypeStruct((B, GDN_HEADS, GDN_D, GDN_D), F32),
        ],
        scratch_shapes=[pltpu.VMEM((GDN_HEADS, GDN_D, GDN_D), F32), pltpu.VMEM((C + SUBLANES, GDN_QKV), F32)],
        compiler_params=_cparams(("parallel", "arbitrary")),
        name=name,
    )(qkv_arr, z_arr, ps_arr, conv_w, alog_pad, dt_pad, norm_w, s0, *extra_args)


def _fox_prep_kernel(tb, first_valid, ps_ref, bf_ref, lf_ref, cq_ref, ck_ref, carry):
    i = pl.program_id(0)

    @pl.when(i == 0)
    def _():
        carry[...] = jnp.zeros(carry.shape, F32)

    x = ps_ref[...] + bf_ref[...]
    rows = i * tb + _iota2((tb, 1), 0)
    lf = jnp.where(rows >= first_valid, jnp.minimum(x, 0.0) - jnp.log(1.0 + jnp.exp(-jnp.abs(x))), 0.0)
    tri = (_iota2((tb, tb), 0) >= _iota2((tb, tb), 1)).astype(F32)
    c = _dot_lhs01(tri, lf) + carry[0:1, :]
    carry[...] = jnp.broadcast_to(c[tb - 1:tb, :], carry.shape)
    lf_ref[...] = lf
    c2 = c * LOG2E
    for h in range(FOX_HEADS):
        cq_ref[h] = jnp.broadcast_to(c2[:, LANE_F + h:LANE_F + h + 1], (tb, LANES))
    ck_ref[...] = c2.T[LANE_F:LANE_F + FOX_HEADS, :]


def fox_prep(ps, bf_pad, first_valid):
    r = ps.shape[0]
    tb = _pick(r, (ROW_TILE, 512, 256, 128))
    return pl.pallas_call(
        functools.partial(_fox_prep_kernel, tb, first_valid),
        grid=(r // tb,),
        in_specs=[pl.BlockSpec((tb, LANES), lambda i: (i, 0)), pl.BlockSpec((1, LANES), lambda i: (0, 0))],
        out_specs=[
            pl.BlockSpec((tb, LANES), lambda i: (i, 0)),
            pl.BlockSpec((FOX_HEADS, tb, LANES), lambda i: (0, i, 0)),
            pl.BlockSpec((FOX_HEADS, tb), lambda i: (0, i)),
        ],
        out_shape=[
            jax.ShapeDtypeStruct((r, LANES), F32),
            jax.ShapeDtypeStruct((FOX_HEADS, r, LANES), F32),
            jax.ShapeDtypeStruct((FOX_HEADS, r), F32),
        ],
        scratch_shapes=[pltpu.VMEM((SUBLANES, LANES), F32)],
        compiler_params=_cparams(("arbitrary",)),
        name="fox_prep",
    )(ps, bf_pad)


def _fox_flash_kernel(tq, first_valid, qi_ref, ki_ref, qt_ref, k_ref, vt_ref, ck_ref, cq_ref, fz_ref, o_ref,
                      m_s, l_s, acc_s, t_s):
    step = pl.program_id(1)
    qi = qi_ref[step]
    ki = ki_ref[step]
    reps = tq // LANES
    heads = qt_ref.shape[0]

    @pl.when(ki == 0)
    def _():
        m_s[...] = jnp.full(m_s.shape, NEG_INF, F32)
        l_s[...] = jnp.zeros(l_s.shape, F32)
        acc_s[...] = jnp.zeros(acc_s.shape, F32)

    def head_steps(h, masked):
        cols = slice(h * FOX_DH, (h + 1) * FOX_DH)
        qt = _bf(qt_ref[h] * (FOX_DH ** -0.5 * LOG2E))
        cq = cq_ref[h]
        m_prev = m_s[h, 0:1, :]
        kb = LANES
        groups = kb // SUBLANES
        mx = None
        for b in range(tq // kb):
            rs = slice(b * kb, (b + 1) * kb)
            t = _dot(_bf(k_ref[rs, cols]), qt) - jnp.concatenate([ck_ref[h, rs, :]] * reps, axis=1)
            if masked:
                kpos = ki * tq + b * kb + _iota2((kb, tq), 0)
                qpos = qi * tq + _iota2((kb, tq), 1)
                t = jnp.where((kpos <= qpos) & (kpos >= first_valid), t, NEG_INF)
            t_s[h, rs, :] = t
            part = jnp.max(t.reshape(groups, SUBLANES, tq), axis=0)
            mx = part if mx is None else jnp.maximum(mx, part)
            yield
        m_new = jnp.maximum(m_prev, jnp.max(mx, axis=0, keepdims=True) + cq)
        shift = cq - m_new
        alpha = jnp.exp2(m_prev - m_new)
        lsum = None
        pv = None
        for b in range(tq // kb):
            rs = slice(b * kb, (b + 1) * kb)
            p = jnp.exp2(t_s[h, rs, :] + shift)
            part = jnp.sum(p.reshape(groups, SUBLANES, tq), axis=0)
            lsum = part if lsum is None else lsum + part
            d = _dot(_bf(vt_ref[h, :, rs]), _bf(p))
            pv = d if pv is None else pv + d
            yield
        l_new = alpha * l_s[h, 0:1, :] + jnp.sum(lsum, axis=0, keepdims=True)
        l_s[h] = jnp.broadcast_to(l_new, (SUBLANES, tq))
        acc_s[h] = alpha * acc_s[h] + pv
        m_s[h] = jnp.broadcast_to(m_new, (SUBLANES, tq))

    def accumulate(masked):
        _run_interleaved([head_steps(h, masked) for h in range(heads)])

    edge = (ki == qi) | (ki == 0)
    pl.when(edge)(lambda: accumulate(True))
    pl.when(jnp.logical_not(edge))(lambda: accumulate(False))

    @pl.when(ki == qi)
    def _():
        rows = qi * tq + _iota2((tq, 1), 0)
        for h in range(heads):
            cols = slice(h * FOX_DH, (h + 1) * FOX_DH)
            o = (acc_s[h] / l_s[h, 0:1, :]).T * _sigmoid(fz_ref[:, cols])
            o_ref[:, cols] = jnp.where(rows >= first_valid, o, 0.0).astype(o_ref.dtype)


def fox_prompt(p, c_rep, c_row, tp, first_valid, out_rows):
    tq = _pick(tp, (ROW_TILE, 512, 384, 256, 128))
    nq = tp // tq
    assert first_valid < tq
    pairs = [(qi, ki) for qi in range(nq) for ki in range(qi + 1)]
    qi_arr = jnp.array([a for a, _ in pairs], jnp.int32)
    ki_arr = jnp.array([b for _, b in pairs], jnp.int32)
    hp = FOX_HEADS_PER_STEP
    wid = hp * FOX_DH
    cb = lambda col, g: col // wid + g

    def heads_t(col, dtype):
        return jnp.transpose(p[:tp, col:col + FOX_W].reshape(tp, FOX_HEADS, FOX_DH), (1, 2, 0)).astype(dtype)

    return pl.pallas_call(
        functools.partial(_fox_flash_kernel, tq, first_valid),
        grid_spec=pltpu.PrefetchScalarGridSpec(
            num_scalar_prefetch=2,
            grid=(FOX_HEADS // hp, len(pairs)),
            in_specs=[
                pl.BlockSpec((hp, FOX_DH, tq), lambda g, t, qa, ka: (g, 0, qa[t])),
                pl.BlockSpec((tq, wid), lambda g, t, qa, ka: (ka[t], cb(COL_FK, g))),
                pl.BlockSpec((hp, FOX_DH, tq), lambda g, t, qa, ka: (g, 0, ka[t])),
                pl.BlockSpec((hp, tq, LANES), lambda g, t, qa, ka: (g, ka[t], 0)),
                pl.BlockSpec((hp, 1, tq), lambda g, t, qa, ka: (g, 0, qa[t])),
                pl.BlockSpec((tq, wid), lambda g, t, qa, ka: (qa[t], cb(COL_FZ, g))),
            ],
            out_specs=pl.BlockSpec((tq, wid), lambda g, t, qa, ka: (qa[t], g)),
            scratch_shapes=[pltpu.VMEM((hp, SUBLANES, tq), F32), pltpu.VMEM((hp, SUBLANES, tq), F32),
                            pltpu.VMEM((hp, FOX_DH, tq), F32), pltpu.VMEM((hp, tq, tq), F32)],
        ),
        out_shape=jax.ShapeDtypeStruct((out_rows, FOX_W), BF16),
        compiler_params=_cparams(("parallel", "arbitrary")),
        name="fox_prompt",
    )(qi_arr, ki_arr, heads_t(COL_FQ, F32), p, heads_t(COL_FV, BF16), c_rep, c_row.reshape(FOX_HEADS, 1, -1), p)


def _fox_sample_kernel(G, ngroups, pt_ref, q_ref, *refs):
    k_refs, v_refs, lft_refs = refs[0:G], refs[G:2 * G], refs[2 * G:3 * G]
    lfn_ref, kn_ref, vn_ref, fz_ref, o_ref, cq_s, cn_s, carry_s, m_s, l_s, acc_s = refs[3 * G:]
    jg = pl.program_id(1)
    rows = q_ref.shape[1]
    page = lft_refs[0].shape[2]
    flat = page * FOX_HEADS
    nn = lfn_ref.shape[2]
    scale = FOX_DH ** -0.5

    @pl.when(jg == 0)
    def _():
        upper = (_iota2((nn, nn), 0) <= _iota2((nn, nn), 1)).astype(F32)
        cn = _dot_rhs01(lfn_ref[0], upper)
        cn_s[...] = cn
        own_q = _iota2((rows, nn), 1) == _div(_iota2((rows, nn), 0), FOX_HEADS)
        cq_s[...] = jnp.broadcast_to(jnp.sum(jnp.where(own_q, cn, 0.0), axis=-1, keepdims=True), cq_s.shape)
        carry_s[...] = jnp.zeros(carry_s.shape, F32)
        m_s[...] = jnp.full(m_s.shape, NEG_INF, F32)
        l_s[...] = jnp.zeros(l_s.shape, F32)
        acc_s[...] = jnp.zeros(acc_s.shape, F32)

    def update(ts, vals_bf):
        cq = cq_s[:, 0:1]
        m_prev = m_s[:, 0:1]
        t_max = functools.reduce(jnp.maximum, ts)
        m_new = jnp.maximum(m_prev, jnp.max(t_max, axis=-1, keepdims=True) + cq)
        shift = cq - m_new
        ps = [jnp.exp(t + shift) for t in ts]
        alpha = jnp.exp(m_prev - m_new)
        l_s[...] = alpha * l_s[...] + jnp.sum(functools.reduce(jnp.add, ps), axis=-1, keepdims=True)
        pv = functools.reduce(jnp.add, [_dot(_bf(p), v) for p, v in zip(ps, vals_bf)])
        acc_s[...] = alpha * acc_s[...] + pv
        m_s[...] = jnp.broadcast_to(m_new, m_s.shape)

    own = (_iota2((rows, flat), 0) & (FOX_HEADS - 1)) == (_iota2((rows, flat), 1) & (FOX_HEADS - 1))
    later = (_iota2((page, flat), 0) > _div(_iota2((page, flat), 1), FOX_HEADS)).astype(BF16)
    head_col = _iota2((FOX_HEADS, flat), 0) == (_iota2((FOX_HEADS, flat), 1) & (FOX_HEADS - 1))
    carry = carry_s[:, 0:1]
    q_bf = _bf(q_ref[0])
    lfts = [lft_refs[g][0] for g in range(G)]
    suffixes = _dot_rhs01(jnp.concatenate(lfts, axis=0), later)
    ts = []
    for g in range(G):
        suffix = suffixes[g * FOX_HEADS:(g + 1) * FOX_HEADS]
        d = jnp.sum(jnp.where(head_col, suffix + carry, 0.0), axis=0, keepdims=True)
        carry = carry + jnp.sum(lfts[g], axis=-1, keepdims=True)
        s = _dot_nt(q_bf, _bf(k_refs[g][0])) * scale
        ts.append(jnp.where(own, s + d, NEG_INF))
    update(ts, [_bf(v_refs[g][0]) for g in range(G)])
    carry_s[...] = jnp.broadcast_to(carry, carry_s.shape)

    @pl.when(jg == ngroups - 1)
    def _():
        nflat = kn_ref.shape[1]
        spread = (_iota2((nn, nflat), 0) == _div(_iota2((nn, nflat), 1), FOX_HEADS)).astype(F32)
        cn_cols = _dot_rhs01(cn_s[...], spread)
        ri = _iota2((rows, nflat), 0)
        ci = _iota2((rows, nflat), 1)
        ok = ((ri & (FOX_HEADS - 1)) == (ci & (FOX_HEADS - 1))) & (_div(ci, FOX_HEADS) <= _div(ri, FOX_HEADS))
        sn = _dot_nt(_bf(q_ref[0]), _bf(kn_ref[0])) * scale
        update([jnp.where(ok, sn - cn_cols, NEG_INF)], [_bf(vn_ref[0])])
        o_ref[0] = acc_s[...] / l_s[...] * _sigmoid(fz_ref[0])


def fox_sample(page_table_flat, q_rows, cache_k, cache_v, cache_lft, lfn, kn_flat, vn_flat, fz_rows, nb, npages):
    flat = cache_k.shape[1]
    page = cache_lft.shape[2]
    rows = q_rows.shape[1]
    nn = lfn.shape[2]
    G = _pick(npages, (FOX_GROUP, 2, 1))
    ngroups = npages // G

    def page_map(g):
        return lambda b, jg, pt: (pt[b * npages + (npages - 1 - (jg * G + g))], 0, 0)

    seq = lambda b, jg, pt: (b, 0, 0)
    row_blk = pl.BlockSpec((1, rows, FOX_DH), seq)
    new_blk = pl.BlockSpec((1, kn_flat.shape[1], FOX_DH), seq)
    in_specs = [row_blk]
    in_specs += [pl.BlockSpec((1, flat, FOX_DH), page_map(g)) for g in range(G)]
    in_specs += [pl.BlockSpec((1, flat, FOX_DH), page_map(g)) for g in range(G)]
    in_specs += [pl.BlockSpec((1, FOX_HEADS, page), page_map(g)) for g in range(G)]
    in_specs += [pl.BlockSpec((1, rows, nn), seq), new_blk, new_blk, row_blk]
    return pl.pallas_call(
        functools.partial(_fox_sample_kernel, G, ngroups),
        grid_spec=pltpu.PrefetchScalarGridSpec(
            num_scalar_prefetch=1,
            grid=(nb, ngroups),
            in_specs=in_specs,
            out_specs=row_blk,
            scratch_shapes=[
                pltpu.VMEM((rows, LANES), F32),
                pltpu.VMEM((rows, nn), F32),
                pltpu.VMEM((FOX_HEADS, LANES), F32),
                pltpu.VMEM((rows, LANES), F32),
                pltpu.VMEM((rows, LANES), F32),
                pltpu.VMEM((rows, FOX_DH), F32),
            ],
        ),
        out_shape=jax.ShapeDtypeStruct((nb, rows, FOX_DH), F32),
        compiler_params=_cparams(("parallel", "arbitrary")),
        name="fox_sample",
    )(page_table_flat, q_rows, *([cache_k] * G), *([cache_v] * G), *([cache_lft] * G), lfn, kn_flat, vn_flat, fz_rows)


def _rwkv_mix_kernel(tm, tp, ns, nq, h_ref, hb_ref, st_ref, mu_ref, w1_ref, a1_ref, g1_ref,
                     xr_ref, xk_ref, xv_ref, t1_ref, t2_ref, t3_ref):
    i = pl.program_id(0)
    h = h_ref[...]
    above = jnp.concatenate([hb_ref[SUBLANES - 1:SUBLANES, :], h[:tm - 1]], axis=0)
    row = i * tm + _iota2((tm, 1), 0)
    seq_start = (row >= tp) & (row < tp + ns) & (((row - tp) & (nq - 1)) == 0)
    prev = jnp.where(seq_start, st_ref[...], jnp.where(row == 0, 0.0, above))
    xx = prev - h
    mu = mu_ref[...]
    mix = lambda j: _bf(h + xx * mu[j:j + 1])
    xr_ref[...] = mix(0)
    xk_ref[...] = mix(2)
    xv_ref[...] = mix(3)
    t1_ref[...] = jnp.tanh(_dot(mix(1), _bf(w1_ref[...]))).astype(t1_ref.dtype)
    t2_ref[...] = _dot(mix(4), _bf(a1_ref[...])).astype(t2_ref.dtype)
    t3_ref[...] = _sigmoid(_dot(mix(5), _bf(g1_ref[...]))).astype(t3_ref.dtype)


def rwkv_mix(h, shift_rows, mu, w1, a1, g1, tp, ns, nq):
    m, d = h.shape
    assert nq & (nq - 1) == 0
    tm = _pick(math.gcd(m, tp), (320, 256, 128))
    first = tp // tm
    spec = pl.BlockSpec((tm, d), lambda i: (i, 0))
    whole = lambda w: pl.BlockSpec(w.shape, lambda i: (0, 0))
    narrow = lambda w: pl.BlockSpec((tm, w.shape[1]), lambda i: (i, 0))
    return pl.pallas_call(
        functools.partial(_rwkv_mix_kernel, tm, tp, ns, nq),
        grid=(m // tm,),
        in_specs=[spec,
                  pl.BlockSpec((SUBLANES, d), lambda i: (jnp.maximum(i * (tm // SUBLANES) - 1, 0), 0)),
                  pl.BlockSpec((tm, d), lambda i: (jnp.maximum(i - first, 0), 0)),
                  pl.BlockSpec((6, d), lambda i: (0, 0)), whole(w1), whole(a1), whole(g1)],
        out_specs=[spec] * 3 + [narrow(w1), narrow(a1), narrow(g1)],
        out_shape=[jax.ShapeDtypeStruct((m, d), BF16)] * 3
        + [jax.ShapeDtypeStruct((m, w.shape[1]), BF16) for w in (w1, a1, g1)],
        compiler_params=_cparams(("parallel",)),
        name="rwkv_mix",
    )(h, h, shift_rows, mu, w1, a1, g1)


def _head_ones():
    return (_div(_iota2((LANES, LANES), 0), RWKV_HEAD) == _div(_iota2((LANES, LANES), 1), RWKV_HEAD)).astype(F32)


def _rwkv_chunk_terms(C, valid, r, k, v, wl, al, prm):
    HD = RWKV_HEAD
    m0 = _iota2((1, LANES), 1) < HD
    bones = _head_ones()
    w0, a0, k_k, k_a, r_k = (prm[i:i + 1] for i in (PRM_W0, PRM_A0, PRM_KK, PRM_KA, PRM_RK))
    wlog = -_softplus(-(w0 + wl)) - 0.5
    lw = jnp.where(valid, -jnp.exp(wlog), 0.0)
    a = _sigmoid(a0 + al)
    kkr = k * k_k
    kk = kkr * lax.rsqrt(_dot_rhs01(kkr * kkr, bones) + L2_EPS)
    k2 = k * (1.0 + (a - 1.0) * k_a)
    bonus = _dot_rhs01(r * k2 * r_k, bones) * v
    rm = jnp.where(valid, r, 0.0)
    k2 = jnp.where(valid, k2, 0.0)
    vm = jnp.where(valid, v, 0.0)
    av = jnp.where(valid, -kk, 0.0)
    bv = jnp.where(valid, kk * a, 0.0)

    ri = _iota2((C, C), 0)
    ci = _iota2((C, C), 1)
    yield
    cum = _dot_lhs01((ri >= ci).astype(F32), lw)
    cum_last = cum[C - 1:C, :]
    inv = jnp.exp(-cum)
    rt = rm * jnp.exp(cum)
    at = av * jnp.exp(cum - lw)
    bt = bv * inv
    kt = k2 * inv
    to_end = jnp.exp(cum_last - cum)
    b_end = bv * to_end
    k_end = k2 * to_end

    def split(x):
        return jnp.concatenate([jnp.where(m0, x, 0.0), jnp.where(m0, 0.0, x)], axis=0)

    def halves(x):
        return x[0:C] + x[C:2 * C]

    yield
    at_s = split(at)
    gram = _dot_nt(_bf(jnp.concatenate([at_s, split(rt)], axis=0)), _bf(jnp.concatenate([bt, kt], axis=0)))
    r2 = _iota2((2 * C, 2 * C), 0)
    c2 = _iota2((2 * C, 2 * C), 1)
    same = _div(r2, C) == _div(c2, C)
    bd_strict = same & (r2 > c2)
    bd_incl = same & (r2 >= c2)

    def bd(block, mask):
        return jnp.where(mask, jnp.concatenate([block, block], axis=1), 0.0)

    a_ab = bd(gram[0:2 * C, 0:C], bd_strict)
    a_ak = bd(gram[0:2 * C, C:2 * C], bd_strict)
    r_b = bd(gram[2 * C:4 * C, 0:C], bd_incl)
    r_k2 = bd(gram[2 * C:4 * C, C:2 * C], bd_incl)
    vs = split(vm)
    akv = halves(_dot(_bf(a_ak), _bf(vs)))
    yield
    t_bd = yield from _tri_inv_steps(a_ab, C)
    tw = _dot(_bf(t_bd), _bf(jnp.concatenate([at_s, split(akv)], axis=1)))
    wt = halves(tw[:, :LANES])
    ut = halves(tw[:, LANES:])
    yield
    rp = rt + halves(_dot(_bf(r_b), _bf(split(wt))))
    y0 = halves(_dot(_bf(jnp.concatenate([r_b, r_k2], axis=1)), _bf(jnp.concatenate([split(ut), vs], axis=0))))
    eye = (_iota2((LANES, LANES), 0) == _iota2((LANES, LANES), 1)).astype(F32)
    m_mat = eye * jnp.exp(cum_last) + bones * _dot_tn(_bf(b_end), _bf(wt))
    n_mat = bones * _dot_tn(_bf(jnp.concatenate([b_end, k_end], axis=0)), _bf(jnp.concatenate([ut, vm], axis=0)))
    return rp, y0, bonus, m_mat, n_mat


def _rwkv_prep_kernel(C, chunk_of_step, first_valid, PG, r_ref, k_ref, v_ref, wl_ref, al_ref, prm_ref,
                      rp_ref, y0_ref, bo_ref, m_ref, n_ref):
    c = pl.program_id(0) if chunk_of_step else 0
    valid = (c * C + _iota2((C, 1), 0)) >= first_valid
    sls = [slice(i * LANES, (i + 1) * LANES) for i in range(PG)]
    terms = _run_interleaved([
        _rwkv_chunk_terms(C, valid, r_ref[:, sl], k_ref[:, sl], v_ref[:, sl], wl_ref[:, sl], al_ref[:, sl], prm_ref[:, sl])
        for sl in sls])
    for i, sl in enumerate(sls):
        rp, y0, bonus, m_mat, n_mat = terms[i]
        rp_ref[:, sl] = rp
        y0_ref[:, sl] = y0
        bo_ref[:, sl] = bonus
        m_ref[0, i] = m_mat
        n_ref[0, i] = n_mat


def _rwkv_scan_kernel(NC, rp_ref, y0_ref, bo_ref, g_ref, m_ref, n_ref, prm_ref, s0_ref, o_ref, sout_ref, H):
    c = pl.program_id(1)
    HD = RWKV_HEAD
    bones = _head_ones()

    @pl.when(c == 0)
    def _():
        zero = jnp.zeros((HD, HD), F32)
        for i in range(RWKV_PAIRS):
            top = jnp.concatenate([s0_ref[0, 2 * i], zero], axis=1)
            bot = jnp.concatenate([zero, s0_ref[0, 2 * i + 1]], axis=1)
            H[i] = jnp.concatenate([top, bot], axis=0).T

    def pair_steps(i):
        sl = slice(i * LANES, (i + 1) * LANES)
        h_bd = H[i]
        y = _dot(_bf(rp_ref[:, sl]), _bf(h_bd)) + y0_ref[:, sl]
        H[i] = _dot3(m_ref[0, i], h_bd) + n_ref[0, i]
        yield
        mean = _dot_rhs01(y, bones) * (1.0 / HD)
        dlt = y - mean
        yield
        var = _dot_rhs01(dlt * dlt, bones) * (1.0 / HD)
        yn = dlt * lax.rsqrt(var + GN_EPS) * prm_ref[PRM_LNW:PRM_LNW + 1, sl] + prm_ref[PRM_LNB:PRM_LNB + 1, sl]
        o_ref[:, sl] = ((yn + bo_ref[:, sl]) * g_ref[:, sl]).astype(o_ref.dtype)

    _run_interleaved([pair_steps(i) for i in range(RWKV_PAIRS)])

    @pl.when(c == NC - 1)
    def _():
        for i in range(RWKV_PAIRS):
            ht = H[i].T
            sout_ref[0, 2 * i] = ht[0:HD, 0:HD]
            sout_ref[0, 2 * i + 1] = ht[HD:2 * HD, HD:2 * HD]


def rwkv(r, k, v, wl, al, g, prm, s0, *, B, NC, C, first_valid, name, out_rows=None):
    assert B == 1 or NC == 1
    nblk = B * NC
    rows = nblk * C
    PG = RWKV_PREP_PAIRS
    wide = pl.BlockSpec((C, PG * LANES), lambda blk, pg: (blk, pg))
    mat = pl.BlockSpec((1, PG, LANES, LANES), lambda blk, pg: (blk, pg, 0, 0))
    mat_shape = jax.ShapeDtypeStruct((nblk, RWKV_PAIRS, LANES, LANES), F32)
    row_shape = jax.ShapeDtypeStruct((rows, D_MODEL), F32)
    rp, y0, bonus, m_all, n_all = pl.pallas_call(
        functools.partial(_rwkv_prep_kernel, C, B == 1, first_valid, PG),
        grid=(nblk, RWKV_PAIRS // PG),
        in_specs=[wide] * 5 + [pl.BlockSpec((SUBLANES, PG * LANES), lambda blk, pg: (0, pg))],
        out_specs=[wide, wide, wide, mat, mat],
        out_shape=[row_shape, row_shape, row_shape, mat_shape, mat_shape],
        compiler_params=_cparams(("parallel", "parallel")),
        name=name + "_terms",
    )(r, k, v, wl, al, prm)

    full = pl.BlockSpec((C, D_MODEL), lambda b, c: (b * NC + c, 0))
    mats = pl.BlockSpec((1, RWKV_PAIRS, LANES, LANES), lambda b, c: (b * NC + c, 0, 0, 0))
    st_spec = pl.BlockSpec((1, 2 * RWKV_PAIRS, RWKV_HEAD, RWKV_HEAD), lambda b, c: (b, 0, 0, 0))
    return pl.pallas_call(
        functools.partial(_rwkv_scan_kernel, NC),
        grid=(B, NC),
        in_specs=[full] * 4 + [mats, mats, pl.BlockSpec((SUBLANES, D_MODEL), lambda b, c: (0, 0)), st_spec],
        out_specs=[full, st_spec],
        out_shape=[
            jax.ShapeDtypeStruct((out_rows or rows, D_MODEL), BF16),
            jax.ShapeDtypeStruct((B, 2 * RWKV_PAIRS, RWKV_HEAD, RWKV_HEAD), F32),
        ],
        scratch_shapes=[pltpu.VMEM((RWKV_PAIRS, LANES, LANES), F32)],
        compiler_params=_cparams(("parallel", "arbitrary")),
        name=name + "_scan",
    )(rp, y0, bonus, g, m_all, n_all, prm, s0)


def _rwkv_lanes_kernel(nq, unroll, x_ref, prm_ref, s0_ref, o_ref, sout_ref, y_s):
    HD = RWKV_HEAD
    w0, a0, k_k, k_a, r_k, ln_w, ln_b = (prm_ref[i] for i in (PRM_W0, PRM_A0, PRM_KK, PRM_KA, PRM_RK,
                                                               PRM_LNW, PRM_LNB))
    sout_ref[...] = s0_ref[...]
    for t in range(nq):
        r, k, v, wl, al, g = (x_ref[i, t] for i in range(6))
        wlog = -_softplus(-(w0 + wl)) - 0.5
        w = jnp.exp(-jnp.exp(wlog))
        a = _sigmoid(a0 + al)
        kkr = k * k_k
        kk = kkr * lax.rsqrt(jnp.sum(kkr * kkr, axis=0, keepdims=True) + L2_EPS)
        k2 = k * (1.0 + (a - 1.0) * k_a)
        av = -kk
        bv = kk * a

        def value_row(i, carry):
            s = sout_ref[0, i]
            sa = jnp.sum(s * av, axis=0, keepdims=True)
            s = s * w + sa * bv + x_ref[2, t, pl.ds(i, 1), :] * k2
            sout_ref[0, i] = s
            y_s[pl.ds(i, 1), :] = jnp.sum(s * r, axis=0, keepdims=True)
            return carry

        lax.fori_loop(0, HD, value_row, 0, unroll=unroll)
        y = y_s[...]
        mean = jnp.mean(y, axis=0, keepdims=True)
        dlt = y - mean
        var = jnp.mean(dlt * dlt, axis=0, keepdims=True)
        yn = dlt * lax.rsqrt(var + GN_EPS) * ln_w + ln_b
        bonus = jnp.sum(r * k2 * r_k, axis=0, keepdims=True) * v
        o_ref[t] = ((yn + bonus) * g).astype(o_ref.dtype)


def rwkv_lanes(xs, prm_col, s0):
    _, nq, d, nb = xs.shape
    heads = d // RWKV_HEAD
    st = pl.BlockSpec((1, RWKV_HEAD, RWKV_HEAD, nb), lambda h: (h, 0, 0, 0))
    return pl.pallas_call(
        functools.partial(_rwkv_lanes_kernel, nq, SUBLANES),
        grid=(heads,),
        in_specs=[pl.BlockSpec((6, nq, RWKV_HEAD, nb), lambda h: (0, 0, h, 0)),
                  pl.BlockSpec((SUBLANES, RWKV_HEAD, 1), lambda h: (0, h, 0)), st],
        out_specs=[pl.BlockSpec((nq, RWKV_HEAD, nb), lambda h: (0, h, 0)), st],
        out_shape=[jax.ShapeDtypeStruct((nq, d, nb), F32), jax.ShapeDtypeStruct(s0.shape, F32)],
        scratch_shapes=[pltpu.VMEM((RWKV_HEAD, nb), F32)],
        compiler_params=_cparams(("parallel",)),
        name="rwkv_sample",
    )(xs, prm_col, s0)


def _pad_lanes(vec, offset):
    out = jnp.zeros((1, LANES), F32)
    return lax.dynamic_update_slice(out, vec.reshape(1, -1).astype(F32), (0, offset))


def _sample_rows(arr, row0, nb, nq, front):
    cols = arr.shape[1]
    s = arr[row0:row0 + nb * nq].reshape(nb, nq, cols)
    s = jnp.pad(s, ((0, 0), (front, 0), (0, 0)))
    return s.reshape(nb * (front + nq), cols)


def _only(x):
    assert x.shape[0] == 1
    return x.reshape(x.shape[1:])


def kernel(x_prompt, x_sample, cache_fox_k, cache_fox_v, cache_fox_logf, state_gdn_conv, state_gdn_S,
           state_rwkv_shift, state_rwkv_S, page_table, meta_tokens, ln_mix, ln_mlp, ln_final,
           w_in0, gdn_conv_w, gdn_A_log, gdn_dt_bias, gdn_norm_w, fox_b_f, w_out0,
           rwkv_mu, rwkv_w0, rwkv_w1, rwkv_w2, rwkv_a0, rwkv_a1, rwkv_a2, rwkv_g1, rwkv_g2,
           rwkv_k_k, rwkv_k_a, rwkv_r_k, rwkv_w_r, rwkv_w_k, rwkv_w_v, rwkv_w_o, rwkv_ln_w, rwkv_ln_b,
           w_up, w_down):
    D = D_MODEL
    assert x_prompt.shape[0] == 1 and x_prompt.shape[2] == D
    seq = x_prompt.shape[1]
    nb, nq = x_sample.shape[0], x_sample.shape[1]
    npages = page_table.shape[1]
    tprompt = N_META + seq
    pad = (-tprompt) % LANES
    tp = tprompt + pad
    ns = nb * nq
    R = -(-(tp + ns) // ROW_TILE) * ROW_TILE
    CS = SUBLANES
    front = CS - nq
    assert 3 <= front

    x0 = jnp.concatenate([jnp.zeros((pad, D), F32), meta_tokens.astype(F32), x_prompt[0],
                          x_sample.reshape(ns, D), jnp.zeros((R - tp - ns, D), F32)], axis=0)

    w_in = w_in0[0]
    o_z = GDN_QKV
    o_a = o_z + GDN_QK
    o_b = o_a + GDN_HEADS
    o_fq = o_b + GDN_HEADS
    o_fk = o_fq + FOX_W
    o_fv = o_fk + FOX_W
    o_ff = o_fv + FOX_W
    o_fz = o_ff + FOX_HEADS
    w_big = jnp.concatenate([w_in[:, :o_a], w_in[:, o_fq:o_ff], w_in[:, o_fz:]], axis=1).astype(BF16)
    w_small = jnp.concatenate([w_in[:, o_a:o_fq], w_in[:, o_ff:o_fz],
                               jnp.zeros((D, LANES - 3 * GDN_HEADS), F32)], axis=1).astype(BF16)

    h0 = rmsnorm(x0, ln_mix[0], BF16)
    p = matmul(h0, w_big, name="in_proj")
    ps = matmul(h0, w_small, name="in_proj_small")

    alog_pad = _pad_lanes(gdn_A_log[0], LANE_A)
    dt_pad = _pad_lanes(gdn_dt_bias[0], LANE_A)
    bf_pad = _pad_lanes(fox_b_f[0], LANE_F)
    conv_w = gdn_conv_w[0]
    norm_w = gdn_norm_w[0].reshape(1, GDN_D)

    GC = 64
    o_gdn_p, s_gdn_p = gdn(p, p, ps, COL_Z // GDN_QK, conv_w, alog_pad, dt_pad, norm_w,
                           jnp.zeros((1, GDN_HEADS, GDN_D, GDN_D), F32), None,
                           B=1, NC=tp // GC, C=GC, first_valid=pad, name="gdn_prompt", out_rows=R)
    qkv_ext = _sample_rows(p[:, :GDN_QKV], tp, nb, nq, front)
    z_ext = _sample_rows(p[:, COL_Z:COL_Z + GDN_QK], tp, nb, nq, front)
    ps_ext = _sample_rows(ps, tp, nb, nq, front)
    o_gdn_s, s_gdn_s = gdn(qkv_ext, z_ext, ps_ext, 0, conv_w, alog_pad, dt_pad, norm_w,
                           _only(state_gdn_S), state_gdn_conv, B=nb, NC=1, C=CS, first_valid=front, name="gdn_sample")
    o_gdn_s = o_gdn_s.reshape(nb, CS, GDN_QK)[:, front:].reshape(ns, GDN_QK)

    lf, cq, ck = fox_prep(ps, bf_pad, pad)
    o_fox_p = fox_prompt(p, cq, ck, tp, pad, R)
    pt_flat = page_table.reshape(-1).astype(jnp.int32)
    lf_s = lf[tp:tp + ns, LANE_F:LANE_F + FOX_HEADS].reshape(nb, nq, FOX_HEADS)
    lfn = jnp.tile(jnp.swapaxes(lf_s, 1, 2), (1, nq, 1))
    lfn = jnp.pad(lfn, ((0, 0), (0, 0), (0, SUBLANES - nq)))
    pool = cache_fox_k.shape[1]
    page = cache_fox_k.shape[2]
    cache_k = _only(cache_fox_k).reshape(pool, page * FOX_HEADS, FOX_DH)
    cache_v = _only(cache_fox_v).reshape(pool, page * FOX_HEADS, FOX_DH)
    cache_lft = jnp.swapaxes(_only(cache_fox_logf), 1, 2)
    def sample_heads(col):
        return p[tp:tp + ns, col:col + FOX_W].reshape(nb, nq * FOX_HEADS, FOX_DH)

    o_fox_s = fox_sample(pt_flat, sample_heads(COL_FQ), cache_k, cache_v, cache_lft, lfn,
                         sample_heads(COL_FK), sample_heads(COL_FV), sample_heads(COL_FZ), nb, npages)
    o_fox_s = o_fox_s.reshape(ns, FOX_W).astype(BF16)

    def with_sample_rows(full, sample):
        full = lax.dynamic_update_slice(full, sample, (tp, 0))
        return lax.dynamic_update_slice(full, jnp.zeros((R - tp - ns, full.shape[1]), full.dtype), (tp + ns, 0))

    x1 = matmul([with_sample_rows(o_gdn_p, o_gdn_s), with_sample_rows(o_fox_p, o_fox_s)], w_out0, layer=0,
                res=x0, name="out_proj")
    u0 = matmul(rmsnorm(x1, ln_mlp[0], BF16), w_up, layer=0, act="relu2", out_dtype=BF16, name="mlp_up0")
    x2 = matmul(u0, w_down, layer=0, res=x1, name="mlp_down0")

    h1 = rmsnorm(x2, ln_mix[1], F32)
    h1_s = h1[tp:tp + ns].reshape(nb, nq, D)
    shift_rows = jnp.concatenate([jnp.repeat(state_rwkv_shift[0].astype(F32), nq, axis=0),
                                  jnp.zeros((R - tp - ns, D), F32)], axis=0)
    def pad_cols(w):
        return jnp.pad(w, ((0, 0), (0, LANES - w.shape[1]))).astype(BF16)

    def pad_rows(w):
        return jnp.pad(w, ((0, LANES - w.shape[0]), (0, 0))).astype(BF16)

    xr, xk, xv, t_w, t_a, t_g = rwkv_mix(h1, shift_rows, rwkv_mu[0], pad_cols(rwkv_w1[0]), pad_cols(rwkv_a1[0]),
                                         rwkv_g1[0], tp, ns, nq)
    r_ = matmul(xr, rwkv_w_r, layer=0, name="rwkv_r")
    k_ = matmul(xk, rwkv_w_k, layer=0, name="rwkv_k")
    v_ = matmul(xv, rwkv_w_v, layer=0, name="rwkv_v")
    wl = matmul(t_w, pad_rows(rwkv_w2[0]), name="rwkv_w2")
    al = matmul(t_a, pad_rows(rwkv_a2[0]), name="rwkv_a2")
    g_ = matmul(t_g, rwkv_g2, layer=0, name="rwkv_g2")

    prm = jnp.stack([rwkv_w0[0], rwkv_a0[0], rwkv_k_k[0], rwkv_k_a[0], rwkv_r_k[0].reshape(D), rwkv_ln_w[0], rwkv_ln_b[0],
                     jnp.zeros((D,), F32)], axis=0).astype(F32)
    RC = 64
    o_rw_p, s_rw_p = rwkv(r_, k_, v_, wl, al, g_, prm, jnp.zeros((1, 2 * RWKV_PAIRS, RWKV_HEAD, RWKV_HEAD), F32),
                          B=1, NC=tp // RC, C=RC, first_valid=pad, name="rwkv_prompt", out_rows=R)
    xs = jnp.stack([t[tp:tp + ns] for t in (r_, k_, v_, wl, al, g_)])
    xs = jnp.transpose(xs.reshape(6, nb, nq, D), (0, 2, 3, 1))
    s0_l = jnp.transpose(_only(state_rwkv_S), (1, 2, 3, 0))
    o_l, s_l = rwkv_lanes(xs, prm.reshape(SUBLANES, D, 1), s0_l)
    o_rw_s = jnp.transpose(o_l, (2, 0, 1)).reshape(ns, D).astype(BF16)
    s_rw_s = jnp.transpose(s_l, (3, 0, 1, 2))
    x3 = matmul(with_sample_rows(o_rw_p, o_rw_s), rwkv_w_o, layer=0, res=x2, name="rwkv_o")
    u1 = matmul(rmsnorm(x3, ln_mlp[1], BF16), w_up, layer=1, act="relu2", out_dtype=BF16, name="mlp_up1")
    x4 = matmul(u1, w_down, layer=1, res=x3, name="mlp_down1")
    y = rmsnorm(x4, ln_final, F32)

    r0 = pad
    y_prompt = y[r0 + N_META:tp].reshape(1, seq, D)
    y_sample = y[tp:tp + ns].reshape(nb, nq, D)

    def kv_rows(col):
        blk = p[:, col:col + FOX_W]
        return (blk[r0:tp].reshape(1, 1, tprompt, FOX_HEADS, FOX_DH),
                blk[tp:tp + ns].reshape(1, nb, nq, FOX_HEADS, FOX_DH))

    fk_p, fk_s = kv_rows(COL_FK)
    fv_p, fv_s = kv_rows(COL_FV)
    lf8 = lf[:, LANE_F:LANE_F + FOX_HEADS]
    lf_p = lf8[r0:tp].reshape(1, 1, tprompt, FOX_HEADS)
    lf_sm = lf8[tp:tp + ns].reshape(1, nb, nq, FOX_HEADS)
    cb_p = p[tp - 3:tp, :GDN_QKV].reshape(1, 1, 3, GDN_QKV)
    cb_s = p[tp:tp + ns, :GDN_QKV].reshape(nb, nq, GDN_QKV)[:, nq - 3:][None]
    gs_p = s_gdn_p[None]
    gs_s = s_gdn_s[None]
    sh_p = h1[tp - 1].reshape(1, 1, D)
    sh_s = h1_s[:, nq - 1][None]
    rs_p = s_rw_p[None]
    rs_s = s_rw_s[None]
    return (y_prompt, y_sample, fk_p, fk_s, fv_p, fv_s, lf_p, lf_sm, cb_p, cb_s, gs_p, gs_s, sh_p, sh_s, rs_p, rs_s)
```

```python
import functools
import math

import jax
import jax.numpy as jnp
from jax import lax
from jax.experimental import pallas as pl
from jax.experimental.pallas import tpu as pltpu

F32 = jnp.float32
BF16 = jnp.bfloat16

D_MODEL = 2048
N_META = 16
GDN_HEADS = 8
GDN_D = 128
GDN_QK = GDN_HEADS * GDN_D
GDN_QKV = 3 * GDN_QK
FOX_HEADS = 8
FOX_DH = 128
FOX_W = FOX_HEADS * FOX_DH
RWKV_HEAD = 64
RWKV_PAIRS = D_MODEL // (2 * RWKV_HEAD)
NORM_EPS = 1e-6
L2_EPS = 1e-6
GN_EPS = 64e-5
NEG_INF = -1e30

LANES = 128
SUBLANES = 8
ROW_TILE = 640
MM_ROW_TILE = 1792
MM_WIDE_MIN_COLS = 8192
LOG2E = 1.4426950408889634
VMEM_LIMIT = 48 * 1024 * 1024
FOX_HEADS_PER_STEP = 8
FOX_GROUP = 16
RWKV_PREP_PAIRS = 16

COL_Z = GDN_QKV
COL_FQ = COL_Z + GDN_QK
COL_FK = COL_FQ + FOX_W
COL_FV = COL_FK + FOX_W
COL_FZ = COL_FV + FOX_W
LANE_A = 0
LANE_B = 8
LANE_F = 16
PRM_W0, PRM_A0, PRM_KK, PRM_KA, PRM_RK, PRM_LNW, PRM_LNB = range(7)


def _pick(n, cands):
    for c in cands:
        if n % c == 0:
            return c
    raise ValueError(f"no tile for {n}")


def _cparams(sem):
    return pltpu.CompilerParams(dimension_semantics=sem, vmem_limit_bytes=VMEM_LIMIT)


def _dot(a, b):
    return jnp.dot(a, b, preferred_element_type=F32)


def _dot_nt(a, b):
    return lax.dot_general(a, b, (((1,), (1,)), ((), ())), preferred_element_type=F32)


def _dot_tn(a, b):
    return lax.dot_general(a, b, (((0,), (0,)), ((), ())), preferred_element_type=F32)


def _bf(x):
    return x.astype(BF16)


def _softplus(x):
    return jnp.maximum(x, 0.0) + jnp.log(1.0 + jnp.exp(-jnp.abs(x)))


def _sigmoid(x):
    return 1.0 / (1.0 + jnp.exp(-x))


def _iota2(shape, dim):
    return lax.broadcasted_iota(jnp.int32, shape, dim)


def _div(x, n):
    assert n & (n - 1) == 0
    return x >> (n.bit_length() - 1)


def _split2(x):
    hi = x.astype(BF16)
    return hi, (x - hi.astype(F32)).astype(BF16)


def _split3(x):
    p1 = x.astype(BF16)
    rem = x - p1.astype(F32)
    p2 = rem.astype(BF16)
    return p1, p2, (rem - p2.astype(F32)).astype(BF16)


def _dot3(a, b):
    m = a.shape[0]
    ah, al = _split2(a)
    bh, bl = _split2(b)
    top = _dot(jnp.concatenate([ah, al], axis=0), bh)
    return (top[:m] + top[m:]) + _dot(ah, bl)


def _dot_lhs01(a01, b):
    a = a01.astype(BF16)
    b1, b2, b3 = _split3(b)
    return _dot(a, b1) + (_dot(a, b2) + _dot(a, b3))


def _dot_rhs01(a, b01):
    b = b01.astype(BF16)
    a1, a2, a3 = _split3(a)
    m = a.shape[0]
    out = _dot(jnp.concatenate([a1, a2, a3], axis=0), b)
    return out[:m] + (out[m:2 * m] + out[2 * m:])


def _tri_inv_steps(n_mat, n):
    size = n_mat.shape[0]
    eye = (_iota2((size, size), 0) == _iota2((size, size), 1)).astype(F32)
    t = eye + n_mat
    if n <= 2:
        return t
    p = _dot(_bf(n_mat), _bf(n_mat))
    yield
    m = 2
    while m < n:
        p_bf = _bf(p)
        if 2 * m < n:
            both = _dot(_bf(jnp.concatenate([p, t], axis=0)), p_bf)
            p = both[:size]
            t = t + both[size:]
        else:
            t = t + _dot(_bf(t), p_bf)
        m *= 2
        yield
    return t


def _run_interleaved(gens):
    results = [None] * len(gens)
    live = list(enumerate(gens))
    while live:
        nxt = []
        for idx, gen in live:
            try:
                next(gen)
                nxt.append((idx, gen))
            except StopIteration as stop:
                results[idx] = stop.value
        live = nxt
    return results


def _rmsnorm_kernel(x_ref, w_ref, o_ref):
    x = x_ref[...]
    ms = jnp.mean(x * x, axis=-1, keepdims=True)
    o_ref[...] = (x * lax.rsqrt(ms + NORM_EPS) * w_ref[...]).astype(o_ref.dtype)


def rmsnorm(x, w, out_dtype):
    m, d = x.shape
    tm = _pick(m, (640, 320, 256, 128))
    return pl.pallas_call(
        _rmsnorm_kernel,
        grid=(m // tm,),
        in_specs=[pl.BlockSpec((tm, d), lambda i: (i, 0)), pl.BlockSpec((1, d), lambda i: (0, 0))],
        out_specs=pl.BlockSpec((tm, d), lambda i: (i, 0)),
        out_shape=jax.ShapeDtypeStruct((m, d), out_dtype),
        compiler_params=_cparams(("parallel",)),
        name="rmsnorm",
    )(x, w.reshape(1, d))


def _mm_kernel(nk, nx, act, has_res, *refs):
    x_refs, w_ref = refs[:nx], refs[nx]
    r_ref = refs[nx + 1] if has_res else None
    o_ref = refs[nx + 1 + has_res]

    def product():
        if nx == 1:
            return _dot(x_refs[0][...], _bf(w_ref[...]))
        acc, k0 = None, 0
        for x_ref in x_refs:
            kw = x_ref.shape[1]
            part = _dot(x_ref[...], _bf(w_ref[k0:k0 + kw, :]))
            acc = part if acc is None else acc + part
            k0 += kw
        return acc

    def finish(acc):
        if act == "relu2":
            acc = jnp.square(jnp.maximum(acc, 0.0))
        elif act == "tanh":
            acc = jnp.tanh(acc)
        elif act == "sigmoid":
            acc = _sigmoid(acc)
        if has_res:
            acc = r_ref[...] + acc
        o_ref[...] = acc.astype(o_ref.dtype)

    if nk == 1:
        finish(product())
    else:
        assert act is None and o_ref.dtype == F32
        k = pl.program_id(2)

        @pl.when(k == 0)
        def _():
            o_ref[...] = r_ref[...] if has_res else jnp.zeros(o_ref.shape, F32)

        o_ref[...] += product()


def matmul(x, w, *, layer=None, act=None, res=None, out_dtype=F32, name="matmul"):
    xs = list(x) if isinstance(x, (list, tuple)) else [x]
    m = xs[0].shape[0]
    k = sum(xi.shape[1] for xi in xs)
    n = w.shape[-1]
    tm = _pick(m, (MM_ROW_TILE, ROW_TILE, 512, 256, 128))
    tk = k if k <= 2048 else 2048
    nk = k // tk
    wide_ok = nk == 1 and res is None and n >= MM_WIDE_MIN_COLS
    tn = _pick(n, ((1024,) if wide_ok else ()) + (512, 256, 128))
    assert len(xs) == 1 or nk == 1
    in_specs = [pl.BlockSpec((tm, tk if len(xs) == 1 else xi.shape[1]), lambda i, j, kk: (i, kk)) for xi in xs]
    if w.ndim == 3:
        in_specs.append(pl.BlockSpec((None, tk, tn), lambda i, j, kk: (layer, kk, j)))
    else:
        in_specs.append(pl.BlockSpec((tk, tn), lambda i, j, kk: (kk, j)))
    args = xs + [w]
    if res is not None:
        in_specs.append(pl.BlockSpec((tm, tn), lambda i, j, kk: (i, j)))
        args.append(res)
    return pl.pallas_call(
        functools.partial(_mm_kernel, nk, len(xs), act, res is not None),
        grid=(m // tm, n // tn, nk),
        in_specs=in_specs,
        out_specs=pl.BlockSpec((tm, tn), lambda i, j, kk: (i, j)),
        out_shape=jax.ShapeDtypeStruct((m, n), out_dtype),
        compiler_params=_cparams(("parallel", "parallel", "arbitrary")),
        name=name,
    )(*args)


def _gdn_kernel(C, NC, first_valid, has_conv, qkv_ref, z_ref, ps_ref, cw_ref, alog_ref, dt_ref, nw_ref, s0_ref,
                *rest):
    conv_ref = rest[0] if has_conv else None
    o_ref, sout_ref, S, ext = rest[1:] if has_conv else rest
    c = pl.program_id(1)

    @pl.when(c == 0)
    def _():
        S[...] = s0_ref[0]
        ext[0:SUBLANES, :] = jnp.zeros((SUBLANES, GDN_QKV), F32)

    x = qkv_ref[...]
    ext[SUBLANES:SUBLANES + C, :] = x
    if has_conv:
        assert NC == 1 and 3 <= first_valid < C
        ext[SUBLANES + first_valid - 3:SUBLANES + first_valid, :] = conv_ref[0, 0]
    cw = cw_ref[...]
    y = ((ext[5:5 + C, :] * cw[0:1] + ext[6:6 + C, :] * cw[1:2]) + ext[7:7 + C, :] * cw[2:3]) + ext[8:8 + C, :] * cw[3:4]
    ext[0:SUBLANES, :] = x[C - SUBLANES:C]
    y = y * _sigmoid(y)

    rows = c * C + _iota2((C, 1), 0)
    valid = rows >= first_valid
    ps = ps_ref[...]
    g_all = jnp.where(valid, -jnp.exp(alog_ref[...]) * _softplus(ps + dt_ref[...]), 0.0)
    beta_all = jnp.where(valid, _sigmoid(ps), 0.0)
    ri = _iota2((C, C), 0)
    ci = _iota2((C, C), 1)
    causal = ri >= ci
    strict = ri > ci
    gc = _dot_lhs01(causal.astype(F32), g_all)
    gct = _dot_rhs01(g_all.T, (ri <= ci).astype(F32))
    nw = nw_ref[...]

    def head_steps(h):
        sl = slice(h * GDN_D, (h + 1) * GDN_D)
        qh = y[:, sl]
        kh = y[:, GDN_QK + h * GDN_D:GDN_QK + (h + 1) * GDN_D]
        vh = jnp.where(valid, y[:, 2 * GDN_QK + h * GDN_D:2 * GDN_QK + (h + 1) * GDN_D], 0.0)
        qh = jnp.where(valid, qh * lax.rsqrt(jnp.sum(qh * qh, axis=-1, keepdims=True) + L2_EPS) * GDN_D ** -0.5, 0.0)
        kh = jnp.where(valid, kh * lax.rsqrt(jnp.sum(kh * kh, axis=-1, keepdims=True) + L2_EPS), 0.0)
        bcol = beta_all[:, LANE_B + h:LANE_B + h + 1]
        gcol = gc[:, LANE_A + h:LANE_A + h + 1]
        grow = gct[LANE_A + h:LANE_A + h + 1, :]
        glast = gc[C - 1:C, LANE_A + h:LANE_A + h + 1]
        diff = gcol - grow
        decay = jnp.where(causal, jnp.exp(jnp.where(causal, diff, 0.0)), 0.0)
        kb = kh * bcol
        a_mat = jnp.where(strict, _dot_nt(_bf(kb), _bf(kh)) * decay, 0.0)
        qk = _dot_nt(_bf(qh), _bf(kh)) * decay
        eg = jnp.exp(gcol)
        rhs = jnp.concatenate([vh * bcol, kb * eg], axis=1)
        q_dec = qh * eg
        k_dec = kh * jnp.exp(glast - gcol)
        yield
        t_mat = yield from _tri_inv_steps(-a_mat, C)
        sol = _dot3(t_mat, rhs)
        u = sol[:, :GDN_D]
        w = sol[:, GDN_D:]
        yield
        s_h = S[h]
        s_bf = _bf(s_h)
        v_new = u - _dot(_bf(w), s_bf)
        o_state = _dot(_bf(q_dec), s_bf)
        yield
        o = o_state + _dot(_bf(qk), _bf(v_new))
        S[h] = s_h * jnp.exp(glast) + _dot_tn(_bf(k_dec), _bf(v_new))
        yield
        on = o * lax.rsqrt(jnp.mean(o * o, axis=-1, keepdims=True) + NORM_EPS) * nw
        zh = z_ref[:, sl]
        o_ref[:, sl] = (on * (zh * _sigmoid(zh))).astype(o_ref.dtype)

    _run_interleaved([head_steps(h) for h in range(GDN_HEADS)])

    @pl.when(c == NC - 1)
    def _():
        sout_ref[0] = S[...]


def gdn(qkv_arr, z_arr, ps_arr, z_col, conv_w, alog_pad, dt_pad, norm_w, s0, conv0, *, B, NC, C, first_valid, name,
        out_rows=None):
    rows = B * NC * C
    has_conv = conv0 is not None
    extra_specs = [pl.BlockSpec((1, 1, 3, GDN_QKV), lambda b, c: (0, b, 0, 0))] if has_conv else []
    extra_args = [conv0] if has_conv else []
    return pl.pallas_call(
        functools.partial(_gdn_kernel, C, NC, first_valid, has_conv),
        grid=(B, NC),
        in_specs=[
            pl.BlockSpec((C, GDN_QKV), lambda b, c: (b * NC + c, 0)),
            pl.BlockSpec((C, GDN_QK), lambda b, c: (b * NC + c, z_col)),
            pl.BlockSpec((C, LANES), lambda b, c: (b * NC + c, 0)),
            pl.BlockSpec((4, GDN_QKV), lambda b, c: (0, 0)),
            pl.BlockSpec((1, LANES), lambda b, c: (0, 0)),
            pl.BlockSpec((1, LANES), lambda b, c: (0, 0)),
            pl.BlockSpec((1, GDN_D), lambda b, c: (0, 0)),
            pl.BlockSpec((1, GDN_HEADS, GDN_D, GDN_D), lambda b, c: (b, 0, 0, 0)),
        ] + extra_specs,
        out_specs=[
            pl.BlockSpec((C, GDN_QK), lambda b, c: (b * NC + c, 0)),
            pl.BlockSpec((1, GDN_HEADS, GDN_D, GDN_D), lambda b, c: (b, 0, 0, 0)),
        ],
        out_shape=[
            jax.ShapeDtypeStruct((out_rows or rows, GDN_QK), BF16),
            jax.ShapeDtypeStruct((B, GDN_HEADS, GDN_D, GDN_D), F32),
        ],
        scratch_shapes=[pltpu.VMEM((GDN_HEADS, GDN_D, GDN_D), F32), pltpu.VMEM((C + SUBLANES, GDN_QKV), F32)],
        compiler_params=_cparams(("parallel", "arbitrary")),
        name=name,
    )(qkv_arr, z_arr, ps_arr, conv_w, alog_pad, dt_pad, norm_w, s0, *extra_args)


def _fox_prep_kernel(tb, first_valid, ps_ref, bf_ref, lf_ref, cq_ref, ck_ref, carry):
    i = pl.program_id(0)

    @pl.when(i == 0)
    def _():
        carry[...] = jnp.zeros(carry.shape, F32)

    x = ps_ref[...] + bf_ref[...]
    rows = i * tb + _iota2((tb, 1), 0)
    lf = jnp.where(rows >= first_valid, jnp.minimum(x, 0.0) - jnp.log(1.0 + jnp.exp(-jnp.abs(x))), 0.0)
    tri = (_iota2((tb, tb), 0) >= _iota2((tb, tb), 1)).astype(F32)
    c = _dot_lhs01(tri, lf) + carry[0:1, :]
    carry[...] = jnp.broadcast_to(c[tb - 1:tb, :], carry.shape)
    lf_ref[...] = lf
    c2 = c * LOG2E
    for h in range(FOX_HEADS):
        cq_ref[h] = jnp.broadcast_to(c2[:, LANE_F + h:LANE_F + h + 1], (tb, LANES))
    ck_ref[...] = c2.T[LANE_F:LANE_F + FOX_HEADS, :]


def fox_prep(ps, bf_pad, first_valid):
    r = ps.shape[0]
    tb = _pick(r, (ROW_TILE, 512, 256, 128))
    return pl.pallas_call(
        functools.partial(_fox_prep_kernel, tb, first_valid),
        grid=(r // tb,),
        in_specs=[pl.BlockSpec((tb, LANES), lambda i: (i, 0)), pl.BlockSpec((1, LANES), lambda i: (0, 0))],
        out_specs=[
            pl.BlockSpec((tb, LANES), lambda i: (i, 0)),
            pl.BlockSpec((FOX_HEADS, tb, LANES), lambda i: (0, i, 0)),
            pl.BlockSpec((FOX_HEADS, tb), lambda i: (0, i)),
        ],
        out_shape=[
            jax.ShapeDtypeStruct((r, LANES), F32),
            jax.ShapeDtypeStruct((FOX_HEADS, r, LANES), F32),
            jax.ShapeDtypeStruct((FOX_HEADS, r), F32),
        ],
        scratch_shapes=[pltpu.VMEM((SUBLANES, LANES), F32)],
        compiler_params=_cparams(("arbitrary",)),
        name="fox_prep",
    )(ps, bf_pad)


def _fox_flash_kernel(tq, first_valid, qi_ref, ki_ref, qt_ref, k_ref, vt_ref, ck_ref, cq_ref, fz_ref, o_ref,
                      m_s, l_s, acc_s, t_s):
    step = pl.program_id(1)
    qi = qi_ref[step]
    ki = ki_ref[step]
    reps = tq // LANES
    heads = qt_ref.shape[0]

    @pl.when(ki == 0)
    def _():
        m_s[...] = jnp.full(m_s.shape, NEG_INF, F32)
        l_s[...] = jnp.zeros(l_s.shape, F32)
        acc_s[...] = jnp.zeros(acc_s.shape, F32)

    def head_steps(h, masked):
        cols = slice(h * FOX_DH, (h + 1) * FOX_DH)
        qt = _bf(qt_ref[h] * (FOX_DH ** -0.5 * LOG2E))
        cq = cq_ref[h]
        m_prev = m_s[h, 0:1, :]
        kb = LANES
        groups = kb // SUBLANES
        mx = None
        for b in range(tq // kb):
            rs = slice(b * kb, (b + 1) * kb)
            t = _dot(_bf(k_ref[rs, cols]), qt) - jnp.concatenate([ck_ref[h, rs, :]] * reps, axis=1)
            if masked:
                kpos = ki * tq + b * kb + _iota2((kb, tq), 0)
                qpos = qi * tq + _iota2((kb, tq), 1)
                t = jnp.where((kpos <= qpos) & (kpos >= first_valid), t, NEG_INF)
            t_s[h, rs, :] = t
            part = jnp.max(t.reshape(groups, SUBLANES, tq), axis=0)
            mx = part if mx is None else jnp.maximum(mx, part)
            yield
        m_new = jnp.maximum(m_prev, jnp.max(mx, axis=0, keepdims=True) + cq)
        shift = cq - m_new
        alpha = jnp.exp2(m_prev - m_new)
        lsum = None
        pv = None
        for b in range(tq // kb):
            rs = slice(b * kb, (b + 1) * kb)
            p = jnp.exp2(t_s[h, rs, :] + shift)
            part = jnp.sum(p.reshape(groups, SUBLANES, tq), axis=0)
            lsum = part if lsum is None else lsum + part
            d = _dot(_bf(vt_ref[h, :, rs]), _bf(p))
            pv = d if pv is None else pv + d
            yield
        l_new = alpha * l_s[h, 0:1, :] + jnp.sum(lsum, axis=0, keepdims=True)
        l_s[h] = jnp.broadcast_to(l_new, (SUBLANES, tq))
        acc_s[h] = alpha * acc_s[h] + pv
        m_s[h] = jnp.broadcast_to(m_new, (SUBLANES, tq))

    def accumulate(masked):
        _run_interleaved([head_steps(h, masked) for h in range(heads)])

    edge = (ki == qi) | (ki == 0)
    pl.when(edge)(lambda: accumulate(True))
    pl.when(jnp.logical_not(edge))(lambda: accumulate(False))

    @pl.when(ki == qi)
    def _():
        rows = qi * tq + _iota2((tq, 1), 0)
        for h in range(heads):
            cols = slice(h * FOX_DH, (h + 1) * FOX_DH)
            o = (acc_s[h] / l_s[h, 0:1, :]).T * _sigmoid(fz_ref[:, cols])
            o_ref[:, cols] = jnp.where(rows >= first_valid, o, 0.0).astype(o_ref.dtype)


def fox_prompt(p, c_rep, c_row, tp, first_valid, out_rows):
    tq = _pick(tp, (ROW_TILE, 512, 384, 256, 128))
    nq = tp // tq
    assert first_valid < tq
    pairs = [(qi, ki) for qi in range(nq) for ki in range(qi + 1)]
    qi_arr = jnp.array([a for a, _ in pairs], jnp.int32)
    ki_arr = jnp.array([b for _, b in pairs], jnp.int32)
    hp = FOX_HEADS_PER_STEP
    wid = hp * FOX_DH
    cb = lambda col, g: col // wid + g

    def heads_t(col, dtype):
        return jnp.transpose(p[:tp, col:col + FOX_W].reshape(tp, FOX_HEADS, FOX_DH), (1, 2, 0)).astype(dtype)

    return pl.pallas_call(
        functools.partial(_fox_flash_kernel, tq, first_valid),
        grid_spec=pltpu.PrefetchScalarGridSpec(
            num_scalar_prefetch=2,
            grid=(FOX_HEADS // hp, len(pairs)),
            in_specs=[
                pl.BlockSpec((hp, FOX_DH, tq), lambda g, t, qa, ka: (g, 0, qa[t])),
                pl.BlockSpec((tq, wid), lambda g, t, qa, ka: (ka[t], cb(COL_FK, g))),
                pl.BlockSpec((hp, FOX_DH, tq), lambda g, t, qa, ka: (g, 0, ka[t])),
                pl.BlockSpec((hp, tq, LANES), lambda g, t, qa, ka: (g, ka[t], 0)),
                pl.BlockSpec((hp, 1, tq), lambda g, t, qa, ka: (g, 0, qa[t])),
                pl.BlockSpec((tq, wid), lambda g, t, qa, ka: (qa[t], cb(COL_FZ, g))),
            ],
            out_specs=pl.BlockSpec((tq, wid), lambda g, t, qa, ka: (qa[t], g)),
            scratch_shapes=[pltpu.VMEM((hp, SUBLANES, tq), F32), pltpu.VMEM((hp, SUBLANES, tq), F32),
                            pltpu.VMEM((hp, FOX_DH, tq), F32), pltpu.VMEM((hp, tq, tq), F32)],
        ),
        out_shape=jax.ShapeDtypeStruct((out_rows, FOX_W), BF16),
        compiler_params=_cparams(("parallel", "arbitrary")),
        name="fox_prompt",
    )(qi_arr, ki_arr, heads_t(COL_FQ, F32), p, heads_t(COL_FV, BF16), c_rep, c_row.reshape(FOX_HEADS, 1, -1), p)


def _fox_sample_kernel(G, ngroups, pt_ref, q_ref, *refs):
    k_refs, v_refs, lft_refs = refs[0:G], refs[G:2 * G], refs[2 * G:3 * G]
    lfn_ref, kn_ref, vn_ref, fz_ref, o_ref, cq_s, cn_s, carry_s, m_s, l_s, acc_s = refs[3 * G:]
    jg = pl.program_id(1)
    rows = q_ref.shape[1]
    page = lft_refs[0].shape[2]
    flat = page * FOX_HEADS
    nn = lfn_ref.shape[2]
    scale = FOX_DH ** -0.5

    @pl.when(jg == 0)
    def _():
        upper = (_iota2((nn, nn), 0) <= _iota2((nn, nn), 1)).astype(F32)
        cn = _dot_rhs01(lfn_ref[0], upper)
        cn_s[...] = cn
        own_q = _iota2((rows, nn), 1) == _div(_iota2((rows, nn), 0), FOX_HEADS)
        cq_s[...] = jnp.broadcast_to(jnp.sum(jnp.where(own_q, cn, 0.0), axis=-1, keepdims=True), cq_s.shape)
        carry_s[...] = jnp.zeros(carry_s.shape, F32)
        m_s[...] = jnp.full(m_s.shape, NEG_INF, F32)
        l_s[...] = jnp.zeros(l_s.shape, F32)
        acc_s[...] = jnp.zeros(acc_s.shape, F32)

    def update(ts, vals_bf):
        cq = cq_s[:, 0:1]
        m_prev = m_s[:, 0:1]
        t_max = functools.reduce(jnp.maximum, ts)
        m_new = jnp.maximum(m_prev, jnp.max(t_max, axis=-1, keepdims=True) + cq)
        shift = cq - m_new
        ps = [jnp.exp(t + shift) for t in ts]
        alpha = jnp.exp(m_prev - m_new)
        l_s[...] = alpha * l_s[...] + jnp.sum(functools.reduce(jnp.add, ps), axis=-1, keepdims=True)
        pv = functools.reduce(jnp.add, [_dot(_bf(p), v) for p, v in zip(ps, vals_bf)])
        acc_s[...] = alpha * acc_s[...] + pv
        m_s[...] = jnp.broadcast_to(m_new, m_s.shape)

    own = (_iota2((rows, flat), 0) & (FOX_HEADS - 1)) == (_iota2((rows, flat), 1) & (FOX_HEADS - 1))
    later = (_iota2((page, flat), 0) > _div(_iota2((page, flat), 1), FOX_HEADS)).astype(BF16)
    head_col = _iota2((FOX_HEADS, flat), 0) == (_iota2((FOX_HEADS, flat), 1) & (FOX_HEADS - 1))
    carry = carry_s[:, 0:1]
    q_bf = _bf(q_ref[0])
    lfts = [lft_refs[g][0] for g in range(G)]
    suffixes = _dot_rhs01(jnp.concatenate(lfts, axis=0), later)
    ts = []
    for g in range(G):
        suffix = suffixes[g * FOX_HEADS:(g + 1) * FOX_HEADS]
        d = jnp.sum(jnp.where(head_col, suffix + carry, 0.0), axis=0, keepdims=True)
        carry = carry + jnp.sum(lfts[g], axis=-1, keepdims=True)
        s = _dot_nt(q_bf, _bf(k_refs[g][0])) * scale
        ts.append(jnp.where(own, s + d, NEG_INF))
    update(ts, [_bf(v_refs[g][0]) for g in range(G)])
    carry_s[...] = jnp.broadcast_to(carry, carry_s.shape)

    @pl.when(jg == ngroups - 1)
    def _():
        nflat = kn_ref.shape[1]
        spread = (_iota2((nn, nflat), 0) == _div(_iota2((nn, nflat), 1), FOX_HEADS)).astype(F32)
        cn_cols = _dot_rhs01(cn_s[...], spread)
        ri = _iota2((rows, nflat), 0)
        ci = _iota2((rows, nflat), 1)
        ok = ((ri & (FOX_HEADS - 1)) == (ci & (FOX_HEADS - 1))) & (_div(ci, FOX_HEADS) <= _div(ri, FOX_HEADS))
        sn = _dot_nt(_bf(q_ref[0]), _bf(kn_ref[0])) * scale
        update([jnp.where(ok, sn - cn_cols, NEG_INF)], [_bf(vn_ref[0])])
        o_ref[0] = acc_s[...] / l_s[...] * _sigmoid(fz_ref[0])


def fox_sample(page_table_flat, q_rows, cache_k, cache_v, cache_lft, lfn, kn_flat, vn_flat, fz_rows, nb, npages):
    flat = cache_k.shape[1]
    page = cache_lft.shape[2]
    rows = q_rows.shape[1]
    nn = lfn.shape[2]
    G = _pick(npages, (FOX_GROUP, 2, 1))
    ngroups = npages // G

    def page_map(g):
        return lambda b, jg, pt: (pt[b * npages + (npages - 1 - (jg * G + g))], 0, 0)

    seq = lambda b, jg, pt: (b, 0, 0)
    row_blk = pl.BlockSpec((1, rows, FOX_DH), seq)
    new_blk = pl.BlockSpec((1, kn_flat.shape[1], FOX_DH), seq)
    in_specs = [row_blk]
    in_specs += [pl.BlockSpec((1, flat, FOX_DH), page_map(g)) for g in range(G)]
    in_specs += [pl.BlockSpec((1, flat, FOX_DH), page_map(g)) for g in range(G)]
    in_specs += [pl.BlockSpec((1, FOX_HEADS, page), page_map(g)) for g in range(G)]
    in_specs += [pl.BlockSpec((1, rows, nn), seq), new_blk, new_blk, row_blk]
    return pl.pallas_call(
        functools.partial(_fox_sample_kernel, G, ngroups),
        grid_spec=pltpu.PrefetchScalarGridSpec(
            num_scalar_prefetch=1,
            grid=(nb, ngroups),
            in_specs=in_specs,
            out_specs=row_blk,
            scratch_shapes=[
                pltpu.VMEM((rows, LANES), F32),
                pltpu.VMEM((rows, nn), F32),
                pltpu.VMEM((FOX_HEADS, LANES), F32),
                pltpu.VMEM((rows, LANES), F32),
                pltpu.VMEM((rows, LANES), F32),
                pltpu.VMEM((rows, FOX_DH), F32),
            ],
        ),
        out_shape=jax.ShapeDtypeStruct((nb, rows, FOX_DH), F32),
        compiler_params=_cparams(("parallel", "arbitrary")),
        name="fox_sample",
    )(page_table_flat, q_rows, *([cache_k] * G), *([cache_v] * G), *([cache_lft] * G), lfn, kn_flat, vn_flat, fz_rows)


def _rwkv_mix_kernel(tm, tp, ns, nq, h_ref, hb_ref, st_ref, mu_ref, w1_ref, a1_ref, g1_ref,
                     xr_ref, xk_ref, xv_ref, t1_ref, t2_ref, t3_ref):
    i = pl.program_id(0)
    h = h_ref[...]
    above = jnp.concatenate([hb_ref[SUBLANES - 1:SUBLANES, :], h[:tm - 1]], axis=0)
    row = i * tm + _iota2((tm, 1), 0)
    seq_start = (row >= tp) & (row < tp + ns) & (((row - tp) & (nq - 1)) == 0)
    prev = jnp.where(seq_start, st_ref[...], jnp.where(row == 0, 0.0, above))
    xx = prev - h
    mu = mu_ref[...]
    mix = lambda j: _bf(h + xx * mu[j:j + 1])
    xr_ref[...] = mix(0)
    xk_ref[...] = mix(2)
    xv_ref[...] = mix(3)
    t1_ref[...] = jnp.tanh(_dot(mix(1), _bf(w1_ref[...]))).astype(t1_ref.dtype)
    t2_ref[...] = _dot(mix(4), _bf(a1_ref[...])).astype(t2_ref.dtype)
    t3_ref[...] = _sigmoid(_dot(mix(5), _bf(g1_ref[...]))).astype(t3_ref.dtype)


def rwkv_mix(h, shift_rows, mu, w1, a1, g1, tp, ns, nq):
    m, d = h.shape
    assert nq & (nq - 1) == 0
    tm = _pick(math.gcd(m, tp), (320, 256, 128))
    first = tp // tm
    spec = pl.BlockSpec((tm, d), lambda i: (i, 0))
    whole = lambda w: pl.BlockSpec(w.shape, lambda i: (0, 0))
    narrow = lambda w: pl.BlockSpec((tm, w.shape[1]), lambda i: (i, 0))
    return pl.pallas_call(
        functools.partial(_rwkv_mix_kernel, tm, tp, ns, nq),
        grid=(m // tm,),
        in_specs=[spec,
                  pl.BlockSpec((SUBLANES, d), lambda i: (jnp.maximum(i * (tm // SUBLANES) - 1, 0), 0)),
                  pl.BlockSpec((tm, d), lambda i: (jnp.maximum(i - first, 0), 0)),
                  pl.BlockSpec((6, d), lambda i: (0, 0)), whole(w1), whole(a1), whole(g1)],
        out_specs=[spec] * 3 + [narrow(w1), narrow(a1), narrow(g1)],
        out_shape=[jax.ShapeDtypeStruct((m, d), BF16)] * 3
        + [jax.ShapeDtypeStruct((m, w.shape[1]), BF16) for w in (w1, a1, g1)],
        compiler_params=_cparams(("parallel",)),
        name="rwkv_mix",
    )(h, h, shift_rows, mu, w1, a1, g1)


def _head_ones():
    return (_div(_iota2((LANES, LANES), 0), RWKV_HEAD) == _div(_iota2((LANES, LANES), 1), RWKV_HEAD)).astype(F32)


def _rwkv_chunk_terms(C, valid, r, k, v, wl, al, prm):
    HD = RWKV_HEAD
    m0 = _iota2((1, LANES), 1) < HD
    bones = _head_ones()
    w0, a0, k_k, k_a, r_k = (prm[i:i + 1] for i in (PRM_W0, PRM_A0, PRM_KK, PRM_KA, PRM_RK))
    wlog = -_softplus(-(w0 + wl)) - 0.5
    lw = jnp.where(valid, -jnp.exp(wlog), 0.0)
    a = _sigmoid(a0 + al)
    kkr = k * k_k
    kk = kkr * lax.rsqrt(_dot_rhs01(kkr * kkr, bones) + L2_EPS)
    k2 = k * (1.0 + (a - 1.0) * k_a)
    bonus = _dot_rhs01(r * k2 * r_k, bones) * v
    rm = jnp.where(valid, r, 0.0)
    k2 = jnp.where(valid, k2, 0.0)
    vm = jnp.where(valid, v, 0.0)
    av = jnp.where(valid, -kk, 0.0)
    bv = jnp.where(valid, kk * a, 0.0)

    ri = _iota2((C, C), 0)
    ci = _iota2((C, C), 1)
    yield
    cum = _dot_lhs01((ri >= ci).astype(F32), lw)
    cum_last = cum[C - 1:C, :]
    inv = jnp.exp(-cum)
    rt = rm * jnp.exp(cum)
    at = av * jnp.exp(cum - lw)
    bt = bv * inv
    kt = k2 * inv
    to_end = jnp.exp(cum_last - cum)
    b_end = bv * to_end
    k_end = k2 * to_end

    def split(x):
        return jnp.concatenate([jnp.where(m0, x, 0.0), jnp.where(m0, 0.0, x)], axis=0)

    def halves(x):
        return x[0:C] + x[C:2 * C]

    yield
    at_s = split(at)
    gram = _dot_nt(_bf(jnp.concatenate([at_s, split(rt)], axis=0)), _bf(jnp.concatenate([bt, kt], axis=0)))
    r2 = _iota2((2 * C, 2 * C), 0)
    c2 = _iota2((2 * C, 2 * C), 1)
    same = _div(r2, C) == _div(c2, C)
    bd_strict = same & (r2 > c2)
    bd_incl = same & (r2 >= c2)

    def bd(block, mask):
        return jnp.where(mask, jnp.concatenate([block, block], axis=1), 0.0)

    a_ab = bd(gram[0:2 * C, 0:C], bd_strict)
    a_ak = bd(gram[0:2 * C, C:2 * C], bd_strict)
    r_b = bd(gram[2 * C:4 * C, 0:C], bd_incl)
    r_k2 = bd(gram[2 * C:4 * C, C:2 * C], bd_incl)
    vs = split(vm)
    akv = halves(_dot(_bf(a_ak), _bf(vs)))
    yield
    t_bd = yield from _tri_inv_steps(a_ab, C)
    tw = _dot(_bf(t_bd), _bf(jnp.concatenate([at_s, split(akv)], axis=1)))
    wt = halves(tw[:, :LANES])
    ut = halves(tw[:, LANES:])
    yield
    rp = rt + halves(_dot(_bf(r_b), _bf(split(wt))))
    y0 = halves(_dot(_bf(jnp.concatenate([r_b, r_k2], axis=1)), _bf(jnp.concatenate([split(ut), vs], axis=0))))
    eye = (_iota2((LANES, LANES), 0) == _iota2((LANES, LANES), 1)).astype(F32)
    m_mat = eye * jnp.exp(cum_last) + bones * _dot_tn(_bf(b_end), _bf(wt))
    n_mat = bones * _dot_tn(_bf(jnp.concatenate([b_end, k_end], axis=0)), _bf(jnp.concatenate([ut, vm], axis=0)))
    return rp, y0, bonus, m_mat, n_mat


def _rwkv_prep_kernel(C, chunk_of_step, first_valid, PG, r_ref, k_ref, v_ref, wl_ref, al_ref, prm_ref,
                      rp_ref, y0_ref, bo_ref, m_ref, n_ref):
    c = pl.program_id(0) if chunk_of_step else 0
    valid = (c * C + _iota2((C, 1), 0)) >= first_valid
    sls = [slice(i * LANES, (i + 1) * LANES) for i in range(PG)]
    terms = _run_interleaved([
        _rwkv_chunk_terms(C, valid, r_ref[:, sl], k_ref[:, sl], v_ref[:, sl], wl_ref[:, sl], al_ref[:, sl], prm_ref[:, sl])
        for sl in sls])
    for i, sl in enumerate(sls):
        rp, y0, bonus, m_mat, n_mat = terms[i]
        rp_ref[:, sl] = rp
        y0_ref[:, sl] = y0
        bo_ref[:, sl] = bonus
        m_ref[0, i] = m_mat
        n_ref[0, i] = n_mat


def _rwkv_scan_kernel(NC, rp_ref, y0_ref, bo_ref, g_ref, m_ref, n_ref, prm_ref, s0_ref, o_ref, sout_ref, H):
    c = pl.program_id(1)
    HD = RWKV_HEAD
    bones = _head_ones()

    @pl.when(c == 0)
    def _():
        zero = jnp.zeros((HD, HD), F32)
        for i in range(RWKV_PAIRS):
            top = jnp.concatenate([s0_ref[0, 2 * i], zero], axis=1)
            bot = jnp.concatenate([zero, s0_ref[0, 2 * i + 1]], axis=1)
            H[i] = jnp.concatenate([top, bot], axis=0).T

    def pair_steps(i):
        sl = slice(i * LANES, (i + 1) * LANES)
        h_bd = H[i]
        y = _dot(_bf(rp_ref[:, sl]), _bf(h_bd)) + y0_ref[:, sl]
        H[i] = _dot3(m_ref[0, i], h_bd) + n_ref[0, i]
        yield
        mean = _dot_rhs01(y, bones) * (1.0 / HD)
        dlt = y - mean
        yield
        var = _dot_rhs01(dlt * dlt, bones) * (1.0 / HD)
        yn = dlt * lax.rsqrt(var + GN_EPS) * prm_ref[PRM_LNW:PRM_LNW + 1, sl] + prm_ref[PRM_LNB:PRM_LNB + 1, sl]
        o_ref[:, sl] = ((yn + bo_ref[:, sl]) * g_ref[:, sl]).astype(o_ref.dtype)

    _run_interleaved([pair_steps(i) for i in range(RWKV_PAIRS)])

    @pl.when(c == NC - 1)
    def _():
        for i in range(RWKV_PAIRS):
            ht = H[i].T
            sout_ref[0, 2 * i] = ht[0:HD, 0:HD]
            sout_ref[0, 2 * i + 1] = ht[HD:2 * HD, HD:2 * HD]


def rwkv(r, k, v, wl, al, g, prm, s0, *, B, NC, C, first_valid, name, out_rows=None):
    assert B == 1 or NC == 1
    nblk = B * NC
    rows = nblk * C
    PG = RWKV_PREP_PAIRS
    wide = pl.BlockSpec((C, PG * LANES), lambda blk, pg: (blk, pg))
    mat = pl.BlockSpec((1, PG, LANES, LANES), lambda blk, pg: (blk, pg, 0, 0))
    mat_shape = jax.ShapeDtypeStruct((nblk, RWKV_PAIRS, LANES, LANES), F32)
    row_shape = jax.ShapeDtypeStruct((rows, D_MODEL), F32)
    rp, y0, bonus, m_all, n_all = pl.pallas_call(
        functools.partial(_rwkv_prep_kernel, C, B == 1, first_valid, PG),
        grid=(nblk, RWKV_PAIRS // PG),
        in_specs=[wide] * 5 + [pl.BlockSpec((SUBLANES, PG * LANES), lambda blk, pg: (0, pg))],
        out_specs=[wide, wide, wide, mat, mat],
        out_shape=[row_shape, row_shape, row_shape, mat_shape, mat_shape],
        compiler_params=_cparams(("parallel", "parallel")),
        name=name + "_terms",
    )(r, k, v, wl, al, prm)

    full = pl.BlockSpec((C, D_MODEL), lambda b, c: (b * NC + c, 0))
    mats = pl.BlockSpec((1, RWKV_PAIRS, LANES, LANES), lambda b, c: (b * NC + c, 0, 0, 0))
    st_spec = pl.BlockSpec((1, 2 * RWKV_PAIRS, RWKV_HEAD, RWKV_HEAD), lambda b, c: (b, 0, 0, 0))
    return pl.pallas_call(
        functools.partial(_rwkv_scan_kernel, NC),
        grid=(B, NC),
        in_specs=[full] * 4 + [mats, mats, pl.BlockSpec((SUBLANES, D_MODEL), lambda b, c: (0, 0)), st_spec],
        out_specs=[full, st_spec],
        out_shape=[
            jax.ShapeDtypeStruct((out_rows or rows, D_MODEL), BF16),
            jax.ShapeDtypeStruct((B, 2 * RWKV_PAIRS, RWKV_HEAD, RWKV_HEAD), F32),
        ],
        scratch_shapes=[pltpu.VMEM((RWKV_PAIRS, LANES, LANES), F32)],
        compiler_params=_cparams(("parallel", "arbitrary")),
        name=name + "_scan",
    )(rp, y0, bonus, g, m_all, n_all, prm, s0)


def _rwkv_lanes_kernel(nq, unroll, x_ref, prm_ref, s0_ref, o_ref, sout_ref, y_s):
    HD = RWKV_HEAD
    w0, a0, k_k, k_a, r_k, ln_w, ln_b = (prm_ref[i] for i in (PRM_W0, PRM_A0, PRM_KK, PRM_KA, PRM_RK,
                                                               PRM_LNW, PRM_LNB))
    sout_ref[...] = s0_ref[...]
    for t in range(nq):
        r, k, v, wl, al, g = (x_ref[i, t] for i in range(6))
        wlog = -_softplus(-(w0 + wl)) - 0.5
        w = jnp.exp(-jnp.exp(wlog))
        a = _sigmoid(a0 + al)
        kkr = k * k_k
        kk = kkr * lax.rsqrt(jnp.sum(kkr * kkr, axis=0, keepdims=True) + L2_EPS)
        k2 = k * (1.0 + (a - 1.0) * k_a)
        av = -kk
        bv = kk * a

        def value_row(i, carry):
            s = sout_ref[0, i]
            sa = jnp.sum(s * av, axis=0, keepdims=True)
            s = s * w + sa * bv + x_ref[2, t, pl.ds(i, 1), :] * k2
            sout_ref[0, i] = s
            y_s[pl.ds(i, 1), :] = jnp.sum(s * r, axis=0, keepdims=True)
            return carry

        lax.fori_loop(0, HD, value_row, 0, unroll=unroll)
        y = y_s[...]
        mean = jnp.mean(y, axis=0, keepdims=True)
        dlt = y - mean
        var = jnp.mean(dlt * dlt, axis=0, keepdims=True)
        yn = dlt * lax.rsqrt(var + GN_EPS) * ln_w + ln_b
        bonus = jnp.sum(r * k2 * r_k, axis=0, keepdims=True) * v
        o_ref[t] = ((yn + bonus) * g).astype(o_ref.dtype)


def rwkv_lanes(xs, prm_col, s0):
    _, nq, d, nb = xs.shape
    heads = d // RWKV_HEAD
    st = pl.BlockSpec((1, RWKV_HEAD, RWKV_HEAD, nb), lambda h: (h, 0, 0, 0))
    return pl.pallas_call(
        functools.partial(_rwkv_lanes_kernel, nq, SUBLANES),
        grid=(heads,),
        in_specs=[pl.BlockSpec((6, nq, RWKV_HEAD, nb), lambda h: (0, 0, h, 0)),
                  pl.BlockSpec((SUBLANES, RWKV_HEAD, 1), lambda h: (0, h, 0)), st],
        out_specs=[pl.BlockSpec((nq, RWKV_HEAD, nb), lambda h: (0, h, 0)), st],
        out_shape=[jax.ShapeDtypeStruct((nq, d, nb), F32), jax.ShapeDtypeStruct(s0.shape, F32)],
        scratch_shapes=[pltpu.VMEM((RWKV_HEAD, nb), F32)],
        compiler_params=_cparams(("parallel",)),
        name="rwkv_sample",
    )(xs, prm_col, s0)


def _pad_lanes(vec, offset):
    out = jnp.zeros((1, LANES), F32)
    return lax.dynamic_update_slice(out, vec.reshape(1, -1).astype(F32), (0, offset))


def _sample_rows(arr, row0, nb, nq, front):
    cols = arr.shape[1]
    s = arr[row0:row0 + nb * nq].reshape(nb, nq, cols)
    s = jnp.pad(s, ((0, 0), (front, 0), (0, 0)))
    return s.reshape(nb * (front + nq), cols)


def _only(x):
    assert x.shape[0] == 1
    return x.reshape(x.shape[1:])


def kernel(x_prompt, x_sample, cache_fox_k, cache_fox_v, cache_fox_logf, state_gdn_conv, state_gdn_S,
           state_rwkv_shift, state_rwkv_S, page_table, meta_tokens, ln_mix, ln_mlp, ln_final,
           w_in0, gdn_conv_w, gdn_A_log, gdn_dt_bias, gdn_norm_w, fox_b_f, w_out0,
           rwkv_mu, rwkv_w0, rwkv_w1, rwkv_w2, rwkv_a0, rwkv_a1, rwkv_a2, rwkv_g1, rwkv_g2,
           rwkv_k_k, rwkv_k_a, rwkv_r_k, rwkv_w_r, rwkv_w_k, rwkv_w_v, rwkv_w_o, rwkv_ln_w, rwkv_ln_b,
           w_up, w_down):
    D = D_MODEL
    assert x_prompt.shape[0] == 1 and x_prompt.shape[2] == D
    seq = x_prompt.shape[1]
    nb, nq = x_sample.shape[0], x_sample.shape[1]
    npages = page_table.shape[1]
    tprompt = N_META + seq
    pad = (-tprompt) % LANES
    tp = tprompt + pad
    ns = nb * nq
    R = -(-(tp + ns) // ROW_TILE) * ROW_TILE
    CS = SUBLANES
    front = CS - nq
    assert 3 <= front

    x0 = jnp.concatenate([jnp.zeros((pad, D), F32), meta_tokens.astype(F32), x_prompt[0],
                          x_sample.reshape(ns, D), jnp.zeros((R - tp - ns, D), F32)], axis=0)

    w_in = w_in0[0]
    o_z = GDN_QKV
    o_a = o_z + GDN_QK
    o_b = o_a + GDN_HEADS
    o_fq = o_b + GDN_HEADS
    o_fk = o_fq + FOX_W
    o_fv = o_fk + FOX_W
    o_ff = o_fv + FOX_W
    o_fz = o_ff + FOX_HEADS
    w_big = jnp.concatenate([w_in[:, :o_a], w_in[:, o_fq:o_ff], w_in[:, o_fz:]], axis=1).astype(BF16)
    w_small = jnp.concatenate([w_in[:, o_a:o_fq], w_in[:, o_ff:o_fz],
                               jnp.zeros((D, LANES - 3 * GDN_HEADS), F32)], axis=1).astype(BF16)

    h0 = rmsnorm(x0, ln_mix[0], BF16)
    p = matmul(h0, w_big, name="in_proj")
    ps = matmul(h0, w_small, name="in_proj_small")

    alog_pad = _pad_lanes(gdn_A_log[0], LANE_A)
    dt_pad = _pad_lanes(gdn_dt_bias[0], LANE_A)
    bf_pad = _pad_lanes(fox_b_f[0], LANE_F)
    conv_w = gdn_conv_w[0]
    norm_w = gdn_norm_w[0].reshape(1, GDN_D)

    GC = 64
    o_gdn_p, s_gdn_p = gdn(p, p, ps, COL_Z // GDN_QK, conv_w, alog_pad, dt_pad, norm_w,
                           jnp.zeros((1, GDN_HEADS, GDN_D, GDN_D), F32), None,
                           B=1, NC=tp // GC, C=GC, first_valid=pad, name="gdn_prompt", out_rows=R)
    qkv_ext = _sample_rows(p[:, :GDN_QKV], tp, nb, nq, front)
    z_ext = _sample_rows(p[:, COL_Z:COL_Z + GDN_QK], tp, nb, nq, front)
    ps_ext = _sample_rows(ps, tp, nb, nq, front)
    o_gdn_s, s_gdn_s = gdn(qkv_ext, z_ext, ps_ext, 0, conv_w, alog_pad, dt_pad, norm_w,
                           _only(state_gdn_S), state_gdn_conv, B=nb, NC=1, C=CS, first_valid=front, name="gdn_sample")
    o_gdn_s = o_gdn_s.reshape(nb, CS, GDN_QK)[:, front:].reshape(ns, GDN_QK)

    lf, cq, ck = fox_prep(ps, bf_pad, pad)
    o_fox_p = fox_prompt(p, cq, ck, tp, pad, R)
    pt_flat = page_table.reshape(-1).astype(jnp.int32)
    lf_s = lf[tp:tp + ns, LANE_F:LANE_F + FOX_HEADS].reshape(nb, nq, FOX_HEADS)
    lfn = jnp.tile(jnp.swapaxes(lf_s, 1, 2), (1, nq, 1))
    lfn = jnp.pad(lfn, ((0, 0), (0, 0), (0, SUBLANES - nq)))
    pool = cache_fox_k.shape[1]
    page = cache_fox_k.shape[2]
    cache_k = _only(cache_fox_k).reshape(pool, page * FOX_HEADS, FOX_DH)
    cache_v = _only(cache_fox_v).reshape(pool, page * FOX_HEADS, FOX_DH)
    cache_lft = jnp.swapaxes(_only(cache_fox_logf), 1, 2)
    def sample_heads(col):
        return p[tp:tp + ns, col:col + FOX_W].reshape(nb, nq * FOX_HEADS, FOX_DH)

    o_fox_s = fox_sample(pt_flat, sample_heads(COL_FQ), cache_k, cache_v, cache_lft, lfn,
                         sample_heads(COL_FK), sample_heads(COL_FV), sample_heads(COL_FZ), nb, npages)
    o_fox_s = o_fox_s.reshape(ns, FOX_W).astype(BF16)

    def with_sample_rows(full, sample):
        full = lax.dynamic_update_slice(full, sample, (tp, 0))
        return lax.dynamic_update_slice(full, jnp.zeros((R - tp - ns, full.shape[1]), full.dtype), (tp + ns, 0))

    x1 = matmul([with_sample_rows(o_gdn_p, o_gdn_s), with_sample_rows(o_fox_p, o_fox_s)], w_out0, layer=0,
                res=x0, name="out_proj")
    u0 = matmul(rmsnorm(x1, ln_mlp[0], BF16), w_up, layer=0, act="relu2", out_dtype=BF16, name="mlp_up0")
    x2 = matmul(u0, w_down, layer=0, res=x1, name="mlp_down0")

    h1 = rmsnorm(x2, ln_mix[1], F32)
    h1_s = h1[tp:tp + ns].reshape(nb, nq, D)
    shift_rows = jnp.concatenate([jnp.repeat(state_rwkv_shift[0].astype(F32), nq, axis=0),
                                  jnp.zeros((R - tp - ns, D), F32)], axis=0)
    def pad_cols(w):
        return jnp.pad(w, ((0, 0), (0, LANES - w.shape[1]))).astype(BF16)

    def pad_rows(w):
        return jnp.pad(w, ((0, LANES - w.shape[0]), (0, 0))).astype(BF16)

    xr, xk, xv, t_w, t_a, t_g = rwkv_mix(h1, shift_rows, rwkv_mu[0], pad_cols(rwkv_w1[0]), pad_cols(rwkv_a1[0]),
                                         rwkv_g1[0], tp, ns, nq)
    r_ = matmul(xr, rwkv_w_r, layer=0, name="rwkv_r")
    k_ = matmul(xk, rwkv_w_k, layer=0, name="rwkv_k")
    v_ = matmul(xv, rwkv_w_v, layer=0, name="rwkv_v")
    wl = matmul(t_w, pad_rows(rwkv_w2[0]), name="rwkv_w2")
    al = matmul(t_a, pad_rows(rwkv_a2[0]), name="rwkv_a2")
    g_ = matmul(t_g, rwkv_g2, layer=0, name="rwkv_g2")

    prm = jnp.stack([rwkv_w0[0], rwkv_a0[0], rwkv_k_k[0], rwkv_k_a[0], rwkv_r_k[0].reshape(D), rwkv_ln_w[0], rwkv_ln_b[0],
                     jnp.zeros((D,), F32)], axis=0).astype(F32)
    RC = 64
    o_rw_p, s_rw_p = rwkv(r_, k_, v_, wl, al, g_, prm, jnp.zeros((1, 2 * RWKV_PAIRS, RWKV_HEAD, RWKV_HEAD), F32),
                          B=1, NC=tp // RC, C=RC, first_valid=pad, name="rwkv_prompt", out_rows=R)
    xs = jnp.stack([t[tp:tp + ns] for t in (r_, k_, v_, wl, al, g_)])
    xs = jnp.transpose(xs.reshape(6, nb, nq, D), (0, 2, 3, 1))
    s0_l = jnp.transpose(_only(state_rwkv_S), (1, 2, 3, 0))
    o_l, s_l = rwkv_lanes(xs, prm.reshape(SUBLANES, D, 1), s0_l)
    o_rw_s = jnp.transpose(o_l, (2, 0, 1)).reshape(ns, D).astype(BF16)
    s_rw_s = jnp.transpose(s_l, (3, 0, 1, 2))
    x3 = matmul(with_sample_rows(o_rw_p, o_rw_s), rwkv_w_o, layer=0, res=x2, name="rwkv_o")
    u1 = matmul(rmsnorm(x3, ln_mlp[1], BF16), w_up, layer=1, act="relu2", out_dtype=BF16, name="mlp_up1")
    x4 = matmul(u1, w_down, layer=1, res=x3, name="mlp_down1")
    y = rmsnorm(x4, ln_final, F32)

    r0 = pad
    y_prompt = y[r0 + N_META:tp].reshape(1, seq, D)
    y_sample = y[tp:tp + ns].reshape(nb, nq, D)

    def kv_rows(col):
        blk = p[:, col:col + FOX_W]
        return (blk[r0:tp].reshape(1, 1, tprompt, FOX_HEADS, FOX_DH),
                blk[tp:tp + ns].reshape(1, nb, nq, FOX_HEADS, FOX_DH))

    fk_p, fk_s = kv_rows(COL_FK)
    fv_p, fv_s = kv_rows(COL_FV)
    lf8 = lf[:, LANE_F:LANE_F + FOX_HEADS]
    lf_p = lf8[r0:tp].reshape(1, 1, tprompt, FOX_HEADS)
    lf_sm = lf8[tp:tp + ns].reshape(1, nb, nq, FOX_HEADS)
    cb_p = p[tp - 3:tp, :GDN_QKV].reshape(1, 1, 3, GDN_QKV)
    cb_s = p[tp:tp + ns, :GDN_QKV].reshape(nb, nq, GDN_QKV)[:, nq - 3:][None]
    gs_p = s_gdn_p[None]
    gs_s = s_gdn_s[None]
    sh_p = h1[tp - 1].reshape(1, 1, D)
    sh_s = h1_s[:, nq - 1][None]
    rs_p = s_rw_p[None]
    rs_s = s_rw_s[None]
    return (y_prompt, y_sample, fk_p, fk_s, fv_p, fv_s, lf_p, lf_sm, cb_p, cb_s, gs_p, gs_s, sh_p, sh_s, rs_p, rs_s)
```
